```python
import math
import jax, jax.numpy as jnp
from jax import lax
import numpy as np

D_MODEL = 1024
BATCH = 1
SEQ = 16384
DEPTH = 2

N_BRANCHES = 4
BRANCH_WIDTH = 256
HEAD_DIM = 64
N_HEADS = BRANCH_WIDTH // HEAD_DIM
N_IDX_HEADS = 4
IDX_DIM = 32
TOPK_MAX = 256
CONV_WIDTH = 3
CHUNK = 128
N_GROUPS = 4
GROUP_DIM = BRANCH_WIDTH // N_GROUPS
Q_BLOCK = 128
PLE_DIM = 256
ROPE_THETA = 10000.0
EPS = 1e-6
IDX_W_SCALE = (N_IDX_HEADS * IDX_DIM) ** -0.5

IN_SPLITS = (
    BRANCH_WIDTH, BRANCH_WIDTH, BRANCH_WIDTH,
    N_IDX_HEADS * IDX_DIM, IDX_DIM, N_IDX_HEADS,
    BRANCH_WIDTH, BRANCH_WIDTH, BRANCH_WIDTH,
    BRANCH_WIDTH, BRANCH_WIDTH, BRANCH_WIDTH,
    BRANCH_WIDTH, BRANCH_WIDTH,
    N_BRANCHES * BRANCH_WIDTH,
    N_BRANCHES * D_MODEL,
)
IN_COLS = sum(IN_SPLITS)

kernel_name = "hybrid_dsa_conv_stickbreak_gmlp_block"


def _split_cols(h, sizes):
    out, off = [], 0
    for n in sizes:
        out.append(h[..., off:off + n])
        off += n
    return out


def rmsnorm(x, g):
    xf = x.astype(jnp.float32)
    y = xf * lax.rsqrt(jnp.mean(xf * xf, axis=-1, keepdims=True) + EPS)
    return (y * g.astype(jnp.float32)).astype(x.dtype)


def layernorm(x, g, b):
    xf = x.astype(jnp.float32)
    mu = jnp.mean(xf, axis=-1, keepdims=True)
    var = jnp.mean(jnp.square(xf - mu), axis=-1, keepdims=True)
    y = (xf - mu) * lax.rsqrt(var + EPS)
    return (y * g.astype(jnp.float32) + b.astype(jnp.float32)).astype(x.dtype)


def rope(x, pos):
    d = x.shape[-1]
    inv_freq = ROPE_THETA ** (-jnp.arange(0, d, 2, dtype=jnp.float32) / d)
    ang = pos.astype(jnp.float32)[..., None] * inv_freq
    cos = jnp.cos(ang)[:, :, None, :]
    sin = jnp.sin(ang)[:, :, None, :]
    xf = x.astype(jnp.float32)
    x1, x2 = xf[..., : d // 2], xf[..., d // 2:]
    return jnp.concatenate([x1 * cos - x2 * sin, x2 * cos + x1 * sin], axis=-1).astype(x.dtype)


def dsa_sparse_attention(q, k, v, iq, ik, iw):
    B, S, H, dh = q.shape
    topk = min(TOPK_MAX, S // 4)
    n_blk = S // Q_BLOCK
    key_pos = jnp.arange(S)
    neg = jnp.finfo(jnp.float32).min
    scale = HEAD_DIM ** -0.5

    def block(i):
        start = i * Q_BLOCK
        qb = lax.dynamic_slice_in_dim(q, start, Q_BLOCK, axis=1)
        iqb = lax.dynamic_slice_in_dim(iq, start, Q_BLOCK, axis=1)
        iwb = lax.dynamic_slice_in_dim(iw, start, Q_BLOCK, axis=1)
        q_pos = start + jnp.arange(Q_BLOCK)
        idx_logits = jnp.einsum('bqhd,bsd->bqhs', iqb, ik).astype(jnp.float32)
        score = jnp.einsum('bqh,bqhs->bqs', iwb.astype(jnp.float32) * IDX_W_SCALE,
                           jax.nn.relu(idx_logits))
        causal = key_pos[None, :] <= q_pos[:, None]
        score = jnp.where(causal[None], score, neg)
        _, sel = lax.top_k(score, topk)
        k_sel = jax.vmap(lambda kk, ii: kk[ii])(k, sel)
        v_sel = jax.vmap(lambda vv, ii: vv[ii])(v, sel)
        logits = jnp.einsum('bqhd,bqkhd->bhqk', qb, k_sel).astype(jnp.float32) * scale
        valid = sel <= q_pos[None, :, None]
        logits = jnp.where(valid[:, None], logits, neg)
        probs = jax.nn.softmax(logits, axis=-1)
        return jnp.einsum('bhqk,bqkhd->bqhd', probs.astype(v.dtype), v_sel)

    out = lax.map(block, jnp.arange(n_blk))
    return jnp.moveaxis(out, 0, 1).reshape(B, S, H * dh)


def short_gated_conv(gate_b, gate_c, x_in, conv_w, conv_b):
    W = x_in.shape[-1]
    y = gate_c * x_in
    conv = lax.conv_general_dilated(
        y, conv_w.astype(y.dtype)[:, None, :],
        window_strides=(1,), padding=[(CONV_WIDTH - 1, 0)],
        dimension_numbers=('NWC', 'WIO', 'NWC'), feature_group_count=W)
    return gate_b * (conv + conv_b)


def stick_breaking_attention(q, k, v):
    B, S, H, dh = q.shape
    n_blk = S // Q_BLOCK
    key_pos = jnp.arange(S)
    scale = HEAD_DIM ** -0.5

    def block(i):
        start = i * Q_BLOCK
        qb = lax.dynamic_slice_in_dim(q, start, Q_BLOCK, axis=1)
        q_pos = start + jnp.arange(Q_BLOCK)
        z = jnp.einsum('bqhd,bshd->bhqs', qb, k).astype(jnp.float32) * scale
        strict = (key_pos[None, :] < q_pos[:, None])[None, None]
        log_beta = jax.nn.log_sigmoid(z)
        log_keep = jnp.where(strict, jax.nn.log_sigmoid(-z), 0.0)
        after = lax.cumsum(log_keep, axis=3, reverse=True) - log_keep
        wts = jnp.where(strict, jnp.exp(log_beta + after), 0.0)
        return jnp.einsum('bhqs,bshd->bqhd', wts.astype(v.dtype), v)

    out = lax.map(block, jnp.arange(n_blk))
    return jnp.moveaxis(out, 0, 1).reshape(B, S, H * dh)


def chunked_spatial_gating(u, v, ln_g, ln_b, w_s, b_s):
    B, S, W = v.shape
    vn = layernorm(v, ln_g, ln_b).reshape(B, S // CHUNK, CHUNK, N_GROUPS, GROUP_DIM)
    mask = jnp.tril(jnp.ones((CHUNK, CHUNK), dtype=w_s.dtype))
    mixed = jnp.einsum('gts,bnsgd->bntgd', w_s * mask, vn) + b_s.T[None, None, :, :, None]
    return u * mixed.reshape(B, S, W)


def hybrid_layer(x, p_i, positions, g_pre, w_in, conv_w, conv_b, ln_g, ln_b,
                 w_spatial, b_spatial, b_merge, w_branch, w_out, g_post, w_ple, w_ple_gate):
    B, S, _ = x.shape
    h = rmsnorm(x, g_pre)
    proj = jnp.einsum('bsd,dc->bsc', h, w_in)
    (a_q, a_k, a_v, a_iq, a_ik, a_iw, b_b, b_c, b_x, c_q, c_k, c_v,
     d_u, d_v, gate_paths, merge_logits) = _split_cols(proj, IN_SPLITS)

    heads = lambda t: t.reshape(B, S, N_HEADS, HEAD_DIM)
    iq = rope(a_iq.reshape(B, S, N_IDX_HEADS, IDX_DIM), positions)
    ik = rope(a_ik[:, :, None, :], positions)[:, :, 0, :]
    o_a = dsa_sparse_attention(rope(heads(a_q), positions), rope(heads(a_k), positions),
                               heads(a_v), iq, ik, a_iw)
    o_b = short_gated_conv(b_b, b_c, b_x, conv_w, conv_b)
    o_c = stick_breaking_attention(heads(c_q), heads(c_k), heads(c_v))
    o_d = chunked_spatial_gating(d_u, d_v, ln_g, ln_b, w_spatial, b_spatial)

    o = jnp.stack([o_a, o_b, o_c, o_d], axis=2)
    o = o * jax.nn.silu(gate_paths.reshape(B, S, N_BRANCHES, BRANCH_WIDTH))
    z = jnp.einsum('bsnw,nwd->bsnd', o, w_branch)
    g = jax.nn.sigmoid((merge_logits + b_merge).reshape(B, S, N_BRANCHES, D_MODEL))
    merged = jnp.sum(g * z, axis=2)
    y = jnp.einsum('bsd,de->bse', merged, w_out)
    x = x + rmsnorm(y, g_post)
    ple = jnp.einsum('bsp,pd->bsd', p_i, w_ple)
    x = x + ple * jax.nn.sigmoid(jnp.einsum('bsd,de->bse', x, w_ple_gate))
    return x


def setup_inputs(seed: int = 0) -> dict:
    key = jax.random.key(seed)
    ks = jax.random.split(key, 20)
    nrm = lambda k, shape, s: jax.random.normal(k, shape, jnp.float32) * s
    W = BRANCH_WIDTH
    return {
        "x": nrm(ks[0], (BATCH, SEQ, D_MODEL), 1.0),
        "p": nrm(ks[1], (DEPTH, BATCH, SEQ, PLE_DIM), 1.0),
        "positions": jnp.broadcast_to(jnp.arange(SEQ, dtype=jnp.int32), (BATCH, SEQ)),
        "g_pre": 1.0 + nrm(ks[2], (DEPTH, D_MODEL), 0.02),
        "w_in": nrm(ks[3], (DEPTH, D_MODEL, IN_COLS), D_MODEL ** -0.5),
        "conv_w": nrm(ks[4], (DEPTH, CONV_WIDTH, W), CONV_WIDTH ** -0.5),
        "conv_b": nrm(ks[5], (DEPTH, W), 0.02),
        "ln_g": 1.0 + nrm(ks[6], (DEPTH, W), 0.02),
        "ln_b": nrm(ks[7], (DEPTH, W), 0.02),
        "w_spatial": nrm(ks[8], (DEPTH, N_GROUPS, CHUNK, CHUNK), CHUNK ** -0.5),
        "b_spatial": 1.0 + nrm(ks[9], (DEPTH, N_GROUPS, CHUNK), 0.02),
        "b_merge": nrm(ks[10], (DEPTH, N_BRANCHES * D_MODEL), 0.02),
        "w_branch": nrm(ks[11], (DEPTH, N_BRANCHES, W, D_MODEL), W ** -0.5),
        "w_out": nrm(ks[12], (DEPTH, D_MODEL, D_MODEL), D_MODEL ** -0.5),
        "g_post": 1.0 + nrm(ks[13], (DEPTH, D_MODEL), 0.02),
        "w_ple": nrm(ks[14], (DEPTH, PLE_DIM, D_MODEL), PLE_DIM ** -0.5),
        "w_ple_gate": nrm(ks[15], (DEPTH, D_MODEL, D_MODEL), D_MODEL ** -0.5),
    }


def reference(x, p, positions, g_pre, w_in, conv_w, conv_b, ln_g, ln_b, w_spatial, b_spatial,
              b_merge, w_branch, w_out, g_post, w_ple, w_ple_gate):
    for i in range(DEPTH):
        x = hybrid_layer(x, p[i], positions, g_pre[i], w_in[i], conv_w[i], conv_b[i], ln_g[i], ln_b[i],
                         w_spatial[i], b_spatial[i], b_merge[i], w_branch[i], w_out[i], g_post[i],
                         w_ple[i], w_ple_gate[i])
    return x
```

```python
import functools

import jax
import jax.numpy as jnp
from jax import lax
from jax.experimental import pallas as pl
from jax.experimental.pallas import tpu as pltpu

D_MODEL = 1024
BRANCH_WIDTH = 256
HEAD_DIM = 64
N_HEADS = 4
N_IDX_HEADS = 4
IDX_DIM = 32
TOPK_MAX = 256
CONV_WIDTH = 3
CHUNK = 128
N_GROUPS = 4
GROUP_DIM = BRANCH_WIDTH // N_GROUPS
PLE_DIM = 256
ROPE_THETA = 10000.0
EPS = 1e-6
IDX_W_SCALE = (N_IDX_HEADS * IDX_DIM) ** -0.5
QK_SCALE = HEAD_DIM ** -0.5

ROW_BLOCK = 256
ATT_BLOCK = 256
COUNT_ROWS = 128
CONV_HALO = 8
MASK_BIAS = -1e30
SCORE_FLOOR = float(jnp.finfo(jnp.float32).min)
SB_DEAD = 104.0
VMEM_LIMIT = 56 * 1024 * 1024

BF16 = jnp.bfloat16
F32 = jnp.float32
NT_DIMS = (((1,), (1,)), ((), ()))


def _dot(a, b):
    return jnp.dot(a, b, preferred_element_type=F32)


def _dot_nt(a, b):
    return lax.dot_general(a, b, NT_DIMS, preferred_element_type=F32)


def _rmsnorm_rows(x, g):
    return x * lax.rsqrt(jnp.mean(x * x, axis=-1, keepdims=True) + EPS) * g


def _sigmoid(x):
    return 1.0 / (1.0 + jnp.exp(-x))


def _params(n_grid_dims=1):
    return pltpu.CompilerParams(
        dimension_semantics=("arbitrary",) * n_grid_dims, vmem_limit_bytes=VMEM_LIMIT)


def _full(shape):
    return pl.BlockSpec(shape, lambda i: (0,) * len(shape))


def _rows(width, block=ROW_BLOCK):
    return pl.BlockSpec((block, width), lambda i: (i, 0))


def _rope_table_kernel(pos_ref, f64_ref, s64_ref, f32_ref, s32_ref,
                       cos64_ref, sin64_ref, cos32_ref, sin32_ref):
    pos = pos_ref[...].astype(F32)
    a64 = pos * f64_ref[...]
    cos64_ref[...] = jnp.cos(a64)
    sin64_ref[...] = jnp.sin(a64) * s64_ref[...]
    a32 = pos * f32_ref[...]
    cos32_ref[...] = jnp.cos(a32)
    sin32_ref[...] = jnp.sin(a32) * s32_ref[...]


def _rope_lane_tables(d, width):
    inv_freq = ROPE_THETA ** (-jnp.arange(0, d, 2, dtype=F32) / d)
    lane = jnp.arange(width)
    freq = inv_freq[lane % (d // 2)][None, :]
    sign = jnp.where((lane % d) < d // 2, -1.0, 1.0).astype(F32)[None, :]
    return freq, sign


def _rope_tables(pos_col):
    s = pos_col.shape[0]
    f64, s64 = _rope_lane_tables(HEAD_DIM, BRANCH_WIDTH)
    f32, s32 = _rope_lane_tables(IDX_DIM, N_IDX_HEADS * IDX_DIM)
    return pl.pallas_call(
        _rope_table_kernel,
        grid=(s // ROW_BLOCK,),
        in_specs=[_rows(1), _full((1, 256)), _full((1, 256)), _full((1, 128)), _full((1, 128))],
        out_specs=[_rows(256), _rows(256), _rows(128), _rows(128)],
        out_shape=[jax.ShapeDtypeStruct((s, 256), F32), jax.ShapeDtypeStruct((s, 256), F32),
                   jax.ShapeDtypeStruct((s, 128), F32), jax.ShapeDtypeStruct((s, 128), F32)],
        compiler_params=_params(),
        name="rope_tables",
    )(pos_col, f64, s64, f32, s32)


def _attn_proj_kernel(x_ref, g_ref, wa_ref, wvt_ref, wi_ref, wiwt_ref, wc_ref,
                      cos64_ref, sin64_ref, cos32_ref, sin32_ref,
                      qa_ref, ka_ref, vat_ref, iq_ref, ik_ref, iwt_ref, cq_ref, ck_ref, cv_ref):
    h = _rmsnorm_rows(x_ref[...], g_ref[...]).astype(BF16)
    c64, s64 = cos64_ref[...], sin64_ref[...]
    c32, s32 = cos32_ref[...], sin32_ref[...]
    pa = _dot(h, wa_ref[...])
    qa_ref[...] = (pa[:, 0:256] * c64 + pa[:, 256:512] * s64) * QK_SCALE
    ka_ref[...] = (pa[:, 512:768] * c64 + pa[:, 768:1024] * s64).astype(BF16)
    vat_ref[0] = _dot_nt(wvt_ref[...], h).astype(BF16)
    pi = _dot(h, wi_ref[...])
    iq_ref[...] = pi[:, 0:128] * c32 + pi[:, 128:256] * s32
    ik = pi[:, 256:384] * c32 + pi[:, 384:512] * s32
    ik_ref[...] = ik[:, 0:IDX_DIM].astype(BF16)
    iwt_ref[...] = _dot_nt(wiwt_ref[...], h) * IDX_W_SCALE
    pc = _dot(h, wc_ref[...])
    cq_ref[...] = (pc[:, 0:256] * QK_SCALE).astype(BF16)
    ck_ref[...] = pc[:, 256:512].astype(BF16)
    cv_ref[...] = pc[:, 512:768].astype(BF16)


def _attn_proj(x, g_pre, w, tables):
    s = x.shape[0]
    nb = s // ROW_BLOCK
    cos64, sin64, cos32, sin32 = tables
    return pl.pallas_call(
        _attn_proj_kernel,
        grid=(nb,),
        in_specs=[_rows(D_MODEL), _full((1, D_MODEL)), _full((D_MODEL, 1024)), _full((256, D_MODEL)),
                  _full((D_MODEL, 512)), _full((8, D_MODEL)), _full((D_MODEL, 768)),
                  _rows(256), _rows(256), _rows(128), _rows(128)],
        out_specs=[_rows(256), _rows(256), pl.BlockSpec((1, 256, ROW_BLOCK), lambda i: (i, 0, 0)),
                   _rows(128), _rows(IDX_DIM), pl.BlockSpec((8, ROW_BLOCK), lambda i: (0, i)),
                   _rows(256), _rows(256), _rows(256)],
        out_shape=[jax.ShapeDtypeStruct((s, 256), F32), jax.ShapeDtypeStruct((s, 256), BF16),
                   jax.ShapeDtypeStruct((nb, 256, ROW_BLOCK), BF16),
                   jax.ShapeDtypeStruct((s, 128), F32), jax.ShapeDtypeStruct((s, IDX_DIM), BF16),
                   jax.ShapeDtypeStruct((8, s), F32),
                   jax.ShapeDtypeStruct((s, 256), BF16), jax.ShapeDtypeStruct((s, 256), BF16),
                   jax.ShapeDtypeStruct((s, 256), BF16)],
        compiler_params=_params(),
        name="attn_proj",
    )(x, g_pre, w["wa"], w["wvt"], w["wi"], w["wiwt"], w["wc"], cos64, sin64, cos32, sin32)


def _local_mix_kernel(x_ref, g_ref, w_ref, convw_ref, convb_ref, lng_ref, lnb_ref, ws_ref, bs_ref,
                      ogb_ref, ogd_ref, ypad_ref):
    t = x_ref.shape[0]

    @pl.when(pl.program_id(0) == 0)
    def _():
        ypad_ref[0:CONV_HALO, :] = jnp.zeros((CONV_HALO, BRANCH_WIDTH), F32)

    h = _rmsnorm_rows(x_ref[...], g_ref[...]).astype(BF16)
    pr = _dot(h, w_ref[...])
    gate_b, gate_c, x_in = pr[:, 0:256], pr[:, 256:512], pr[:, 512:768]
    d_u, d_v = pr[:, 768:1024], pr[:, 1024:1280]
    silu_b, silu_d = pr[:, 1280:1536], pr[:, 1536:1792]

    y = gate_c * x_in
    ypad_ref[CONV_HALO:CONV_HALO + t, :] = y
    y1 = ypad_ref[CONV_HALO - 1:CONV_HALO - 1 + t, :]
    y2 = ypad_ref[CONV_HALO - 2:CONV_HALO - 2 + t, :]
    conv = convw_ref[2:3, :] * y + convw_ref[1:2, :] * y1 + convw_ref[0:1, :] * y2
    ypad_ref[0:CONV_HALO, :] = y[t - CONV_HALO:t, :]
    o_b = gate_b * (conv + convb_ref[...])
    ogb_ref[...] = (o_b * (silu_b * _sigmoid(silu_b))).astype(BF16)

    mu = jnp.mean(d_v, axis=-1, keepdims=True)
    dc = d_v - mu
    var = jnp.mean(dc * dc, axis=-1, keepdims=True)
    vn = dc * lax.rsqrt(var + EPS) * lng_ref[...] + lnb_ref[...]
    group = lax.broadcasted_iota(jnp.int32, (1, BRANCH_WIDTH), 1) // GROUP_DIM
    tril = (lax.broadcasted_iota(jnp.int32, (CHUNK, CHUNK), 0)
            >= lax.broadcasted_iota(jnp.int32, (CHUNK, CHUNK), 1))
    wm = [jnp.where(tril, ws_ref[g], 0.0).astype(BF16) for g in range(N_GROUPS)]
    mixed = []
    for c in range(t // CHUNK):
        vc = vn[c * CHUNK:(c + 1) * CHUNK, :]
        m = bs_ref[...]
        for g in range(N_GROUPS):
            m = m + _dot(wm[g], jnp.where(group == g, vc, 0.0).astype(BF16))
        mixed.append(m)
    o_d = d_u * jnp.concatenate(mixed, axis=0)
    ogd_ref[...] = (o_d * (silu_d * _sigmoid(silu_d))).astype(BF16)


def _local_mix(x, g_pre, w):
    s = x.shape[0]
    return pl.pallas_call(
        _local_mix_kernel,
        grid=(s // ROW_BLOCK,),
        in_specs=[_rows(D_MODEL), _full((1, D_MODEL)), _full((D_MODEL, 1792)), _full((8, 256)),
                  _full((1, 256)), _full((1, 256)), _full((1, 256)),
                  _full((N_GROUPS, CHUNK, CHUNK)), _full((CHUNK, 256))],
        out_specs=[_rows(256), _rows(256)],
        out_shape=[jax.ShapeDtypeStruct((s, 256), BF16), jax.ShapeDtypeStruct((s, 256), BF16)],
        scratch_shapes=[pltpu.VMEM((ROW_BLOCK + CONV_HALO, BRANCH_WIDTH), F32)],
        compiler_params=_params(),
        name="local_mix",
    )(x, g_pre, w["w2"], w["conv_w"], w["conv_b"], w["ln_g"], w["ln_b"], w["w_spatial"], w["b_spatial"])


def _key_to_float(key):
    bits = jnp.where(key >= 0, key, key ^ jnp.int32(0x7FFFFFFF))
    return lax.bitcast_convert_type(bits, F32)


def _count_pass(sc_ref, n_steps, cand, strict):
    q = sc_ref.shape[1]
    cand8 = jnp.broadcast_to(cand, (8, q))
    n_acc = 4

    def body(c, accs):
        accs = list(accs)
        base = pl.multiple_of(c * COUNT_ROWS, COUNT_ROWS)
        for r in range(COUNT_ROWS // 8):
            v = sc_ref[pl.ds(base + 8 * r, 8), :]
            hit = (v > cand8) if strict else (v >= cand8)
            accs[r % n_acc] = accs[r % n_acc] + jnp.where(hit, 1, 0)
        return tuple(accs)

    zero = jnp.zeros((8, q), jnp.int32)
    accs = lax.fori_loop(0, n_steps, body, (zero,) * n_acc)
    total = accs[0] + accs[1] + accs[2] + accs[3]
    return jnp.sum(total, axis=0, keepdims=True)


def _dsa_kernel(q_ref, iq_ref, iwt_ref, k_ref, vt_ref, ik_ref, o_ref,
                sc_ref, qm_ref, iqt_ref, acc_ref, m_ref, l_ref, *, topk):
    nq = q_ref.shape[0]
    kb = ATT_BLOCK
    i = pl.program_id(0)
    n_tiles = i + 1
    t_idx = i * nq + lax.broadcasted_iota(jnp.int32, (1, nq), 1)
    row = lax.broadcasted_iota(jnp.int32, (kb, 1), 0)

    qt = q_ref[...].T
    head_of_row = lax.broadcasted_iota(jnp.int32, (BRANCH_WIDTH, 1), 0) // HEAD_DIM
    for h in range(N_HEADS):
        qm_ref[h] = jnp.where(head_of_row == h, qt, 0.0).astype(BF16)
    iqt_ref[...] = iq_ref[...].T.astype(BF16)
    w_rows = [iwt_ref[h:h + 1, :] for h in range(N_IDX_HEADS)]

    def score_tile(j, carry):
        base = pl.multiple_of(j * kb, kb)
        ikb = ik_ref[pl.ds(base, kb), :]
        sc = jnp.zeros((kb, nq), F32)
        for h in range(N_IDX_HEADS):
            logit = _dot(ikb, iqt_ref[h * IDX_DIM:(h + 1) * IDX_DIM, :])
            sc = sc + w_rows[h] * jnp.maximum(logit, 0.0)
        sc = jnp.where(base + row <= t_idx, sc, SCORE_FLOOR)
        sc_ref[pl.ds(base, kb), :] = sc
        return carry

    lax.fori_loop(0, n_tiles, score_tile, 0)

    n_steps = n_tiles * (kb // COUNT_ROWS)
    cnt0 = _count_pass(sc_ref, n_steps, jnp.zeros((1, nq), F32), strict=False)
    key0 = jnp.where(cnt0 >= topk, jnp.int32(0), jnp.int32(-2 ** 31))

    def bisect(b, key):
        cand = key | (jnp.int32(1) << (30 - b))
        cnt = _count_pass(sc_ref, n_steps, _key_to_float(cand), strict=False)
        return jnp.where(cnt >= topk, cand, key)

    key = lax.fori_loop(0, 31, bisect, key0)
    tau = _key_to_float(key)
    n_above = _count_pass(sc_ref, n_steps, tau, strict=True)
    quota = (topk - n_above).astype(F32)
    tau = jnp.where(t_idx < topk, -jnp.inf, tau)

    m_ref[...] = jnp.full(m_ref.shape, MASK_BIAS, F32)
    l_ref[...] = jnp.zeros(l_ref.shape, F32)
    acc_ref[...] = jnp.zeros(acc_ref.shape, F32)
    incl_lower = (lax.broadcasted_iota(jnp.int32, (kb, kb), 0)
                  >= lax.broadcasted_iota(jnp.int32, (kb, kb), 1)).astype(BF16)

    def attend_tile(j, ties_before):
        base = pl.multiple_of(j * kb, kb)
        sc = sc_ref[pl.ds(base, kb), :]
        tie = sc == tau
        ties_upto = _dot(incl_lower, jnp.where(tie, 1.0, 0.0).astype(BF16)) + ties_before
        keep = (sc > tau) | (tie & (ties_upto <= quota))
        keep = keep & (base + row <= t_idx)
        bias = jnp.where(keep, 0.0, MASK_BIAS)
        kt = k_ref[pl.ds(base, kb), :]
        vt = vt_ref[j]
        for h in range(N_HEADS):
            logit = _dot(kt, qm_ref[h]) + bias
            m_old = m_ref[h:h + 1, :]
            m_new = jnp.maximum(m_old, jnp.max(logit, axis=0, keepdims=True))
            p = jnp.exp(logit - m_new)
            alpha = jnp.exp(m_old - m_new)
            l_ref[h:h + 1, :] = alpha * l_ref[h:h + 1, :] + jnp.sum(p, axis=0, keepdims=True)
            hs = slice(h * HEAD_DIM, (h + 1) * HEAD_DIM)
            acc_ref[hs, :] = alpha * acc_ref[hs, :] + _dot(vt[hs, :], p.astype(BF16))
            m_ref[h:h + 1, :] = m_new
        return ties_upto[kb - 1:kb, :]

    lax.fori_loop(0, n_tiles, attend_tile, jnp.zeros((1, nq), F32))

    for h in range(N_HEADS):
        hs = slice(h * HEAD_DIM, (h + 1) * HEAD_DIM)
        acc_ref[hs, :] = acc_ref[hs, :] / l_ref[h:h + 1, :]
    o_ref[...] = acc_ref[...].T


def _dsa_attention(qa, iq, iwt, ka, vat, ik):
    s = qa.shape[0]
    nq = ATT_BLOCK
    topk = min(TOPK_MAX, s // 4)
    return pl.pallas_call(
        functools.partial(_dsa_kernel, topk=topk),
        grid=(s // nq,),
        in_specs=[_rows(256, nq), _rows(128, nq), pl.BlockSpec((8, nq), lambda i: (0, i)),
                  _full((s, 256)), _full((s // ATT_BLOCK, 256, ATT_BLOCK)), _full((s, IDX_DIM))],
        out_specs=_rows(256, nq),
        out_shape=jax.ShapeDtypeStruct((s, 256), F32),
        scratch_shapes=[pltpu.VMEM((s, nq), F32), pltpu.VMEM((N_HEADS, 256, nq), BF16),
                        pltpu.VMEM((128, nq), BF16), pltpu.VMEM((256, nq), F32),
                        pltpu.VMEM((8, nq), F32), pltpu.VMEM((8, nq), F32)],
        compiler_params=_params(),
        name="dsa_attn",
    )(qa, iq, iwt, ka, vat, ik)


def _sb_kernel(q_ref, k_ref, v_ref, o_ref, acc_ref, carry_ref):
    nq = q_ref.shape[0]
    kb = ATT_BLOCK
    i = pl.program_id(0)
    q = q_ref[...]
    lane_head = lax.broadcasted_iota(jnp.int32, (1, BRANCH_WIDTH), 1) // HEAD_DIM
    qh = [jnp.where(lane_head == h, q, jnp.zeros_like(q)) for h in range(N_HEADS)]
    later = (lax.broadcasted_iota(jnp.int32, (kb, kb), 0)
             > lax.broadcasted_iota(jnp.int32, (kb, kb), 1)).astype(BF16)
    t_idx = i * nq + lax.broadcasted_iota(jnp.int32, (nq, 1), 0)
    col = lax.broadcasted_iota(jnp.int32, (1, kb), 1)
    acc_ref[...] = jnp.zeros(acc_ref.shape, F32)
    carry_ref[...] = jnp.zeros(carry_ref.shape, F32)

    def alive(state):
        j, live = state
        return jnp.logical_and(j >= 0, live > 0)

    def walk(state):
        j, _ = state
        base = pl.multiple_of(j * kb, kb)
        kt = k_ref[pl.ds(base, kb), :]
        vt = v_ref[pl.ds(base, kb), :]
        strict = (base + col) < t_idx
        lowest = jnp.full((1, 1), jnp.inf, F32)
        for h in range(N_HEADS):
            z = _dot_nt(qh[h], kt)
            softplus = jnp.maximum(z, 0.0) + jnp.log1p(jnp.exp(-jnp.abs(z)))
            sp = jnp.where(strict, softplus, 0.0)
            hi = sp.astype(BF16)
            lo = (sp - hi.astype(F32)).astype(BF16)
            after = _dot(hi, later) + _dot(lo, later)
            c = carry_ref[h]
            wts = jnp.where(strict, jnp.exp(z - softplus - c - after), 0.0)
            vh = jnp.where(lane_head == h, vt, jnp.zeros_like(vt))
            acc_ref[...] += _dot(wts.astype(BF16), vh)
            c_new = c + jnp.sum(sp, axis=1, keepdims=True)
            carry_ref[h] = c_new
            lowest = jnp.minimum(lowest, jnp.min(c_new, axis=0, keepdims=True))
        live = (lowest[0, 0] < SB_DEAD).astype(jnp.int32)
        return j - 1, live

    lax.while_loop(alive, walk, (i, jnp.int32(1)))
    o_ref[...] = acc_ref[...]


def _sb_attention(cq, ck, cv):
    s = cq.shape[0]
    nq = ATT_BLOCK
    return pl.pallas_call(
        _sb_kernel,
        grid=(s // nq,),
        in_specs=[_rows(256, nq), _full((s, 256)), _full((s, 256))],
        out_specs=_rows(256, nq),
        out_shape=jax.ShapeDtypeStruct((s, 256), F32),
        scratch_shapes=[pltpu.VMEM((nq, 256), F32), pltpu.VMEM((N_HEADS, nq, 1), F32)],
        compiler_params=_params(),
        name="sb_attn",
    )(cq, ck, cv)


def _merge_out_kernel(x_ref, oa_ref, oc_ref, ogb_ref, ogd_ref, p_ref, g_ref, wg_ref, wm_ref, bm_ref,
                      wb_ref, wo_ref, gpost_ref, wple_ref, wpg_ref, out_ref):
    x = x_ref[...]
    h = _rmsnorm_rows(x, g_ref[...]).astype(BF16)
    gates = _dot(h, wg_ref[...])
    silu = gates * _sigmoid(gates)
    branch_in = [(oa_ref[...] * silu[:, 0:256]).astype(BF16), ogb_ref[...],
                 (oc_ref[...] * silu[:, 256:512]).astype(BF16), ogd_ref[...]]
    merged = jnp.zeros((x.shape[0], D_MODEL), F32)
    for n in range(4):
        cols = slice(n * D_MODEL, (n + 1) * D_MODEL)
        gate = _sigmoid(_dot(h, wm_ref[:, cols]) + bm_ref[:, cols])
        merged = merged + gate * _dot(branch_in[n], wb_ref[n])
    y = _dot(merged.astype(BF16), wo_ref[...])
    x1 = x + _rmsnorm_rows(y, gpost_ref[...])
    ple = _dot(p_ref[...].astype(BF16), wple_ref[...])
    out_ref[...] = x1 + ple * _sigmoid(_dot(x1.astype(BF16), wpg_ref[...]))


def _merge_out(x, o_a, o_c, og_b, og_d, p_i, g_pre, w):
    s = x.shape[0]
    return pl.pallas_call(
        _merge_out_kernel,
        grid=(s // ROW_BLOCK,),
        in_specs=[_rows(D_MODEL), _rows(256), _rows(256), _rows(256), _rows(256), _rows(PLE_DIM),
                  _full((1, D_MODEL)), _full((D_MODEL, 512)), _full((D_MODEL, 4 * D_MODEL)),
                  _full((1, 4 * D_MODEL)), _full((4, BRANCH_WIDTH, D_MODEL)), _full((D_MODEL, D_MODEL)),
                  _full((1, D_MODEL)), _full((PLE_DIM, D_MODEL)), _full((D_MODEL, D_MODEL))],
        out_specs=_rows(D_MODEL),
        out_shape=jax.ShapeDtypeStruct((s, D_MODEL), F32),
        compiler_params=_params(),
        name="merge_out",
    )(x, o_a, o_c, og_b, og_d, p_i, g_pre, w["wg"], w["wm"], w["b_merge"], w["w_branch"], w["w_out"],
      w["g_post"], w["w_ple"], w["w_ple_gate"])


def _swap_halves(w, d):
    n = w.shape[1]
    idx = jnp.arange(n)
    return w[:, (idx // d) * d + (idx % d + d // 2) % d]


def _layer_weights(i, w_in, conv_w, conv_b, ln_g, ln_b, w_spatial, b_spatial, b_merge, w_branch, w_out,
                   g_post, w_ple, w_ple_gate):
    wi = w_in[i]
    sizes = (256, 256, 256, 128, 32, 4, 256, 256, 256, 256, 256, 256, 256, 256, 1024, 4096)
    cols, off = [], 0
    for n in sizes:
        cols.append(wi[:, off:off + n])
        off += n
    (a_q, a_k, a_v, a_iq, a_ik, a_iw, b_b, b_c, b_x, c_q, c_k, c_v, d_u, d_v, gates, merge) = cols
    pad = lambda m, n: jnp.pad(m, ((0, 0), (0, n - m.shape[1])))
    row = lambda v: v[None, :]
    return {
        "wa": jnp.concatenate([a_q, _swap_halves(a_q, HEAD_DIM), a_k, _swap_halves(a_k, HEAD_DIM)], 1).astype(BF16),
        "wvt": a_v.T.astype(BF16),
        "wi": jnp.concatenate([a_iq, _swap_halves(a_iq, IDX_DIM), pad(a_ik, 128),
                               pad(_swap_halves(a_ik, IDX_DIM), 128)], 1).astype(BF16),
        "wiwt": pad(a_iw, 8).T.astype(BF16),
        "wc": jnp.concatenate([c_q, c_k, c_v], 1).astype(BF16),
        "w2": jnp.concatenate([b_b, b_c, b_x, d_u, d_v, gates[:, 256:512], gates[:, 768:1024]], 1).astype(BF16),
        "wg": jnp.concatenate([gates[:, 0:256], gates[:, 512:768]], 1).astype(BF16),
        "wm": merge.astype(BF16),
        "conv_w": jnp.pad(conv_w[i], ((0, 8 - CONV_WIDTH), (0, 0))),
        "conv_b": row(conv_b[i]), "ln_g": row(ln_g[i]), "ln_b": row(ln_b[i]),
        "w_spatial": w_spatial[i],
        "b_spatial": jnp.repeat(b_spatial[i].T, GROUP_DIM, axis=1),
        "b_merge": row(b_merge[i]),
        "w_branch": w_branch[i].astype(BF16), "w_out": w_out[i].astype(BF16), "g_post": row(g_post[i]),
        "w_ple": w_ple[i].astype(BF16), "w_ple_gate": w_ple_gate[i].astype(BF16),
    }


def kernel(x, p, positions, g_pre, w_in, conv_w, conv_b, ln_g, ln_b, w_spatial, b_spatial, b_merge,
           w_branch, w_out, g_post, w_ple, w_ple_gate):
    batch, s, _ = x.shape
    assert batch == 1 and s % ROW_BLOCK == 0 and s % ATT_BLOCK == 0 and ATT_BLOCK >= min(TOPK_MAX, s // 4)
    depth = w_in.shape[0]
    xs = x[0]
    tables = _rope_tables(positions[0][:, None])
    for i in range(depth):
        w = _layer_weights(i, w_in, conv_w, conv_b, ln_g, ln_b, w_spatial, b_spatial, b_merge, w_branch,
                           w_out, g_post, w_ple, w_ple_gate)
        g = g_pre[i][None, :]
        qa, ka, vat, iq, ik, iwt, cq, ck, cv = _attn_proj(xs, g, w, tables)
        og_b, og_d = _local_mix(xs, g, w)
        o_a = _dsa_attention(qa, iq, iwt, ka, vat, ik)
        o_c = _sb_attention(cq, ck, cv)
        xs = _merge_out(xs, o_a, o_c, og_b, og_d, p[i][0], g, w)
    return xs[None]
```

```python
import functools

import jax
import jax.numpy as jnp
from jax import lax
from jax.experimental import pallas as pl
from jax.experimental.pallas import tpu as pltpu

D_MODEL = 1024
BRANCH_WIDTH = 256
HEAD_DIM = 64
N_HEADS = 4
N_IDX_HEADS = 4
IDX_DIM = 32
TOPK_MAX = 256
CONV_WIDTH = 3
CHUNK = 128
N_GROUPS = 4
GROUP_DIM = BRANCH_WIDTH // N_GROUPS
PLE_DIM = 256
ROPE_THETA = 10000.0
EPS = 1e-6
IDX_W_SCALE = (N_IDX_HEADS * IDX_DIM) ** -0.5
QK_SCALE = HEAD_DIM ** -0.5
LOG2_E = 1.4426950408889634

ROW_BLOCK = 256
ATT_BLOCK = 256
COUNT_ROWS = 256
CONV_HALO = 8
MASK_BIAS = -1e30
SCORE_FLOOR = float(jnp.finfo(jnp.float32).min)
SB_DEAD = 104.0
VMEM_LIMIT = 56 * 1024 * 1024

BF16 = jnp.bfloat16
F32 = jnp.float32
NT_DIMS = (((1,), (1,)), ((), ()))


def _dot(a, b):
    return jnp.dot(a, b, preferred_element_type=F32)


def _dot_nt(a, b):
    return lax.dot_general(a, b, NT_DIMS, preferred_element_type=F32)


def _rmsnorm_rows(x, g):
    return x * lax.rsqrt(jnp.mean(x * x, axis=-1, keepdims=True) + EPS) * g


def _sigmoid(x):
    return 1.0 / (1.0 + jnp.exp(-x))


def _params(n_grid_dims=1):
    return pltpu.CompilerParams(
        dimension_semantics=("arbitrary",) * n_grid_dims, vmem_limit_bytes=VMEM_LIMIT)


def _full(shape):
    return pl.BlockSpec(shape, lambda i: (0,) * len(shape))


def _rows(width, block=ROW_BLOCK):
    return pl.BlockSpec((block, width), lambda i: (i, 0))


def _rope_table_kernel(pos_ref, f64_ref, s64_ref, f32_ref, s32_ref,
                       cos64_ref, sin64_ref, cos32_ref, sin32_ref):
    pos = pos_ref[...].astype(F32)
    a64 = pos * f64_ref[...]
    cos64_ref[...] = jnp.cos(a64)
    sin64_ref[...] = jnp.sin(a64) * s64_ref[...]
    a32 = pos * f32_ref[...]
    cos32_ref[...] = jnp.cos(a32)
    sin32_ref[...] = jnp.sin(a32) * s32_ref[...]


def _rope_lane_tables(d, width):
    inv_freq = ROPE_THETA ** (-jnp.arange(0, d, 2, dtype=F32) / d)
    lane = jnp.arange(width)
    freq = inv_freq[lane % (d // 2)][None, :]
    sign = jnp.where((lane % d) < d // 2, -1.0, 1.0).astype(F32)[None, :]
    return freq, sign


def _rope_tables(pos_col):
    s = pos_col.shape[0]
    f64, s64 = _rope_lane_tables(HEAD_DIM, BRANCH_WIDTH)
    f32, s32 = _rope_lane_tables(IDX_DIM, N_IDX_HEADS * IDX_DIM)
    return pl.pallas_call(
        _rope_table_kernel,
        grid=(s // ROW_BLOCK,),
        in_specs=[_rows(1), _full((1, 256)), _full((1, 256)), _full((1, 128)), _full((1, 128))],
        out_specs=[_rows(256), _rows(256), _rows(128), _rows(128)],
        out_shape=[jax.ShapeDtypeStruct((s, 256), F32), jax.ShapeDtypeStruct((s, 256), F32),
                   jax.ShapeDtypeStruct((s, 128), F32), jax.ShapeDtypeStruct((s, 128), F32)],
        compiler_params=_params(),
        name="rope_tables",
    )(pos_col, f64, s64, f32, s32)


def _attn_proj_kernel(x_ref, g_ref, wa_ref, wi_ref, wc_ref,
                      cos64_ref, sin64_ref, cos32_ref, sin32_ref,
                      qa_ref, ka_ref, vat_ref, iq_ref, ik_ref, iwt_ref, cq_ref, ck_ref, cv_ref):
    h = _rmsnorm_rows(x_ref[...], g_ref[...]).astype(BF16)
    c64, s64 = cos64_ref[...], sin64_ref[...]
    c32, s32 = cos32_ref[...], sin32_ref[...]
    pa = _dot(h, wa_ref[...])
    qa_ref[...] = (pa[:, 0:256] * c64 + pa[:, 256:512] * s64) * (QK_SCALE * LOG2_E)
    ka_ref[...] = (pa[:, 512:768] * c64 + pa[:, 768:1024] * s64).astype(BF16)
    vat_ref[0] = pa[:, 1024:1280].T.astype(BF16)
    pi = _dot(h, wi_ref[...])
    iq_ref[...] = pi[:, 0:128] * c32 + pi[:, 128:256] * s32
    ik = pi[:, 256:384] * c32 + pi[:, 384:512] * s32
    ik_ref[...] = ik[:, 0:IDX_DIM].astype(BF16)
    iwt_ref[...] = (pi[:, 512:640] * IDX_W_SCALE).T[0:8, :]
    pc = _dot(h, wc_ref[...])
    cq_ref[...] = (pc[:, 0:256] * QK_SCALE).astype(BF16)
    ck_ref[...] = pc[:, 256:512].astype(BF16)
    cv_ref[...] = pc[:, 512:768].astype(BF16)


def _attn_proj(x, g_pre, w, tables):
    s = x.shape[0]
    nb = s // ROW_BLOCK
    cos64, sin64, cos32, sin32 = tables
    return pl.pallas_call(
        _attn_proj_kernel,
        grid=(nb,),
        in_specs=[_rows(D_MODEL), _full((1, D_MODEL)), _full((D_MODEL, 1280)), _full((D_MODEL, 640)),
                  _full((D_MODEL, 768)), _rows(256), _rows(256), _rows(128), _rows(128)],
        out_specs=[_rows(256), _rows(256), pl.BlockSpec((1, 256, ROW_BLOCK), lambda i: (i, 0, 0)),
                   _rows(128), _rows(IDX_DIM), pl.BlockSpec((8, ROW_BLOCK), lambda i: (0, i)),
                   _rows(256), _rows(256), _rows(256)],
        out_shape=[jax.ShapeDtypeStruct((s, 256), F32), jax.ShapeDtypeStruct((s, 256), BF16),
                   jax.ShapeDtypeStruct((nb, 256, ROW_BLOCK), BF16),
                   jax.ShapeDtypeStruct((s, 128), F32), jax.ShapeDtypeStruct((s, IDX_DIM), BF16),
                   jax.ShapeDtypeStruct((8, s), F32),
                   jax.ShapeDtypeStruct((s, 256), BF16), jax.ShapeDtypeStruct((s, 256), BF16),
                   jax.ShapeDtypeStruct((s, 256), BF16)],
        compiler_params=_params(),
        name="attn_proj",
    )(x, g_pre, w["wa"], w["wi"], w["wc"], cos64, sin64, cos32, sin32)


def _local_mix_kernel(x_ref, g_ref, w_ref, convw_ref, convb_ref, lng_ref, lnb_ref, ws_ref, bs_ref,
                      ogb_ref, ogd_ref, ypad_ref):
    t = x_ref.shape[0]

    @pl.when(pl.program_id(0) == 0)
    def _():
        ypad_ref[0:CONV_HALO, :] = jnp.zeros((CONV_HALO, BRANCH_WIDTH), F32)

    h = _rmsnorm_rows(x_ref[...], g_ref[...]).astype(BF16)
    pr = _dot(h, w_ref[...])
    gate_b, gate_c, x_in = pr[:, 0:256], pr[:, 256:512], pr[:, 512:768]
    d_u, d_v = pr[:, 768:1024], pr[:, 1024:1280]
    silu_b, silu_d = pr[:, 1280:1536], pr[:, 1536:1792]

    y = gate_c * x_in
    ypad_ref[CONV_HALO:CONV_HALO + t, :] = y
    y1 = ypad_ref[CONV_HALO - 1:CONV_HALO - 1 + t, :]
    y2 = ypad_ref[CONV_HALO - 2:CONV_HALO - 2 + t, :]
    conv = convw_ref[2:3, :] * y + convw_ref[1:2, :] * y1 + convw_ref[0:1, :] * y2
    ypad_ref[0:CONV_HALO, :] = y[t - CONV_HALO:t, :]
    o_b = gate_b * (conv + convb_ref[...])
    ogb_ref[...] = (o_b * (silu_b * _sigmoid(silu_b))).astype(BF16)

    mu = jnp.mean(d_v, axis=-1, keepdims=True)
    dc = d_v - mu
    var = jnp.mean(dc * dc, axis=-1, keepdims=True)
    vn = dc * lax.rsqrt(var + EPS) * lng_ref[...] + lnb_ref[...]
    group = lax.broadcasted_iota(jnp.int32, (1, BRANCH_WIDTH), 1) // GROUP_DIM
    tril = (lax.broadcasted_iota(jnp.int32, (CHUNK, CHUNK), 0)
            >= lax.broadcasted_iota(jnp.int32, (CHUNK, CHUNK), 1))
    wm = [jnp.where(tril, ws_ref[g], 0.0).astype(BF16) for g in range(N_GROUPS)]
    mixed = []
    for c in range(t // CHUNK):
        vc = vn[c * CHUNK:(c + 1) * CHUNK, :]
        m = bs_ref[...]
        for g in range(N_GROUPS):
            m = m + _dot(wm[g], jnp.where(group == g, vc, 0.0).astype(BF16))
        mixed.append(m)
    o_d = d_u * jnp.concatenate(mixed, axis=0)
    ogd_ref[...] = (o_d * (silu_d * _sigmoid(silu_d))).astype(BF16)


def _local_mix(x, g_pre, w):
    s = x.shape[0]
    return pl.pallas_call(
        _local_mix_kernel,
        grid=(s // ROW_BLOCK,),
        in_specs=[_rows(D_MODEL), _full((1, D_MODEL)), _full((D_MODEL, 1792)), _full((8, 256)),
                  _full((1, 256)), _full((1, 256)), _full((1, 256)),
                  _full((N_GROUPS, CHUNK, CHUNK)), _full((CHUNK, 256))],
        out_specs=[_rows(256), _rows(256)],
        out_shape=[jax.ShapeDtypeStruct((s, 256), BF16), jax.ShapeDtypeStruct((s, 256), BF16)],
        scratch_shapes=[pltpu.VMEM((ROW_BLOCK + CONV_HALO, BRANCH_WIDTH), F32)],
        compiler_params=_params(),
        name="local_mix",
    )(x, g_pre, w["w2"], w["conv_w"], w["conv_b"], w["ln_g"], w["ln_b"], w["w_spatial"], w["b_spatial"])


def _key_to_float(key):
    bits = jnp.where(key >= 0, key, key ^ jnp.int32(0x7FFFFFFF))
    return lax.bitcast_convert_type(bits, F32)


def _count_pass(sc_ref, n_steps, cand, strict):
    q = sc_ref.shape[1]
    cand8 = jnp.broadcast_to(cand, (8, q))
    n_acc = 4

    def body(c, accs):
        accs = list(accs)
        base = pl.multiple_of(c * COUNT_ROWS, COUNT_ROWS)
        blk = sc_ref[pl.ds(base, COUNT_ROWS), :]
        for r in range(COUNT_ROWS // 8):
            v = blk[8 * r:8 * r + 8, :]
            hit = (v > cand8) if strict else (v >= cand8)
            accs[r % n_acc] = accs[r % n_acc] + jnp.where(hit, 1, 0)
        return tuple(accs)

    zero = jnp.zeros((8, q), jnp.int32)
    accs = lax.fori_loop(0, n_steps, body, (zero,) * n_acc)
    total = accs[0] + accs[1] + accs[2] + accs[3]
    return jnp.sum(total, axis=0, keepdims=True)


def _dsa_kernel(q_ref, iq_ref, iwt_ref, k_ref, vt_ref, ik_ref, o_ref,
                sc_ref, qm_ref, iqt_ref, acc_ref, m_ref, l_ref, *, topk):
    nq = q_ref.shape[0]
    kb = ATT_BLOCK
    i = pl.program_id(0)
    n_tiles = i + 1
    t_idx = i * nq + lax.broadcasted_iota(jnp.int32, (1, nq), 1)
    row = lax.broadcasted_iota(jnp.int32, (kb, 1), 0)

    qt = q_ref[...].T
    head_of_row = lax.broadcasted_iota(jnp.int32, (BRANCH_WIDTH, 1), 0) // HEAD_DIM
    for h in range(N_HEADS):
        qm_ref[h] = jnp.where(head_of_row == h, qt, 0.0).astype(BF16)
    iqt_ref[...] = iq_ref[...].T.astype(BF16)
    w_rows = [iwt_ref[h:h + 1, :] for h in range(N_IDX_HEADS)]

    def score_tile(j, carry):
        base = pl.multiple_of(j * kb, kb)
        ikb = ik_ref[pl.ds(base, kb), :]
        sc = jnp.zeros((kb, nq), F32)
        for h in range(N_IDX_HEADS):
            logit = _dot(ikb, iqt_ref[h * IDX_DIM:(h + 1) * IDX_DIM, :])
            sc = sc + w_rows[h] * jnp.maximum(logit, 0.0)
        sc = jnp.where(base + row <= t_idx, sc, SCORE_FLOOR)
        sc_ref[pl.ds(base, kb), :] = sc
        return carry

    lax.fori_loop(0, n_tiles, score_tile, 0)

    n_steps = n_tiles * (kb // COUNT_ROWS)
    cnt0 = _count_pass(sc_ref, n_steps, jnp.zeros((1, nq), F32), strict=False)
    key0 = jnp.where(cnt0 >= topk, jnp.int32(0), jnp.int32(-2 ** 31))

    def bisect(b, key):
        cand = key | (jnp.int32(1) << (30 - b))
        cnt = _count_pass(sc_ref, n_steps, _key_to_float(cand), strict=False)
        return jnp.where(cnt >= topk, cand, key)

    key = lax.fori_loop(0, 31, bisect, key0)
    tau = _key_to_float(key)
    n_above = _count_pass(sc_ref, n_steps, tau, strict=True)
    quota = (topk - n_above).astype(F32)
    tau = jnp.where(t_idx < topk, -jnp.inf, tau)

    m_ref[...] = jnp.full(m_ref.shape, MASK_BIAS, F32)
    l_ref[...] = jnp.zeros(l_ref.shape, F32)
    acc_ref[...] = jnp.zeros(acc_ref.shape, F32)
    incl_lower = (lax.broadcasted_iota(jnp.int32, (kb, kb), 0)
                  >= lax.broadcasted_iota(jnp.int32, (kb, kb), 1)).astype(BF16)

    def attend_tile(j, ties_before):
        jc = jnp.minimum(j, n_tiles - 1)
        base = pl.multiple_of(jc * kb, kb)
        sc = sc_ref[pl.ds(base, kb), :]
        tie = sc == tau
        ties_upto = _dot(incl_lower, jnp.where(tie, 1.0, 0.0).astype(BF16)) + ties_before
        keep = (sc > tau) | (tie & (ties_upto <= quota))
        keep = keep & (j * kb + row <= t_idx)
        bias = jnp.where(keep, 0.0, MASK_BIAS)
        kt = k_ref[pl.ds(base, kb), :]
        vt = vt_ref[jc]
        logits = [_dot(kt, qm_ref[h]) + bias for h in range(N_HEADS)]
        m_old = [m_ref[h:h + 1, :] for h in range(N_HEADS)]
        m_new = [jnp.maximum(m_old[h], jnp.max(logits[h], axis=0, keepdims=True)) for h in range(N_HEADS)]
        alpha = [jnp.exp2(m_old[h] - m_new[h]) for h in range(N_HEADS)]
        probs = [jnp.exp2(logits[h] - m_new[h]) for h in range(N_HEADS)]
        for h in range(N_HEADS):
            m_ref[h:h + 1, :] = m_new[h]
            l_ref[h:h + 1, :] = alpha[h] * l_ref[h:h + 1, :] + jnp.sum(probs[h], axis=0, keepdims=True)
        for h in range(N_HEADS):
            hs = slice(h * HEAD_DIM, (h + 1) * HEAD_DIM)
            acc_ref[hs, :] = alpha[h] * acc_ref[hs, :] + _dot(vt[hs, :], probs[h].astype(BF16))
        return ties_upto[kb - 1:kb, :]

    lax.fori_loop(0, n_tiles, attend_tile, jnp.zeros((1, nq), F32))

    for h in range(N_HEADS):
        hs = slice(h * HEAD_DIM, (h + 1) * HEAD_DIM)
        acc_ref[hs, :] = acc_ref[hs, :] / l_ref[h:h + 1, :]
    o_ref[...] = acc_ref[...].T


def _dsa_attention(qa, iq, iwt, ka, vat, ik):
    s = qa.shape[0]
    nq = ATT_BLOCK
    topk = min(TOPK_MAX, s // 4)
    return pl.pallas_call(
        functools.partial(_dsa_kernel, topk=topk),
        grid=(s // nq,),
        in_specs=[_rows(256, nq), _rows(128, nq), pl.BlockSpec((8, nq), lambda i: (0, i)),
                  _full((s, 256)), _full((s // ATT_BLOCK, 256, ATT_BLOCK)), _full((s, IDX_DIM))],
        out_specs=_rows(256, nq),
        out_shape=jax.ShapeDtypeStruct((s, 256), F32),
        scratch_shapes=[pltpu.VMEM((s, nq), F32), pltpu.VMEM((N_HEADS, 256, nq), BF16),
                        pltpu.VMEM((128, nq), BF16), pltpu.VMEM((256, nq), F32),
                        pltpu.VMEM((8, nq), F32), pltpu.VMEM((8, nq), F32)],
        compiler_params=_params(),
        name="dsa_attn",
    )(qa, iq, iwt, ka, vat, ik)


def _sb_kernel(q_ref, k_ref, v_ref, o_ref, acc_ref, carry_ref):
    nq = q_ref.shape[0]
    kb = ATT_BLOCK
    i = pl.program_id(0)
    q = q_ref[...]
    lane_head = lax.broadcasted_iota(jnp.int32, (1, BRANCH_WIDTH), 1) // HEAD_DIM
    qh = [jnp.where(lane_head == h, q, jnp.zeros_like(q)) for h in range(N_HEADS)]
    later = (lax.broadcasted_iota(jnp.int32, (kb, kb), 0)
             > lax.broadcasted_iota(jnp.int32, (kb, kb), 1)).astype(BF16)
    t_idx = i * nq + lax.broadcasted_iota(jnp.int32, (nq, 1), 0)
    col = lax.broadcasted_iota(jnp.int32, (1, kb), 1)
    acc_ref[...] = jnp.zeros(acc_ref.shape, F32)
    carry_ref[...] = jnp.zeros(carry_ref.shape, F32)

    def alive(state):
        j, live = state
        return jnp.logical_and(j >= 0, live > 0)

    def walk(state):
        j, _ = state
        base = pl.multiple_of(j * kb, kb)
        kt = k_ref[pl.ds(base, kb), :]
        vt = v_ref[pl.ds(base, kb), :]
        strict = (base + col) < t_idx
        lowest = jnp.full((1, 1), jnp.inf, F32)
        for h in range(N_HEADS):
            z = _dot_nt(qh[h], kt)
            softplus = jnp.maximum(z, 0.0) + jnp.log1p(jnp.exp(-jnp.abs(z)))
            sp = jnp.where(strict, softplus, 0.0)
            hi = sp.astype(BF16)
            lo = (sp - hi.astype(F32)).astype(BF16)
            after = _dot(hi, later) + _dot(lo, later)
            c = carry_ref[h]
            wts = jnp.where(strict, jnp.exp(z - softplus - c - after), 0.0)
            vh = jnp.where(lane_head == h, vt, jnp.zeros_like(vt))
            acc_ref[...] += _dot(wts.astype(BF16), vh)
            c_new = c + jnp.sum(sp, axis=1, keepdims=True)
            carry_ref[h] = c_new
            lowest = jnp.minimum(lowest, jnp.min(c_new, axis=0, keepdims=True))
        live = (lowest[0, 0] < SB_DEAD).astype(jnp.int32)
        return j - 1, live

    lax.while_loop(alive, walk, (i, jnp.int32(1)))
    o_ref[...] = acc_ref[...]


def _sb_attention(cq, ck, cv):
    s = cq.shape[0]
    nq = ATT_BLOCK
    return pl.pallas_call(
        _sb_kernel,
        grid=(s // nq,),
        in_specs=[_rows(256, nq), _full((s, 256)), _full((s, 256))],
        out_specs=_rows(256, nq),
        out_shape=jax.ShapeDtypeStruct((s, 256), F32),
        scratch_shapes=[pltpu.VMEM((nq, 256), F32), pltpu.VMEM((N_HEADS, nq, 1), F32)],
        compiler_params=_params(),
        name="sb_attn",
    )(cq, ck, cv)


def _merge_out_kernel(x_ref, oa_ref, oc_ref, ogb_ref, ogd_ref, p_ref, g_ref, wg_ref, wm_ref, bm_ref,
                      wb_ref, wo_ref, gpost_ref, wple_ref, wpg_ref, out_ref):
    x = x_ref[...]
    h = _rmsnorm_rows(x, g_ref[...]).astype(BF16)
    gates = _dot(h, wg_ref[...])
    silu = gates * _sigmoid(gates)
    branch_in = [(oa_ref[...] * silu[:, 0:256]).astype(BF16), ogb_ref[...],
                 (oc_ref[...] * silu[:, 256:512]).astype(BF16), ogd_ref[...]]
    merged = jnp.zeros((x.shape[0], D_MODEL), F32)
    for n in range(4):
        cols = slice(n * D_MODEL, (n + 1) * D_MODEL)
        gate = _sigmoid(_dot(h, wm_ref[:, cols]) + bm_ref[:, cols])
        merged = merged + gate * _dot(branch_in[n], wb_ref[n])
    y = _dot(merged.astype(BF16), wo_ref[...])
    x1 = x + _rmsnorm_rows(y, gpost_ref[...])
    ple = _dot(p_ref[...].astype(BF16), wple_ref[...])
    out_ref[...] = x1 + ple * _sigmoid(_dot(x1.astype(BF16), wpg_ref[...]))


def _merge_out(x, o_a, o_c, og_b, og_d, p_i, g_pre, w):
    s = x.shape[0]
    return pl.pallas_call(
        _merge_out_kernel,
        grid=(s // ROW_BLOCK,),
        in_specs=[_rows(D_MODEL), _rows(256), _rows(256), _rows(256), _rows(256), _rows(PLE_DIM),
                  _full((1, D_MODEL)), _full((D_MODEL, 512)), _full((D_MODEL, 4 * D_MODEL)),
                  _full((1, 4 * D_MODEL)), _full((4, BRANCH_WIDTH, D_MODEL)), _full((D_MODEL, D_MODEL)),
                  _full((1, D_MODEL)), _full((PLE_DIM, D_MODEL)), _full((D_MODEL, D_MODEL))],
        out_specs=_rows(D_MODEL),
        out_shape=jax.ShapeDtypeStruct((s, D_MODEL), F32),
        compiler_params=_params(),
        name="merge_out",
    )(x, o_a, o_c, og_b, og_d, p_i, g_pre, w["wg"], w["wm"], w["b_merge"], w["w_branch"], w["w_out"],
      w["g_post"], w["w_ple"], w["w_ple_gate"])


def _swap_halves(w, d):
    r = w.reshape(w.shape[0], w.shape[1] // d, 2, d // 2)
    return jnp.concatenate([r[:, :, 1:2], r[:, :, 0:1]], axis=2).reshape(w.shape)


def _layer_weights(i, w_in, conv_w, conv_b, ln_g, ln_b, w_spatial, b_spatial, b_merge, w_branch, w_out,
                   g_post, w_ple, w_ple_gate):
    wi = w_in[i]
    sizes = (256, 256, 256, 128, 32, 4, 256, 256, 256, 256, 256, 256, 256, 256, 1024, 4096)
    cols, off = [], 0
    for n in sizes:
        cols.append(wi[:, off:off + n])
        off += n
    (a_q, a_k, a_v, a_iq, a_ik, a_iw, b_b, b_c, b_x, c_q, c_k, c_v, d_u, d_v, gates, merge) = cols
    pad = lambda m, n: jnp.pad(m, ((0, 0), (0, n - m.shape[1])))
    row = lambda v: v[None, :]
    return {
        "wa": jnp.concatenate([a_q, _swap_halves(a_q, HEAD_DIM), a_k, _swap_halves(a_k, HEAD_DIM), a_v],
                              1).astype(BF16),
        "wi": jnp.concatenate([a_iq, _swap_halves(a_iq, IDX_DIM), pad(a_ik, 128),
                               pad(_swap_halves(a_ik, IDX_DIM), 128), pad(a_iw, 128)], 1).astype(BF16),
        "wc": jnp.concatenate([c_q, c_k, c_v], 1).astype(BF16),
        "w2": jnp.concatenate([b_b, b_c, b_x, d_u, d_v, gates[:, 256:512], gates[:, 768:1024]], 1).astype(BF16),
        "wg": jnp.concatenate([gates[:, 0:256], gates[:, 512:768]], 1).astype(BF16),
        "wm": merge.astype(BF16),
        "conv_w": jnp.pad(conv_w[i], ((0, 8 - CONV_WIDTH), (0, 0))),
        "conv_b": row(conv_b[i]), "ln_g": row(ln_g[i]), "ln_b": row(ln_b[i]),
        "w_spatial": w_spatial[i],
        "b_spatial": jnp.repeat(b_spatial[i].T, GROUP_DIM, axis=1),
        "b_merge": row(b_merge[i]),
        "w_branch": w_branch[i].astype(BF16), "w_out": w_out[i].astype(BF16), "g_post": row(g_post[i]),
        "w_ple": w_ple[i].astype(BF16), "w_ple_gate": w_ple_gate[i].astype(BF16),
    }


def kernel(x, p, positions, g_pre, w_in, conv_w, conv_b, ln_g, ln_b, w_spatial, b_spatial, b_merge,
           w_branch, w_out, g_post, w_ple, w_ple_gate):
    batch, s, _ = x.shape
    assert batch == 1 and s % ROW_BLOCK == 0 and s % ATT_BLOCK == 0 and ATT_BLOCK >= min(TOPK_MAX, s // 4)
    depth = w_in.shape[0]
    xs = x[0]
    tables = _rope_tables(positions[0][:, None])
    for i in range(depth):
        w = _layer_weights(i, w_in, conv_w, conv_b, ln_g, ln_b, w_spatial, b_spatial, b_merge, w_branch,
                           w_out, g_post, w_ple, w_ple_gate)
        g = g_pre[i][None, :]
        qa, ka, vat, iq, ik, iwt, cq, ck, cv = _attn_proj(xs, g, w, tables)
        og_b, og_d = _local_mix(xs, g, w)
        o_a = _dsa_attention(qa, iq, iwt, ka, vat, ik)
        o_c = _sb_attention(cq, ck, cv)
        xs = _merge_out(xs, o_a, o_c, og_b, og_d, p[i][0], g, w)
    return xs[None]
```

```python
import functools

import jax
import jax.numpy as jnp
from jax import lax
from jax.experimental import pallas as pl
from jax.experimental.pallas import tpu as pltpu

D_MODEL = 1024
BRANCH_WIDTH = 256
HEAD_DIM = 64
N_HEADS = 4
N_IDX_HEADS = 4
IDX_DIM = 32
TOPK_MAX = 256
CONV_WIDTH = 3
CHUNK = 128
N_GROUPS = 4
GROUP_DIM = BRANCH_WIDTH // N_GROUPS
PLE_DIM = 256
ROPE_THETA = 10000.0
EPS = 1e-6
IDX_W_SCALE = (N_IDX_HEADS * IDX_DIM) ** -0.5
QK_SCALE = HEAD_DIM ** -0.5
LOG2_E = 1.4426950408889634

ROW_BLOCK = 256
ATT_BLOCK = 256
COUNT_ROWS = 256
CONV_HALO = 8
MASK_BIAS = -1e30
SCORE_FLOOR = float(jnp.finfo(jnp.float32).min)
SB_DEAD = 104.0
VMEM_LIMIT = 56 * 1024 * 1024

BF16 = jnp.bfloat16
F32 = jnp.float32
NT_DIMS = (((1,), (1,)), ((), ()))


def _dot(a, b):
    return jnp.dot(a, b, preferred_element_type=F32)


def _dot_nt(a, b):
    return lax.dot_general(a, b, NT_DIMS, preferred_element_type=F32)


def _rmsnorm_rows(x, g):
    return x * lax.rsqrt(jnp.mean(x * x, axis=-1, keepdims=True) + EPS) * g


def _sigmoid(x):
    return 1.0 / (1.0 + jnp.exp(-x))


def _params(n_grid_dims=1):
    return pltpu.CompilerParams(
        dimension_semantics=("arbitrary",) * n_grid_dims, vmem_limit_bytes=VMEM_LIMIT)


def _full(shape):
    return pl.BlockSpec(shape, lambda i: (0,) * len(shape))


def _rows(width, block=ROW_BLOCK):
    return pl.BlockSpec((block, width), lambda i: (i, 0))


def _rope_table_kernel(pos_ref, f64_ref, s64_ref, f32_ref, s32_ref,
                       cos64_ref, sin64_ref, cos32_ref, sin32_ref):
    pos = pos_ref[...].astype(F32)
    a64 = pos * f64_ref[...]
    cos64_ref[...] = jnp.cos(a64)
    sin64_ref[...] = jnp.sin(a64) * s64_ref[...]
    a32 = pos * f32_ref[...]
    cos32_ref[...] = jnp.cos(a32)
    sin32_ref[...] = jnp.sin(a32) * s32_ref[...]


def _rope_lane_tables(d, width):
    inv_freq = ROPE_THETA ** (-jnp.arange(0, d, 2, dtype=F32) / d)
    lane = jnp.arange(width)
    freq = inv_freq[lane % (d // 2)][None, :]
    sign = jnp.where((lane % d) < d // 2, -1.0, 1.0).astype(F32)[None, :]
    return freq, sign


def _rope_tables(pos_col):
    s = pos_col.shape[0]
    f64, s64 = _rope_lane_tables(HEAD_DIM, BRANCH_WIDTH)
    f32, s32 = _rope_lane_tables(IDX_DIM, N_IDX_HEADS * IDX_DIM)
    return pl.pallas_call(
        _rope_table_kernel,
        grid=(s // ROW_BLOCK,),
        in_specs=[_rows(1), _full((1, 256)), _full((1, 256)), _full((1, 128)), _full((1, 128))],
        out_specs=[_rows(256), _rows(256), _rows(128), _rows(128)],
        out_shape=[jax.ShapeDtypeStruct((s, 256), F32), jax.ShapeDtypeStruct((s, 256), F32),
                   jax.ShapeDtypeStruct((s, 128), F32), jax.ShapeDtypeStruct((s, 128), F32)],
        compiler_params=_params(),
        name="rope_tables",
    )(pos_col, f64, s64, f32, s32)


def _attn_proj_kernel(x_ref, g_ref, wa_ref, wi_ref, wc_ref,
                      cos64_ref, sin64_ref, cos32_ref, sin32_ref,
                      qa_ref, ka_ref, vat_ref, iq_ref, ik_ref, iwt_ref, cq_ref, ck_ref, cv_ref):
    h = _rmsnorm_rows(x_ref[...], g_ref[...]).astype(BF16)
    c64, s64 = cos64_ref[...], sin64_ref[...]
    c32, s32 = cos32_ref[...], sin32_ref[...]
    pa = _dot(h, wa_ref[...])
    qa_ref[...] = (pa[:, 0:256] * c64 + pa[:, 256:512] * s64) * (QK_SCALE * LOG2_E)
    ka_ref[...] = (pa[:, 512:768] * c64 + pa[:, 768:1024] * s64).astype(BF16)
    vat_ref[0] = pa[:, 1024:1280].T.astype(BF16)
    pi = _dot(h, wi_ref[...])
    iq_ref[...] = pi[:, 0:128] * c32 + pi[:, 128:256] * s32
    ik = pi[:, 256:384] * c32 + pi[:, 384:512] * s32
    ik_ref[...] = ik[:, 0:IDX_DIM].astype(BF16)
    iwt_ref[...] = (pi[:, 512:640] * IDX_W_SCALE).T[0:8, :]
    pc = _dot(h, wc_ref[...])
    cq_ref[...] = (pc[:, 0:256] * QK_SCALE).astype(BF16)
    ck_ref[...] = pc[:, 256:512].astype(BF16)
    cv_ref[...] = pc[:, 512:768].astype(BF16)


def _attn_proj(x, g_pre, w, tables):
    s = x.shape[0]
    nb = s // ROW_BLOCK
    cos64, sin64, cos32, sin32 = tables
    return pl.pallas_call(
        _attn_proj_kernel,
        grid=(nb,),
        in_specs=[_rows(D_MODEL), _full((1, D_MODEL)), _full((D_MODEL, 1280)), _full((D_MODEL, 640)),
                  _full((D_MODEL, 768)), _rows(256), _rows(256), _rows(128), _rows(128)],
        out_specs=[_rows(256), _rows(256), pl.BlockSpec((1, 256, ROW_BLOCK), lambda i: (i, 0, 0)),
                   _rows(128), _rows(IDX_DIM), pl.BlockSpec((8, ROW_BLOCK), lambda i: (0, i)),
                   _rows(256), _rows(256), _rows(256)],
        out_shape=[jax.ShapeDtypeStruct((s, 256), F32), jax.ShapeDtypeStruct((s, 256), BF16),
                   jax.ShapeDtypeStruct((nb, 256, ROW_BLOCK), BF16),
                   jax.ShapeDtypeStruct((s, 128), F32), jax.ShapeDtypeStruct((s, IDX_DIM), BF16),
                   jax.ShapeDtypeStruct((8, s), F32),
                   jax.ShapeDtypeStruct((s, 256), BF16), jax.ShapeDtypeStruct((s, 256), BF16),
                   jax.ShapeDtypeStruct((s, 256), BF16)],
        compiler_params=_params(),
        name="attn_proj",
    )(x, g_pre, w["wa"], w["wi"], w["wc"], cos64, sin64, cos32, sin32)


def _local_mix_kernel(x_ref, g_ref, w_ref, convw_ref, convb_ref, lng_ref, lnb_ref, ws_ref, bs_ref,
                      ogb_ref, ogd_ref, ypad_ref):
    t = x_ref.shape[0]

    @pl.when(pl.program_id(0) == 0)
    def _():
        ypad_ref[0:CONV_HALO, :] = jnp.zeros((CONV_HALO, BRANCH_WIDTH), F32)

    h = _rmsnorm_rows(x_ref[...], g_ref[...]).astype(BF16)
    pr = _dot(h, w_ref[...])
    gate_b, gate_c, x_in = pr[:, 0:256], pr[:, 256:512], pr[:, 512:768]
    d_u, d_v = pr[:, 768:1024], pr[:, 1024:1280]
    silu_b, silu_d = pr[:, 1280:1536], pr[:, 1536:1792]

    y = gate_c * x_in
    ypad_ref[CONV_HALO:CONV_HALO + t, :] = y
    y1 = ypad_ref[CONV_HALO - 1:CONV_HALO - 1 + t, :]
    y2 = ypad_ref[CONV_HALO - 2:CONV_HALO - 2 + t, :]
    conv = convw_ref[2:3, :] * y + convw_ref[1:2, :] * y1 + convw_ref[0:1, :] * y2
    ypad_ref[0:CONV_HALO, :] = y[t - CONV_HALO:t, :]
    o_b = gate_b * (conv + convb_ref[...])
    ogb_ref[...] = (o_b * (silu_b * _sigmoid(silu_b))).astype(BF16)

    mu = jnp.mean(d_v, axis=-1, keepdims=True)
    dc = d_v - mu
    var = jnp.mean(dc * dc, axis=-1, keepdims=True)
    vn = dc * lax.rsqrt(var + EPS) * lng_ref[...] + lnb_ref[...]
    group = lax.broadcasted_iota(jnp.int32, (1, BRANCH_WIDTH), 1) // GROUP_DIM
    tril = (lax.broadcasted_iota(jnp.int32, (CHUNK, CHUNK), 0)
            >= lax.broadcasted_iota(jnp.int32, (CHUNK, CHUNK), 1))
    wm = [jnp.where(tril, ws_ref[g], 0.0).astype(BF16) for g in range(N_GROUPS)]
    mixed = []
    for c in range(t // CHUNK):
        vc = vn[c * CHUNK:(c + 1) * CHUNK, :]
        m = bs_ref[...]
        for g in range(N_GROUPS):
            m = m + _dot(wm[g], jnp.where(group == g, vc, 0.0).astype(BF16))
        mixed.append(m)
    o_d = d_u * jnp.concatenate(mixed, axis=0)
    ogd_ref[...] = (o_d * (silu_d * _sigmoid(silu_d))).astype(BF16)


def _local_mix(x, g_pre, w):
    s = x.shape[0]
    return pl.pallas_call(
        _local_mix_kernel,
        grid=(s // ROW_BLOCK,),
        in_specs=[_rows(D_MODEL), _full((1, D_MODEL)), _full((D_MODEL, 1792)), _full((8, 256)),
                  _full((1, 256)), _full((1, 256)), _full((1, 256)),
                  _full((N_GROUPS, CHUNK, CHUNK)), _full((CHUNK, 256))],
        out_specs=[_rows(256), _rows(256)],
        out_shape=[jax.ShapeDtypeStruct((s, 256), BF16), jax.ShapeDtypeStruct((s, 256), BF16)],
        scratch_shapes=[pltpu.VMEM((ROW_BLOCK + CONV_HALO, BRANCH_WIDTH), F32)],
        compiler_params=_params(),
        name="local_mix",
    )(x, g_pre, w["w2"], w["conv_w"], w["conv_b"], w["ln_g"], w["ln_b"], w["w_spatial"], w["b_spatial"])


I16 = jnp.int16
I16_MIN = -32768
PACK_ROWS = 16


def _sortable_halves(score):
    bits = lax.bitcast_convert_type(score, jnp.int32)
    key = bits ^ ((bits >> 31) & jnp.int32(0x7FFFFFFF))
    hi = (key >> 16).astype(I16)
    lo = ((key & jnp.int32(0xFFFF)) + jnp.int32(I16_MIN)).astype(I16)
    return hi, lo


def _rows16(row32):
    return jnp.broadcast_to(row32, (PACK_ROWS, row32.shape[1])).astype(I16)


def _count_pass(ref, n_steps, cand, strict):
    q = ref.shape[1]
    cand16 = _rows16(cand)
    one, zero = jnp.ones((PACK_ROWS, q), I16), jnp.zeros((PACK_ROWS, q), I16)
    n_acc = 4

    def body(c, accs):
        accs = list(accs)
        base = pl.multiple_of(c * COUNT_ROWS, COUNT_ROWS)
        blk = ref[pl.ds(base, COUNT_ROWS), :]
        for r in range(COUNT_ROWS // PACK_ROWS):
            v = blk[PACK_ROWS * r:PACK_ROWS * (r + 1), :]
            hit = (v > cand16) if strict else (v >= cand16)
            accs[r % n_acc] = accs[r % n_acc] + jnp.where(hit, one, zero)
        return tuple(accs)

    assert ref.shape[0] // (PACK_ROWS * n_acc) < 2 ** 15
    accs = lax.fori_loop(0, n_steps, body, (zero,) * n_acc)
    total = sum(a.astype(jnp.int32) for a in accs)
    return jnp.sum(total, axis=0, keepdims=True)


def _bisect16(ref, n_steps, target):
    cnt0 = _count_pass(ref, n_steps, jnp.zeros_like(target), strict=False)
    t0 = jnp.where(cnt0 >= target, jnp.int32(0), jnp.int32(I16_MIN))

    def step(b, t):
        cand = t | (jnp.int32(1) << (14 - b))
        cnt = _count_pass(ref, n_steps, cand, strict=False)
        return jnp.where(cnt >= target, cand, t)

    return lax.fori_loop(0, 15, step, t0)


def _dsa_kernel(q_ref, iq_ref, iwt_ref, k_ref, vt_ref, ik_ref, o_ref,
                hi_ref, lo_ref, qm_ref, iqt_ref, acc_ref, m_ref, l_ref, *, topk):
    nq = q_ref.shape[0]
    kb = ATT_BLOCK
    i = pl.program_id(0)
    n_tiles = i + 1
    t_idx = i * nq + lax.broadcasted_iota(jnp.int32, (1, nq), 1)
    row = lax.broadcasted_iota(jnp.int32, (kb, 1), 0)

    qt = q_ref[...].T
    head_of_row = lax.broadcasted_iota(jnp.int32, (BRANCH_WIDTH, 1), 0) // HEAD_DIM
    for h in range(N_HEADS):
        qm_ref[h] = jnp.where(head_of_row == h, qt, 0.0).astype(BF16)
    iqt_ref[...] = iq_ref[...].T.astype(BF16)
    w_rows = [iwt_ref[h:h + 1, :] for h in range(N_IDX_HEADS)]

    def score_tile(j, carry):
        base = pl.multiple_of(j * kb, kb)
        ikb = ik_ref[pl.ds(base, kb), :]
        sc = jnp.zeros((kb, nq), F32)
        for h in range(N_IDX_HEADS):
            logit = _dot(ikb, iqt_ref[h * IDX_DIM:(h + 1) * IDX_DIM, :])
            sc = sc + w_rows[h] * jnp.maximum(logit, 0.0)
        sc = jnp.where(base + row <= t_idx, sc, SCORE_FLOOR)
        hi_ref[pl.ds(base, kb), :], lo_ref[pl.ds(base, kb), :] = _sortable_halves(sc)
        return carry

    lax.fori_loop(0, n_tiles, score_tile, 0)

    n_steps = n_tiles * (kb // COUNT_ROWS)
    want = jnp.full((1, nq), topk, jnp.int32)
    t_hi = _bisect16(hi_ref, n_steps, want)
    above_hi = _count_pass(hi_ref, n_steps, t_hi, strict=True)
    t_hi16 = _rows16(t_hi)
    floor16 = jnp.full((PACK_ROWS, nq), I16_MIN, I16)

    def keep_bucket(c, carry):
        base = pl.multiple_of(c * COUNT_ROWS, COUNT_ROWS)
        hi, lo = hi_ref[pl.ds(base, COUNT_ROWS), :], lo_ref[pl.ds(base, COUNT_ROWS), :]
        lo_ref[pl.ds(base, COUNT_ROWS), :] = jnp.concatenate(
            [jnp.where(hi[PACK_ROWS * r:PACK_ROWS * (r + 1), :] == t_hi16,
                       lo[PACK_ROWS * r:PACK_ROWS * (r + 1), :], floor16)
             for r in range(COUNT_ROWS // PACK_ROWS)], axis=0)
        return carry

    lax.fori_loop(0, n_steps, keep_bucket, 0)
    t_lo = _bisect16(lo_ref, n_steps, want - above_hi)
    n_above = above_hi + _count_pass(lo_ref, n_steps, t_lo, strict=True)
    quota = (topk - n_above).astype(F32)
    t_hi = jnp.where(t_idx < topk, jnp.int32(I16_MIN), t_hi)
    t_hi16, t_lo16 = _rows16(t_hi), _rows16(t_lo)

    m_ref[...] = jnp.full(m_ref.shape, MASK_BIAS, F32)
    l_ref[...] = jnp.zeros(l_ref.shape, F32)
    acc_ref[...] = jnp.zeros(acc_ref.shape, F32)
    incl_lower = (lax.broadcasted_iota(jnp.int32, (kb, kb), 0)
                  >= lax.broadcasted_iota(jnp.int32, (kb, kb), 1)).astype(BF16)
    one16, zero16 = jnp.ones((PACK_ROWS, nq), BF16), jnp.zeros((PACK_ROWS, nq), BF16)

    def attend_tile(j, ties_before):
        jc = jnp.minimum(j, n_tiles - 1)
        base = pl.multiple_of(jc * kb, kb)
        hi_t, lo_t = hi_ref[pl.ds(base, kb), :], lo_ref[pl.ds(base, kb), :]
        above, tie = [], []
        for r in range(kb // PACK_ROWS):
            hi16 = hi_t[PACK_ROWS * r:PACK_ROWS * (r + 1), :]
            lo16 = lo_t[PACK_ROWS * r:PACK_ROWS * (r + 1), :]
            bucket = hi16 == t_hi16
            above.append(jnp.where((hi16 > t_hi16) | (bucket & (lo16 > t_lo16)), one16, zero16))
            tie.append(jnp.where(bucket & (lo16 == t_lo16), one16, zero16))
        above, tie = jnp.concatenate(above, axis=0), jnp.concatenate(tie, axis=0)
        ties_upto = _dot(incl_lower, tie) + ties_before
        keep = above.astype(F32) + jnp.where(ties_upto <= quota, tie.astype(F32), 0.0)
        bias = jnp.where((keep > 0.0) & (j * kb + row <= t_idx), 0.0, MASK_BIAS)
        kt = k_ref[pl.ds(base, kb), :]
        vt = vt_ref[jc]
        logits = [_dot(kt, qm_ref[h]) + bias for h in range(N_HEADS)]
        m_old = [m_ref[h:h + 1, :] for h in range(N_HEADS)]
        m_new = [jnp.maximum(m_old[h], jnp.max(logits[h], axis=0, keepdims=True)) for h in range(N_HEADS)]
        alpha = [jnp.exp2(m_old[h] - m_new[h]) for h in range(N_HEADS)]
        probs = [jnp.exp2(logits[h] - m_new[h]) for h in range(N_HEADS)]
        for h in range(N_HEADS):
            m_ref[h:h + 1, :] = m_new[h]
            l_ref[h:h + 1, :] = alpha[h] * l_ref[h:h + 1, :] + jnp.sum(probs[h], axis=0, keepdims=True)
        for h in range(N_HEADS):
            hs = slice(h * HEAD_DIM, (h + 1) * HEAD_DIM)
            acc_ref[hs, :] = alpha[h] * acc_ref[hs, :] + _dot(vt[hs, :], probs[h].astype(BF16))
        return ties_upto[kb - 1:kb, :]

    lax.fori_loop(0, n_tiles, attend_tile, jnp.zeros((1, nq), F32))

    for h in range(N_HEADS):
        hs = slice(h * HEAD_DIM, (h + 1) * HEAD_DIM)
        acc_ref[hs, :] = acc_ref[hs, :] / l_ref[h:h + 1, :]
    o_ref[...] = acc_ref[...].T


def _dsa_attention(qa, iq, iwt, ka, vat, ik):
    s = qa.shape[0]
    nq = ATT_BLOCK
    topk = min(TOPK_MAX, s // 4)
    return pl.pallas_call(
        functools.partial(_dsa_kernel, topk=topk),
        grid=(s // nq,),
        in_specs=[_rows(256, nq), _rows(128, nq), pl.BlockSpec((8, nq), lambda i: (0, i)),
                  _full((s, 256)), _full((s // ATT_BLOCK, 256, ATT_BLOCK)), _full((s, IDX_DIM))],
        out_specs=_rows(256, nq),
        out_shape=jax.ShapeDtypeStruct((s, 256), F32),
        scratch_shapes=[pltpu.VMEM((s, nq), I16), pltpu.VMEM((s, nq), I16), pltpu.VMEM((N_HEADS, 256, nq), BF16),
                        pltpu.VMEM((128, nq), BF16), pltpu.VMEM((256, nq), F32),
                        pltpu.VMEM((8, nq), F32), pltpu.VMEM((8, nq), F32)],
        compiler_params=_params(),
        name="dsa_attn",
    )(qa, iq, iwt, ka, vat, ik)


def _sb_kernel(q_ref, k_ref, v_ref, o_ref, acc_ref, carry_ref):
    nq = q_ref.shape[0]
    kb = ATT_BLOCK
    i = pl.program_id(0)
    q = q_ref[...]
    lane_head = lax.broadcasted_iota(jnp.int32, (1, BRANCH_WIDTH), 1) // HEAD_DIM
    qh = [jnp.where(lane_head == h, q, jnp.zeros_like(q)) for h in range(N_HEADS)]
    later = (lax.broadcasted_iota(jnp.int32, (kb, kb), 0)
             > lax.broadcasted_iota(jnp.int32, (kb, kb), 1)).astype(BF16)
    t_idx = i * nq + lax.broadcasted_iota(jnp.int32, (nq, 1), 0)
    col = lax.broadcasted_iota(jnp.int32, (1, kb), 1)
    acc_ref[...] = jnp.zeros(acc_ref.shape, F32)
    carry_ref[...] = jnp.zeros(carry_ref.shape, F32)

    def alive(state):
        j, live = state
        return jnp.logical_and(j >= 0, live > 0)

    def walk(state):
        j, _ = state
        base = pl.multiple_of(j * kb, kb)
        kt = k_ref[pl.ds(base, kb), :]
        vt = v_ref[pl.ds(base, kb), :]
        strict = (base + col) < t_idx
        lowest = jnp.full((1, 1), jnp.inf, F32)
        for h in range(N_HEADS):
            z = _dot_nt(qh[h], kt)
            softplus = jnp.maximum(z, 0.0) + jnp.log1p(jnp.exp(-jnp.abs(z)))
            sp = jnp.where(strict, softplus, 0.0)
            hi = sp.astype(BF16)
            lo = (sp - hi.astype(F32)).astype(BF16)
            after = _dot(hi, later) + _dot(lo, later)
            c = carry_ref[h]
            wts = jnp.where(strict, jnp.exp(z - softplus - c - after), 0.0)
            vh = jnp.where(lane_head == h, vt, jnp.zeros_like(vt))
            acc_ref[...] += _dot(wts.astype(BF16), vh)
            c_new = c + jnp.sum(sp, axis=1, keepdims=True)
            carry_ref[h] = c_new
            lowest = jnp.minimum(lowest, jnp.min(c_new, axis=0, keepdims=True))
        live = (lowest[0, 0] < SB_DEAD).astype(jnp.int32)
        return j - 1, live

    lax.while_loop(alive, walk, (i, jnp.int32(1)))
    o_ref[...] = acc_ref[...]


def _sb_attention(cq, ck, cv):
    s = cq.shape[0]
    nq = ATT_BLOCK
    return pl.pallas_call(
        _sb_kernel,
        grid=(s // nq,),
        in_specs=[_rows(256, nq), _full((s, 256)), _full((s, 256))],
        out_specs=_rows(256, nq),
        out_shape=jax.ShapeDtypeStruct((s, 256), F32),
        scratch_shapes=[pltpu.VMEM((nq, 256), F32), pltpu.VMEM((N_HEADS, nq, 1), F32)],
        compiler_params=_params(),
        name="sb_attn",
    )(cq, ck, cv)


def _merge_out_kernel(x_ref, oa_ref, oc_ref, ogb_ref, ogd_ref, p_ref, g_ref, wg_ref, wm_ref, bm_ref,
                      wb_ref, wo_ref, gpost_ref, wple_ref, wpg_ref, out_ref):
    x = x_ref[...]
    h = _rmsnorm_rows(x, g_ref[...]).astype(BF16)
    gates = _dot(h, wg_ref[...])
    silu = gates * _sigmoid(gates)
    branch_in = [(oa_ref[...] * silu[:, 0:256]).astype(BF16), ogb_ref[...],
                 (oc_ref[...] * silu[:, 256:512]).astype(BF16), ogd_ref[...]]
    merged = jnp.zeros((x.shape[0], D_MODEL), F32)
    for n in range(4):
        cols = slice(n * D_MODEL, (n + 1) * D_MODEL)
        gate = _sigmoid(_dot(h, wm_ref[:, cols]) + bm_ref[:, cols])
        merged = merged + gate * _dot(branch_in[n], wb_ref[n])
    y = _dot(merged.astype(BF16), wo_ref[...])
    x1 = x + _rmsnorm_rows(y, gpost_ref[...])
    ple = _dot(p_ref[...].astype(BF16), wple_ref[...])
    out_ref[...] = x1 + ple * _sigmoid(_dot(x1.astype(BF16), wpg_ref[...]))


def _merge_out(x, o_a, o_c, og_b, og_d, p_i, g_pre, w):
    s = x.shape[0]
    return pl.pallas_call(
        _merge_out_kernel,
        grid=(s // ROW_BLOCK,),
        in_specs=[_rows(D_MODEL), _rows(256), _rows(256), _rows(256), _rows(256), _rows(PLE_DIM),
                  _full((1, D_MODEL)), _full((D_MODEL, 512)), _full((D_MODEL, 4 * D_MODEL)),
                  _full((1, 4 * D_MODEL)), _full((4, BRANCH_WIDTH, D_MODEL)), _full((D_MODEL, D_MODEL)),
                  _full((1, D_MODEL)), _full((PLE_DIM, D_MODEL)), _full((D_MODEL, D_MODEL))],
        out_specs=_rows(D_MODEL),
        out_shape=jax.ShapeDtypeStruct((s, D_MODEL), F32),
        compiler_params=_params(),
        name="merge_out",
    )(x, o_a, o_c, og_b, og_d, p_i, g_pre, w["wg"], w["wm"], w["b_merge"], w["w_branch"], w["w_out"],
      w["g_post"], w["w_ple"], w["w_ple_gate"])


IN_COLS = 8100
IN_HEAD = 932
PREP_ROWS = 128


def _swap_halves(x, d):
    n = x.shape[1]
    lane = lax.broadcasted_iota(jnp.int32, (1, n), 1)
    return jnp.where(lane % d < d // 2, pltpu.roll(x, n - d // 2, 1), pltpu.roll(x, d // 2, 1))


def _weight_layout_kernel(w_ref, wa_ref, wi_ref, wc_ref, w2_ref, wg_ref, wm_ref):
    a_q, a_k = w_ref[0, :, 0:256], w_ref[0, :, 256:512]
    wa_ref[0, :, 0:256] = a_q.astype(BF16)
    wa_ref[0, :, 256:512] = _swap_halves(a_q, HEAD_DIM).astype(BF16)
    wa_ref[0, :, 512:768] = a_k.astype(BF16)
    wa_ref[0, :, 768:1024] = _swap_halves(a_k, HEAD_DIM).astype(BF16)
    wa_ref[0, :, 1024:1280] = w_ref[0, :, 512:768].astype(BF16)
    a_iq = w_ref[0, :, 768:896]
    tail = w_ref[0, :, 896:1024]
    lane = lax.broadcasted_iota(jnp.int32, (1, 128), 1)
    wi_ref[0, :, 0:128] = a_iq.astype(BF16)
    wi_ref[0, :, 128:256] = _swap_halves(a_iq, IDX_DIM).astype(BF16)
    wi_ref[0, :, 256:384] = jnp.where(lane < IDX_DIM, tail, 0.0).astype(BF16)
    wi_ref[0, :, 384:512] = jnp.where(lane < IDX_DIM, _swap_halves(tail, IDX_DIM), 0.0).astype(BF16)
    wi_ref[0, :, 512:640] = jnp.where(lane < N_IDX_HEADS, pltpu.roll(tail, 128 - IDX_DIM, 1), 0.0).astype(BF16)

    def rest(lo, hi):
        return w_ref[0, :, IN_HEAD + lo:IN_HEAD + hi].astype(BF16)

    w2_ref[0, :, 0:768] = rest(0, 768)
    wc_ref[0] = rest(768, 1536)
    w2_ref[0, :, 768:1280] = rest(1536, 2048)
    wg_ref[0, :, 0:256] = rest(2048, 2304)
    w2_ref[0, :, 1280:1536] = rest(2304, 2560)
    wg_ref[0, :, 256:512] = rest(2560, 2816)
    w2_ref[0, :, 1536:1792] = rest(2816, 3072)
    wm_ref[0] = rest(3072, 7168)


def _weight_layout(w_in):
    depth = w_in.shape[0]
    assert w_in.shape[1:] == (D_MODEL, IN_COLS)
    widths = (1280, 640, 768, 1792, 512, 4096)
    return pl.pallas_call(
        _weight_layout_kernel,
        grid=(depth, D_MODEL // PREP_ROWS),
        in_specs=[pl.BlockSpec((1, PREP_ROWS, IN_COLS), lambda l, r: (l, r, 0))],
        out_specs=[pl.BlockSpec((1, PREP_ROWS, n), lambda l, r: (l, r, 0)) for n in widths],
        out_shape=[jax.ShapeDtypeStruct((depth, D_MODEL, n), BF16) for n in widths],
        compiler_params=_params(2),
        name="weight_layout",
    )(w_in)


def _layer_weights(i, w_proj, conv_w, conv_b, ln_g, ln_b, w_spatial, b_spatial, b_merge, w_branch, w_out,
                   g_post, w_ple, w_ple_gate):
    row = lambda v: v[None, :]
    wa, wi, wc, w2, wg, wm = (w[i] for w in w_proj)
    return {
        "wa": wa, "wi": wi, "wc": wc, "w2": w2, "wg": wg, "wm": wm,
        "conv_w": jnp.pad(conv_w[i], ((0, 8 - CONV_WIDTH), (0, 0))),
        "conv_b": row(conv_b[i]), "ln_g": row(ln_g[i]), "ln_b": row(ln_b[i]),
        "w_spatial": w_spatial[i],
        "b_spatial": jnp.repeat(b_spatial[i].T, GROUP_DIM, axis=1),
        "b_merge": row(b_merge[i]),
        "w_branch": w_branch[i].astype(BF16), "w_out": w_out[i].astype(BF16), "g_post": row(g_post[i]),
        "w_ple": w_ple[i].astype(BF16), "w_ple_gate": w_ple_gate[i].astype(BF16),
    }


def kernel(x, p, positions, g_pre, w_in, conv_w, conv_b, ln_g, ln_b, w_spatial, b_spatial, b_merge,
           w_branch, w_out, g_post, w_ple, w_ple_gate):
    batch, s, _ = x.shape
    assert batch == 1 and s % ROW_BLOCK == 0 and s % ATT_BLOCK == 0 and ATT_BLOCK >= min(TOPK_MAX, s // 4)
    depth = w_in.shape[0]
    xs = x[0]
    tables = _rope_tables(positions[0][:, None])
    w_proj = _weight_layout(w_in)
    for i in range(depth):
        w = _layer_weights(i, w_proj, conv_w, conv_b, ln_g, ln_b, w_spatial, b_spatial, b_merge, w_branch,
                           w_out, g_post, w_ple, w_ple_gate)
        g = g_pre[i][None, :]
        qa, ka, vat, iq, ik, iwt, cq, ck, cv = _attn_proj(xs, g, w, tables)
        og_b, og_d = _local_mix(xs, g, w)
        o_a = _dsa_attention(qa, iq, iwt, ka, vat, ik)
        o_c = _sb_attention(cq, ck, cv)
        xs = _merge_out(xs, o_a, o_c, og_b, og_d, p[i][0], g, w)
    return xs[None]
```

```python
import functools

import jax
import jax.numpy as jnp
from jax import lax
from jax.experimental import pallas as pl
from jax.experimental.pallas import tpu as pltpu

D_MODEL = 1024
BRANCH_WIDTH = 256
HEAD_DIM = 64
N_HEADS = 4
N_IDX_HEADS = 4
IDX_DIM = 32
TOPK_MAX = 256
CONV_WIDTH = 3
CHUNK = 128
N_GROUPS = 4
GROUP_DIM = BRANCH_WIDTH // N_GROUPS
PLE_DIM = 256
ROPE_THETA = 10000.0
EPS = 1e-6
IDX_W_SCALE = (N_IDX_HEADS * IDX_DIM) ** -0.5
QK_SCALE = HEAD_DIM ** -0.5
LOG2_E = 1.4426950408889634

ROW_BLOCK = 256
ATT_BLOCK = 256
DSA_KEY_TILE = 512
COUNT_ROWS = 256
CONV_HALO = 8
MASK_BIAS = -1e30
SCORE_FLOOR = float(jnp.finfo(jnp.float32).min)
SB_DEAD = 104.0
VMEM_LIMIT = 56 * 1024 * 1024

BF16 = jnp.bfloat16
F32 = jnp.float32
NT_DIMS = (((1,), (1,)), ((), ()))


def _dot(a, b):
    return jnp.dot(a, b, preferred_element_type=F32)


def _dot_nt(a, b):
    return lax.dot_general(a, b, NT_DIMS, preferred_element_type=F32)


def _rmsnorm_rows(x, g):
    return x * lax.rsqrt(jnp.mean(x * x, axis=-1, keepdims=True) + EPS) * g


def _sigmoid(x):
    return 1.0 / (1.0 + jnp.exp(-x))


def _params(n_grid_dims=1):
    return pltpu.CompilerParams(
        dimension_semantics=("arbitrary",) * n_grid_dims, vmem_limit_bytes=VMEM_LIMIT)


def _full(shape):
    return pl.BlockSpec(shape, lambda i: (0,) * len(shape))


def _rows(width, block=ROW_BLOCK):
    return pl.BlockSpec((block, width), lambda i: (i, 0))


def _rope_table_kernel(pos_ref, f64_ref, s64_ref, f32_ref, s32_ref,
                       cos64_ref, sin64_ref, cos32_ref, sin32_ref):
    pos = pos_ref[...].astype(F32)
    a64 = pos * f64_ref[...]
    cos64_ref[...] = jnp.cos(a64)
    sin64_ref[...] = jnp.sin(a64) * s64_ref[...]
    a32 = pos * f32_ref[...]
    cos32_ref[...] = jnp.cos(a32)
    sin32_ref[...] = jnp.sin(a32) * s32_ref[...]


def _rope_lane_tables(d, width):
    inv_freq = ROPE_THETA ** (-jnp.arange(0, d, 2, dtype=F32) / d)
    lane = jnp.arange(width)
    freq = inv_freq[lane % (d // 2)][None, :]
    sign = jnp.where((lane % d) < d // 2, -1.0, 1.0).astype(F32)[None, :]
    return freq, sign


def _rope_tables(pos_col):
    s = pos_col.shape[0]
    f64, s64 = _rope_lane_tables(HEAD_DIM, BRANCH_WIDTH)
    f32, s32 = _rope_lane_tables(IDX_DIM, N_IDX_HEADS * IDX_DIM)
    return pl.pallas_call(
        _rope_table_kernel,
        grid=(s // ROW_BLOCK,),
        in_specs=[_rows(1), _full((1, 256)), _full((1, 256)), _full((1, 128)), _full((1, 128))],
        out_specs=[_rows(256), _rows(256), _rows(128), _rows(128)],
        out_shape=[jax.ShapeDtypeStruct((s, 256), F32), jax.ShapeDtypeStruct((s, 256), F32),
                   jax.ShapeDtypeStruct((s, 128), F32), jax.ShapeDtypeStruct((s, 128), F32)],
        compiler_params=_params(),
        name="rope_tables",
    )(pos_col, f64, s64, f32, s32)


def _attn_proj_kernel(x_ref, g_ref, wa_ref, wi_ref, wc_ref,
                      cos64_ref, sin64_ref, cos32_ref, sin32_ref,
                      qa_ref, ka_ref, vat_ref, iq_ref, ik_ref, iwt_ref, cq_ref, ck_ref, cv_ref):
    h = _rmsnorm_rows(x_ref[...], g_ref[...]).astype(BF16)
    c64, s64 = cos64_ref[...], sin64_ref[...]
    c32, s32 = cos32_ref[...], sin32_ref[...]
    pa = _dot(h, wa_ref[...])
    qa_ref[...] = (pa[:, 0:256] * c64 + pa[:, 256:512] * s64) * (QK_SCALE * LOG2_E)
    ka_ref[...] = (pa[:, 512:768] * c64 + pa[:, 768:1024] * s64).astype(BF16)
    vat_ref[0] = pa[:, 1024:1280].T.astype(BF16)
    pi = _dot(h, wi_ref[...])
    iq_ref[...] = pi[:, 0:128] * c32 + pi[:, 128:256] * s32
    ik = pi[:, 256:384] * c32 + pi[:, 384:512] * s32
    ik_ref[...] = ik[:, 0:IDX_DIM].astype(BF16)
    iwt_ref[...] = (pi[:, 512:640] * IDX_W_SCALE).T[0:8, :]
    pc = _dot(h, wc_ref[...])
    cq_ref[...] = (pc[:, 0:256] * QK_SCALE).astype(BF16)
    ck_ref[...] = pc[:, 256:512].astype(BF16)
    cv_ref[...] = pc[:, 512:768].astype(BF16)


def _attn_proj(x, g_pre, w, tables):
    s = x.shape[0]
    nb = s // ROW_BLOCK
    cos64, sin64, cos32, sin32 = tables
    return pl.pallas_call(
        _attn_proj_kernel,
        grid=(nb,),
        in_specs=[_rows(D_MODEL), _full((1, D_MODEL)), _full((D_MODEL, 1280)), _full((D_MODEL, 640)),
                  _full((D_MODEL, 768)), _rows(256), _rows(256), _rows(128), _rows(128)],
        out_specs=[_rows(256), _rows(256), pl.BlockSpec((1, 256, ROW_BLOCK), lambda i: (i, 0, 0)),
                   _rows(128), _rows(IDX_DIM), pl.BlockSpec((8, ROW_BLOCK), lambda i: (0, i)),
                   _rows(256), _rows(256), _rows(256)],
        out_shape=[jax.ShapeDtypeStruct((s, 256), F32), jax.ShapeDtypeStruct((s, 256), BF16),
                   jax.ShapeDtypeStruct((nb, 256, ROW_BLOCK), BF16),
                   jax.ShapeDtypeStruct((s, 128), F32), jax.ShapeDtypeStruct((s, IDX_DIM), BF16),
                   jax.ShapeDtypeStruct((8, s), F32),
                   jax.ShapeDtypeStruct((s, 256), BF16), jax.ShapeDtypeStruct((s, 256), BF16),
                   jax.ShapeDtypeStruct((s, 256), BF16)],
        compiler_params=_params(),
        name="attn_proj",
    )(x, g_pre, w["wa"], w["wi"], w["wc"], cos64, sin64, cos32, sin32)


def _local_mix_kernel(x_ref, g_ref, w_ref, convw_ref, convb_ref, lng_ref, lnb_ref, ws_ref, bs_ref,
                      ogb_ref, ogd_ref, ypad_ref):
    t = x_ref.shape[0]

    @pl.when(pl.program_id(0) == 0)
    def _():
        ypad_ref[0:CONV_HALO, :] = jnp.zeros((CONV_HALO, BRANCH_WIDTH), F32)

    h = _rmsnorm_rows(x_ref[...], g_ref[...]).astype(BF16)
    pr = _dot(h, w_ref[...])
    gate_b, gate_c, x_in = pr[:, 0:256], pr[:, 256:512], pr[:, 512:768]
    d_u, d_v = pr[:, 768:1024], pr[:, 1024:1280]
    silu_b, silu_d = pr[:, 1280:1536], pr[:, 1536:1792]

    y = gate_c * x_in
    ypad_ref[CONV_HALO:CONV_HALO + t, :] = y
    y1 = ypad_ref[CONV_HALO - 1:CONV_HALO - 1 + t, :]
    y2 = ypad_ref[CONV_HALO - 2:CONV_HALO - 2 + t, :]
    conv = convw_ref[2:3, :] * y + convw_ref[1:2, :] * y1 + convw_ref[0:1, :] * y2
    ypad_ref[0:CONV_HALO, :] = y[t - CONV_HALO:t, :]
    o_b = gate_b * (conv + convb_ref[...])
    ogb_ref[...] = (o_b * (silu_b * _sigmoid(silu_b))).astype(BF16)

    mu = jnp.mean(d_v, axis=-1, keepdims=True)
    dc = d_v - mu
    var = jnp.mean(dc * dc, axis=-1, keepdims=True)
    vn = dc * lax.rsqrt(var + EPS) * lng_ref[...] + lnb_ref[...]
    group = lax.broadcasted_iota(jnp.int32, (1, BRANCH_WIDTH), 1) // GROUP_DIM
    tril = (lax.broadcasted_iota(jnp.int32, (CHUNK, CHUNK), 0)
            >= lax.broadcasted_iota(jnp.int32, (CHUNK, CHUNK), 1))
    wm = [jnp.where(tril, ws_ref[g], 0.0).astype(BF16) for g in range(N_GROUPS)]
    mixed = []
    for c in range(t // CHUNK):
        vc = vn[c * CHUNK:(c + 1) * CHUNK, :]
        m = bs_ref[...]
        for g in range(N_GROUPS):
            m = m + _dot(wm[g], jnp.where(group == g, vc, 0.0).astype(BF16))
        mixed.append(m)
    o_d = d_u * jnp.concatenate(mixed, axis=0)
    ogd_ref[...] = (o_d * (silu_d * _sigmoid(silu_d))).astype(BF16)


def _local_mix(x, g_pre, w):
    s = x.shape[0]
    return pl.pallas_call(
        _local_mix_kernel,
        grid=(s // ROW_BLOCK,),
        in_specs=[_rows(D_MODEL), _full((1, D_MODEL)), _full((D_MODEL, 1792)), _full((8, 256)),
                  _full((1, 256)), _full((1, 256)), _full((1, 256)),
                  _full((N_GROUPS, CHUNK, CHUNK)), _full((CHUNK, 256))],
        out_specs=[_rows(256), _rows(256)],
        out_shape=[jax.ShapeDtypeStruct((s, 256), BF16), jax.ShapeDtypeStruct((s, 256), BF16)],
        scratch_shapes=[pltpu.VMEM((ROW_BLOCK + CONV_HALO, BRANCH_WIDTH), F32)],
        compiler_params=_params(),
        name="local_mix",
    )(x, g_pre, w["w2"], w["conv_w"], w["conv_b"], w["ln_g"], w["ln_b"], w["w_spatial"], w["b_spatial"])


I16 = jnp.int16
I16_MIN = -32768
PACK_ROWS = 16


def _sortable_halves(score):
    bits = lax.bitcast_convert_type(score, jnp.int32)
    key = bits ^ ((bits >> 31) & jnp.int32(0x7FFFFFFF))
    hi = (key >> 16).astype(I16)
    lo = ((key & jnp.int32(0xFFFF)) + jnp.int32(I16_MIN)).astype(I16)
    return hi, lo


def _rows16(row32):
    return jnp.broadcast_to(row32, (PACK_ROWS, row32.shape[1])).astype(I16)


def _count_pass(ref, n_steps, cand, strict):
    q = ref.shape[1]
    cand16 = _rows16(cand)
    one, zero = jnp.ones((PACK_ROWS, q), I16), jnp.zeros((PACK_ROWS, q), I16)
    n_acc = 4

    def body(c, accs):
        accs = list(accs)
        base = pl.multiple_of(c * COUNT_ROWS, COUNT_ROWS)
        blk = ref[pl.ds(base, COUNT_ROWS), :]
        for r in range(COUNT_ROWS // PACK_ROWS):
            v = blk[PACK_ROWS * r:PACK_ROWS * (r + 1), :]
            hit = (v > cand16) if strict else (v >= cand16)
            accs[r % n_acc] = accs[r % n_acc] + jnp.where(hit, one, zero)
        return tuple(accs)

    assert ref.shape[0] // (PACK_ROWS * n_acc) < 2 ** 15
    accs = lax.fori_loop(0, n_steps, body, (zero,) * n_acc)
    total = sum(a.astype(jnp.int32) for a in accs)
    return jnp.sum(total, axis=0, keepdims=True)


def _bisect16(ref, n_steps, target):
    cnt0 = _count_pass(ref, n_steps, jnp.zeros_like(target), strict=False)
    t0 = jnp.where(cnt0 >= target, jnp.int32(0), jnp.int32(I16_MIN))

    def step(b, t):
        cand = t | (jnp.int32(1) << (14 - b))
        cnt = _count_pass(ref, n_steps, cand, strict=False)
        return jnp.where(cnt >= target, cand, t)

    return lax.fori_loop(0, 15, step, t0)


def _dsa_kernel(q_ref, iq_ref, iwt_ref, k_ref, vt_ref, ik_ref, o_ref,
                hi_ref, lo_ref, qm_ref, iqt_ref, acc_ref, m_ref, l_ref, *, topk):
    nq = q_ref.shape[0]
    kb = DSA_KEY_TILE
    i = pl.program_id(0)
    n_tiles = (i * nq + nq + kb - 1) // kb
    t_idx = i * nq + lax.broadcasted_iota(jnp.int32, (1, nq), 1)
    row = lax.broadcasted_iota(jnp.int32, (kb, 1), 0)

    qt = q_ref[...].T
    head_of_row = lax.broadcasted_iota(jnp.int32, (BRANCH_WIDTH, 1), 0) // HEAD_DIM
    for h in range(N_HEADS):
        qm_ref[h] = jnp.where(head_of_row == h, qt, 0.0).astype(BF16)
    iqt_ref[...] = iq_ref[...].T.astype(BF16)
    w_rows = [iwt_ref[h:h + 1, :] for h in range(N_IDX_HEADS)]

    def score_tile(j, carry):
        base = pl.multiple_of(j * kb, kb)
        ikb = ik_ref[pl.ds(base, kb), :]
        sc = jnp.zeros((kb, nq), F32)
        for h in range(N_IDX_HEADS):
            logit = _dot(ikb, iqt_ref[h * IDX_DIM:(h + 1) * IDX_DIM, :])
            sc = sc + w_rows[h] * jnp.maximum(logit, 0.0)
        sc = jnp.where(base + row <= t_idx, sc, SCORE_FLOOR)
        hi_ref[pl.ds(base, kb), :], lo_ref[pl.ds(base, kb), :] = _sortable_halves(sc)
        return carry

    lax.fori_loop(0, n_tiles, score_tile, 0)

    n_steps = n_tiles * (kb // COUNT_ROWS)
    want = jnp.full((1, nq), topk, jnp.int32)
    t_hi = _bisect16(hi_ref, n_steps, want)
    above_hi = _count_pass(hi_ref, n_steps, t_hi, strict=True)
    t_hi16 = _rows16(t_hi)
    floor16 = jnp.full((PACK_ROWS, nq), I16_MIN, I16)

    def keep_bucket(c, carry):
        base = pl.multiple_of(c * COUNT_ROWS, COUNT_ROWS)
        hi, lo = hi_ref[pl.ds(base, COUNT_ROWS), :], lo_ref[pl.ds(base, COUNT_ROWS), :]
        lo_ref[pl.ds(base, COUNT_ROWS), :] = jnp.concatenate(
            [jnp.where(hi[PACK_ROWS * r:PACK_ROWS * (r + 1), :] == t_hi16,
                       lo[PACK_ROWS * r:PACK_ROWS * (r + 1), :], floor16)
             for r in range(COUNT_ROWS // PACK_ROWS)], axis=0)
        return carry

    lax.fori_loop(0, n_steps, keep_bucket, 0)
    t_lo = _bisect16(lo_ref, n_steps, want - above_hi)
    n_above = above_hi + _count_pass(lo_ref, n_steps, t_lo, strict=True)
    quota = (topk - n_above).astype(F32)
    t_hi = jnp.where(t_idx < topk, jnp.int32(I16_MIN), t_hi)
    t_hi16, t_lo16 = _rows16(t_hi), _rows16(t_lo)

    m_ref[...] = jnp.full(m_ref.shape, MASK_BIAS, F32)
    l_ref[...] = jnp.zeros(l_ref.shape, F32)
    acc_ref[...] = jnp.zeros(acc_ref.shape, F32)
    incl_lower = (lax.broadcasted_iota(jnp.int32, (kb, kb), 0)
                  >= lax.broadcasted_iota(jnp.int32, (kb, kb), 1)).astype(BF16)
    one16, zero16 = jnp.ones((PACK_ROWS, nq), BF16), jnp.zeros((PACK_ROWS, nq), BF16)

    def attend_tile(j, ties_before, on_diagonal):
        base = pl.multiple_of(j * kb, kb)
        hi_t, lo_t = hi_ref[pl.ds(base, kb), :], lo_ref[pl.ds(base, kb), :]
        above, tie = [], []
        for r in range(kb // PACK_ROWS):
            hi16 = hi_t[PACK_ROWS * r:PACK_ROWS * (r + 1), :]
            lo16 = lo_t[PACK_ROWS * r:PACK_ROWS * (r + 1), :]
            bucket = hi16 == t_hi16
            above.append(jnp.where((hi16 > t_hi16) | (bucket & (lo16 > t_lo16)), one16, zero16))
            tie.append(jnp.where(bucket & (lo16 == t_lo16), one16, zero16))
        above, tie = jnp.concatenate(above, axis=0), jnp.concatenate(tie, axis=0)
        ties_upto = _dot(incl_lower, tie) + ties_before
        keep = above + tie * jnp.where(ties_upto <= quota, 1.0, 0.0).astype(BF16)
        if on_diagonal:
            keep = keep * jnp.where(base + row <= t_idx, 1.0, 0.0).astype(BF16)
        bias = ((keep - 1.0) * (-MASK_BIAS)).astype(F32)
        kt = k_ref[pl.ds(base, kb), :]
        n_sub = kb // ATT_BLOCK
        vts = [vt_ref[j * n_sub + u] for u in range(n_sub)]
        logits = [_dot(kt, qm_ref[h]) + bias for h in range(N_HEADS)]
        m_old = [m_ref[h:h + 1, :] for h in range(N_HEADS)]
        m_new = [jnp.maximum(m_old[h], jnp.max(logits[h], axis=0, keepdims=True)) for h in range(N_HEADS)]
        alpha = [jnp.exp2(m_old[h] - m_new[h]) for h in range(N_HEADS)]
        probs = [jnp.exp2(logits[h] - m_new[h]) for h in range(N_HEADS)]
        for h in range(N_HEADS):
            m_ref[h:h + 1, :] = m_new[h]
            l_ref[h:h + 1, :] = alpha[h] * l_ref[h:h + 1, :] + jnp.sum(probs[h], axis=0, keepdims=True)
        for h in range(N_HEADS):
            hs = slice(h * HEAD_DIM, (h + 1) * HEAD_DIM)
            pb = probs[h].astype(BF16)
            pv = sum(_dot(vts[u][hs, :], pb[u * ATT_BLOCK:(u + 1) * ATT_BLOCK, :]) for u in range(n_sub))
            acc_ref[hs, :] = alpha[h] * acc_ref[hs, :] + pv
        return ties_upto[kb - 1:kb, :]

    n_below = (i * nq) // kb
    ties = lax.fori_loop(0, n_below, functools.partial(attend_tile, on_diagonal=False), jnp.zeros((1, nq), F32))
    lax.fori_loop(n_below, n_tiles, functools.partial(attend_tile, on_diagonal=True), ties)

    for h in range(N_HEADS):
        hs = slice(h * HEAD_DIM, (h + 1) * HEAD_DIM)
        acc_ref[hs, :] = acc_ref[hs, :] / l_ref[h:h + 1, :]
    o_ref[...] = acc_ref[...].T


def _dsa_attention(qa, iq, iwt, ka, vat, ik):
    s = qa.shape[0]
    nq = ATT_BLOCK
    topk = min(TOPK_MAX, s // 4)
    assert s % DSA_KEY_TILE == 0 and DSA_KEY_TILE % ATT_BLOCK == 0
    return pl.pallas_call(
        functools.partial(_dsa_kernel, topk=topk),
        grid=(s // nq,),
        in_specs=[_rows(256, nq), _rows(128, nq), pl.BlockSpec((8, nq), lambda i: (0, i)),
                  _full((s, 256)), _full((s // ATT_BLOCK, 256, ATT_BLOCK)), _full((s, IDX_DIM))],
        out_specs=_rows(256, nq),
        out_shape=jax.ShapeDtypeStruct((s, 256), F32),
        scratch_shapes=[pltpu.VMEM((s, nq), I16), pltpu.VMEM((s, nq), I16), pltpu.VMEM((N_HEADS, 256, nq), BF16),
                        pltpu.VMEM((128, nq), BF16), pltpu.VMEM((256, nq), F32),
                        pltpu.VMEM((8, nq), F32), pltpu.VMEM((8, nq), F32)],
        compiler_params=_params(),
        name="dsa_attn",
    )(qa, iq, iwt, ka, vat, ik)


def _sb_kernel(q_ref, k_ref, v_ref, o_ref, acc_ref, carry_ref):
    nq = q_ref.shape[0]
    kb = ATT_BLOCK
    i = pl.program_id(0)
    q = q_ref[...]
    lane_head = lax.broadcasted_iota(jnp.int32, (1, BRANCH_WIDTH), 1) // HEAD_DIM
    qh = [jnp.where(lane_head == h, q, jnp.zeros_like(q)) for h in range(N_HEADS)]
    later = (lax.broadcasted_iota(jnp.int32, (kb, kb), 0)
             > lax.broadcasted_iota(jnp.int32, (kb, kb), 1)).astype(BF16)
    t_idx = i * nq + lax.broadcasted_iota(jnp.int32, (nq, 1), 0)
    col = lax.broadcasted_iota(jnp.int32, (1, kb), 1)
    acc_ref[...] = jnp.zeros(acc_ref.shape, F32)
    carry_ref[...] = jnp.zeros(carry_ref.shape, F32)

    def alive(state):
        j, live = state
        return jnp.logical_and(j >= 0, live > 0)

    def walk(state):
        j, _ = state
        base = pl.multiple_of(j * kb, kb)
        kt = k_ref[pl.ds(base, kb), :]
        vt = v_ref[pl.ds(base, kb), :]
        strict = (base + col) < t_idx
        lowest = jnp.full((1, 1), jnp.inf, F32)
        for h in range(N_HEADS):
            z = _dot_nt(qh[h], kt)
            softplus = jnp.maximum(z, 0.0) + jnp.log1p(jnp.exp(-jnp.abs(z)))
            sp = jnp.where(strict, softplus, 0.0)
            hi = sp.astype(BF16)
            lo = (sp - hi.astype(F32)).astype(BF16)
            after = _dot(hi, later) + _dot(lo, later)
            c = carry_ref[h]
            wts = jnp.where(strict, jnp.exp(z - softplus - c - after), 0.0)
            vh = jnp.where(lane_head == h, vt, jnp.zeros_like(vt))
            acc_ref[...] += _dot(wts.astype(BF16), vh)
            c_new = c + jnp.sum(sp, axis=1, keepdims=True)
            carry_ref[h] = c_new
            lowest = jnp.minimum(lowest, jnp.min(c_new, axis=0, keepdims=True))
        live = (lowest[0, 0] < SB_DEAD).astype(jnp.int32)
        return j - 1, live

    lax.while_loop(alive, walk, (i, jnp.int32(1)))
    o_ref[...] = acc_ref[...]


def _sb_attention(cq, ck, cv):
    s = cq.shape[0]
    nq = ATT_BLOCK
    return pl.pallas_call(
        _sb_kernel,
        grid=(s // nq,),
        in_specs=[_rows(256, nq), _full((s, 256)), _full((s, 256))],
        out_specs=_rows(256, nq),
        out_shape=jax.ShapeDtypeStruct((s, 256), F32),
        scratch_shapes=[pltpu.VMEM((nq, 256), F32), pltpu.VMEM((N_HEADS, nq, 1), F32)],
        compiler_params=_params(),
        name="sb_attn",
    )(cq, ck, cv)


def _merge_out_kernel(x_ref, oa_ref, oc_ref, ogb_ref, ogd_ref, p_ref, g_ref, wg_ref, wm_ref, bm_ref,
                      wb_ref, wo_ref, gpost_ref, wple_ref, wpg_ref, out_ref):
    x = x_ref[...]
    h = _rmsnorm_rows(x, g_ref[...]).astype(BF16)
    gates = _dot(h, wg_ref[...])
    silu = gates * _sigmoid(gates)
    branch_in = [(oa_ref[...] * silu[:, 0:256]).astype(BF16), ogb_ref[...],
                 (oc_ref[...] * silu[:, 256:512]).astype(BF16), ogd_ref[...]]
    merged = jnp.zeros((x.shape[0], D_MODEL), F32)
    for n in range(4):
        cols = slice(n * D_MODEL, (n + 1) * D_MODEL)
        gate = _sigmoid(_dot(h, wm_ref[:, cols]) + bm_ref[:, cols])
        merged = merged + gate * _dot(branch_in[n], wb_ref[n])
    y = _dot(merged.astype(BF16), wo_ref[...])
    x1 = x + _rmsnorm_rows(y, gpost_ref[...])
    ple = _dot(p_ref[...].astype(BF16), wple_ref[...])
    out_ref[...] = x1 + ple * _sigmoid(_dot(x1.astype(BF16), wpg_ref[...]))


def _merge_out(x, o_a, o_c, og_b, og_d, p_i, g_pre, w):
    s = x.shape[0]
    return pl.pallas_call(
        _merge_out_kernel,
        grid=(s // ROW_BLOCK,),
        in_specs=[_rows(D_MODEL), _rows(256), _rows(256), _rows(256), _rows(256), _rows(PLE_DIM),
                  _full((1, D_MODEL)), _full((D_MODEL, 512)), _full((D_MODEL, 4 * D_MODEL)),
                  _full((1, 4 * D_MODEL)), _full((4, BRANCH_WIDTH, D_MODEL)), _full((D_MODEL, D_MODEL)),
                  _full((1, D_MODEL)), _full((PLE_DIM, D_MODEL)), _full((D_MODEL, D_MODEL))],
        out_specs=_rows(D_MODEL),
        out_shape=jax.ShapeDtypeStruct((s, D_MODEL), F32),
        compiler_params=_params(),
        name="merge_out",
    )(x, o_a, o_c, og_b, og_d, p_i, g_pre, w["wg"], w["wm"], w["b_merge"], w["w_branch"], w["w_out"],
      w["g_post"], w["w_ple"], w["w_ple_gate"])


IN_COLS = 8100
IN_HEAD = 932
PREP_ROWS = 128


def _swap_halves(x, d):
    n = x.shape[1]
    lane = lax.broadcasted_iota(jnp.int32, (1, n), 1)
    return jnp.where(lane % d < d // 2, pltpu.roll(x, n - d // 2, 1), pltpu.roll(x, d // 2, 1))


def _weight_layout_kernel(w_ref, wa_ref, wi_ref, wc_ref, w2_ref, wg_ref, wm_ref):
    a_q, a_k = w_ref[0, :, 0:256], w_ref[0, :, 256:512]
    wa_ref[0, :, 0:256] = a_q.astype(BF16)
    wa_ref[0, :, 256:512] = _swap_halves(a_q, HEAD_DIM).astype(BF16)
    wa_ref[0, :, 512:768] = a_k.astype(BF16)
    wa_ref[0, :, 768:1024] = _swap_halves(a_k, HEAD_DIM).astype(BF16)
    wa_ref[0, :, 1024:1280] = w_ref[0, :, 512:768].astype(BF16)
    a_iq = w_ref[0, :, 768:896]
    tail = w_ref[0, :, 896:1024]
    lane = lax.broadcasted_iota(jnp.int32, (1, 128), 1)
    wi_ref[0, :, 0:128] = a_iq.astype(BF16)
    wi_ref[0, :, 128:256] = _swap_halves(a_iq, IDX_DIM).astype(BF16)
    wi_ref[0, :, 256:384] = jnp.where(lane < IDX_DIM, tail, 0.0).astype(BF16)
    wi_ref[0, :, 384:512] = jnp.where(lane < IDX_DIM, _swap_halves(tail, IDX_DIM), 0.0).astype(BF16)
    wi_ref[0, :, 512:640] = jnp.where(lane < N_IDX_HEADS, pltpu.roll(tail, 128 - IDX_DIM, 1), 0.0).astype(BF16)

    def rest(lo, hi):
        return w_ref[0, :, IN_HEAD + lo:IN_HEAD + hi].astype(BF16)

    w2_ref[0, :, 0:768] = rest(0, 768)
    wc_ref[0] = rest(768, 1536)
    w2_ref[0, :, 768:1280] = rest(1536, 2048)
    wg_ref[0, :, 0:256] = rest(2048, 2304)
    w2_ref[0, :, 1280:1536] = rest(2304, 2560)
    wg_ref[0, :, 256:512] = rest(2560, 2816)
    w2_ref[0, :, 1536:1792] = rest(2816, 3072)
    wm_ref[0] = rest(3072, 7168)


def _weight_layout(w_in):
    depth = w_in.shape[0]
    assert w_in.shape[1:] == (D_MODEL, IN_COLS)
    widths = (1280, 640, 768, 1792, 512, 4096)
    return pl.pallas_call(
        _weight_layout_kernel,
        grid=(depth, D_MODEL // PREP_ROWS),
        in_specs=[pl.BlockSpec((1, PREP_ROWS, IN_COLS), lambda l, r: (l, r, 0))],
        out_specs=[pl.BlockSpec((1, PREP_ROWS, n), lambda l, r: (l, r, 0)) for n in widths],
        out_shape=[jax.ShapeDtypeStruct((depth, D_MODEL, n), BF16) for n in widths],
        compiler_params=_params(2),
        name="weight_layout",
    )(w_in)


def _layer_weights(i, w_proj, conv_w, conv_b, ln_g, ln_b, w_spatial, b_spatial, b_merge, w_branch, w_out,
                   g_post, w_ple, w_ple_gate):
    row = lambda v: v[None, :]
    wa, wi, wc, w2, wg, wm = (w[i] for w in w_proj)
    return {
        "wa": wa, "wi": wi, "wc": wc, "w2": w2, "wg": wg, "wm": wm,
        "conv_w": jnp.pad(conv_w[i], ((0, 8 - CONV_WIDTH), (0, 0))),
        "conv_b": row(conv_b[i]), "ln_g": row(ln_g[i]), "ln_b": row(ln_b[i]),
        "w_spatial": w_spatial[i],
        "b_spatial": jnp.repeat(b_spatial[i].T, GROUP_DIM, axis=1),
        "b_merge": row(b_merge[i]),
        "w_branch": w_branch[i].astype(BF16), "w_out": w_out[i].astype(BF16), "g_post": row(g_post[i]),
        "w_ple": w_ple[i].astype(BF16), "w_ple_gate": w_ple_gate[i].astype(BF16),
    }


def kernel(x, p, positions, g_pre, w_in, conv_w, conv_b, ln_g, ln_b, w_spatial, b_spatial, b_merge,
           w_branch, w_out, g_post, w_ple, w_ple_gate):
    batch, s, _ = x.shape
    assert batch == 1 and s % ROW_BLOCK == 0 and s % ATT_BLOCK == 0 and ATT_BLOCK >= min(TOPK_MAX, s // 4)
    depth = w_in.shape[0]
    xs = x[0]
    tables = _rope_tables(positions[0][:, None])
    w_proj = _weight_layout(w_in)
    for i in range(depth):
        w = _layer_weights(i, w_proj, conv_w, conv_b, ln_g, ln_b, w_spatial, b_spatial, b_merge, w_branch,
                           w_out, g_post, w_ple, w_ple_gate)
        g = g_pre[i][None, :]
        qa, ka, vat, iq, ik, iwt, cq, ck, cv = _attn_proj(xs, g, w, tables)
        og_b, og_d = _local_mix(xs, g, w)
        o_a = _dsa_attention(qa, iq, iwt, ka, vat, ik)
        o_c = _sb_attention(cq, ck, cv)
        xs = _merge_out(xs, o_a, o_c, og_b, og_d, p[i][0], g, w)
    return xs[None]
```

```python
import functools

import jax
import jax.numpy as jnp
from jax import lax
from jax.experimental import pallas as pl
from jax.experimental.pallas import tpu as pltpu

D_MODEL = 1024
BRANCH_WIDTH = 256
HEAD_DIM = 64
N_HEADS = 4
N_IDX_HEADS = 4
IDX_DIM = 32
TOPK_MAX = 256
CONV_WIDTH = 3
CHUNK = 128
N_GROUPS = 4
GROUP_DIM = BRANCH_WIDTH // N_GROUPS
PLE_DIM = 256
ROPE_THETA = 10000.0
EPS = 1e-6
IDX_W_SCALE = (N_IDX_HEADS * IDX_DIM) ** -0.5
QK_SCALE = HEAD_DIM ** -0.5
LOG2_E = 1.4426950408889634

ROW_BLOCK = 256
ATT_BLOCK = 256
DSA_KEY_TILE = 512
COUNT_ROWS = 256
CONV_HALO = 8
MASK_BIAS = -1e30
SCORE_FLOOR = float(jnp.finfo(jnp.float32).min)
SB_DEAD = 152.0
VMEM_LIMIT = 56 * 1024 * 1024

BF16 = jnp.bfloat16
F32 = jnp.float32
NT_DIMS = (((1,), (1,)), ((), ()))


def _dot(a, b):
    return jnp.dot(a, b, preferred_element_type=F32)


def _dot_nt(a, b):
    return lax.dot_general(a, b, NT_DIMS, preferred_element_type=F32)


def _rmsnorm_rows(x, g):
    return x * lax.rsqrt(jnp.mean(x * x, axis=-1, keepdims=True) + EPS) * g


def _sigmoid(x):
    return 1.0 / (1.0 + jnp.exp(-x))


def _params(n_grid_dims=1):
    return pltpu.CompilerParams(
        dimension_semantics=("arbitrary",) * n_grid_dims, vmem_limit_bytes=VMEM_LIMIT)


def _full(shape):
    return pl.BlockSpec(shape, lambda i: (0,) * len(shape))


def _rows(width, block=ROW_BLOCK):
    return pl.BlockSpec((block, width), lambda i: (i, 0))


def _tile_lanes(x, period):
    lane = lax.broadcasted_iota(jnp.int32, (1, 128), 1)
    x = jnp.where(lane < period, x, 0.0)
    while period < 128:
        x = x + pltpu.roll(x, period, 1)
        period *= 2
    return x


def _rope_table_kernel(pos_ref, freq_ref, s64_ref, s32_ref, cos64_ref, sin64_ref, cos32_ref, sin32_ref):
    angle = pos_ref[...].astype(F32) * freq_ref[...]
    cos, sin = jnp.cos(angle), jnp.sin(angle)
    n64, n32 = HEAD_DIM // 2, IDX_DIM // 2
    cos_h, sin_h = _tile_lanes(cos, n64), _tile_lanes(sin, n64)
    cos64_ref[...] = jnp.concatenate([cos_h, cos_h], axis=1)
    sin64_ref[...] = jnp.concatenate([sin_h, sin_h], axis=1) * s64_ref[...]
    cos32_ref[...] = _tile_lanes(pltpu.roll(cos, 128 - n64, 1), n32)
    sin32_ref[...] = _tile_lanes(pltpu.roll(sin, 128 - n64, 1), n32) * s32_ref[...]


def _rope_sign(d, width):
    lane = jnp.arange(width)
    return jnp.where((lane % d) < d // 2, -1.0, 1.0).astype(F32)[None, :]


def _rope_tables(pos_col):
    s = pos_col.shape[0]
    inv_freq = lambda d: ROPE_THETA ** (-jnp.arange(0, d, 2, dtype=F32) / d)
    freq = jnp.concatenate([inv_freq(HEAD_DIM), inv_freq(IDX_DIM), jnp.zeros((80,), F32)])[None, :]
    s64, s32 = _rope_sign(HEAD_DIM, BRANCH_WIDTH), _rope_sign(IDX_DIM, N_IDX_HEADS * IDX_DIM)
    return pl.pallas_call(
        _rope_table_kernel,
        grid=(s // ROW_BLOCK,),
        in_specs=[_rows(1), _full((1, 128)), _full((1, 256)), _full((1, 128))],
        out_specs=[_rows(256), _rows(256), _rows(128), _rows(128)],
        out_shape=[jax.ShapeDtypeStruct((s, 256), F32), jax.ShapeDtypeStruct((s, 256), F32),
                   jax.ShapeDtypeStruct((s, 128), F32), jax.ShapeDtypeStruct((s, 128), F32)],
        compiler_params=_params(),
        name="rope_tables",
    )(pos_col, freq, s64, s32)


def _attn_proj_kernel(x_ref, g_ref, wa_ref, wi_ref, wc_ref,
                      cos64_ref, sin64_ref, cos32_ref, sin32_ref,
                      qa_ref, ka_ref, vat_ref, iq_ref, ik_ref, iwt_ref, cq_ref, ck_ref, cv_ref):
    h = _rmsnorm_rows(x_ref[...], g_ref[...]).astype(BF16)
    c64, s64 = cos64_ref[...], sin64_ref[...]
    c32, s32 = cos32_ref[...], sin32_ref[...]
    pa = _dot(h, wa_ref[...])
    qa_ref[...] = (pa[:, 0:256] * c64 + pa[:, 256:512] * s64) * (QK_SCALE * LOG2_E)
    ka_ref[...] = (pa[:, 512:768] * c64 + pa[:, 768:1024] * s64).astype(BF16)
    vat_ref[0] = pa[:, 1024:1280].T.astype(BF16)
    pi = _dot(h, wi_ref[...])
    iq_ref[...] = pi[:, 0:128] * c32 + pi[:, 128:256] * s32
    ik = pi[:, 256:384] * c32 + pi[:, 384:512] * s32
    ik_ref[...] = ik[:, 0:IDX_DIM].astype(BF16)
    iwt_ref[...] = (pi[:, 512:640] * IDX_W_SCALE).T[0:8, :]
    pc = _dot(h, wc_ref[...])
    cq_ref[...] = (pc[:, 0:256] * (QK_SCALE * LOG2_E)).astype(BF16)
    ck_ref[...] = pc[:, 256:512].astype(BF16)
    cv_ref[...] = pc[:, 512:768].astype(BF16)


def _attn_proj(x, g_pre, w, tables):
    s = x.shape[0]
    nb = s // ROW_BLOCK
    cos64, sin64, cos32, sin32 = tables
    return pl.pallas_call(
        _attn_proj_kernel,
        grid=(nb,),
        in_specs=[_rows(D_MODEL), _full((1, D_MODEL)), _full((D_MODEL, 1280)), _full((D_MODEL, 640)),
                  _full((D_MODEL, 768)), _rows(256), _rows(256), _rows(128), _rows(128)],
        out_specs=[_rows(256), _rows(256), pl.BlockSpec((1, 256, ROW_BLOCK), lambda i: (i, 0, 0)),
                   _rows(128), _rows(IDX_DIM), pl.BlockSpec((8, ROW_BLOCK), lambda i: (0, i)),
                   _rows(256), _rows(256), _rows(256)],
        out_shape=[jax.ShapeDtypeStruct((s, 256), F32), jax.ShapeDtypeStruct((s, 256), BF16),
                   jax.ShapeDtypeStruct((nb, 256, ROW_BLOCK), BF16),
                   jax.ShapeDtypeStruct((s, 128), F32), jax.ShapeDtypeStruct((s, IDX_DIM), BF16),
                   jax.ShapeDtypeStruct((8, s), F32),
                   jax.ShapeDtypeStruct((s, 256), BF16), jax.ShapeDtypeStruct((s, 256), BF16),
                   jax.ShapeDtypeStruct((s, 256), BF16)],
        compiler_params=_params(),
        name="attn_proj",
    )(x, g_pre, w["wa"], w["wi"], w["wc"], cos64, sin64, cos32, sin32)


def _local_mix_kernel(x_ref, g_ref, w_ref, convw_ref, convb_ref, lng_ref, lnb_ref, ws_ref, bs_ref,
                      ogb_ref, ogd_ref, ypad_ref):
    t = x_ref.shape[0]

    @pl.when(pl.program_id(0) == 0)
    def _():
        ypad_ref[0:CONV_HALO, :] = jnp.zeros((CONV_HALO, BRANCH_WIDTH), F32)

    h = _rmsnorm_rows(x_ref[...], g_ref[...]).astype(BF16)
    pr = _dot(h, w_ref[...])
    gate_b, gate_c, x_in = pr[:, 0:256], pr[:, 256:512], pr[:, 512:768]
    d_u, d_v = pr[:, 768:1024], pr[:, 1024:1280]
    silu_b, silu_d = pr[:, 1280:1536], pr[:, 1536:1792]

    y = gate_c * x_in
    ypad_ref[CONV_HALO:CONV_HALO + t, :] = y
    y1 = ypad_ref[CONV_HALO - 1:CONV_HALO - 1 + t, :]
    y2 = ypad_ref[CONV_HALO - 2:CONV_HALO - 2 + t, :]
    conv = convw_ref[2:3, :] * y + convw_ref[1:2, :] * y1 + convw_ref[0:1, :] * y2
    ypad_ref[0:CONV_HALO, :] = y[t - CONV_HALO:t, :]
    o_b = gate_b * (conv + convb_ref[...])
    ogb_ref[...] = (o_b * (silu_b * _sigmoid(silu_b))).astype(BF16)

    mu = jnp.mean(d_v, axis=-1, keepdims=True)
    dc = d_v - mu
    var = jnp.mean(dc * dc, axis=-1, keepdims=True)
    vn = dc * lax.rsqrt(var + EPS) * lng_ref[...] + lnb_ref[...]
    group = lax.broadcasted_iota(jnp.int32, (1, BRANCH_WIDTH), 1) // GROUP_DIM
    tril = (lax.broadcasted_iota(jnp.int32, (CHUNK, CHUNK), 0)
            >= lax.broadcasted_iota(jnp.int32, (CHUNK, CHUNK), 1))
    wm = [jnp.where(tril, ws_ref[g], 0.0).astype(BF16) for g in range(N_GROUPS)]
    mixed = []
    for c in range(t // CHUNK):
        vc = vn[c * CHUNK:(c + 1) * CHUNK, :]
        m = bs_ref[...]
        for g in range(N_GROUPS):
            m = m + _dot(wm[g], jnp.where(group == g, vc, 0.0).astype(BF16))
        mixed.append(m)
    o_d = d_u * jnp.concatenate(mixed, axis=0)
    ogd_ref[...] = (o_d * (silu_d * _sigmoid(silu_d))).astype(BF16)


def _local_mix(x, g_pre, w):
    s = x.shape[0]
    return pl.pallas_call(
        _local_mix_kernel,
        grid=(s // ROW_BLOCK,),
        in_specs=[_rows(D_MODEL), _full((1, D_MODEL)), _full((D_MODEL, 1792)), _full((8, 256)),
                  _full((1, 256)), _full((1, 256)), _full((1, 256)),
                  _full((N_GROUPS, CHUNK, CHUNK)), _full((CHUNK, 256))],
        out_specs=[_rows(256), _rows(256)],
        out_shape=[jax.ShapeDtypeStruct((s, 256), BF16), jax.ShapeDtypeStruct((s, 256), BF16)],
        scratch_shapes=[pltpu.VMEM((ROW_BLOCK + CONV_HALO, BRANCH_WIDTH), F32)],
        compiler_params=_params(),
        name="local_mix",
    )(x, g_pre, w["w2"], w["conv_w"], w["conv_b"], w["ln_g"], w["ln_b"], w["w_spatial"], w["b_spatial"])


I16 = jnp.int16
I16_MIN = -32768
PACK_ROWS = 16
ACC_ROWS = HEAD_DIM + PACK_ROWS


def _sortable_halves(score):
    bits = lax.bitcast_convert_type(score, jnp.int32)
    key = bits ^ ((bits >> 31) & jnp.int32(0x7FFFFFFF))
    hi = (key >> 16).astype(I16)
    lo = ((key & jnp.int32(0xFFFF)) + jnp.int32(I16_MIN)).astype(I16)
    return hi, lo


def _rows16(row32):
    return jnp.broadcast_to(row32, (PACK_ROWS, row32.shape[1])).astype(I16)


def _count_pass(ref, n_steps, cand, strict):
    q = ref.shape[1]
    cand16 = _rows16(cand)
    one, zero = jnp.ones((PACK_ROWS, q), I16), jnp.zeros((PACK_ROWS, q), I16)
    n_acc = 4

    def body(c, accs):
        accs = list(accs)
        base = pl.multiple_of(c * COUNT_ROWS, COUNT_ROWS)
        blk = ref[pl.ds(base, COUNT_ROWS), :]
        for r in range(COUNT_ROWS // PACK_ROWS):
            v = blk[PACK_ROWS * r:PACK_ROWS * (r + 1), :]
            hit = (v > cand16) if strict else (v >= cand16)
            accs[r % n_acc] = accs[r % n_acc] + jnp.where(hit, one, zero)
        return tuple(accs)

    assert ref.shape[0] // (PACK_ROWS * n_acc) < 2 ** 15
    accs = lax.fori_loop(0, n_steps, body, (zero,) * n_acc)
    total = sum(a.astype(jnp.int32) for a in accs)
    return jnp.sum(total, axis=0, keepdims=True)


def _bisect16(ref, n_steps, target):
    cnt0 = _count_pass(ref, n_steps, jnp.zeros_like(target), strict=False)
    t0 = jnp.where(cnt0 >= target, jnp.int32(0), jnp.int32(I16_MIN))

    def step(b, t):
        cand = t | (jnp.int32(1) << (14 - b))
        cnt = _count_pass(ref, n_steps, cand, strict=False)
        return jnp.where(cnt >= target, cand, t)

    return lax.fori_loop(0, 15, step, t0)


def _dsa_kernel(q_ref, iq_ref, iwt_ref, k_ref, vt_ref, ik_ref, o_ref,
                hi_ref, lo_ref, qm_ref, iqt_ref, acc_ref, m_ref, *, topk):
    nq = q_ref.shape[0]
    kb = DSA_KEY_TILE
    i = pl.program_id(0)
    n_tiles = (i * nq + nq + kb - 1) // kb
    t_idx = i * nq + lax.broadcasted_iota(jnp.int32, (1, nq), 1)
    row = lax.broadcasted_iota(jnp.int32, (kb, 1), 0)

    qt = q_ref[...].T
    head_of_row = lax.broadcasted_iota(jnp.int32, (BRANCH_WIDTH, 1), 0) // HEAD_DIM
    for h in range(N_HEADS):
        qm_ref[h] = jnp.where(head_of_row == h, qt, 0.0).astype(BF16)
    iqt_ref[...] = iq_ref[...].T.astype(BF16)
    w_rows = [iwt_ref[h:h + 1, :] for h in range(N_IDX_HEADS)]

    def score_tile(j, carry, on_diagonal):
        base = pl.multiple_of(j * kb, kb)
        ikb = ik_ref[pl.ds(base, kb), :]
        sc = w_rows[0] * jnp.maximum(_dot(ikb, iqt_ref[0:IDX_DIM, :]), 0.0)
        for h in range(1, N_IDX_HEADS):
            logit = _dot(ikb, iqt_ref[h * IDX_DIM:(h + 1) * IDX_DIM, :])
            sc = sc + w_rows[h] * jnp.maximum(logit, 0.0)
        if on_diagonal:
            sc = jnp.where(base + row <= t_idx, sc, SCORE_FLOOR)
        hi_ref[pl.ds(base, kb), :], lo_ref[pl.ds(base, kb), :] = _sortable_halves(sc)
        return carry

    n_below = (i * nq) // kb
    lax.fori_loop(0, n_below, functools.partial(score_tile, on_diagonal=False), 0)
    lax.fori_loop(n_below, n_tiles, functools.partial(score_tile, on_diagonal=True), 0)

    n_steps = n_tiles * (kb // COUNT_ROWS)
    want = jnp.full((1, nq), topk, jnp.int32)
    t_hi = _bisect16(hi_ref, n_steps, want)
    above_hi = _count_pass(hi_ref, n_steps, t_hi, strict=True)
    t_hi16 = _rows16(t_hi)
    floor16 = jnp.full((PACK_ROWS, nq), I16_MIN, I16)

    def keep_bucket(c, carry):
        base = pl.multiple_of(c * COUNT_ROWS, COUNT_ROWS)
        hi, lo = hi_ref[pl.ds(base, COUNT_ROWS), :], lo_ref[pl.ds(base, COUNT_ROWS), :]
        lo_ref[pl.ds(base, COUNT_ROWS), :] = jnp.concatenate(
            [jnp.where(hi[PACK_ROWS * r:PACK_ROWS * (r + 1), :] == t_hi16,
                       lo[PACK_ROWS * r:PACK_ROWS * (r + 1), :], floor16)
             for r in range(COUNT_ROWS // PACK_ROWS)], axis=0)
        return carry

    lax.fori_loop(0, n_steps, keep_bucket, 0)
    t_lo = _bisect16(lo_ref, n_steps, want - above_hi)
    n_above = above_hi + _count_pass(lo_ref, n_steps, t_lo, strict=True)
    quota = (topk - n_above).astype(F32)
    t_hi = jnp.where(t_idx < topk, jnp.int32(I16_MIN), t_hi)
    t_hi16, t_lo16 = _rows16(t_hi), _rows16(t_lo)

    m_ref[...] = jnp.full(m_ref.shape, MASK_BIAS, F32)
    acc_ref[...] = jnp.zeros(acc_ref.shape, F32)
    incl_lower = (lax.broadcasted_iota(jnp.int32, (kb, kb), 0)
                  >= lax.broadcasted_iota(jnp.int32, (kb, kb), 1)).astype(BF16)
    one16, zero16 = jnp.ones((PACK_ROWS, nq), BF16), jnp.zeros((PACK_ROWS, nq), BF16)

    def attend_tile(j, ties_before, on_diagonal):
        base = pl.multiple_of(j * kb, kb)
        hi_t, lo_t = hi_ref[pl.ds(base, kb), :], lo_ref[pl.ds(base, kb), :]
        above, tie = [], []
        for r in range(kb // PACK_ROWS):
            hi16 = hi_t[PACK_ROWS * r:PACK_ROWS * (r + 1), :]
            lo16 = lo_t[PACK_ROWS * r:PACK_ROWS * (r + 1), :]
            bucket = hi16 == t_hi16
            above.append(jnp.where((hi16 > t_hi16) | (bucket & (lo16 > t_lo16)), one16, zero16))
            tie.append(jnp.where(bucket & (lo16 == t_lo16), one16, zero16))
        above, tie = jnp.concatenate(above, axis=0), jnp.concatenate(tie, axis=0)
        ties_upto = _dot(incl_lower, tie) + ties_before
        keep = above + tie * jnp.where(ties_upto <= quota, 1.0, 0.0).astype(BF16)
        if on_diagonal:
            keep = keep * jnp.where(base + row <= t_idx, 1.0, 0.0).astype(BF16)
        bias = ((keep - 1.0) * (-MASK_BIAS)).astype(F32)
        kt = k_ref[pl.ds(base, kb), :]
        n_sub = kb // ATT_BLOCK
        vts = [vt_ref[j * n_sub + u] for u in range(n_sub)]
        logits = [_dot(kt, qm_ref[h]) + bias for h in range(N_HEADS)]
        m_old = [m_ref[h:h + 1, :] for h in range(N_HEADS)]
        m_new = [jnp.maximum(m_old[h], jnp.max(logits[h], axis=0, keepdims=True)) for h in range(N_HEADS)]
        alpha = [jnp.exp2(m_old[h] - m_new[h]) for h in range(N_HEADS)]
        probs = [jnp.exp2(logits[h] - m_new[h]) for h in range(N_HEADS)]
        ones_rows = jnp.ones((PACK_ROWS, ATT_BLOCK), BF16)
        for h in range(N_HEADS):
            m_ref[h:h + 1, :] = m_new[h]
            hs = slice(h * HEAD_DIM, (h + 1) * HEAD_DIM)
            pb = probs[h].astype(BF16)
            pv = sum(_dot(jnp.concatenate([vts[u][hs, :], ones_rows], axis=0),
                          pb[u * ATT_BLOCK:(u + 1) * ATT_BLOCK, :]) for u in range(n_sub))
            rows = slice(h * ACC_ROWS, (h + 1) * ACC_ROWS)
            acc_ref[rows, :] = alpha[h] * acc_ref[rows, :] + pv
        return ties_upto[kb - 1:kb, :]

    n_below = (i * nq) // kb
    ties = lax.fori_loop(0, n_below, functools.partial(attend_tile, on_diagonal=False), jnp.zeros((1, nq), F32))
    lax.fori_loop(n_below, n_tiles, functools.partial(attend_tile, on_diagonal=True), ties)

    heads = [acc_ref[h * ACC_ROWS:h * ACC_ROWS + HEAD_DIM, :]
             / acc_ref[h * ACC_ROWS + HEAD_DIM:h * ACC_ROWS + HEAD_DIM + 1, :] for h in range(N_HEADS)]
    o_ref[...] = jnp.concatenate(heads, axis=0).T


def _dsa_attention(qa, iq, iwt, ka, vat, ik):
    s = qa.shape[0]
    nq = ATT_BLOCK
    topk = min(TOPK_MAX, s // 4)
    assert s % DSA_KEY_TILE == 0 and DSA_KEY_TILE % ATT_BLOCK == 0
    return pl.pallas_call(
        functools.partial(_dsa_kernel, topk=topk),
        grid=(s // nq,),
        in_specs=[_rows(256, nq), _rows(128, nq), pl.BlockSpec((8, nq), lambda i: (0, i)),
                  _full((s, 256)), _full((s // ATT_BLOCK, 256, ATT_BLOCK)), _full((s, IDX_DIM))],
        out_specs=_rows(256, nq),
        out_shape=jax.ShapeDtypeStruct((s, 256), F32),
        scratch_shapes=[pltpu.VMEM((s, nq), I16), pltpu.VMEM((s, nq), I16), pltpu.VMEM((N_HEADS, 256, nq), BF16),
                        pltpu.VMEM((128, nq), BF16), pltpu.VMEM((N_HEADS * ACC_ROWS, nq), F32),
                        pltpu.VMEM((8, nq), F32)],
        compiler_params=_params(),
        name="dsa_attn",
    )(qa, iq, iwt, ka, vat, ik)


def _sb_kernel(q_ref, k_ref, v_ref, o_ref, acc_ref, carry_ref):
    nq = q_ref.shape[0]
    kb = ATT_BLOCK
    i = pl.program_id(0)
    q = q_ref[...]
    lane_head = lax.broadcasted_iota(jnp.int32, (1, BRANCH_WIDTH), 1) // HEAD_DIM
    qh = [jnp.where(lane_head == h, q, jnp.zeros_like(q)) for h in range(N_HEADS)]
    later = (lax.broadcasted_iota(jnp.int32, (2 * kb, kb), 0) % kb
             > lax.broadcasted_iota(jnp.int32, (2 * kb, kb), 1)).astype(BF16)
    acc_ref[...] = jnp.zeros(acc_ref.shape, F32)
    carry_ref[...] = jnp.zeros(carry_ref.shape, F32)

    def walk_tile(j, on_diagonal):
        base = pl.multiple_of(j * kb, kb)
        kt = k_ref[pl.ds(base, kb), :]
        vt = v_ref[pl.ds(base, kb), :]
        if on_diagonal:
            strict = (lax.broadcasted_iota(jnp.int32, (1, kb), 1)
                      < lax.broadcasted_iota(jnp.int32, (nq, 1), 0))
        lowest = jnp.full((1, 1), jnp.inf, F32)
        for h in range(N_HEADS):
            z = _dot_nt(qh[h], kt)
            softplus = jnp.maximum(z, 0.0) + jnp.log2(1.0 + jnp.exp2(-jnp.abs(z)))
            sp = jnp.where(strict, softplus, 0.0) if on_diagonal else softplus
            hi = sp.astype(BF16)
            lo = (sp - hi.astype(F32)).astype(BF16)
            after = _dot(jnp.concatenate([hi, lo], axis=1), later)
            c = carry_ref[h]
            wts = jnp.exp2((z - softplus) - (after + c))
            if on_diagonal:
                wts = jnp.where(strict, wts, 0.0)
            vh = jnp.where(lane_head == h, vt, jnp.zeros_like(vt))
            acc_ref[...] += _dot(wts.astype(BF16), vh)
            c_new = c + (after[:, 0:1] + sp[:, 0:1])
            carry_ref[h] = c_new
            lowest = jnp.minimum(lowest, jnp.min(c_new, axis=0, keepdims=True))
        return (lowest[0, 0] < SB_DEAD).astype(jnp.int32)

    def alive(state):
        j, live = state
        return jnp.logical_and(j >= 0, live > 0)

    def walk(state):
        j, _ = state
        return j - 1, walk_tile(j, on_diagonal=False)

    lax.while_loop(alive, walk, (i - 1, walk_tile(i, on_diagonal=True)))
    o_ref[...] = acc_ref[...]


def _sb_attention(cq, ck, cv):
    s = cq.shape[0]
    nq = ATT_BLOCK
    return pl.pallas_call(
        _sb_kernel,
        grid=(s // nq,),
        in_specs=[_rows(256, nq), _full((s, 256)), _full((s, 256))],
        out_specs=_rows(256, nq),
        out_shape=jax.ShapeDtypeStruct((s, 256), F32),
        scratch_shapes=[pltpu.VMEM((nq, 256), F32), pltpu.VMEM((N_HEADS, nq, 1), F32)],
        compiler_params=_params(),
        name="sb_attn",
    )(cq, ck, cv)


def _merge_out_kernel(x_ref, oa_ref, oc_ref, ogb_ref, ogd_ref, p_ref, g_ref, wg_ref, wm_ref, bm_ref,
                      wb_ref, wo_ref, gpost_ref, wple_ref, wpg_ref, out_ref):
    x = x_ref[...]
    h = _rmsnorm_rows(x, g_ref[...]).astype(BF16)
    gates = _dot(h, wg_ref[...])
    silu = gates * _sigmoid(gates)
    branch_in = [(oa_ref[...] * silu[:, 0:256]).astype(BF16), ogb_ref[...],
                 (oc_ref[...] * silu[:, 256:512]).astype(BF16), ogd_ref[...]]
    merged = jnp.zeros((x.shape[0], D_MODEL), F32)
    for n in range(4):
        cols = slice(n * D_MODEL, (n + 1) * D_MODEL)
        gate = _sigmoid(_dot(h, wm_ref[:, cols]) + bm_ref[:, cols])
        merged = merged + gate * _dot(branch_in[n], wb_ref[n])
    y = _dot(merged.astype(BF16), wo_ref[...])
    x1 = x + _rmsnorm_rows(y, gpost_ref[...])
    ple = _dot(p_ref[...].astype(BF16), wple_ref[...])
    out_ref[...] = x1 + ple * _sigmoid(_dot(x1.astype(BF16), wpg_ref[...]))


def _merge_out(x, o_a, o_c, og_b, og_d, p_i, g_pre, w):
    s = x.shape[0]
    return pl.pallas_call(
        _merge_out_kernel,
        grid=(s // ROW_BLOCK,),
        in_specs=[_rows(D_MODEL), _rows(256), _rows(256), _rows(256), _rows(256), _rows(PLE_DIM),
                  _full((1, D_MODEL)), _full((D_MODEL, 512)), _full((D_MODEL, 4 * D_MODEL)),
                  _full((1, 4 * D_MODEL)), _full((4, BRANCH_WIDTH, D_MODEL)), _full((D_MODEL, D_MODEL)),
                  _full((1, D_MODEL)), _full((PLE_DIM, D_MODEL)), _full((D_MODEL, D_MODEL))],
        out_specs=_rows(D_MODEL),
        out_shape=jax.ShapeDtypeStruct((s, D_MODEL), F32),
        compiler_params=_params(),
        name="merge_out",
    )(x, o_a, o_c, og_b, og_d, p_i, g_pre, w["wg"], w["wm"], w["b_merge"], w["w_branch"], w["w_out"],
      w["g_post"], w["w_ple"], w["w_ple_gate"])


IN_COLS = 8100
IN_HEAD = 932
PREP_ROWS = 128


def _swap_halves(x, d):
    n = x.shape[1]
    lane = lax.broadcasted_iota(jnp.int32, (1, n), 1)
    return jnp.where(lane % d < d // 2, pltpu.roll(x, n - d // 2, 1), pltpu.roll(x, d // 2, 1))


def _weight_layout_kernel(w_ref, wa_ref, wi_ref, wc_ref, w2_ref, wg_ref, wm_ref):
    a_q, a_k = w_ref[0, :, 0:256], w_ref[0, :, 256:512]
    wa_ref[0, :, 0:256] = a_q.astype(BF16)
    wa_ref[0, :, 256:512] = _swap_halves(a_q, HEAD_DIM).astype(BF16)
    wa_ref[0, :, 512:768] = a_k.astype(BF16)
    wa_ref[0, :, 768:1024] = _swap_halves(a_k, HEAD_DIM).astype(BF16)
    wa_ref[0, :, 1024:1280] = w_ref[0, :, 512:768].astype(BF16)
    a_iq = w_ref[0, :, 768:896]
    tail = w_ref[0, :, 896:1024]
    lane = lax.broadcasted_iota(jnp.int32, (1, 128), 1)
    wi_ref[0, :, 0:128] = a_iq.astype(BF16)
    wi_ref[0, :, 128:256] = _swap_halves(a_iq, IDX_DIM).astype(BF16)
    wi_ref[0, :, 256:384] = jnp.where(lane < IDX_DIM, tail, 0.0).astype(BF16)
    wi_ref[0, :, 384:512] = jnp.where(lane < IDX_DIM, _swap_halves(tail, IDX_DIM), 0.0).astype(BF16)
    wi_ref[0, :, 512:640] = jnp.where(lane < N_IDX_HEADS, pltpu.roll(tail, 128 - IDX_DIM, 1), 0.0).astype(BF16)

    def rest(lo, hi):
        return w_ref[0, :, IN_HEAD + lo:IN_HEAD + hi].astype(BF16)

    w2_ref[0, :, 0:768] = rest(0, 768)
    wc_ref[0] = rest(768, 1536)
    w2_ref[0, :, 768:1280] = rest(1536, 2048)
    wg_ref[0, :, 0:256] = rest(2048, 2304)
    w2_ref[0, :, 1280:1536] = rest(2304, 2560)
    wg_ref[0, :, 256:512] = rest(2560, 2816)
    w2_ref[0, :, 1536:1792] = rest(2816, 3072)
    wm_ref[0] = rest(3072, 7168)


def _weight_layout(w_in):
    depth = w_in.shape[0]
    assert w_in.shape[1:] == (D_MODEL, IN_COLS)
    widths = (1280, 640, 768, 1792, 512, 4096)
    return pl.pallas_call(
        _weight_layout_kernel,
        grid=(depth, D_MODEL // PREP_ROWS),
        in_specs=[pl.BlockSpec((1, PREP_ROWS, IN_COLS), lambda l, r: (l, r, 0))],
        out_specs=[pl.BlockSpec((1, PREP_ROWS, n), lambda l, r: (l, r, 0)) for n in widths],
        out_shape=[jax.ShapeDtypeStruct((depth, D_MODEL, n), BF16) for n in widths],
        compiler_params=_params(2),
        name="weight_layout",
    )(w_in)


def _layer_weights(i, w_proj, conv_w, conv_b, ln_g, ln_b, w_spatial, b_spatial, b_merge, w_branch, w_out,
                   g_post, w_ple, w_ple_gate):
    row = lambda v: v[None, :]
    wa, wi, wc, w2, wg, wm = (w[i] for w in w_proj)
    return {
        "wa": wa, "wi": wi, "wc": wc, "w2": w2, "wg": wg, "wm": wm,
        "conv_w": jnp.pad(conv_w[i], ((0, 8 - CONV_WIDTH), (0, 0))),
        "conv_b": row(conv_b[i]), "ln_g": row(ln_g[i]), "ln_b": row(ln_b[i]),
        "w_spatial": w_spatial[i],
        "b_spatial": jnp.repeat(b_spatial[i].T, GROUP_DIM, axis=1),
        "b_merge": row(b_merge[i]),
        "w_branch": w_branch[i].astype(BF16), "w_out": w_out[i].astype(BF16), "g_post": row(g_post[i]),
        "w_ple": w_ple[i].astype(BF16), "w_ple_gate": w_ple_gate[i].astype(BF16),
    }


def kernel(x, p, positions, g_pre, w_in, conv_w, conv_b, ln_g, ln_b, w_spatial, b_spatial, b_merge,
           w_branch, w_out, g_post, w_ple, w_ple_gate):
    batch, s, _ = x.shape
    assert batch == 1 and s % ROW_BLOCK == 0 and s % ATT_BLOCK == 0 and ATT_BLOCK >= min(TOPK_MAX, s // 4)
    depth = w_in.shape[0]
    xs = x[0]
    tables = _rope_tables(positions[0][:, None])
    w_proj = _weight_layout(w_in)
    for i in range(depth):
        w = _layer_weights(i, w_proj, conv_w, conv_b, ln_g, ln_b, w_spatial, b_spatial, b_merge, w_branch,
                           w_out, g_post, w_ple, w_ple_gate)
        g = g_pre[i][None, :]
        qa, ka, vat, iq, ik, iwt, cq, ck, cv = _attn_proj(xs, g, w, tables)
        og_b, og_d = _local_mix(xs, g, w)
        o_a = _dsa_attention(qa, iq, iwt, ka, vat, ik)
        o_c = _sb_attention(cq, ck, cv)
        xs = _merge_out(xs, o_a, o_c, og_b, og_d, p[i][0], g, w)
    return xs[None]
```

```python
import functools

import jax
import jax.numpy as jnp
from jax import lax
from jax.experimental import pallas as pl
from jax.experimental.pallas import tpu as pltpu

D_MODEL = 1024
BRANCH_WIDTH = 256
HEAD_DIM = 64
N_HEADS = 4
N_IDX_HEADS = 4
IDX_DIM = 32
TOPK_MAX = 256
CONV_WIDTH = 3
CHUNK = 128
N_GROUPS = 4
GROUP_DIM = BRANCH_WIDTH // N_GROUPS
PLE_DIM = 256
ROPE_THETA = 10000.0
EPS = 1e-6
IDX_W_SCALE = (N_IDX_HEADS * IDX_DIM) ** -0.5
QK_SCALE = HEAD_DIM ** -0.5
LOG2_E = 1.4426950408889634

ROW_BLOCK = 256
ATT_BLOCK = 256
DSA_KEY_TILE = 512
COUNT_ROWS = 512
CONV_HALO = 8
MASK_BIAS = -1e30
SCORE_FLOOR = float(jnp.finfo(jnp.float32).min)
SB_DEAD = 152.0
VMEM_LIMIT = 56 * 1024 * 1024

BF16 = jnp.bfloat16
F32 = jnp.float32
NT_DIMS = (((1,), (1,)), ((), ()))


def _dot(a, b):
    return jnp.dot(a, b, preferred_element_type=F32)


def _dot_nt(a, b):
    return lax.dot_general(a, b, NT_DIMS, preferred_element_type=F32)


def _rmsnorm_rows(x, g):
    return x * lax.rsqrt(jnp.mean(x * x, axis=-1, keepdims=True) + EPS) * g


def _sigmoid(x):
    return 1.0 / (1.0 + jnp.exp(-x))


def _params(n_grid_dims=1):
    return pltpu.CompilerParams(
        dimension_semantics=("arbitrary",) * n_grid_dims, vmem_limit_bytes=VMEM_LIMIT)


def _full(shape):
    return pl.BlockSpec(shape, lambda i: (0,) * len(shape))


def _rows(width, block=ROW_BLOCK):
    return pl.BlockSpec((block, width), lambda i: (i, 0))


def _tile_lanes(x, period):
    lane = lax.broadcasted_iota(jnp.int32, (1, 128), 1)
    x = jnp.where(lane < period, x, 0.0)
    while period < 128:
        x = x + pltpu.roll(x, period, 1)
        period *= 2
    return x


def _rope_table_kernel(pos_ref, freq_ref, s64_ref, s32_ref, cos64_ref, sin64_ref, cos32_ref, sin32_ref):
    angle = pos_ref[...].astype(F32) * freq_ref[...]
    cos, sin = jnp.cos(angle), jnp.sin(angle)
    n64, n32 = HEAD_DIM // 2, IDX_DIM // 2
    cos_h, sin_h = _tile_lanes(cos, n64), _tile_lanes(sin, n64)
    cos64_ref[...] = jnp.concatenate([cos_h, cos_h], axis=1)
    sin64_ref[...] = jnp.concatenate([sin_h, sin_h], axis=1) * s64_ref[...]
    cos32_ref[...] = _tile_lanes(pltpu.roll(cos, 128 - n64, 1), n32)
    sin32_ref[...] = _tile_lanes(pltpu.roll(sin, 128 - n64, 1), n32) * s32_ref[...]


def _rope_sign(d, width):
    lane = jnp.arange(width)
    return jnp.where((lane % d) < d // 2, -1.0, 1.0).astype(F32)[None, :]


def _rope_tables(pos_col):
    s = pos_col.shape[0]
    inv_freq = lambda d: ROPE_THETA ** (-jnp.arange(0, d, 2, dtype=F32) / d)
    freq = jnp.concatenate([inv_freq(HEAD_DIM), inv_freq(IDX_DIM), jnp.zeros((80,), F32)])[None, :]
    s64, s32 = _rope_sign(HEAD_DIM, BRANCH_WIDTH), _rope_sign(IDX_DIM, N_IDX_HEADS * IDX_DIM)
    return pl.pallas_call(
        _rope_table_kernel,
        grid=(s // ROW_BLOCK,),
        in_specs=[_rows(1), _full((1, 128)), _full((1, 256)), _full((1, 128))],
        out_specs=[_rows(256), _rows(256), _rows(128), _rows(128)],
        out_shape=[jax.ShapeDtypeStruct((s, 256), F32), jax.ShapeDtypeStruct((s, 256), F32),
                   jax.ShapeDtypeStruct((s, 128), F32), jax.ShapeDtypeStruct((s, 128), F32)],
        compiler_params=_params(),
        name="rope_tables",
    )(pos_col, freq, s64, s32)


def _attn_proj_kernel(x_ref, g_ref, wa_ref, wi_ref, wc_ref,
                      cos64_ref, sin64_ref, cos32_ref, sin32_ref,
                      qa_ref, ka_ref, vat_ref, iq_ref, ik_ref, iwt_ref, cq_ref, ck_ref, cv_ref):
    h = _rmsnorm_rows(x_ref[...], g_ref[...]).astype(BF16)
    c64, s64 = cos64_ref[...], sin64_ref[...]
    c32, s32 = cos32_ref[...], sin32_ref[...]
    pa = _dot(h, wa_ref[...])
    qa_ref[...] = (pa[:, 0:256] * c64 + pa[:, 256:512] * s64) * (QK_SCALE * LOG2_E)
    ka_ref[...] = (pa[:, 512:768] * c64 + pa[:, 768:1024] * s64).astype(BF16)
    vat_ref[0] = pa[:, 1024:1280].T.astype(BF16)
    pi = _dot(h, wi_ref[...])
    iq_ref[...] = pi[:, 0:128] * c32 + pi[:, 128:256] * s32
    ik = pi[:, 256:384] * c32 + pi[:, 384:512] * s32
    ik_ref[...] = ik[:, 0:IDX_DIM].astype(BF16)
    iwt_ref[...] = (pi[:, 512:640] * IDX_W_SCALE).T[0:8, :]
    pc = _dot(h, wc_ref[...])
    cq_ref[...] = (pc[:, 0:256] * (QK_SCALE * LOG2_E)).astype(BF16)
    ck_ref[...] = pc[:, 256:512].astype(BF16)
    cv_ref[...] = pc[:, 512:768].astype(BF16)


def _attn_proj(x, g_pre, w, tables):
    s = x.shape[0]
    nb = s // ROW_BLOCK
    cos64, sin64, cos32, sin32 = tables
    return pl.pallas_call(
        _attn_proj_kernel,
        grid=(nb,),
        in_specs=[_rows(D_MODEL), _full((1, D_MODEL)), _full((D_MODEL, 1280)), _full((D_MODEL, 640)),
                  _full((D_MODEL, 768)), _rows(256), _rows(256), _rows(128), _rows(128)],
        out_specs=[_rows(256), _rows(256), pl.BlockSpec((1, 256, ROW_BLOCK), lambda i: (i, 0, 0)),
                   _rows(128), _rows(IDX_DIM), pl.BlockSpec((8, ROW_BLOCK), lambda i: (0, i)),
                   _rows(256), _rows(256), _rows(256)],
        out_shape=[jax.ShapeDtypeStruct((s, 256), F32), jax.ShapeDtypeStruct((s, 256), BF16),
                   jax.ShapeDtypeStruct((nb, 256, ROW_BLOCK), BF16),
                   jax.ShapeDtypeStruct((s, 128), F32), jax.ShapeDtypeStruct((s, IDX_DIM), BF16),
                   jax.ShapeDtypeStruct((8, s), F32),
                   jax.ShapeDtypeStruct((s, 256), BF16), jax.ShapeDtypeStruct((s, 256), BF16),
                   jax.ShapeDtypeStruct((s, 256), BF16)],
        compiler_params=_params(),
        name="attn_proj",
    )(x, g_pre, w["wa"], w["wi"], w["wc"], cos64, sin64, cos32, sin32)


def _local_mix_kernel(x_ref, g_ref, w_ref, convw_ref, convb_ref, lng_ref, lnb_ref, ws_ref, bs_ref,
                      ogb_ref, ogd_ref, ypad_ref):
    t = x_ref.shape[0]

    @pl.when(pl.program_id(0) == 0)
    def _():
        ypad_ref[0:CONV_HALO, :] = jnp.zeros((CONV_HALO, BRANCH_WIDTH), F32)

    h = _rmsnorm_rows(x_ref[...], g_ref[...]).astype(BF16)
    pr = _dot(h, w_ref[...])
    gate_b, gate_c, x_in = pr[:, 0:256], pr[:, 256:512], pr[:, 512:768]
    d_u, d_v = pr[:, 768:1024], pr[:, 1024:1280]
    silu_b, silu_d = pr[:, 1280:1536], pr[:, 1536:1792]

    y = gate_c * x_in
    ypad_ref[CONV_HALO:CONV_HALO + t, :] = y
    y1 = ypad_ref[CONV_HALO - 1:CONV_HALO - 1 + t, :]
    y2 = ypad_ref[CONV_HALO - 2:CONV_HALO - 2 + t, :]
    conv = convw_ref[2:3, :] * y + convw_ref[1:2, :] * y1 + convw_ref[0:1, :] * y2
    ypad_ref[0:CONV_HALO, :] = y[t - CONV_HALO:t, :]
    o_b = gate_b * (conv + convb_ref[...])
    ogb_ref[...] = (o_b * (silu_b * _sigmoid(silu_b))).astype(BF16)

    mu = jnp.mean(d_v, axis=-1, keepdims=True)
    dc = d_v - mu
    var = jnp.mean(dc * dc, axis=-1, keepdims=True)
    vn = dc * lax.rsqrt(var + EPS) * lng_ref[...] + lnb_ref[...]
    group = lax.broadcasted_iota(jnp.int32, (1, BRANCH_WIDTH), 1) // GROUP_DIM
    tril = (lax.broadcasted_iota(jnp.int32, (CHUNK, CHUNK), 0)
            >= lax.broadcasted_iota(jnp.int32, (CHUNK, CHUNK), 1))
    wm = [jnp.where(tril, ws_ref[g], 0.0).astype(BF16) for g in range(N_GROUPS)]
    mixed = []
    for c in range(t // CHUNK):
        vc = vn[c * CHUNK:(c + 1) * CHUNK, :]
        m = bs_ref[...]
        for g in range(N_GROUPS):
            m = m + _dot(wm[g], jnp.where(group == g, vc, 0.0).astype(BF16))
        mixed.append(m)
    o_d = d_u * jnp.concatenate(mixed, axis=0)
    ogd_ref[...] = (o_d * (silu_d * _sigmoid(silu_d))).astype(BF16)


def _local_mix(x, g_pre, w):
    s = x.shape[0]
    return pl.pallas_call(
        _local_mix_kernel,
        grid=(s // ROW_BLOCK,),
        in_specs=[_rows(D_MODEL), _full((1, D_MODEL)), _full((D_MODEL, 1792)), _full((8, 256)),
                  _full((1, 256)), _full((1, 256)), _full((1, 256)),
                  _full((N_GROUPS, CHUNK, CHUNK)), _full((CHUNK, 256))],
        out_specs=[_rows(256), _rows(256)],
        out_shape=[jax.ShapeDtypeStruct((s, 256), BF16), jax.ShapeDtypeStruct((s, 256), BF16)],
        scratch_shapes=[pltpu.VMEM((ROW_BLOCK + CONV_HALO, BRANCH_WIDTH), F32)],
        compiler_params=_params(),
        name="local_mix",
    )(x, g_pre, w["w2"], w["conv_w"], w["conv_b"], w["ln_g"], w["ln_b"], w["w_spatial"], w["b_spatial"])


I16 = jnp.int16
I16_MIN = -32768
PACK_ROWS = 16
ACC_ROWS = HEAD_DIM + PACK_ROWS


def _sortable_halves(score):
    bits = lax.bitcast_convert_type(score, jnp.int32)
    key = bits ^ ((bits >> 31) & jnp.int32(0x7FFFFFFF))
    hi = (key >> 16).astype(I16)
    lo = ((key & jnp.int32(0xFFFF)) + jnp.int32(I16_MIN)).astype(I16)
    return hi, lo


def _rows16(row32):
    return jnp.broadcast_to(row32, (PACK_ROWS, row32.shape[1])).astype(I16)


def _count_pass(ref, n_steps, cand):
    q = ref.shape[1]
    cand16 = _rows16(cand)
    one, zero = jnp.ones((PACK_ROWS, q), I16), jnp.zeros((PACK_ROWS, q), I16)
    n_acc = 4

    def body(c, accs):
        accs = list(accs)
        base = pl.multiple_of(c * COUNT_ROWS, COUNT_ROWS)
        blk = ref[pl.ds(base, COUNT_ROWS), :]
        for r in range(COUNT_ROWS // PACK_ROWS):
            v = blk[PACK_ROWS * r:PACK_ROWS * (r + 1), :]
            accs[r % n_acc] = accs[r % n_acc] + jnp.where(v >= cand16, one, zero)
        return tuple(accs)

    assert ref.shape[0] // (PACK_ROWS * n_acc) < 2 ** 15
    accs = lax.fori_loop(0, n_steps, body, (zero,) * n_acc)
    total = sum(a.astype(jnp.int32) for a in accs)
    return jnp.sum(total, axis=0, keepdims=True)


def _bisect16(ref, n_steps, target):
    zero = jnp.zeros_like(target)

    def probe(cand, t, above):
        cnt = _count_pass(ref, n_steps, cand)
        enough = cnt >= target
        return jnp.where(enough, cand, t), jnp.where(enough, above, cnt)

    def step(b, state):
        t, above = state
        return probe(t | (jnp.int32(1) << (14 - b)), t, above)

    return lax.fori_loop(0, 15, step, probe(zero, jnp.full_like(target, I16_MIN), zero))


def _dsa_kernel(q_ref, iq_ref, iwt_ref, k_ref, vt_ref, ik_ref, o_ref,
                hi_ref, lo_ref, qm_ref, iqt_ref, acc_ref, m_ref, *, topk):
    nq = q_ref.shape[0]
    kb = DSA_KEY_TILE
    i = pl.program_id(0)
    n_tiles = (i * nq + nq + kb - 1) // kb
    t_idx = i * nq + lax.broadcasted_iota(jnp.int32, (1, nq), 1)
    row = lax.broadcasted_iota(jnp.int32, (kb, 1), 0)

    qt = q_ref[...].T
    head_of_row = lax.broadcasted_iota(jnp.int32, (BRANCH_WIDTH, 1), 0) // HEAD_DIM
    for h in range(N_HEADS):
        qm_ref[h] = jnp.where(head_of_row == h, qt, 0.0).astype(BF16)
    iqt_ref[...] = iq_ref[...].T.astype(BF16)
    w_rows = [iwt_ref[h:h + 1, :] for h in range(N_IDX_HEADS)]

    def score_tile(j, carry, on_diagonal):
        base = pl.multiple_of(j * kb, kb)
        ikb = ik_ref[pl.ds(base, kb), :]
        sc = w_rows[0] * jnp.maximum(_dot(ikb, iqt_ref[0:IDX_DIM, :]), 0.0)
        for h in range(1, N_IDX_HEADS):
            logit = _dot(ikb, iqt_ref[h * IDX_DIM:(h + 1) * IDX_DIM, :])
            sc = sc + w_rows[h] * jnp.maximum(logit, 0.0)
        if on_diagonal:
            sc = jnp.where(base + row <= t_idx, sc, SCORE_FLOOR)
        hi_ref[pl.ds(base, kb), :], lo_ref[pl.ds(base, kb), :] = _sortable_halves(sc)
        return carry

    n_below = (i * nq) // kb
    lax.fori_loop(0, n_below, functools.partial(score_tile, on_diagonal=False), 0)
    lax.fori_loop(n_below, n_tiles, functools.partial(score_tile, on_diagonal=True), 0)

    n_steps = n_tiles * (kb // COUNT_ROWS)
    want = jnp.full((1, nq), topk, jnp.int32)
    t_hi, above_hi = _bisect16(hi_ref, n_steps, want)
    t_hi16 = _rows16(t_hi)
    floor16 = jnp.full((PACK_ROWS, nq), I16_MIN, I16)

    def keep_bucket(c, carry):
        base = pl.multiple_of(c * COUNT_ROWS, COUNT_ROWS)
        hi, lo = hi_ref[pl.ds(base, COUNT_ROWS), :], lo_ref[pl.ds(base, COUNT_ROWS), :]
        lo_ref[pl.ds(base, COUNT_ROWS), :] = jnp.concatenate(
            [jnp.where(hi[PACK_ROWS * r:PACK_ROWS * (r + 1), :] == t_hi16,
                       lo[PACK_ROWS * r:PACK_ROWS * (r + 1), :], floor16)
             for r in range(COUNT_ROWS // PACK_ROWS)], axis=0)
        return carry

    lax.fori_loop(0, n_steps, keep_bucket, 0)
    t_lo, above_lo = _bisect16(lo_ref, n_steps, want - above_hi)
    quota = (topk - above_hi - above_lo).astype(F32)
    t_hi = jnp.where(t_idx < topk, jnp.int32(I16_MIN), t_hi)
    t_hi16, t_lo16 = _rows16(t_hi), _rows16(t_lo)

    m_ref[...] = jnp.full(m_ref.shape, MASK_BIAS, F32)
    acc_ref[...] = jnp.zeros(acc_ref.shape, F32)
    incl_lower = (lax.broadcasted_iota(jnp.int32, (kb, kb), 0)
                  >= lax.broadcasted_iota(jnp.int32, (kb, kb), 1)).astype(BF16)
    one16, zero16 = jnp.ones((PACK_ROWS, nq), BF16), jnp.zeros((PACK_ROWS, nq), BF16)

    def attend_tile(j, ties_before, on_diagonal):
        base = pl.multiple_of(j * kb, kb)
        hi_t, lo_t = hi_ref[pl.ds(base, kb), :], lo_ref[pl.ds(base, kb), :]
        above, tie = [], []
        for r in range(kb // PACK_ROWS):
            hi16 = hi_t[PACK_ROWS * r:PACK_ROWS * (r + 1), :]
            lo16 = lo_t[PACK_ROWS * r:PACK_ROWS * (r + 1), :]
            bucket = hi16 == t_hi16
            above.append(jnp.where((hi16 > t_hi16) | (bucket & (lo16 > t_lo16)), one16, zero16))
            tie.append(jnp.where(bucket & (lo16 == t_lo16), one16, zero16))
        above, tie = jnp.concatenate(above, axis=0), jnp.concatenate(tie, axis=0)
        ties_upto = _dot(incl_lower, tie) + ties_before
        keep = above + tie * jnp.where(ties_upto <= quota, 1.0, 0.0).astype(BF16)
        if on_diagonal:
            keep = keep * jnp.where(base + row <= t_idx, 1.0, 0.0).astype(BF16)
        bias = ((keep - 1.0) * (-MASK_BIAS)).astype(F32)
        kt = k_ref[pl.ds(base, kb), :]
        n_sub = kb // ATT_BLOCK
        vts = [vt_ref[j * n_sub + u] for u in range(n_sub)]
        logits = [_dot(kt, qm_ref[h]) + bias for h in range(N_HEADS)]
        m_old = [m_ref[h:h + 1, :] for h in range(N_HEADS)]
        m_new = [jnp.maximum(m_old[h], jnp.max(logits[h], axis=0, keepdims=True)) for h in range(N_HEADS)]
        alpha = [jnp.exp2(m_old[h] - m_new[h]) for h in range(N_HEADS)]
        probs = [jnp.exp2(logits[h] - m_new[h]) for h in range(N_HEADS)]
        ones_rows = jnp.ones((PACK_ROWS, ATT_BLOCK), BF16)
        for h in range(N_HEADS):
            m_ref[h:h + 1, :] = m_new[h]
            hs = slice(h * HEAD_DIM, (h + 1) * HEAD_DIM)
            pb = probs[h].astype(BF16)
            pv = sum(_dot(jnp.concatenate([vts[u][hs, :], ones_rows], axis=0),
                          pb[u * ATT_BLOCK:(u + 1) * ATT_BLOCK, :]) for u in range(n_sub))
            rows = slice(h * ACC_ROWS, (h + 1) * ACC_ROWS)
            acc_ref[rows, :] = alpha[h] * acc_ref[rows, :] + pv
        return ties_upto[kb - 1:kb, :]

    n_below = (i * nq) // kb
    ties = lax.fori_loop(0, n_below, functools.partial(attend_tile, on_diagonal=False), jnp.zeros((1, nq), F32))
    lax.fori_loop(n_below, n_tiles, functools.partial(attend_tile, on_diagonal=True), ties)

    heads = [acc_ref[h * ACC_ROWS:h * ACC_ROWS + HEAD_DIM, :]
             / acc_ref[h * ACC_ROWS + HEAD_DIM:h * ACC_ROWS + HEAD_DIM + 1, :] for h in range(N_HEADS)]
    o_ref[...] = jnp.concatenate(heads, axis=0).T


def _dsa_attention(qa, iq, iwt, ka, vat, ik):
    s = qa.shape[0]
    nq = ATT_BLOCK
    topk = min(TOPK_MAX, s // 4)
    assert s % DSA_KEY_TILE == 0 and DSA_KEY_TILE % ATT_BLOCK == 0
    return pl.pallas_call(
        functools.partial(_dsa_kernel, topk=topk),
        grid=(s // nq,),
        in_specs=[_rows(256, nq), _rows(128, nq), pl.BlockSpec((8, nq), lambda i: (0, i)),
                  _full((s, 256)), _full((s // ATT_BLOCK, 256, ATT_BLOCK)), _full((s, IDX_DIM))],
        out_specs=_rows(256, nq),
        out_shape=jax.ShapeDtypeStruct((s, 256), F32),
        scratch_shapes=[pltpu.VMEM((s, nq), I16), pltpu.VMEM((s, nq), I16), pltpu.VMEM((N_HEADS, 256, nq), BF16),
                        pltpu.VMEM((128, nq), BF16), pltpu.VMEM((N_HEADS * ACC_ROWS, nq), F32),
                        pltpu.VMEM((8, nq), F32)],
        compiler_params=_params(),
        name="dsa_attn",
    )(qa, iq, iwt, ka, vat, ik)


def _sb_kernel(q_ref, k_ref, v_ref, o_ref, acc_ref, carry_ref):
    nq = q_ref.shape[0]
    kb = ATT_BLOCK
    i = pl.program_id(0)
    q = q_ref[...]
    lane_head = lax.broadcasted_iota(jnp.int32, (1, BRANCH_WIDTH), 1) // HEAD_DIM
    qh = [jnp.where(lane_head == h, q, jnp.zeros_like(q)) for h in range(N_HEADS)]
    later = (lax.broadcasted_iota(jnp.int32, (2 * kb, kb), 0) % kb
             > lax.broadcasted_iota(jnp.int32, (2 * kb, kb), 1)).astype(BF16)
    acc_ref[...] = jnp.zeros(acc_ref.shape, F32)
    carry_ref[...] = jnp.zeros(carry_ref.shape, F32)

    def walk_tile(j, on_diagonal):
        base = pl.multiple_of(j * kb, kb)
        kt = k_ref[pl.ds(base, kb), :]
        vt = v_ref[pl.ds(base, kb), :]
        if on_diagonal:
            strict = (lax.broadcasted_iota(jnp.int32, (1, kb), 1)
                      < lax.broadcasted_iota(jnp.int32, (nq, 1), 0))
        lowest = jnp.full((1, 1), jnp.inf, F32)
        for h in range(N_HEADS):
            z = _dot_nt(qh[h], kt)
            softplus = jnp.maximum(z, 0.0) + jnp.log2(1.0 + jnp.exp2(-jnp.abs(z)))
            sp = jnp.where(strict, softplus, 0.0) if on_diagonal else softplus
            hi = sp.astype(BF16)
            lo = (sp - hi.astype(F32)).astype(BF16)
            after = _dot(jnp.concatenate([hi, lo], axis=1), later)
            c = carry_ref[h]
            wts = jnp.exp2((z - softplus) - (after + c))
            if on_diagonal:
                wts = jnp.where(strict, wts, 0.0)
            vh = jnp.where(lane_head == h, vt, jnp.zeros_like(vt))
            acc_ref[...] += _dot(wts.astype(BF16), vh)
            c_new = c + (after[:, 0:1] + sp[:, 0:1])
            carry_ref[h] = c_new
            lowest = jnp.minimum(lowest, jnp.min(c_new, axis=0, keepdims=True))
        return (lowest[0, 0] < SB_DEAD).astype(jnp.int32)

    def alive(state):
        j, live = state
        return jnp.logical_and(j >= 0, live > 0)

    def walk(state):
        j, _ = state
        return j - 1, walk_tile(j, on_diagonal=False)

    lax.while_loop(alive, walk, (i - 1, walk_tile(i, on_diagonal=True)))
    o_ref[...] = acc_ref[...]


def _sb_attention(cq, ck, cv):
    s = cq.shape[0]
    nq = ATT_BLOCK
    return pl.pallas_call(
        _sb_kernel,
        grid=(s // nq,),
        in_specs=[_rows(256, nq), _full((s, 256)), _full((s, 256))],
        out_specs=_rows(256, nq),
        out_shape=jax.ShapeDtypeStruct((s, 256), F32),
        scratch_shapes=[pltpu.VMEM((nq, 256), F32), pltpu.VMEM((N_HEADS, nq, 1), F32)],
        compiler_params=_params(),
        name="sb_attn",
    )(cq, ck, cv)


def _merge_out_kernel(x_ref, oa_ref, oc_ref, ogb_ref, ogd_ref, p_ref, g_ref, wg_ref, wm_ref, bm_ref,
                      wb_ref, wo_ref, gpost_ref, wple_ref, wpg_ref, out_ref):
    x = x_ref[...]
    h = _rmsnorm_rows(x, g_ref[...]).astype(BF16)
    gates = _dot(h, wg_ref[...])
    silu = gates * _sigmoid(gates)
    branch_in = [(oa_ref[...] * silu[:, 0:256]).astype(BF16), ogb_ref[...],
                 (oc_ref[...] * silu[:, 256:512]).astype(BF16), ogd_ref[...]]
    merged = jnp.zeros((x.shape[0], D_MODEL), F32)
    for n in range(4):
        cols = slice(n * D_MODEL, (n + 1) * D_MODEL)
        gate = _sigmoid(_dot(h, wm_ref[:, cols]) + bm_ref[:, cols])
        merged = merged + gate * _dot(branch_in[n], wb_ref[n])
    y = _dot(merged.astype(BF16), wo_ref[...])
    x1 = x + _rmsnorm_rows(y, gpost_ref[...])
    ple = _dot(p_ref[...].astype(BF16), wple_ref[...])
    out_ref[...] = x1 + ple * _sigmoid(_dot(x1.astype(BF16), wpg_ref[...]))


def _merge_out(x, o_a, o_c, og_b, og_d, p_i, g_pre, w):
    s = x.shape[0]
    return pl.pallas_call(
        _merge_out_kernel,
        grid=(s // ROW_BLOCK,),
        in_specs=[_rows(D_MODEL), _rows(256), _rows(256), _rows(256), _rows(256), _rows(PLE_DIM),
                  _full((1, D_MODEL)), _full((D_MODEL, 512)), _full((D_MODEL, 4 * D_MODEL)),
                  _full((1, 4 * D_MODEL)), _full((4, BRANCH_WIDTH, D_MODEL)), _full((D_MODEL, D_MODEL)),
                  _full((1, D_MODEL)), _full((PLE_DIM, D_MODEL)), _full((D_MODEL, D_MODEL))],
        out_specs=_rows(D_MODEL),
        out_shape=jax.ShapeDtypeStruct((s, D_MODEL), F32),
        compiler_params=_params(),
        name="merge_out",
    )(x, o_a, o_c, og_b, og_d, p_i, g_pre, w["wg"], w["wm"], w["b_merge"], w["w_branch"], w["w_out"],
      w["g_post"], w["w_ple"], w["w_ple_gate"])


IN_COLS = 8100
IN_HEAD = 932
PREP_ROWS = 128


def _swap_halves(x, d):
    n = x.shape[1]
    lane = lax.broadcasted_iota(jnp.int32, (1, n), 1)
    return jnp.where(lane % d < d // 2, pltpu.roll(x, n - d // 2, 1), pltpu.roll(x, d // 2, 1))


def _weight_layout_kernel(w_ref, wa_ref, wi_ref, wc_ref, w2_ref, wg_ref, wm_ref):
    a_q, a_k = w_ref[0, :, 0:256], w_ref[0, :, 256:512]
    wa_ref[0, :, 0:256] = a_q.astype(BF16)
    wa_ref[0, :, 256:512] = _swap_halves(a_q, HEAD_DIM).astype(BF16)
    wa_ref[0, :, 512:768] = a_k.astype(BF16)
    wa_ref[0, :, 768:1024] = _swap_halves(a_k, HEAD_DIM).astype(BF16)
    wa_ref[0, :, 1024:1280] = w_ref[0, :, 512:768].astype(BF16)
    a_iq = w_ref[0, :, 768:896]
    tail = w_ref[0, :, 896:1024]
    lane = lax.broadcasted_iota(jnp.int32, (1, 128), 1)
    wi_ref[0, :, 0:128] = a_iq.astype(BF16)
    wi_ref[0, :, 128:256] = _swap_halves(a_iq, IDX_DIM).astype(BF16)
    wi_ref[0, :, 256:384] = jnp.where(lane < IDX_DIM, tail, 0.0).astype(BF16)
    wi_ref[0, :, 384:512] = jnp.where(lane < IDX_DIM, _swap_halves(tail, IDX_DIM), 0.0).astype(BF16)
    wi_ref[0, :, 512:640] = jnp.where(lane < N_IDX_HEADS, pltpu.roll(tail, 128 - IDX_DIM, 1), 0.0).astype(BF16)

    def rest(lo, hi):
        return w_ref[0, :, IN_HEAD + lo:IN_HEAD + hi].astype(BF16)

    w2_ref[0, :, 0:768] = rest(0, 768)
    wc_ref[0] = rest(768, 1536)
    w2_ref[0, :, 768:1280] = rest(1536, 2048)
    wg_ref[0, :, 0:256] = rest(2048, 2304)
    w2_ref[0, :, 1280:1536] = rest(2304, 2560)
    wg_ref[0, :, 256:512] = rest(2560, 2816)
    w2_ref[0, :, 1536:1792] = rest(2816, 3072)
    wm_ref[0] = rest(3072, 7168)


def _weight_layout(w_in):
    depth = w_in.shape[0]
    assert w_in.shape[1:] == (D_MODEL, IN_COLS)
    widths = (1280, 640, 768, 1792, 512, 4096)
    return pl.pallas_call(
        _weight_layout_kernel,
        grid=(depth, D_MODEL // PREP_ROWS),
        in_specs=[pl.BlockSpec((1, PREP_ROWS, IN_COLS), lambda l, r: (l, r, 0))],
        out_specs=[pl.BlockSpec((1, PREP_ROWS, n), lambda l, r: (l, r, 0)) for n in widths],
        out_shape=[jax.ShapeDtypeStruct((depth, D_MODEL, n), BF16) for n in widths],
        compiler_params=_params(2),
        name="weight_layout",
    )(w_in)


def _layer_weights(i, w_proj, conv_w, conv_b, ln_g, ln_b, w_spatial, b_spatial, b_merge, w_branch, w_out,
                   g_post, w_ple, w_ple_gate):
    row = lambda v: v[None, :]
    wa, wi, wc, w2, wg, wm = (w[i] for w in w_proj)
    return {
        "wa": wa, "wi": wi, "wc": wc, "w2": w2, "wg": wg, "wm": wm,
        "conv_w": jnp.pad(conv_w[i], ((0, 8 - CONV_WIDTH), (0, 0))),
        "conv_b": row(conv_b[i]), "ln_g": row(ln_g[i]), "ln_b": row(ln_b[i]),
        "w_spatial": w_spatial[i],
        "b_spatial": jnp.repeat(b_spatial[i].T, GROUP_DIM, axis=1),
        "b_merge": row(b_merge[i]),
        "w_branch": w_branch[i].astype(BF16), "w_out": w_out[i].astype(BF16), "g_post": row(g_post[i]),
        "w_ple": w_ple[i].astype(BF16), "w_ple_gate": w_ple_gate[i].astype(BF16),
    }


def kernel(x, p, positions, g_pre, w_in, conv_w, conv_b, ln_g, ln_b, w_spatial, b_spatial, b_merge,
           w_branch, w_out, g_post, w_ple, w_ple_gate):
    batch, s, _ = x.shape
    assert batch == 1 and s % ROW_BLOCK == 0 and s % ATT_BLOCK == 0 and ATT_BLOCK >= min(TOPK_MAX, s // 4)
    depth = w_in.shape[0]
    xs = x[0]
    tables = _rope_tables(positions[0][:, None])
    w_proj = _weight_layout(w_in)
    for i in range(depth):
        w = _layer_weights(i, w_proj, conv_w, conv_b, ln_g, ln_b, w_spatial, b_spatial, b_merge, w_branch,
                           w_out, g_post, w_ple, w_ple_gate)
        g = g_pre[i][None, :]
        qa, ka, vat, iq, ik, iwt, cq, ck, cv = _attn_proj(xs, g, w, tables)
        og_b, og_d = _local_mix(xs, g, w)
        o_a = _dsa_attention(qa, iq, iwt, ka, vat, ik)
        o_c = _sb_attention(cq, ck, cv)
        xs = _merge_out(xs, o_a, o_c, og_b, og_d, p[i][0], g, w)
    return xs[None]
```

```python
import functools

import jax
import jax.numpy as jnp
from jax import lax
from jax.experimental import pallas as pl
from jax.experimental.pallas import tpu as pltpu

D_MODEL = 1024
BRANCH_WIDTH = 256
HEAD_DIM = 64
N_HEADS = 4
N_IDX_HEADS = 4
IDX_DIM = 32
TOPK_MAX = 256
CONV_WIDTH = 3
CHUNK = 128
N_GROUPS = 4
GROUP_DIM = BRANCH_WIDTH // N_GROUPS
PLE_DIM = 256
ROPE_THETA = 10000.0
EPS = 1e-6
IDX_W_SCALE = (N_IDX_HEADS * IDX_DIM) ** -0.5
QK_SCALE = HEAD_DIM ** -0.5
LOG2_E = 1.4426950408889634

ROW_BLOCK = 256
ATT_BLOCK = 256
DSA_KEY_TILE = 1024
TIE_BLOCK = 512
COUNT_ROWS = 1024
CONV_HALO = 8
MASK_BIAS = -1e30
SCORE_FLOOR = float(jnp.finfo(jnp.float32).min)
SB_DEAD = 152.0
VMEM_LIMIT = 56 * 1024 * 1024

BF16 = jnp.bfloat16
F32 = jnp.float32
NT_DIMS = (((1,), (1,)), ((), ()))


def _dot(a, b):
    return jnp.dot(a, b, preferred_element_type=F32)


def _dot_nt(a, b):
    return lax.dot_general(a, b, NT_DIMS, preferred_element_type=F32)


def _rmsnorm_rows(x, g):
    return x * lax.rsqrt(jnp.mean(x * x, axis=-1, keepdims=True) + EPS) * g


def _sigmoid(x):
    return 1.0 / (1.0 + jnp.exp(-x))


def _params(n_grid_dims=1):
    return pltpu.CompilerParams(
        dimension_semantics=("arbitrary",) * n_grid_dims, vmem_limit_bytes=VMEM_LIMIT)


def _full(shape):
    return pl.BlockSpec(shape, lambda i: (0,) * len(shape))


def _rows(width, block=ROW_BLOCK):
    return pl.BlockSpec((block, width), lambda i: (i, 0))


def _tile_lanes(x, period):
    lane = lax.broadcasted_iota(jnp.int32, (1, 128), 1)
    x = jnp.where(lane < period, x, 0.0)
    while period < 128:
        x = x + pltpu.roll(x, period, 1)
        period *= 2
    return x


def _rope_table_kernel(pos_ref, freq_ref, s64_ref, s32_ref, cos64_ref, sin64_ref, cos32_ref, sin32_ref):
    angle = pos_ref[...].astype(F32) * freq_ref[...]
    cos, sin = jnp.cos(angle), jnp.sin(angle)
    n64, n32 = HEAD_DIM // 2, IDX_DIM // 2
    cos_h, sin_h = _tile_lanes(cos, n64), _tile_lanes(sin, n64)
    cos64_ref[...] = jnp.concatenate([cos_h, cos_h], axis=1)
    sin64_ref[...] = jnp.concatenate([sin_h, sin_h], axis=1) * s64_ref[...]
    cos32_ref[...] = _tile_lanes(pltpu.roll(cos, 128 - n64, 1), n32)
    sin32_ref[...] = _tile_lanes(pltpu.roll(sin, 128 - n64, 1), n32) * s32_ref[...]


def _rope_sign(d, width):
    lane = jnp.arange(width)
    return jnp.where((lane % d) < d // 2, -1.0, 1.0).astype(F32)[None, :]


def _rope_tables(pos_col):
    s = pos_col.shape[0]
    inv_freq = lambda d: ROPE_THETA ** (-jnp.arange(0, d, 2, dtype=F32) / d)
    freq = jnp.concatenate([inv_freq(HEAD_DIM), inv_freq(IDX_DIM), jnp.zeros((80,), F32)])[None, :]
    s64, s32 = _rope_sign(HEAD_DIM, BRANCH_WIDTH), _rope_sign(IDX_DIM, N_IDX_HEADS * IDX_DIM)
    return pl.pallas_call(
        _rope_table_kernel,
        grid=(s // ROW_BLOCK,),
        in_specs=[_rows(1), _full((1, 128)), _full((1, 256)), _full((1, 128))],
        out_specs=[_rows(256), _rows(256), _rows(128), _rows(128)],
        out_shape=[jax.ShapeDtypeStruct((s, 256), F32), jax.ShapeDtypeStruct((s, 256), F32),
                   jax.ShapeDtypeStruct((s, 128), F32), jax.ShapeDtypeStruct((s, 128), F32)],
        compiler_params=_params(),
        name="rope_tables",
    )(pos_col, freq, s64, s32)


def _attn_proj_kernel(x_ref, g_ref, wa_ref, wi_ref, wc_ref,
                      cos64_ref, sin64_ref, cos32_ref, sin32_ref,
                      qa_ref, ka_ref, vat_ref, iq_ref, ik_ref, iwt_ref, cq_ref, ck_ref, cv_ref):
    h = _rmsnorm_rows(x_ref[...], g_ref[...]).astype(BF16)
    c64, s64 = cos64_ref[...], sin64_ref[...]
    c32, s32 = cos32_ref[...], sin32_ref[...]
    pa = _dot(h, wa_ref[...])
    qa_ref[...] = (pa[:, 0:256] * c64 + pa[:, 256:512] * s64) * (QK_SCALE * LOG2_E)
    ka_ref[...] = (pa[:, 512:768] * c64 + pa[:, 768:1024] * s64).astype(BF16)
    vat_ref[0] = pa[:, 1024:1280].T.astype(BF16)
    pi = _dot(h, wi_ref[...])
    iq_ref[...] = pi[:, 0:128] * c32 + pi[:, 128:256] * s32
    ik = pi[:, 256:384] * c32 + pi[:, 384:512] * s32
    ik_ref[...] = ik[:, 0:IDX_DIM].astype(BF16)
    iwt_ref[...] = (pi[:, 512:640] * IDX_W_SCALE).T[0:8, :]
    pc = _dot(h, wc_ref[...])
    cq_ref[...] = (pc[:, 0:256] * (QK_SCALE * LOG2_E)).astype(BF16)
    ck_ref[...] = pc[:, 256:512].astype(BF16)
    cv_ref[...] = pc[:, 512:768].astype(BF16)


def _attn_proj(x, g_pre, w, tables):
    s = x.shape[0]
    nb = s // ROW_BLOCK
    cos64, sin64, cos32, sin32 = tables
    return pl.pallas_call(
        _attn_proj_kernel,
        grid=(nb,),
        in_specs=[_rows(D_MODEL), _full((1, D_MODEL)), _full((D_MODEL, 1280)), _full((D_MODEL, 640)),
                  _full((D_MODEL, 768)), _rows(256), _rows(256), _rows(128), _rows(128)],
        out_specs=[_rows(256), _rows(256), pl.BlockSpec((1, 256, ROW_BLOCK), lambda i: (i, 0, 0)),
                   _rows(128), _rows(IDX_DIM), pl.BlockSpec((8, ROW_BLOCK), lambda i: (0, i)),
                   _rows(256), _rows(256), _rows(256)],
        out_shape=[jax.ShapeDtypeStruct((s, 256), F32), jax.ShapeDtypeStruct((s, 256), BF16),
                   jax.ShapeDtypeStruct((nb, 256, ROW_BLOCK), BF16),
                   jax.ShapeDtypeStruct((s, 128), F32), jax.ShapeDtypeStruct((s, IDX_DIM), BF16),
                   jax.ShapeDtypeStruct((8, s), F32),
                   jax.ShapeDtypeStruct((s, 256), BF16), jax.ShapeDtypeStruct((s, 256), BF16),
                   jax.ShapeDtypeStruct((s, 256), BF16)],
        compiler_params=_params(),
        name="attn_proj",
    )(x, g_pre, w["wa"], w["wi"], w["wc"], cos64, sin64, cos32, sin32)


def _local_mix_kernel(x_ref, g_ref, w_ref, convw_ref, convb_ref, lng_ref, lnb_ref, ws_ref, bs_ref,
                      ogb_ref, ogd_ref, ypad_ref):
    t = x_ref.shape[0]

    @pl.when(pl.program_id(0) == 0)
    def _():
        ypad_ref[0:CONV_HALO, :] = jnp.zeros((CONV_HALO, BRANCH_WIDTH), F32)

    h = _rmsnorm_rows(x_ref[...], g_ref[...]).astype(BF16)
    pr = _dot(h, w_ref[...])
    gate_b, gate_c, x_in = pr[:, 0:256], pr[:, 256:512], pr[:, 512:768]
    d_u, d_v = pr[:, 768:1024], pr[:, 1024:1280]
    silu_b, silu_d = pr[:, 1280:1536], pr[:, 1536:1792]

    y = gate_c * x_in
    ypad_ref[CONV_HALO:CONV_HALO + t, :] = y
    y1 = ypad_ref[CONV_HALO - 1:CONV_HALO - 1 + t, :]
    y2 = ypad_ref[CONV_HALO - 2:CONV_HALO - 2 + t, :]
    conv = convw_ref[2:3, :] * y + convw_ref[1:2, :] * y1 + convw_ref[0:1, :] * y2
    ypad_ref[0:CONV_HALO, :] = y[t - CONV_HALO:t, :]
    o_b = gate_b * (conv + convb_ref[...])
    ogb_ref[...] = (o_b * (silu_b * _sigmoid(silu_b))).astype(BF16)

    mu = jnp.mean(d_v, axis=-1, keepdims=True)
    dc = d_v - mu
    var = jnp.mean(dc * dc, axis=-1, keepdims=True)
    vn = dc * lax.rsqrt(var + EPS) * lng_ref[...] + lnb_ref[...]
    group = lax.broadcasted_iota(jnp.int32, (1, BRANCH_WIDTH), 1) // GROUP_DIM
    tril = (lax.broadcasted_iota(jnp.int32, (CHUNK, CHUNK), 0)
            >= lax.broadcasted_iota(jnp.int32, (CHUNK, CHUNK), 1))
    wm = [jnp.where(tril, ws_ref[g], 0.0).astype(BF16) for g in range(N_GROUPS)]
    mixed = []
    for c in range(t // CHUNK):
        vc = vn[c * CHUNK:(c + 1) * CHUNK, :]
        m = bs_ref[...]
        for g in range(N_GROUPS):
            m = m + _dot(wm[g], jnp.where(group == g, vc, 0.0).astype(BF16))
        mixed.append(m)
    o_d = d_u * jnp.concatenate(mixed, axis=0)
    ogd_ref[...] = (o_d * (silu_d * _sigmoid(silu_d))).astype(BF16)


def _local_mix(x, g_pre, w):
    s = x.shape[0]
    return pl.pallas_call(
        _local_mix_kernel,
        grid=(s // ROW_BLOCK,),
        in_specs=[_rows(D_MODEL), _full((1, D_MODEL)), _full((D_MODEL, 1792)), _full((8, 256)),
                  _full((1, 256)), _full((1, 256)), _full((1, 256)),
                  _full((N_GROUPS, CHUNK, CHUNK)), _full((CHUNK, 256))],
        out_specs=[_rows(256), _rows(256)],
        out_shape=[jax.ShapeDtypeStruct((s, 256), BF16), jax.ShapeDtypeStruct((s, 256), BF16)],
        scratch_shapes=[pltpu.VMEM((ROW_BLOCK + CONV_HALO, BRANCH_WIDTH), F32)],
        compiler_params=_params(),
        name="local_mix",
    )(x, g_pre, w["w2"], w["conv_w"], w["conv_b"], w["ln_g"], w["ln_b"], w["w_spatial"], w["b_spatial"])


I16 = jnp.int16
I16_MIN = -32768
PACK_ROWS = 16
ACC_ROWS = HEAD_DIM + PACK_ROWS


def _sortable_halves(score):
    bits = lax.bitcast_convert_type(score, jnp.int32)
    key = bits ^ ((bits >> 31) & jnp.int32(0x7FFFFFFF))
    hi = (key >> 16).astype(I16)
    lo = ((key & jnp.int32(0xFFFF)) + jnp.int32(I16_MIN)).astype(I16)
    return hi, lo


def _rows16(row32):
    return jnp.broadcast_to(row32, (PACK_ROWS, row32.shape[1])).astype(I16)


def _count_pass(ref, n_steps, cand):
    q = ref.shape[1]
    cand16 = _rows16(cand)
    one, zero = jnp.ones((PACK_ROWS, q), I16), jnp.zeros((PACK_ROWS, q), I16)
    n_acc = 4

    def body(c, accs):
        accs = list(accs)
        base = pl.multiple_of(c * COUNT_ROWS, COUNT_ROWS)
        blk = ref[pl.ds(base, COUNT_ROWS), :]
        for r in range(COUNT_ROWS // PACK_ROWS):
            v = blk[PACK_ROWS * r:PACK_ROWS * (r + 1), :]
            accs[r % n_acc] = accs[r % n_acc] + jnp.where(v >= cand16, one, zero)
        return tuple(accs)

    assert ref.shape[0] // (PACK_ROWS * n_acc) < 2 ** 15
    accs = lax.fori_loop(0, n_steps, body, (zero,) * n_acc)
    total = sum(a.astype(jnp.int32) for a in accs)
    return jnp.sum(total, axis=0, keepdims=True)


def _bisect16(ref, n_steps, target):
    zero = jnp.zeros_like(target)

    def probe(cand, t, above):
        cnt = _count_pass(ref, n_steps, cand)
        enough = cnt >= target
        return jnp.where(enough, cand, t), jnp.where(enough, above, cnt)

    def step(b, state):
        t, above = state
        return probe(t | (jnp.int32(1) << (14 - b)), t, above)

    return lax.fori_loop(0, 15, step, probe(zero, jnp.full_like(target, I16_MIN), zero))


def _dsa_kernel(q_ref, iq_ref, iwt_ref, k_ref, vt_ref, ik_ref, o_ref,
                hi_ref, lo_ref, qm_ref, iqt_ref, acc_ref, m_ref, *, topk):
    nq = q_ref.shape[0]
    kb = DSA_KEY_TILE
    i = pl.program_id(0)
    n_tiles = (i * nq + nq + kb - 1) // kb
    t_idx = i * nq + lax.broadcasted_iota(jnp.int32, (1, nq), 1)
    row = lax.broadcasted_iota(jnp.int32, (kb, 1), 0)

    qt = q_ref[...].T
    head_of_row = lax.broadcasted_iota(jnp.int32, (BRANCH_WIDTH, 1), 0) // HEAD_DIM
    for h in range(N_HEADS):
        qm_ref[h] = jnp.where(head_of_row == h, qt, 0.0).astype(BF16)
    iqt_ref[...] = iq_ref[...].T.astype(BF16)
    w_rows = [iwt_ref[h:h + 1, :] for h in range(N_IDX_HEADS)]

    def score_tile(j, carry, on_diagonal):
        base = pl.multiple_of(j * kb, kb)
        ikb = ik_ref[pl.ds(base, kb), :]
        sc = w_rows[0] * jnp.maximum(_dot(ikb, iqt_ref[0:IDX_DIM, :]), 0.0)
        for h in range(1, N_IDX_HEADS):
            logit = _dot(ikb, iqt_ref[h * IDX_DIM:(h + 1) * IDX_DIM, :])
            sc = sc + w_rows[h] * jnp.maximum(logit, 0.0)
        if on_diagonal:
            sc = jnp.where(base + row <= t_idx, sc, SCORE_FLOOR)
        hi_ref[pl.ds(base, kb), :], lo_ref[pl.ds(base, kb), :] = _sortable_halves(sc)
        return carry

    n_below = (i * nq) // kb
    lax.fori_loop(0, n_below, functools.partial(score_tile, on_diagonal=False), 0)
    lax.fori_loop(n_below, n_tiles, functools.partial(score_tile, on_diagonal=True), 0)

    n_steps = n_tiles * (kb // COUNT_ROWS)
    want = jnp.full((1, nq), topk, jnp.int32)
    t_hi, above_hi = _bisect16(hi_ref, n_steps, want)
    t_hi16 = _rows16(t_hi)
    floor16 = jnp.full((PACK_ROWS, nq), I16_MIN, I16)

    def keep_bucket(c, carry):
        base = pl.multiple_of(c * COUNT_ROWS, COUNT_ROWS)
        hi, lo = hi_ref[pl.ds(base, COUNT_ROWS), :], lo_ref[pl.ds(base, COUNT_ROWS), :]
        lo_ref[pl.ds(base, COUNT_ROWS), :] = jnp.concatenate(
            [jnp.where(hi[PACK_ROWS * r:PACK_ROWS * (r + 1), :] == t_hi16,
                       lo[PACK_ROWS * r:PACK_ROWS * (r + 1), :], floor16)
             for r in range(COUNT_ROWS // PACK_ROWS)], axis=0)
        return carry

    lax.fori_loop(0, n_steps, keep_bucket, 0)
    t_lo, above_lo = _bisect16(lo_ref, n_steps, want - above_hi)
    quota = (topk - above_hi - above_lo).astype(F32)
    t_hi = jnp.where(t_idx < topk, jnp.int32(I16_MIN), t_hi)
    t_hi16, t_lo16 = _rows16(t_hi), _rows16(t_lo)

    m_ref[...] = jnp.full(m_ref.shape, MASK_BIAS, F32)
    acc_ref[...] = jnp.zeros(acc_ref.shape, F32)
    incl_lower = (lax.broadcasted_iota(jnp.int32, (TIE_BLOCK, TIE_BLOCK), 0)
                  >= lax.broadcasted_iota(jnp.int32, (TIE_BLOCK, TIE_BLOCK), 1)).astype(BF16)
    one16, zero16 = jnp.ones((PACK_ROWS, nq), BF16), jnp.zeros((PACK_ROWS, nq), BF16)

    def attend_tile(j, ties_before, on_diagonal):
        base = pl.multiple_of(j * kb, kb)
        hi_t, lo_t = hi_ref[pl.ds(base, kb), :], lo_ref[pl.ds(base, kb), :]
        above, tie = [], []
        for r in range(kb // PACK_ROWS):
            hi16 = hi_t[PACK_ROWS * r:PACK_ROWS * (r + 1), :]
            lo16 = lo_t[PACK_ROWS * r:PACK_ROWS * (r + 1), :]
            bucket = hi16 == t_hi16
            above.append(jnp.where((hi16 > t_hi16) | (bucket & (lo16 > t_lo16)), one16, zero16))
            tie.append(jnp.where(bucket & (lo16 == t_lo16), one16, zero16))
        above, tie = jnp.concatenate(above, axis=0), jnp.concatenate(tie, axis=0)
        parts, running = [], ties_before
        for u in range(kb // TIE_BLOCK):
            parts.append(_dot(incl_lower, tie[u * TIE_BLOCK:(u + 1) * TIE_BLOCK, :]) + running)
            running = parts[-1][TIE_BLOCK - 1:TIE_BLOCK, :]
        ties_upto = jnp.concatenate(parts, axis=0)
        keep = above + tie * jnp.where(ties_upto <= quota, 1.0, 0.0).astype(BF16)
        if on_diagonal:
            keep = keep * jnp.where(base + row <= t_idx, 1.0, 0.0).astype(BF16)
        bias = ((keep - 1.0) * (-MASK_BIAS)).astype(F32)
        kt = k_ref[pl.ds(base, kb), :]
        n_sub = kb // ATT_BLOCK
        vts = [vt_ref[j * n_sub + u] for u in range(n_sub)]
        logits = [_dot(kt, qm_ref[h]) + bias for h in range(N_HEADS)]
        m_old = [m_ref[h:h + 1, :] for h in range(N_HEADS)]
        m_new = [jnp.maximum(m_old[h], jnp.max(logits[h], axis=0, keepdims=True)) for h in range(N_HEADS)]
        alpha = [jnp.exp2(m_old[h] - m_new[h]) for h in range(N_HEADS)]
        probs = [jnp.exp2(logits[h] - m_new[h]) for h in range(N_HEADS)]
        ones_rows = jnp.ones((PACK_ROWS, ATT_BLOCK), BF16)
        for h in range(N_HEADS):
            m_ref[h:h + 1, :] = m_new[h]
            hs = slice(h * HEAD_DIM, (h + 1) * HEAD_DIM)
            pb = probs[h].astype(BF16)
            pv = sum(_dot(jnp.concatenate([vts[u][hs, :], ones_rows], axis=0),
                          pb[u * ATT_BLOCK:(u + 1) * ATT_BLOCK, :]) for u in range(n_sub))
            rows = slice(h * ACC_ROWS, (h + 1) * ACC_ROWS)
            acc_ref[rows, :] = alpha[h] * acc_ref[rows, :] + pv
        return ties_upto[kb - 1:kb, :]

    n_below = (i * nq) // kb
    ties = lax.fori_loop(0, n_below, functools.partial(attend_tile, on_diagonal=False), jnp.zeros((1, nq), F32))
    lax.fori_loop(n_below, n_tiles, functools.partial(attend_tile, on_diagonal=True), ties)

    heads = [acc_ref[h * ACC_ROWS:h * ACC_ROWS + HEAD_DIM, :]
             / acc_ref[h * ACC_ROWS + HEAD_DIM:h * ACC_ROWS + HEAD_DIM + 1, :] for h in range(N_HEADS)]
    o_ref[...] = jnp.concatenate(heads, axis=0).T


def _dsa_attention(qa, iq, iwt, ka, vat, ik):
    s = qa.shape[0]
    nq = ATT_BLOCK
    topk = min(TOPK_MAX, s // 4)
    assert s % DSA_KEY_TILE == 0 and DSA_KEY_TILE % ATT_BLOCK == 0
    return pl.pallas_call(
        functools.partial(_dsa_kernel, topk=topk),
        grid=(s // nq,),
        in_specs=[_rows(256, nq), _rows(128, nq), pl.BlockSpec((8, nq), lambda i: (0, i)),
                  _full((s, 256)), _full((s // ATT_BLOCK, 256, ATT_BLOCK)), _full((s, IDX_DIM))],
        out_specs=_rows(256, nq),
        out_shape=jax.ShapeDtypeStruct((s, 256), F32),
        scratch_shapes=[pltpu.VMEM((s, nq), I16), pltpu.VMEM((s, nq), I16), pltpu.VMEM((N_HEADS, 256, nq), BF16),
                        pltpu.VMEM((128, nq), BF16), pltpu.VMEM((N_HEADS * ACC_ROWS, nq), F32),
                        pltpu.VMEM((8, nq), F32)],
        compiler_params=_params(),
        name="dsa_attn",
    )(qa, iq, iwt, ka, vat, ik)


def _sb_kernel(q_ref, k_ref, v_ref, o_ref, acc_ref, carry_ref):
    nq = q_ref.shape[0]
    kb = ATT_BLOCK
    i = pl.program_id(0)
    q = q_ref[...]
    lane_head = lax.broadcasted_iota(jnp.int32, (1, BRANCH_WIDTH), 1) // HEAD_DIM
    qh = [jnp.where(lane_head == h, q, jnp.zeros_like(q)) for h in range(N_HEADS)]
    later = (lax.broadcasted_iota(jnp.int32, (2 * kb, kb), 0) % kb
             > lax.broadcasted_iota(jnp.int32, (2 * kb, kb), 1)).astype(BF16)
    acc_ref[...] = jnp.zeros(acc_ref.shape, F32)
    carry_ref[...] = jnp.zeros(carry_ref.shape, F32)

    def walk_tile(j, on_diagonal):
        base = pl.multiple_of(j * kb, kb)
        kt = k_ref[pl.ds(base, kb), :]
        vt = v_ref[pl.ds(base, kb), :]
        if on_diagonal:
            strict = (lax.broadcasted_iota(jnp.int32, (1, kb), 1)
                      < lax.broadcasted_iota(jnp.int32, (nq, 1), 0))
        lowest = jnp.full((1, 1), jnp.inf, F32)
        for h in range(N_HEADS):
            z = _dot_nt(qh[h], kt)
            softplus = jnp.maximum(z, 0.0) + jnp.log2(1.0 + jnp.exp2(-jnp.abs(z)))
            sp = jnp.where(strict, softplus, 0.0) if on_diagonal else softplus
            hi = sp.astype(BF16)
            lo = (sp - hi.astype(F32)).astype(BF16)
            after = _dot(jnp.concatenate([hi, lo], axis=1), later)
            c = carry_ref[h]
            wts = jnp.exp2((z - softplus) - (after + c))
            if on_diagonal:
                wts = jnp.where(strict, wts, 0.0)
            vh = jnp.where(lane_head == h, vt, jnp.zeros_like(vt))
            acc_ref[...] += _dot(wts.astype(BF16), vh)
            c_new = c + (after[:, 0:1] + sp[:, 0:1])
            carry_ref[h] = c_new
            lowest = jnp.minimum(lowest, jnp.min(c_new, axis=0, keepdims=True))
        return (lowest[0, 0] < SB_DEAD).astype(jnp.int32)

    def alive(state):
        j, live = state
        return jnp.logical_and(j >= 0, live > 0)

    def walk(state):
        j, _ = state
        return j - 1, walk_tile(j, on_diagonal=False)

    lax.while_loop(alive, walk, (i - 1, walk_tile(i, on_diagonal=True)))
    o_ref[...] = acc_ref[...]


def _sb_attention(cq, ck, cv):
    s = cq.shape[0]
    nq = ATT_BLOCK
    return pl.pallas_call(
        _sb_kernel,
        grid=(s // nq,),
        in_specs=[_rows(256, nq), _full((s, 256)), _full((s, 256))],
        out_specs=_rows(256, nq),
        out_shape=jax.ShapeDtypeStruct((s, 256), F32),
        scratch_shapes=[pltpu.VMEM((nq, 256), F32), pltpu.VMEM((N_HEADS, nq, 1), F32)],
        compiler_params=_params(),
        name="sb_attn",
    )(cq, ck, cv)


def _merge_out_kernel(x_ref, oa_ref, oc_ref, ogb_ref, ogd_ref, p_ref, g_ref, wg_ref, wm_ref, bm_ref,
                      wb_ref, wo_ref, gpost_ref, wple_ref, wpg_ref, out_ref):
    x = x_ref[...]
    h = _rmsnorm_rows(x, g_ref[...]).astype(BF16)
    gates = _dot(h, wg_ref[...])
    silu = gates * _sigmoid(gates)
    branch_in = [(oa_ref[...] * silu[:, 0:256]).astype(BF16), ogb_ref[...],
                 (oc_ref[...] * silu[:, 256:512]).astype(BF16), ogd_ref[...]]
    merged = jnp.zeros((x.shape[0], D_MODEL), F32)
    for n in range(4):
        cols = slice(n * D_MODEL, (n + 1) * D_MODEL)
        gate = _sigmoid(_dot(h, wm_ref[:, cols]) + bm_ref[:, cols])
        merged = merged + gate * _dot(branch_in[n], wb_ref[n])
    y = _dot(merged.astype(BF16), wo_ref[...])
    x1 = x + _rmsnorm_rows(y, gpost_ref[...])
    ple = _dot(p_ref[...].astype(BF16), wple_ref[...])
    out_ref[...] = x1 + ple * _sigmoid(_dot(x1.astype(BF16), wpg_ref[...]))


def _merge_out(x, o_a, o_c, og_b, og_d, p_i, g_pre, w):
    s = x.shape[0]
    return pl.pallas_call(
        _merge_out_kernel,
        grid=(s // ROW_BLOCK,),
        in_specs=[_rows(D_MODEL), _rows(256), _rows(256), _rows(256), _rows(256), _rows(PLE_DIM),
                  _full((1, D_MODEL)), _full((D_MODEL, 512)), _full((D_MODEL, 4 * D_MODEL)),
                  _full((1, 4 * D_MODEL)), _full((4, BRANCH_WIDTH, D_MODEL)), _full((D_MODEL, D_MODEL)),
                  _full((1, D_MODEL)), _full((PLE_DIM, D_MODEL)), _full((D_MODEL, D_MODEL))],
        out_specs=_rows(D_MODEL),
        out_shape=jax.ShapeDtypeStruct((s, D_MODEL), F32),
        compiler_params=_params(),
        name="merge_out",
    )(x, o_a, o_c, og_b, og_d, p_i, g_pre, w["wg"], w["wm"], w["b_merge"], w["w_branch"], w["w_out"],
      w["g_post"], w["w_ple"], w["w_ple_gate"])


IN_COLS = 8100
IN_HEAD = 932
PREP_ROWS = 128


def _swap_halves(x, d):
    n = x.shape[1]
    lane = lax.broadcasted_iota(jnp.int32, (1, n), 1)
    return jnp.where(lane % d < d // 2, pltpu.roll(x, n - d // 2, 1), pltpu.roll(x, d // 2, 1))


def _weight_layout_kernel(w_ref, wa_ref, wi_ref, wc_ref, w2_ref, wg_ref, wm_ref):
    a_q, a_k = w_ref[0, :, 0:256], w_ref[0, :, 256:512]
    wa_ref[0, :, 0:256] = a_q.astype(BF16)
    wa_ref[0, :, 256:512] = _swap_halves(a_q, HEAD_DIM).astype(BF16)
    wa_ref[0, :, 512:768] = a_k.astype(BF16)
    wa_ref[0, :, 768:1024] = _swap_halves(a_k, HEAD_DIM).astype(BF16)
    wa_ref[0, :, 1024:1280] = w_ref[0, :, 512:768].astype(BF16)
    a_iq = w_ref[0, :, 768:896]
    tail = w_ref[0, :, 896:1024]
    lane = lax.broadcasted_iota(jnp.int32, (1, 128), 1)
    wi_ref[0, :, 0:128] = a_iq.astype(BF16)
    wi_ref[0, :, 128:256] = _swap_halves(a_iq, IDX_DIM).astype(BF16)
    wi_ref[0, :, 256:384] = jnp.where(lane < IDX_DIM, tail, 0.0).astype(BF16)
    wi_ref[0, :, 384:512] = jnp.where(lane < IDX_DIM, _swap_halves(tail, IDX_DIM), 0.0).astype(BF16)
    wi_ref[0, :, 512:640] = jnp.where(lane < N_IDX_HEADS, pltpu.roll(tail, 128 - IDX_DIM, 1), 0.0).astype(BF16)

    def rest(lo, hi):
        return w_ref[0, :, IN_HEAD + lo:IN_HEAD + hi].astype(BF16)

    w2_ref[0, :, 0:768] = rest(0, 768)
    wc_ref[0] = rest(768, 1536)
    w2_ref[0, :, 768:1280] = rest(1536, 2048)
    wg_ref[0, :, 0:256] = rest(2048, 2304)
    w2_ref[0, :, 1280:1536] = rest(2304, 2560)
    wg_ref[0, :, 256:512] = rest(2560, 2816)
    w2_ref[0, :, 1536:1792] = rest(2816, 3072)
    wm_ref[0] = rest(3072, 7168)


def _weight_layout(w_in):
    depth = w_in.shape[0]
    assert w_in.shape[1:] == (D_MODEL, IN_COLS)
    widths = (1280, 640, 768, 1792, 512, 4096)
    return pl.pallas_call(
        _weight_layout_kernel,
        grid=(depth, D_MODEL // PREP_ROWS),
        in_specs=[pl.BlockSpec((1, PREP_ROWS, IN_COLS), lambda l, r: (l, r, 0))],
        out_specs=[pl.BlockSpec((1, PREP_ROWS, n), lambda l, r: (l, r, 0)) for n in widths],
        out_shape=[jax.ShapeDtypeStruct((depth, D_MODEL, n), BF16) for n in widths],
        compiler_params=_params(2),
        name="weight_layout",
    )(w_in)


def _layer_weights(i, w_proj, conv_w, conv_b, ln_g, ln_b, w_spatial, b_spatial, b_merge, w_branch, w_out,
                   g_post, w_ple, w_ple_gate):
    row = lambda v: v[None, :]
    wa, wi, wc, w2, wg, wm = (w[i] for w in w_proj)
    return {
        "wa": wa, "wi": wi, "wc": wc, "w2": w2, "wg": wg, "wm": wm,
        "conv_w": jnp.pad(conv_w[i], ((0, 8 - CONV_WIDTH), (0, 0))),
        "conv_b": row(conv_b[i]), "ln_g": row(ln_g[i]), "ln_b": row(ln_b[i]),
        "w_spatial": w_spatial[i],
        "b_spatial": jnp.repeat(b_spatial[i].T, GROUP_DIM, axis=1),
        "b_merge": row(b_merge[i]),
        "w_branch": w_branch[i].astype(BF16), "w_out": w_out[i].astype(BF16), "g_post": row(g_post[i]),
        "w_ple": w_ple[i].astype(BF16), "w_ple_gate": w_ple_gate[i].astype(BF16),
    }


def kernel(x, p, positions, g_pre, w_in, conv_w, conv_b, ln_g, ln_b, w_spatial, b_spatial, b_merge,
           w_branch, w_out, g_post, w_ple, w_ple_gate):
    batch, s, _ = x.shape
    assert batch == 1 and s % ROW_BLOCK == 0 and s % ATT_BLOCK == 0 and ATT_BLOCK >= min(TOPK_MAX, s // 4)
    depth = w_in.shape[0]
    xs = x[0]
    tables = _rope_tables(positions[0][:, None])
    w_proj = _weight_layout(w_in)
    for i in range(depth):
        w = _layer_weights(i, w_proj, conv_w, conv_b, ln_g, ln_b, w_spatial, b_spatial, b_merge, w_branch,
                           w_out, g_post, w_ple, w_ple_gate)
        g = g_pre[i][None, :]
        qa, ka, vat, iq, ik, iwt, cq, ck, cv = _attn_proj(xs, g, w, tables)
        og_b, og_d = _local_mix(xs, g, w)
        o_a = _dsa_attention(qa, iq, iwt, ka, vat, ik)
        o_c = _sb_attention(cq, ck, cv)
        xs = _merge_out(xs, o_a, o_c, og_b, og_d, p[i][0], g, w)
    return xs[None]
```

```python
import functools

import jax
import jax.numpy as jnp
from jax import lax
from jax.experimental import pallas as pl
from jax.experimental.pallas import tpu as pltpu

D_MODEL = 1024
BRANCH_WIDTH = 256
HEAD_DIM = 64
N_HEADS = 4
N_IDX_HEADS = 4
IDX_DIM = 32
TOPK_MAX = 256
CONV_WIDTH = 3
CHUNK = 128
N_GROUPS = 4
GROUP_DIM = BRANCH_WIDTH // N_GROUPS
PLE_DIM = 256
ROPE_THETA = 10000.0
EPS = 1e-6
IDX_W_SCALE = (N_IDX_HEADS * IDX_DIM) ** -0.5
QK_SCALE = HEAD_DIM ** -0.5
LOG2_E = 1.4426950408889634

ROW_BLOCK = 256
ATT_BLOCK = 256
DSA_KEY_TILE = 1024
TIE_BLOCK = 256
COUNT_ROWS = 1024
CONV_HALO = 8
MASK_BIAS = -1e30
SCORE_FLOOR = float(jnp.finfo(jnp.float32).min)
SB_DEAD = 152.0
CAP_SLACK = 1.03
MIN_DENOMINATOR = 2.0 ** -64
VMEM_LIMIT = 56 * 1024 * 1024

BF16 = jnp.bfloat16
F32 = jnp.float32
NT_DIMS = (((1,), (1,)), ((), ()))


def _dot(a, b):
    return jnp.dot(a, b, preferred_element_type=F32)


def _dot_nt(a, b):
    return lax.dot_general(a, b, NT_DIMS, preferred_element_type=F32)


def _rmsnorm_rows(x, g):
    return x * lax.rsqrt(jnp.mean(x * x, axis=-1, keepdims=True) + EPS) * g


def _sigmoid(x):
    return 1.0 / (1.0 + jnp.exp(-x))


def _params(n_grid_dims=1):
    return pltpu.CompilerParams(
        dimension_semantics=("arbitrary",) * n_grid_dims, vmem_limit_bytes=VMEM_LIMIT)


def _full(shape):
    return pl.BlockSpec(shape, lambda i: (0,) * len(shape))


def _rows(width, block=ROW_BLOCK):
    return pl.BlockSpec((block, width), lambda i: (i, 0))


def _tile_lanes(x, period):
    lane = lax.broadcasted_iota(jnp.int32, (1, 128), 1)
    x = jnp.where(lane < period, x, 0.0)
    while period < 128:
        x = x + pltpu.roll(x, period, 1)
        period *= 2
    return x


def _rope_table_kernel(pos_ref, freq_ref, s64_ref, s32_ref, cos64_ref, sin64_ref, cos32_ref, sin32_ref):
    angle = pos_ref[...].astype(F32) * freq_ref[...]
    cos, sin = jnp.cos(angle), jnp.sin(angle)
    n64, n32 = HEAD_DIM // 2, IDX_DIM // 2
    cos_h, sin_h = _tile_lanes(cos, n64), _tile_lanes(sin, n64)
    cos64_ref[...] = jnp.concatenate([cos_h, cos_h], axis=1)
    sin64_ref[...] = jnp.concatenate([sin_h, sin_h], axis=1) * s64_ref[...]
    cos32_ref[...] = _tile_lanes(pltpu.roll(cos, 128 - n64, 1), n32)
    sin32_ref[...] = _tile_lanes(pltpu.roll(sin, 128 - n64, 1), n32) * s32_ref[...]


def _rope_sign(d, width):
    lane = jnp.arange(width)
    return jnp.where((lane % d) < d // 2, -1.0, 1.0).astype(F32)[None, :]


def _rope_tables(pos_col):
    s = pos_col.shape[0]
    inv_freq = lambda d: ROPE_THETA ** (-jnp.arange(0, d, 2, dtype=F32) / d)
    freq = jnp.concatenate([inv_freq(HEAD_DIM), inv_freq(IDX_DIM), jnp.zeros((80,), F32)])[None, :]
    s64, s32 = _rope_sign(HEAD_DIM, BRANCH_WIDTH), _rope_sign(IDX_DIM, N_IDX_HEADS * IDX_DIM)
    return pl.pallas_call(
        _rope_table_kernel,
        grid=(s // ROW_BLOCK,),
        in_specs=[_rows(1), _full((1, 128)), _full((1, 256)), _full((1, 128))],
        out_specs=[_rows(256), _rows(256), _rows(128), _rows(128)],
        out_shape=[jax.ShapeDtypeStruct((s, 256), F32), jax.ShapeDtypeStruct((s, 256), F32),
                   jax.ShapeDtypeStruct((s, 128), F32), jax.ShapeDtypeStruct((s, 128), F32)],
        compiler_params=_params(),
        name="rope_tables",
    )(pos_col, freq, s64, s32)


def _attn_proj_kernel(x_ref, g_ref, wa_ref, wi_ref, wc_ref,
                      cos64_ref, sin64_ref, cos32_ref, sin32_ref,
                      qa_ref, ka_ref, vat_ref, iq_ref, ik_ref, iwt_ref, cq_ref, ck_ref, cv_ref):
    h = _rmsnorm_rows(x_ref[...], g_ref[...]).astype(BF16)
    c64, s64 = cos64_ref[...], sin64_ref[...]
    c32, s32 = cos32_ref[...], sin32_ref[...]
    pa = _dot(h, wa_ref[...])
    qa_ref[...] = (pa[:, 0:256] * c64 + pa[:, 256:512] * s64) * (QK_SCALE * LOG2_E)
    ka_ref[...] = (pa[:, 512:768] * c64 + pa[:, 768:1024] * s64).astype(BF16)
    vat_ref[0] = pa[:, 1024:1280].T.astype(BF16)
    pi = _dot(h, wi_ref[...])
    iq_ref[...] = pi[:, 0:128] * c32 + pi[:, 128:256] * s32
    ik = pi[:, 256:384] * c32 + pi[:, 384:512] * s32
    ik_ref[...] = ik[:, 0:IDX_DIM].astype(BF16)
    iwt_ref[...] = (pi[:, 512:640] * IDX_W_SCALE).T[0:8, :]
    pc = _dot(h, wc_ref[...])
    cq_ref[...] = (pc[:, 0:256] * (QK_SCALE * LOG2_E)).astype(BF16)
    ck_ref[...] = pc[:, 256:512].astype(BF16)
    cv_ref[...] = pc[:, 512:768].astype(BF16)


def _attn_proj(x, g_pre, w, tables):
    s = x.shape[0]
    nb = s // ROW_BLOCK
    cos64, sin64, cos32, sin32 = tables
    return pl.pallas_call(
        _attn_proj_kernel,
        grid=(nb,),
        in_specs=[_rows(D_MODEL), _full((1, D_MODEL)), _full((D_MODEL, 1280)), _full((D_MODEL, 640)),
                  _full((D_MODEL, 768)), _rows(256), _rows(256), _rows(128), _rows(128)],
        out_specs=[_rows(256), _rows(256), pl.BlockSpec((1, 256, ROW_BLOCK), lambda i: (i, 0, 0)),
                   _rows(128), _rows(IDX_DIM), pl.BlockSpec((8, ROW_BLOCK), lambda i: (0, i)),
                   _rows(256), _rows(256), _rows(256)],
        out_shape=[jax.ShapeDtypeStruct((s, 256), F32), jax.ShapeDtypeStruct((s, 256), BF16),
                   jax.ShapeDtypeStruct((nb, 256, ROW_BLOCK), BF16),
                   jax.ShapeDtypeStruct((s, 128), F32), jax.ShapeDtypeStruct((s, IDX_DIM), BF16),
                   jax.ShapeDtypeStruct((8, s), F32),
                   jax.ShapeDtypeStruct((s, 256), BF16), jax.ShapeDtypeStruct((s, 256), BF16),
                   jax.ShapeDtypeStruct((s, 256), BF16)],
        compiler_params=_params(),
        name="attn_proj",
    )(x, g_pre, w["wa"], w["wi"], w["wc"], cos64, sin64, cos32, sin32)


def _local_mix_kernel(x_ref, g_ref, w_ref, convw_ref, convb_ref, lng_ref, lnb_ref, ws_ref, bs_ref,
                      ogb_ref, ogd_ref, ypad_ref):
    t = x_ref.shape[0]

    @pl.when(pl.program_id(0) == 0)
    def _():
        ypad_ref[0:CONV_HALO, :] = jnp.zeros((CONV_HALO, BRANCH_WIDTH), F32)

    h = _rmsnorm_rows(x_ref[...], g_ref[...]).astype(BF16)
    pr = _dot(h, w_ref[...])
    gate_b, gate_c, x_in = pr[:, 0:256], pr[:, 256:512], pr[:, 512:768]
    d_u, d_v = pr[:, 768:1024], pr[:, 1024:1280]
    silu_b, silu_d = pr[:, 1280:1536], pr[:, 1536:1792]

    y = gate_c * x_in
    ypad_ref[CONV_HALO:CONV_HALO + t, :] = y
    y1 = ypad_ref[CONV_HALO - 1:CONV_HALO - 1 + t, :]
    y2 = ypad_ref[CONV_HALO - 2:CONV_HALO - 2 + t, :]
    conv = convw_ref[2:3, :] * y + convw_ref[1:2, :] * y1 + convw_ref[0:1, :] * y2
    ypad_ref[0:CONV_HALO, :] = y[t - CONV_HALO:t, :]
    o_b = gate_b * (conv + convb_ref[...])
    ogb_ref[...] = (o_b * (silu_b * _sigmoid(silu_b))).astype(BF16)

    mu = jnp.mean(d_v, axis=-1, keepdims=True)
    dc = d_v - mu
    var = jnp.mean(dc * dc, axis=-1, keepdims=True)
    vn = dc * lax.rsqrt(var + EPS) * lng_ref[...] + lnb_ref[...]
    group = lax.broadcasted_iota(jnp.int32, (1, BRANCH_WIDTH), 1) // GROUP_DIM
    tril = (lax.broadcasted_iota(jnp.int32, (CHUNK, CHUNK), 0)
            >= lax.broadcasted_iota(jnp.int32, (CHUNK, CHUNK), 1))
    wm = [jnp.where(tril, ws_ref[g], 0.0).astype(BF16) for g in range(N_GROUPS)]
    mixed = []
    for c in range(t // CHUNK):
        vc = vn[c * CHUNK:(c + 1) * CHUNK, :]
        m = bs_ref[...]
        for g in range(N_GROUPS):
            m = m + _dot(wm[g], jnp.where(group == g, vc, 0.0).astype(BF16))
        mixed.append(m)
    o_d = d_u * jnp.concatenate(mixed, axis=0)
    ogd_ref[...] = (o_d * (silu_d * _sigmoid(silu_d))).astype(BF16)


def _local_mix(x, g_pre, w):
    s = x.shape[0]
    return pl.pallas_call(
        _local_mix_kernel,
        grid=(s // ROW_BLOCK,),
        in_specs=[_rows(D_MODEL), _full((1, D_MODEL)), _full((D_MODEL, 1792)), _full((8, 256)),
                  _full((1, 256)), _full((1, 256)), _full((1, 256)),
                  _full((N_GROUPS, CHUNK, CHUNK)), _full((CHUNK, 256))],
        out_specs=[_rows(256), _rows(256)],
        out_shape=[jax.ShapeDtypeStruct((s, 256), BF16), jax.ShapeDtypeStruct((s, 256), BF16)],
        scratch_shapes=[pltpu.VMEM((ROW_BLOCK + CONV_HALO, BRANCH_WIDTH), F32)],
        compiler_params=_params(),
        name="local_mix",
    )(x, g_pre, w["w2"], w["conv_w"], w["conv_b"], w["ln_g"], w["ln_b"], w["w_spatial"], w["b_spatial"])


I16 = jnp.int16
I16_MIN = -32768
PACK_ROWS = 16
ACC_ROWS = HEAD_DIM + PACK_ROWS


def _sortable_halves(score):
    bits = lax.bitcast_convert_type(score, jnp.int32)
    key = bits ^ ((bits >> 31) & jnp.int32(0x7FFFFFFF))
    hi = (key >> 16).astype(I16)
    lo = ((key & jnp.int32(0xFFFF)) + jnp.int32(I16_MIN)).astype(I16)
    return hi, lo


def _rows16(row32):
    return jnp.broadcast_to(row32, (PACK_ROWS, row32.shape[1])).astype(I16)


def _count_pass(ref, n_steps, cand):
    q = ref.shape[1]
    cand16 = _rows16(cand)
    one, zero = jnp.ones((PACK_ROWS, q), I16), jnp.zeros((PACK_ROWS, q), I16)
    n_acc = 4

    def body(c, accs):
        accs = list(accs)
        base = pl.multiple_of(c * COUNT_ROWS, COUNT_ROWS)
        blk = ref[pl.ds(base, COUNT_ROWS), :]
        for r in range(COUNT_ROWS // PACK_ROWS):
            v = blk[PACK_ROWS * r:PACK_ROWS * (r + 1), :]
            accs[r % n_acc] = accs[r % n_acc] + jnp.where(v >= cand16, one, zero)
        return tuple(accs)

    assert ref.shape[0] // (PACK_ROWS * n_acc) < 2 ** 15
    accs = lax.fori_loop(0, n_steps, body, (zero,) * n_acc)
    total = sum(a.astype(jnp.int32) for a in accs)
    return jnp.sum(total, axis=0, keepdims=True)


def _bisect16(ref, n_steps, target):
    zero = jnp.zeros_like(target)

    def probe(cand, t, above):
        cnt = _count_pass(ref, n_steps, cand)
        enough = cnt >= target
        return jnp.where(enough, cand, t), jnp.where(enough, above, cnt)

    def step(b, state):
        t, above = state
        return probe(t | (jnp.int32(1) << (14 - b)), t, above)

    return lax.fori_loop(0, 15, step, probe(zero, jnp.full_like(target, I16_MIN), zero))


def _dsa_kernel(q_ref, iq_ref, iwt_ref, k_ref, vt_ref, ik_ref, o_ref,
                hi_ref, lo_ref, qm_ref, iqt_ref, acc_ref, m_ref, knorm_ref, *, topk):
    nq = q_ref.shape[0]
    kb = DSA_KEY_TILE
    i = pl.program_id(0)
    n_tiles = (i * nq + nq + kb - 1) // kb
    t_idx = i * nq + lax.broadcasted_iota(jnp.int32, (1, nq), 1)
    row = lax.broadcasted_iota(jnp.int32, (kb, 1), 0)

    qt = q_ref[...].T
    head_of_row = lax.broadcasted_iota(jnp.int32, (BRANCH_WIDTH, 1), 0) // HEAD_DIM
    for h in range(N_HEADS):
        qm_ref[h] = jnp.where(head_of_row == h, qt, 0.0).astype(BF16)
    iqt_ref[...] = iq_ref[...].T.astype(BF16)

    @pl.when(i == 0)
    def _():
        head_of_lane = (lax.broadcasted_iota(jnp.int32, (BRANCH_WIDTH, 1), 0) // HEAD_DIM
                        == lax.broadcasted_iota(jnp.int32, (1, 128), 1)).astype(BF16)

        def widest(c, best):
            kf = k_ref[pl.ds(pl.multiple_of(c * kb, kb), kb), :].astype(F32)
            return jnp.maximum(best, jnp.max(_dot((kf * kf).astype(BF16), head_of_lane), axis=0, keepdims=True))

        best = lax.fori_loop(0, k_ref.shape[0] // kb, widest, jnp.zeros((1, 128), F32))
        knorm_ref[...] = jnp.broadcast_to(best, knorm_ref.shape)

    logit_cap = jnp.zeros((1, nq), F32)
    for h in range(N_HEADS):
        qf = qm_ref[h].astype(F32)
        q_sq = jnp.sum(qf * qf, axis=0, keepdims=True)
        logit_cap = jnp.maximum(logit_cap, jnp.sqrt(q_sq * knorm_ref[0:1, h:h + 1]) * CAP_SLACK + CAP_SLACK)
    w_rows = [iwt_ref[h:h + 1, :] for h in range(N_IDX_HEADS)]

    def score_tile(j, carry, on_diagonal):
        base = pl.multiple_of(j * kb, kb)
        ikb = ik_ref[pl.ds(base, kb), :]
        sc = w_rows[0] * jnp.maximum(_dot(ikb, iqt_ref[0:IDX_DIM, :]), 0.0)
        for h in range(1, N_IDX_HEADS):
            logit = _dot(ikb, iqt_ref[h * IDX_DIM:(h + 1) * IDX_DIM, :])
            sc = sc + w_rows[h] * jnp.maximum(logit, 0.0)
        if on_diagonal:
            sc = jnp.where(base + row <= t_idx, sc, SCORE_FLOOR)
        hi_ref[pl.ds(base, kb), :], lo_ref[pl.ds(base, kb), :] = _sortable_halves(sc)
        return carry

    n_below = (i * nq) // kb
    lax.fori_loop(0, n_below, functools.partial(score_tile, on_diagonal=False), 0)
    lax.fori_loop(n_below, n_tiles, functools.partial(score_tile, on_diagonal=True), 0)

    n_steps = n_tiles * (kb // COUNT_ROWS)
    want = jnp.full((1, nq), topk, jnp.int32)
    t_hi, above_hi = _bisect16(hi_ref, n_steps, want)
    t_hi16 = _rows16(t_hi)
    floor16 = jnp.full((PACK_ROWS, nq), I16_MIN, I16)

    def keep_bucket(c, carry):
        base = pl.multiple_of(c * COUNT_ROWS, COUNT_ROWS)
        hi, lo = hi_ref[pl.ds(base, COUNT_ROWS), :], lo_ref[pl.ds(base, COUNT_ROWS), :]
        lo_ref[pl.ds(base, COUNT_ROWS), :] = jnp.concatenate(
            [jnp.where(hi[PACK_ROWS * r:PACK_ROWS * (r + 1), :] == t_hi16,
                       lo[PACK_ROWS * r:PACK_ROWS * (r + 1), :], floor16)
             for r in range(COUNT_ROWS // PACK_ROWS)], axis=0)
        return carry

    lax.fori_loop(0, n_steps, keep_bucket, 0)
    t_lo, above_lo = _bisect16(lo_ref, n_steps, want - above_hi)
    quota = (topk - above_hi - above_lo).astype(F32)
    t_hi = jnp.where(t_idx < topk, jnp.int32(I16_MIN), t_hi)
    t_hi16, t_lo16 = _rows16(t_hi), _rows16(t_lo)

    incl_lower = (lax.broadcasted_iota(jnp.int32, (TIE_BLOCK, TIE_BLOCK), 0)
                  >= lax.broadcasted_iota(jnp.int32, (TIE_BLOCK, TIE_BLOCK), 1)).astype(BF16)
    one16, zero16 = jnp.ones((PACK_ROWS, nq), BF16), jnp.zeros((PACK_ROWS, nq), BF16)

    def attend_tile(j, ties_before, on_diagonal, running_max):
        base = pl.multiple_of(j * kb, kb)
        hi_t, lo_t = hi_ref[pl.ds(base, kb), :], lo_ref[pl.ds(base, kb), :]
        above, tie = [], []
        for r in range(kb // PACK_ROWS):
            hi16 = hi_t[PACK_ROWS * r:PACK_ROWS * (r + 1), :]
            lo16 = lo_t[PACK_ROWS * r:PACK_ROWS * (r + 1), :]
            bucket = hi16 == t_hi16
            above.append(jnp.where((hi16 > t_hi16) | (bucket & (lo16 > t_lo16)), one16, zero16))
            tie.append(jnp.where(bucket & (lo16 == t_lo16), one16, zero16))
        above, tie = jnp.concatenate(above, axis=0), jnp.concatenate(tie, axis=0)
        parts, running = [], ties_before
        for u in range(kb // TIE_BLOCK):
            parts.append(_dot(incl_lower, tie[u * TIE_BLOCK:(u + 1) * TIE_BLOCK, :]) + running)
            running = parts[-1][TIE_BLOCK - 1:TIE_BLOCK, :]
        ties_upto = jnp.concatenate(parts, axis=0)
        keep = above + tie * jnp.where(ties_upto <= quota, 1.0, 0.0).astype(BF16)
        if on_diagonal:
            keep = keep * jnp.where(base + row <= t_idx, 1.0, 0.0).astype(BF16)
        bias = ((keep - 1.0) * (-MASK_BIAS)).astype(F32)
        kt = k_ref[pl.ds(base, kb), :]
        n_sub = kb // ATT_BLOCK
        vts = [vt_ref[j * n_sub + u] for u in range(n_sub)]
        ones_rows = jnp.ones((PACK_ROWS, ATT_BLOCK), BF16)

        def value_matmul(h, p):
            hs = slice(h * HEAD_DIM, (h + 1) * HEAD_DIM)
            pb = p.astype(BF16)
            return sum(_dot(jnp.concatenate([vts[u][hs, :], ones_rows], axis=0),
                            pb[u * ATT_BLOCK:(u + 1) * ATT_BLOCK, :]) for u in range(n_sub))

        if running_max:
            logits = [_dot(kt, qm_ref[h]) + bias for h in range(N_HEADS)]
            m_old = [m_ref[h:h + 1, :] for h in range(N_HEADS)]
            m_new = [jnp.maximum(m_old[h], jnp.max(logits[h], axis=0, keepdims=True)) for h in range(N_HEADS)]
            alpha = [jnp.exp2(m_old[h] - m_new[h]) for h in range(N_HEADS)]
            probs = [jnp.exp2(logits[h] - m_new[h]) for h in range(N_HEADS)]
            for h in range(N_HEADS):
                m_ref[h:h + 1, :] = m_new[h]
                rows = slice(h * ACC_ROWS, (h + 1) * ACC_ROWS)
                acc_ref[rows, :] = alpha[h] * acc_ref[rows, :] + value_matmul(h, probs[h])
        else:
            shift = bias - logit_cap
            probs = [jnp.exp2(_dot(kt, qm_ref[h]) + shift) for h in range(N_HEADS)]
            for h in range(N_HEADS):
                rows = slice(h * ACC_ROWS, (h + 1) * ACC_ROWS)
                acc_ref[rows, :] += value_matmul(h, probs[h])
        return ties_upto[kb - 1:kb, :]

    n_below = (i * nq) // kb

    def attend(running_max):
        acc_ref[...] = jnp.zeros(acc_ref.shape, F32)
        below = functools.partial(attend_tile, on_diagonal=False, running_max=running_max)
        diagonal = functools.partial(attend_tile, on_diagonal=True, running_max=running_max)
        lax.fori_loop(n_below, n_tiles, diagonal, lax.fori_loop(0, n_below, below, jnp.zeros((1, nq), F32)))

    attend(running_max=False)
    denominators = jnp.concatenate(
        [acc_ref[h * ACC_ROWS + HEAD_DIM:h * ACC_ROWS + HEAD_DIM + 1, :] for h in range(N_HEADS)], axis=0)
    weakest = jnp.min(jnp.min(denominators, axis=1, keepdims=True), axis=0, keepdims=True)[0, 0]

    @pl.when(jnp.logical_not(weakest >= MIN_DENOMINATOR))
    def _():
        m_ref[...] = jnp.full(m_ref.shape, MASK_BIAS, F32)
        attend(running_max=True)

    heads = [acc_ref[h * ACC_ROWS:h * ACC_ROWS + HEAD_DIM, :]
             / acc_ref[h * ACC_ROWS + HEAD_DIM:h * ACC_ROWS + HEAD_DIM + 1, :] for h in range(N_HEADS)]
    o_ref[...] = jnp.concatenate(heads, axis=0).T


def _dsa_attention(qa, iq, iwt, ka, vat, ik):
    s = qa.shape[0]
    nq = ATT_BLOCK
    topk = min(TOPK_MAX, s // 4)
    assert s % DSA_KEY_TILE == 0 and DSA_KEY_TILE % ATT_BLOCK == 0
    return pl.pallas_call(
        functools.partial(_dsa_kernel, topk=topk),
        grid=(s // nq,),
        in_specs=[_rows(256, nq), _rows(128, nq), pl.BlockSpec((8, nq), lambda i: (0, i)),
                  _full((s, 256)), _full((s // ATT_BLOCK, 256, ATT_BLOCK)), _full((s, IDX_DIM))],
        out_specs=_rows(256, nq),
        out_shape=jax.ShapeDtypeStruct((s, 256), F32),
        scratch_shapes=[pltpu.VMEM((s, nq), I16), pltpu.VMEM((s, nq), I16), pltpu.VMEM((N_HEADS, 256, nq), BF16),
                        pltpu.VMEM((128, nq), BF16), pltpu.VMEM((N_HEADS * ACC_ROWS, nq), F32),
                        pltpu.VMEM((8, nq), F32), pltpu.VMEM((8, 128), F32)],
        compiler_params=_params(),
        name="dsa_attn",
    )(qa, iq, iwt, ka, vat, ik)


def _sb_kernel(q_ref, k_ref, v_ref, o_ref, acc_ref, carry_ref):
    nq = q_ref.shape[0]
    kb = ATT_BLOCK
    i = pl.program_id(0)
    q = q_ref[...]
    lane_head = lax.broadcasted_iota(jnp.int32, (1, BRANCH_WIDTH), 1) // HEAD_DIM
    qh = [jnp.where(lane_head == h, q, jnp.zeros_like(q)) for h in range(N_HEADS)]
    later = (lax.broadcasted_iota(jnp.int32, (2 * kb, kb), 0) % kb
             > lax.broadcasted_iota(jnp.int32, (2 * kb, kb), 1)).astype(BF16)
    acc_ref[...] = jnp.zeros(acc_ref.shape, F32)
    carry_ref[...] = jnp.zeros(carry_ref.shape, F32)

    def walk_tile(j, on_diagonal):
        base = pl.multiple_of(j * kb, kb)
        kt = k_ref[pl.ds(base, kb), :]
        vt = v_ref[pl.ds(base, kb), :]
        if on_diagonal:
            strict = (lax.broadcasted_iota(jnp.int32, (1, kb), 1)
                      < lax.broadcasted_iota(jnp.int32, (nq, 1), 0))
        lowest = jnp.full((1, 1), jnp.inf, F32)
        for h in range(N_HEADS):
            z = _dot_nt(qh[h], kt)
            softplus = jnp.maximum(z, 0.0) + jnp.log2(1.0 + jnp.exp2(-jnp.abs(z)))
            sp = jnp.where(strict, softplus, 0.0) if on_diagonal else softplus
            hi = sp.astype(BF16)
            lo = (sp - hi.astype(F32)).astype(BF16)
            after = _dot(jnp.concatenate([hi, lo], axis=1), later)
            c = carry_ref[h]
            wts = jnp.exp2((z - softplus) - (after + c))
            if on_diagonal:
                wts = jnp.where(strict, wts, 0.0)
            vh = jnp.where(lane_head == h, vt, jnp.zeros_like(vt))
            acc_ref[...] += _dot(wts.astype(BF16), vh)
            c_new = c + (after[:, 0:1] + sp[:, 0:1])
            carry_ref[h] = c_new
            lowest = jnp.minimum(lowest, jnp.min(c_new, axis=0, keepdims=True))
        return (lowest[0, 0] < SB_DEAD).astype(jnp.int32)

    def alive(state):
        j, live = state
        return jnp.logical_and(j >= 0, live > 0)

    def walk(state):
        j, _ = state
        return j - 1, walk_tile(j, on_diagonal=False)

    lax.while_loop(alive, walk, (i - 1, walk_tile(i, on_diagonal=True)))
    o_ref[...] = acc_ref[...]


def _sb_attention(cq, ck, cv):
    s = cq.shape[0]
    nq = ATT_BLOCK
    return pl.pallas_call(
        _sb_kernel,
        grid=(s // nq,),
        in_specs=[_rows(256, nq), _full((s, 256)), _full((s, 256))],
        out_specs=_rows(256, nq),
        out_shape=jax.ShapeDtypeStruct((s, 256), F32),
        scratch_shapes=[pltpu.VMEM((nq, 256), F32), pltpu.VMEM((N_HEADS, nq, 1), F32)],
        compiler_params=_params(),
        name="sb_attn",
    )(cq, ck, cv)


def _merge_out_kernel(x_ref, oa_ref, oc_ref, ogb_ref, ogd_ref, p_ref, g_ref, wg_ref, wm_ref, bm_ref,
                      wb_ref, wo_ref, gpost_ref, wple_ref, wpg_ref, out_ref):
    x = x_ref[...]
    h = _rmsnorm_rows(x, g_ref[...]).astype(BF16)
    gates = _dot(h, wg_ref[...])
    silu = gates * _sigmoid(gates)
    branch_in = [(oa_ref[...] * silu[:, 0:256]).astype(BF16), ogb_ref[...],
                 (oc_ref[...] * silu[:, 256:512]).astype(BF16), ogd_ref[...]]
    merged = jnp.zeros((x.shape[0], D_MODEL), F32)
    for n in range(4):
        cols = slice(n * D_MODEL, (n + 1) * D_MODEL)
        gate = _sigmoid(_dot(h, wm_ref[:, cols]) + bm_ref[:, cols])
        merged = merged + gate * _dot(branch_in[n], wb_ref[n])
    y = _dot(merged.astype(BF16), wo_ref[...])
    x1 = x + _rmsnorm_rows(y, gpost_ref[...])
    ple = _dot(p_ref[...].astype(BF16), wple_ref[...])
    out_ref[...] = x1 + ple * _sigmoid(_dot(x1.astype(BF16), wpg_ref[...]))


def _merge_out(x, o_a, o_c, og_b, og_d, p_i, g_pre, w):
    s = x.shape[0]
    return pl.pallas_call(
        _merge_out_kernel,
        grid=(s // ROW_BLOCK,),
        in_specs=[_rows(D_MODEL), _rows(256), _rows(256), _rows(256), _rows(256), _rows(PLE_DIM),
                  _full((1, D_MODEL)), _full((D_MODEL, 512)), _full((D_MODEL, 4 * D_MODEL)),
                  _full((1, 4 * D_MODEL)), _full((4, BRANCH_WIDTH, D_MODEL)), _full((D_MODEL, D_MODEL)),
                  _full((1, D_MODEL)), _full((PLE_DIM, D_MODEL)), _full((D_MODEL, D_MODEL))],
        out_specs=_rows(D_MODEL),
        out_shape=jax.ShapeDtypeStruct((s, D_MODEL), F32),
        compiler_params=_params(),
        name="merge_out",
    )(x, o_a, o_c, og_b, og_d, p_i, g_pre, w["wg"], w["wm"], w["b_merge"], w["w_branch"], w["w_out"],
      w["g_post"], w["w_ple"], w["w_ple_gate"])


IN_COLS = 8100
IN_HEAD = 932
PREP_ROWS = 128


def _swap_halves(x, d):
    n = x.shape[1]
    lane = lax.broadcasted_iota(jnp.int32, (1, n), 1)
    return jnp.where(lane % d < d // 2, pltpu.roll(x, n - d // 2, 1), pltpu.roll(x, d // 2, 1))


def _weight_layout_kernel(w_ref, wa_ref, wi_ref, wc_ref, w2_ref, wg_ref, wm_ref):
    a_q, a_k = w_ref[0, :, 0:256], w_ref[0, :, 256:512]
    wa_ref[0, :, 0:256] = a_q.astype(BF16)
    wa_ref[0, :, 256:512] = _swap_halves(a_q, HEAD_DIM).astype(BF16)
    wa_ref[0, :, 512:768] = a_k.astype(BF16)
    wa_ref[0, :, 768:1024] = _swap_halves(a_k, HEAD_DIM).astype(BF16)
    wa_ref[0, :, 1024:1280] = w_ref[0, :, 512:768].astype(BF16)
    a_iq = w_ref[0, :, 768:896]
    tail = w_ref[0, :, 896:1024]
    lane = lax.broadcasted_iota(jnp.int32, (1, 128), 1)
    wi_ref[0, :, 0:128] = a_iq.astype(BF16)
    wi_ref[0, :, 128:256] = _swap_halves(a_iq, IDX_DIM).astype(BF16)
    wi_ref[0, :, 256:384] = jnp.where(lane < IDX_DIM, tail, 0.0).astype(BF16)
    wi_ref[0, :, 384:512] = jnp.where(lane < IDX_DIM, _swap_halves(tail, IDX_DIM), 0.0).astype(BF16)
    wi_ref[0, :, 512:640] = jnp.where(lane < N_IDX_HEADS, pltpu.roll(tail, 128 - IDX_DIM, 1), 0.0).astype(BF16)

    def rest(lo, hi):
        return w_ref[0, :, IN_HEAD + lo:IN_HEAD + hi].astype(BF16)

    w2_ref[0, :, 0:768] = rest(0, 768)
    wc_ref[0] = rest(768, 1536)
    w2_ref[0, :, 768:1280] = rest(1536, 2048)
    wg_ref[0, :, 0:256] = rest(2048, 2304)
    w2_ref[0, :, 1280:1536] = rest(2304, 2560)
    wg_ref[0, :, 256:512] = rest(2560, 2816)
    w2_ref[0, :, 1536:1792] = rest(2816, 3072)
    wm_ref[0] = rest(3072, 7168)


def _weight_layout(w_in):
    depth = w_in.shape[0]
    assert w_in.shape[1:] == (D_MODEL, IN_COLS)
    widths = (1280, 640, 768, 1792, 512, 4096)
    return pl.pallas_call(
        _weight_layout_kernel,
        grid=(depth, D_MODEL // PREP_ROWS),
        in_specs=[pl.BlockSpec((1, PREP_ROWS, IN_COLS), lambda l, r: (l, r, 0))],
        out_specs=[pl.BlockSpec((1, PREP_ROWS, n), lambda l, r: (l, r, 0)) for n in widths],
        out_shape=[jax.ShapeDtypeStruct((depth, D_MODEL, n), BF16) for n in widths],
        compiler_params=_params(2),
        name="weight_layout",
    )(w_in)


def _layer_weights(i, w_proj, conv_w, conv_b, ln_g, ln_b, w_spatial, b_spatial, b_merge, w_branch, w_out,
                   g_post, w_ple, w_ple_gate):
    row = lambda v: v[None, :]
    wa, wi, wc, w2, wg, wm = (w[i] for w in w_proj)
    return {
        "wa": wa, "wi": wi, "wc": wc, "w2": w2, "wg": wg, "wm": wm,
        "conv_w": jnp.pad(conv_w[i], ((0, 8 - CONV_WIDTH), (0, 0))),
        "conv_b": row(conv_b[i]), "ln_g": row(ln_g[i]), "ln_b": row(ln_b[i]),
        "w_spatial": w_spatial[i],
        "b_spatial": jnp.repeat(b_spatial[i].T, GROUP_DIM, axis=1),
        "b_merge": row(b_merge[i]),
        "w_branch": w_branch[i].astype(BF16), "w_out": w_out[i].astype(BF16), "g_post": row(g_post[i]),
        "w_ple": w_ple[i].astype(BF16), "w_ple_gate": w_ple_gate[i].astype(BF16),
    }


def kernel(x, p, positions, g_pre, w_in, conv_w, conv_b, ln_g, ln_b, w_spatial, b_spatial, b_merge,
           w_branch, w_out, g_post, w_ple, w_ple_gate):
    batch, s, _ = x.shape
    assert batch == 1 and s % ROW_BLOCK == 0 and s % ATT_BLOCK == 0 and ATT_BLOCK >= min(TOPK_MAX, s // 4)
    depth = w_in.shape[0]
    xs = x[0]
    tables = _rope_tables(positions[0][:, None])
    w_proj = _weight_layout(w_in)
    for i in range(depth):
        w = _layer_weights(i, w_proj, conv_w, conv_b, ln_g, ln_b, w_spatial, b_spatial, b_merge, w_branch,
                           w_out, g_post, w_ple, w_ple_gate)
        g = g_pre[i][None, :]
        qa, ka, vat, iq, ik, iwt, cq, ck, cv = _attn_proj(xs, g, w, tables)
        og_b, og_d = _local_mix(xs, g, w)
        o_a = _dsa_attention(qa, iq, iwt, ka, vat, ik)
        o_c = _sb_attention(cq, ck, cv)
        xs = _merge_out(xs, o_a, o_c, og_b, og_d, p[i][0], g, w)
    return xs[None]
```

```python
import functools

import jax
import jax.numpy as jnp
from jax import lax
from jax.experimental import pallas as pl
from jax.experimental.pallas import tpu as pltpu

D_MODEL = 1024
BRANCH_WIDTH = 256
HEAD_DIM = 64
N_HEADS = 4
N_IDX_HEADS = 4
IDX_DIM = 32
TOPK_MAX = 256
CONV_WIDTH = 3
CHUNK = 128
N_GROUPS = 4
GROUP_DIM = BRANCH_WIDTH // N_GROUPS
PLE_DIM = 256
ROPE_THETA = 10000.0
EPS = 1e-6
IDX_W_SCALE = (N_IDX_HEADS * IDX_DIM) ** -0.5
QK_SCALE = HEAD_DIM ** -0.5
LOG2_E = 1.4426950408889634

ROW_BLOCK = 256
ATT_BLOCK = 256
DSA_KEY_TILE = 1024
TIE_BLOCK = 256
COUNT_ROWS = 1024
CONV_HALO = 8
MASK_BIAS = -1e30
SCORE_FLOOR = float(jnp.finfo(jnp.float32).min)
SB_DEAD = 152.0
CAP_SLACK = 1.03
MIN_DENOMINATOR = 2.0 ** -64
VMEM_LIMIT = 56 * 1024 * 1024

BF16 = jnp.bfloat16
F32 = jnp.float32
NT_DIMS = (((1,), (1,)), ((), ()))


def _dot(a, b):
    return jnp.dot(a, b, preferred_element_type=F32)


def _dot_nt(a, b):
    return lax.dot_general(a, b, NT_DIMS, preferred_element_type=F32)


def _rmsnorm_rows(x, g):
    return x * lax.rsqrt(jnp.mean(x * x, axis=-1, keepdims=True) + EPS) * g


def _sigmoid(x):
    return 1.0 / (1.0 + jnp.exp(-x))


def _params(n_grid_dims=1):
    return pltpu.CompilerParams(
        dimension_semantics=("arbitrary",) * n_grid_dims, vmem_limit_bytes=VMEM_LIMIT)


def _full(shape):
    return pl.BlockSpec(shape, lambda i: (0,) * len(shape))


def _rows(width, block=ROW_BLOCK):
    return pl.BlockSpec((block, width), lambda i: (i, 0))


def _tile_lanes(x, period):
    lane = lax.broadcasted_iota(jnp.int32, (1, 128), 1)
    x = jnp.where(lane < period, x, 0.0)
    while period < 128:
        x = x + pltpu.roll(x, period, 1)
        period *= 2
    return x


def _rope_table_kernel(pos_ref, freq_ref, s64_ref, s32_ref, cos64_ref, sin64_ref, cos32_ref, sin32_ref):
    angle = pos_ref[...].astype(F32) * freq_ref[...]
    cos, sin = jnp.cos(angle), jnp.sin(angle)
    n64, n32 = HEAD_DIM // 2, IDX_DIM // 2
    cos_h, sin_h = _tile_lanes(cos, n64), _tile_lanes(sin, n64)
    cos64_ref[...] = jnp.concatenate([cos_h, cos_h], axis=1)
    sin64_ref[...] = jnp.concatenate([sin_h, sin_h], axis=1) * s64_ref[...]
    cos32_ref[...] = _tile_lanes(pltpu.roll(cos, 128 - n64, 1), n32)
    sin32_ref[...] = _tile_lanes(pltpu.roll(sin, 128 - n64, 1), n32) * s32_ref[...]


def _rope_sign(d, width):
    lane = jnp.arange(width)
    return jnp.where((lane % d) < d // 2, -1.0, 1.0).astype(F32)[None, :]


def _rope_tables(pos_col):
    s = pos_col.shape[0]
    inv_freq = lambda d: ROPE_THETA ** (-jnp.arange(0, d, 2, dtype=F32) / d)
    freq = jnp.concatenate([inv_freq(HEAD_DIM), inv_freq(IDX_DIM), jnp.zeros((80,), F32)])[None, :]
    s64, s32 = _rope_sign(HEAD_DIM, BRANCH_WIDTH), _rope_sign(IDX_DIM, N_IDX_HEADS * IDX_DIM)
    return pl.pallas_call(
        _rope_table_kernel,
        grid=(s // ROW_BLOCK,),
        in_specs=[_rows(1), _full((1, 128)), _full((1, 256)), _full((1, 128))],
        out_specs=[_rows(256), _rows(256), _rows(128), _rows(128)],
        out_shape=[jax.ShapeDtypeStruct((s, 256), F32), jax.ShapeDtypeStruct((s, 256), F32),
                   jax.ShapeDtypeStruct((s, 128), F32), jax.ShapeDtypeStruct((s, 128), F32)],
        compiler_params=_params(),
        name="rope_tables",
    )(pos_col, freq, s64, s32)


def _attn_proj_kernel(x_ref, g_ref, wa_ref, wi_ref, wc_ref,
                      cos64_ref, sin64_ref, cos32_ref, sin32_ref,
                      qa_ref, ka_ref, vat_ref, iq_ref, ik_ref, iwt_ref, cq_ref, ck_ref, cv_ref):
    h = _rmsnorm_rows(x_ref[...], g_ref[...]).astype(BF16)
    c64, s64 = cos64_ref[...], sin64_ref[...]
    c32, s32 = cos32_ref[...], sin32_ref[...]
    pa = _dot(h, wa_ref[...])
    qa_ref[...] = (pa[:, 0:256] * c64 + pa[:, 256:512] * s64) * (QK_SCALE * LOG2_E)
    ka_ref[...] = (pa[:, 512:768] * c64 + pa[:, 768:1024] * s64).astype(BF16)
    vat_ref[0] = pa[:, 1024:1280].T.astype(BF16)
    pi = _dot(h, wi_ref[...])
    iq_ref[...] = pi[:, 0:128] * c32 + pi[:, 128:256] * s32
    ik = pi[:, 256:384] * c32 + pi[:, 384:512] * s32
    ik_ref[...] = ik[:, 0:IDX_DIM].astype(BF16)
    iwt_ref[...] = (pi[:, 512:640] * IDX_W_SCALE).T[0:8, :]
    pc = _dot(h, wc_ref[...])
    cq_ref[...] = (pc[:, 0:256] * (QK_SCALE * LOG2_E)).astype(BF16)
    ck_ref[...] = pc[:, 256:512].astype(BF16)
    cv_ref[...] = pc[:, 512:768].astype(BF16)


def _attn_proj(x, g_pre, w, tables):
    s = x.shape[0]
    nb = s // ROW_BLOCK
    cos64, sin64, cos32, sin32 = tables
    return pl.pallas_call(
        _attn_proj_kernel,
        grid=(nb,),
        in_specs=[_rows(D_MODEL), _full((1, D_MODEL)), _full((D_MODEL, 1280)), _full((D_MODEL, 640)),
                  _full((D_MODEL, 768)), _rows(256), _rows(256), _rows(128), _rows(128)],
        out_specs=[_rows(256), _rows(256), pl.BlockSpec((1, 256, ROW_BLOCK), lambda i: (i, 0, 0)),
                   _rows(128), _rows(IDX_DIM), pl.BlockSpec((8, ROW_BLOCK), lambda i: (0, i)),
                   _rows(256), _rows(256), _rows(256)],
        out_shape=[jax.ShapeDtypeStruct((s, 256), F32), jax.ShapeDtypeStruct((s, 256), BF16),
                   jax.ShapeDtypeStruct((nb, 256, ROW_BLOCK), BF16),
                   jax.ShapeDtypeStruct((s, 128), F32), jax.ShapeDtypeStruct((s, IDX_DIM), BF16),
                   jax.ShapeDtypeStruct((8, s), F32),
                   jax.ShapeDtypeStruct((s, 256), BF16), jax.ShapeDtypeStruct((s, 256), BF16),
                   jax.ShapeDtypeStruct((s, 256), BF16)],
        compiler_params=_params(),
        name="attn_proj",
    )(x, g_pre, w["wa"], w["wi"], w["wc"], cos64, sin64, cos32, sin32)


def _local_mix_kernel(x_ref, g_ref, w_ref, convw_ref, convb_ref, lng_ref, lnb_ref, ws_ref, bs_ref,
                      ogb_ref, ogd_ref, ypad_ref):
    t = x_ref.shape[0]

    @pl.when(pl.program_id(0) == 0)
    def _():
        ypad_ref[0:CONV_HALO, :] = jnp.zeros((CONV_HALO, BRANCH_WIDTH), F32)

    h = _rmsnorm_rows(x_ref[...], g_ref[...]).astype(BF16)
    pr = _dot(h, w_ref[...])
    gate_b, gate_c, x_in = pr[:, 0:256], pr[:, 256:512], pr[:, 512:768]
    d_u, d_v = pr[:, 768:1024], pr[:, 1024:1280]
    silu_b, silu_d = pr[:, 1280:1536], pr[:, 1536:1792]

    y = gate_c * x_in
    ypad_ref[CONV_HALO:CONV_HALO + t, :] = y
    y1 = ypad_ref[CONV_HALO - 1:CONV_HALO - 1 + t, :]
    y2 = ypad_ref[CONV_HALO - 2:CONV_HALO - 2 + t, :]
    conv = convw_ref[2:3, :] * y + convw_ref[1:2, :] * y1 + convw_ref[0:1, :] * y2
    ypad_ref[0:CONV_HALO, :] = y[t - CONV_HALO:t, :]
    o_b = gate_b * (conv + convb_ref[...])
    ogb_ref[...] = (o_b * (silu_b * _sigmoid(silu_b))).astype(BF16)

    mu = jnp.mean(d_v, axis=-1, keepdims=True)
    dc = d_v - mu
    var = jnp.mean(dc * dc, axis=-1, keepdims=True)
    vn = dc * lax.rsqrt(var + EPS) * lng_ref[...] + lnb_ref[...]
    group = lax.broadcasted_iota(jnp.int32, (1, BRANCH_WIDTH), 1) // GROUP_DIM
    tril = (lax.broadcasted_iota(jnp.int32, (CHUNK, CHUNK), 0)
            >= lax.broadcasted_iota(jnp.int32, (CHUNK, CHUNK), 1))
    wm = [jnp.where(tril, ws_ref[g], 0.0).astype(BF16) for g in range(N_GROUPS)]
    mixed = []
    for c in range(t // CHUNK):
        vc = vn[c * CHUNK:(c + 1) * CHUNK, :]
        m = bs_ref[...]
        for g in range(N_GROUPS):
            m = m + _dot(wm[g], jnp.where(group == g, vc, 0.0).astype(BF16))
        mixed.append(m)
    o_d = d_u * jnp.concatenate(mixed, axis=0)
    ogd_ref[...] = (o_d * (silu_d * _sigmoid(silu_d))).astype(BF16)


def _local_mix(x, g_pre, w):
    s = x.shape[0]
    return pl.pallas_call(
        _local_mix_kernel,
        grid=(s // ROW_BLOCK,),
        in_specs=[_rows(D_MODEL), _full((1, D_MODEL)), _full((D_MODEL, 1792)), _full((8, 256)),
                  _full((1, 256)), _full((1, 256)), _full((1, 256)),
                  _full((N_GROUPS, CHUNK, CHUNK)), _full((CHUNK, 256))],
        out_specs=[_rows(256), _rows(256)],
        out_shape=[jax.ShapeDtypeStruct((s, 256), BF16), jax.ShapeDtypeStruct((s, 256), BF16)],
        scratch_shapes=[pltpu.VMEM((ROW_BLOCK + CONV_HALO, BRANCH_WIDTH), F32)],
        compiler_params=_params(),
        name="local_mix",
    )(x, g_pre, w["w2"], w["conv_w"], w["conv_b"], w["ln_g"], w["ln_b"], w["w_spatial"], w["b_spatial"])


I16 = jnp.int16
I16_MIN = -32768
PACK_ROWS = 16
ACC_ROWS = HEAD_DIM + PACK_ROWS


def _sortable_halves(score):
    bits = lax.bitcast_convert_type(score, jnp.int32)
    key = bits ^ ((bits >> 31) & jnp.int32(0x7FFFFFFF))
    hi = (key >> 16).astype(I16)
    lo = ((key & jnp.int32(0xFFFF)) + jnp.int32(I16_MIN)).astype(I16)
    return hi, lo


def _rows16(row32):
    return jnp.broadcast_to(row32, (PACK_ROWS, row32.shape[1])).astype(I16)


def _count_pass(ref, n_steps, cand):
    q = ref.shape[1]
    cand16 = _rows16(cand)
    one, zero = jnp.ones((PACK_ROWS, q), I16), jnp.zeros((PACK_ROWS, q), I16)
    n_acc = 4

    def body(c, accs):
        accs = list(accs)
        base = pl.multiple_of(c * COUNT_ROWS, COUNT_ROWS)
        blk = ref[pl.ds(base, COUNT_ROWS), :]
        for r in range(COUNT_ROWS // PACK_ROWS):
            v = blk[PACK_ROWS * r:PACK_ROWS * (r + 1), :]
            accs[r % n_acc] = accs[r % n_acc] + jnp.where(v >= cand16, one, zero)
        return tuple(accs)

    assert ref.shape[0] // (PACK_ROWS * n_acc) < 2 ** 15
    accs = lax.fori_loop(0, n_steps, body, (zero,) * n_acc)
    total = sum(a.astype(jnp.int32) for a in accs)
    return jnp.sum(total, axis=0, keepdims=True)


def _bisect16(ref, n_steps, target):
    zero = jnp.zeros_like(target)

    def probe(cand, t, above):
        cnt = _count_pass(ref, n_steps, cand)
        enough = cnt >= target
        return jnp.where(enough, cand, t), jnp.where(enough, above, cnt)

    def step(b, state):
        t, above = state
        return probe(t | (jnp.int32(1) << (14 - b)), t, above)

    return lax.fori_loop(0, 15, step, probe(zero, jnp.full_like(target, I16_MIN), zero))


def _dsa_kernel(q_ref, iq_ref, iwt_ref, k_ref, vt_ref, ik_ref, o_ref,
                hi_ref, lo_ref, qm_ref, iqt_ref, acc_ref, m_ref, knorm_ref, *, topk):
    nq = q_ref.shape[0]
    kb = DSA_KEY_TILE
    i = pl.program_id(0)
    n_tiles = (i * nq + nq + kb - 1) // kb
    t_idx = i * nq + lax.broadcasted_iota(jnp.int32, (1, nq), 1)
    row = lax.broadcasted_iota(jnp.int32, (kb, 1), 0)

    qt = q_ref[...].T
    head_of_row = lax.broadcasted_iota(jnp.int32, (BRANCH_WIDTH, 1), 0) // HEAD_DIM
    for h in range(N_HEADS):
        qm_ref[h] = jnp.where(head_of_row == h, qt, 0.0).astype(BF16)
    iqt_ref[...] = iq_ref[...].T.astype(BF16)

    @pl.when(i == 0)
    def _():
        head_of_lane = (lax.broadcasted_iota(jnp.int32, (BRANCH_WIDTH, 1), 0) // HEAD_DIM
                        == lax.broadcasted_iota(jnp.int32, (1, 128), 1)).astype(BF16)

        def widest(c, best):
            kf = k_ref[pl.ds(pl.multiple_of(c * kb, kb), kb), :].astype(F32)
            return jnp.maximum(best, jnp.max(_dot((kf * kf).astype(BF16), head_of_lane), axis=0, keepdims=True))

        best = lax.fori_loop(0, k_ref.shape[0] // kb, widest, jnp.zeros((1, 128), F32))
        knorm_ref[...] = jnp.broadcast_to(best, knorm_ref.shape)

    logit_cap = jnp.zeros((1, nq), F32)
    for h in range(N_HEADS):
        qf = qm_ref[h].astype(F32)
        q_sq = jnp.sum(qf * qf, axis=0, keepdims=True)
        logit_cap = jnp.maximum(logit_cap, jnp.sqrt(q_sq * knorm_ref[0:1, h:h + 1]) * CAP_SLACK + CAP_SLACK)
    w_rows = [iwt_ref[h:h + 1, :] for h in range(N_IDX_HEADS)]

    def score_tile(j, carry, on_diagonal):
        base = pl.multiple_of(j * kb, kb)
        ikb = ik_ref[pl.ds(base, kb), :]
        sc = w_rows[0] * jnp.maximum(_dot(ikb, iqt_ref[0:IDX_DIM, :]), 0.0)
        for h in range(1, N_IDX_HEADS):
            logit = _dot(ikb, iqt_ref[h * IDX_DIM:(h + 1) * IDX_DIM, :])
            sc = sc + w_rows[h] * jnp.maximum(logit, 0.0)
        if on_diagonal:
            sc = jnp.where(base + row <= t_idx, sc, SCORE_FLOOR)
        hi_ref[pl.ds(base, kb), :], lo_ref[pl.ds(base, kb), :] = _sortable_halves(sc)
        return carry

    n_below = (i * nq) // kb
    lax.fori_loop(0, n_below, functools.partial(score_tile, on_diagonal=False), 0)
    lax.fori_loop(n_below, n_tiles, functools.partial(score_tile, on_diagonal=True), 0)

    n_steps = n_tiles * (kb // COUNT_ROWS)
    want = jnp.full((1, nq), topk, jnp.int32)
    t_hi, above_hi = _bisect16(hi_ref, n_steps, want)
    t_hi16 = _rows16(t_hi)
    floor16 = jnp.full((PACK_ROWS, nq), I16_MIN, I16)

    def keep_bucket(c, carry):
        base = pl.multiple_of(c * COUNT_ROWS, COUNT_ROWS)
        hi, lo = hi_ref[pl.ds(base, COUNT_ROWS), :], lo_ref[pl.ds(base, COUNT_ROWS), :]
        lo_ref[pl.ds(base, COUNT_ROWS), :] = jnp.concatenate(
            [jnp.where(hi[PACK_ROWS * r:PACK_ROWS * (r + 1), :] == t_hi16,
                       lo[PACK_ROWS * r:PACK_ROWS * (r + 1), :], floor16)
             for r in range(COUNT_ROWS // PACK_ROWS)], axis=0)
        return carry

    lax.fori_loop(0, n_steps, keep_bucket, 0)
    t_lo, above_lo = _bisect16(lo_ref, n_steps, want - above_hi)
    quota = (topk - above_hi - above_lo).astype(F32)
    t_hi = jnp.where(t_idx < topk, jnp.int32(I16_MIN), t_hi)
    t_hi16, t_lo16 = _rows16(t_hi), _rows16(t_lo)

    incl_lower = (lax.broadcasted_iota(jnp.int32, (TIE_BLOCK, TIE_BLOCK), 0)
                  >= lax.broadcasted_iota(jnp.int32, (TIE_BLOCK, TIE_BLOCK), 1)).astype(BF16)
    one16, zero16 = jnp.ones((PACK_ROWS, nq), BF16), jnp.zeros((PACK_ROWS, nq), BF16)

    def attend_tile(j, ties_before, on_diagonal, running_max):
        base = pl.multiple_of(j * kb, kb)
        hi_t, lo_t = hi_ref[pl.ds(base, kb), :], lo_ref[pl.ds(base, kb), :]
        above, tie = [], []
        for r in range(kb // PACK_ROWS):
            hi16 = hi_t[PACK_ROWS * r:PACK_ROWS * (r + 1), :]
            lo16 = lo_t[PACK_ROWS * r:PACK_ROWS * (r + 1), :]
            bucket = hi16 == t_hi16
            above.append(jnp.where((hi16 > t_hi16) | (bucket & (lo16 > t_lo16)), one16, zero16))
            tie.append(jnp.where(bucket & (lo16 == t_lo16), one16, zero16))
        above, tie = jnp.concatenate(above, axis=0), jnp.concatenate(tie, axis=0)
        parts, running = [], ties_before
        for u in range(kb // TIE_BLOCK):
            parts.append(_dot(incl_lower, tie[u * TIE_BLOCK:(u + 1) * TIE_BLOCK, :]) + running)
            running = parts[-1][TIE_BLOCK - 1:TIE_BLOCK, :]
        ties_upto = jnp.concatenate(parts, axis=0)
        keep = above + tie * jnp.where(ties_upto <= quota, 1.0, 0.0).astype(BF16)
        if on_diagonal:
            keep = keep * jnp.where(base + row <= t_idx, 1.0, 0.0).astype(BF16)
        bias = ((keep - 1.0) * (-MASK_BIAS)).astype(F32)
        kt = k_ref[pl.ds(base, kb), :]
        n_sub = kb // ATT_BLOCK
        vts = [vt_ref[j * n_sub + u] for u in range(n_sub)]
        ones_rows = jnp.ones((PACK_ROWS, ATT_BLOCK), BF16)

        def value_matmul(h, p):
            hs = slice(h * HEAD_DIM, (h + 1) * HEAD_DIM)
            pb = p.astype(BF16)
            return sum(_dot(jnp.concatenate([vts[u][hs, :], ones_rows], axis=0),
                            pb[u * ATT_BLOCK:(u + 1) * ATT_BLOCK, :]) for u in range(n_sub))

        if running_max:
            logits = [_dot(kt, qm_ref[h]) + bias for h in range(N_HEADS)]
            m_old = [m_ref[h:h + 1, :] for h in range(N_HEADS)]
            m_new = [jnp.maximum(m_old[h], jnp.max(logits[h], axis=0, keepdims=True)) for h in range(N_HEADS)]
            alpha = [jnp.exp2(m_old[h] - m_new[h]) for h in range(N_HEADS)]
            probs = [jnp.exp2(logits[h] - m_new[h]) for h in range(N_HEADS)]
            for h in range(N_HEADS):
                m_ref[h:h + 1, :] = m_new[h]
                rows = slice(h * ACC_ROWS, (h + 1) * ACC_ROWS)
                acc_ref[rows, :] = alpha[h] * acc_ref[rows, :] + value_matmul(h, probs[h])
        else:
            shift = bias - logit_cap
            probs = [jnp.exp2(_dot(kt, qm_ref[h]) + shift) for h in range(N_HEADS)]
            for h in range(N_HEADS):
                rows = slice(h * ACC_ROWS, (h + 1) * ACC_ROWS)
                acc_ref[rows, :] += value_matmul(h, probs[h])
        return ties_upto[kb - 1:kb, :]

    n_below = (i * nq) // kb

    def attend(running_max):
        acc_ref[...] = jnp.zeros(acc_ref.shape, F32)
        below = functools.partial(attend_tile, on_diagonal=False, running_max=running_max)
        diagonal = functools.partial(attend_tile, on_diagonal=True, running_max=running_max)
        lax.fori_loop(n_below, n_tiles, diagonal, lax.fori_loop(0, n_below, below, jnp.zeros((1, nq), F32)))

    attend(running_max=False)
    denominators = jnp.concatenate(
        [acc_ref[h * ACC_ROWS + HEAD_DIM:h * ACC_ROWS + HEAD_DIM + 1, :] for h in range(N_HEADS)], axis=0)
    weakest = jnp.min(jnp.min(denominators, axis=1, keepdims=True), axis=0, keepdims=True)[0, 0]

    @pl.when(jnp.logical_not(weakest >= MIN_DENOMINATOR))
    def _():
        m_ref[...] = jnp.full(m_ref.shape, MASK_BIAS, F32)
        attend(running_max=True)

    heads = [acc_ref[h * ACC_ROWS:h * ACC_ROWS + HEAD_DIM, :]
             / acc_ref[h * ACC_ROWS + HEAD_DIM:h * ACC_ROWS + HEAD_DIM + 1, :] for h in range(N_HEADS)]
    o_ref[...] = jnp.concatenate(heads, axis=0).T


def _dsa_attention(qa, iq, iwt, ka, vat, ik):
    s = qa.shape[0]
    nq = ATT_BLOCK
    topk = min(TOPK_MAX, s // 4)
    assert s % DSA_KEY_TILE == 0 and DSA_KEY_TILE % ATT_BLOCK == 0
    return pl.pallas_call(
        functools.partial(_dsa_kernel, topk=topk),
        grid=(s // nq,),
        in_specs=[_rows(256, nq), _rows(128, nq), pl.BlockSpec((8, nq), lambda i: (0, i)),
                  _full((s, 256)), _full((s // ATT_BLOCK, 256, ATT_BLOCK)), _full((s, IDX_DIM))],
        out_specs=_rows(256, nq),
        out_shape=jax.ShapeDtypeStruct((s, 256), F32),
        scratch_shapes=[pltpu.VMEM((s, nq), I16), pltpu.VMEM((s, nq), I16), pltpu.VMEM((N_HEADS, 256, nq), BF16),
                        pltpu.VMEM((128, nq), BF16), pltpu.VMEM((N_HEADS * ACC_ROWS, nq), F32),
                        pltpu.VMEM((8, nq), F32), pltpu.VMEM((8, 128), F32)],
        compiler_params=_params(),
        name="dsa_attn",
    )(qa, iq, iwt, ka, vat, ik)


def _sb_kernel(q_ref, k_ref, v_ref, o_ref, acc_ref, carry_ref):
    nq = q_ref.shape[0]
    kb = ATT_BLOCK
    i = pl.program_id(0)
    q = q_ref[...]
    lane_head = lax.broadcasted_iota(jnp.int32, (1, BRANCH_WIDTH), 1) // HEAD_DIM
    qh = [jnp.where(lane_head == h, q, jnp.zeros_like(q)) for h in range(N_HEADS)]
    later = (lax.broadcasted_iota(jnp.int32, (2 * kb, kb), 0) % kb
             > lax.broadcasted_iota(jnp.int32, (2 * kb, kb), 1)).astype(BF16)
    acc_ref[...] = jnp.zeros(acc_ref.shape, F32)
    carry_ref[...] = jnp.zeros(carry_ref.shape, F32)

    def walk_tile(j, on_diagonal):
        base = pl.multiple_of(j * kb, kb)
        kt = k_ref[pl.ds(base, kb), :]
        vt = v_ref[pl.ds(base, kb), :]
        if on_diagonal:
            strict = (lax.broadcasted_iota(jnp.int32, (1, kb), 1)
                      < lax.broadcasted_iota(jnp.int32, (nq, 1), 0))
        heads = range(N_HEADS)
        z = [_dot_nt(qh[h], kt) for h in heads]
        softplus = [jnp.maximum(z[h], 0.0) + jnp.log2(1.0 + jnp.exp2(-jnp.abs(z[h]))) for h in heads]
        sp = [jnp.where(strict, softplus[h], 0.0) for h in heads] if on_diagonal else softplus
        hi = [sp[h].astype(BF16) for h in heads]
        lo = [(sp[h] - hi[h].astype(F32)).astype(BF16) for h in heads]
        after = [_dot(jnp.concatenate([hi[h], lo[h]], axis=1), later) for h in heads]
        c = [carry_ref[h] for h in heads]
        wts = [jnp.exp2((z[h] - softplus[h]) - (after[h] + c[h])) for h in heads]
        if on_diagonal:
            wts = [jnp.where(strict, wts[h], 0.0) for h in heads]
        out = sum(_dot(wts[h].astype(BF16), jnp.where(lane_head == h, vt, jnp.zeros_like(vt))) for h in heads)
        acc_ref[...] += out
        lowest = jnp.full((1, 1), jnp.inf, F32)
        for h in heads:
            c_new = c[h] + (after[h][:, 0:1] + sp[h][:, 0:1])
            carry_ref[h] = c_new
            lowest = jnp.minimum(lowest, jnp.min(c_new, axis=0, keepdims=True))
        return (lowest[0, 0] < SB_DEAD).astype(jnp.int32)

    def alive(state):
        j, live = state
        return jnp.logical_and(j >= 0, live > 0)

    def walk(state):
        j, _ = state
        return j - 1, walk_tile(j, on_diagonal=False)

    lax.while_loop(alive, walk, (i - 1, walk_tile(i, on_diagonal=True)))
    o_ref[...] = acc_ref[...]


def _sb_attention(cq, ck, cv):
    s = cq.shape[0]
    nq = ATT_BLOCK
    return pl.pallas_call(
        _sb_kernel,
        grid=(s // nq,),
        in_specs=[_rows(256, nq), _full((s, 256)), _full((s, 256))],
        out_specs=_rows(256, nq),
        out_shape=jax.ShapeDtypeStruct((s, 256), F32),
        scratch_shapes=[pltpu.VMEM((nq, 256), F32), pltpu.VMEM((N_HEADS, nq, 1), F32)],
        compiler_params=_params(),
        name="sb_attn",
    )(cq, ck, cv)


def _merge_out_kernel(x_ref, oa_ref, oc_ref, ogb_ref, ogd_ref, p_ref, g_ref, wg_ref, wm_ref, bm_ref,
                      wb_ref, wo_ref, gpost_ref, wple_ref, wpg_ref, out_ref):
    x = x_ref[...]
    h = _rmsnorm_rows(x, g_ref[...]).astype(BF16)
    gates = _dot(h, wg_ref[...])
    silu = gates * _sigmoid(gates)
    branch_in = [(oa_ref[...] * silu[:, 0:256]).astype(BF16), ogb_ref[...],
                 (oc_ref[...] * silu[:, 256:512]).astype(BF16), ogd_ref[...]]
    merged = jnp.zeros((x.shape[0], D_MODEL), F32)
    for n in range(4):
        cols = slice(n * D_MODEL, (n + 1) * D_MODEL)
        gate = _sigmoid(_dot(h, wm_ref[:, cols]) + bm_ref[:, cols])
        merged = merged + gate * _dot(branch_in[n], wb_ref[n])
    y = _dot(merged.astype(BF16), wo_ref[...])
    x1 = x + _rmsnorm_rows(y, gpost_ref[...])
    ple = _dot(p_ref[...].astype(BF16), wple_ref[...])
    out_ref[...] = x1 + ple * _sigmoid(_dot(x1.astype(BF16), wpg_ref[...]))


def _merge_out(x, o_a, o_c, og_b, og_d, p_i, g_pre, w):
    s = x.shape[0]
    return pl.pallas_call(
        _merge_out_kernel,
        grid=(s // ROW_BLOCK,),
        in_specs=[_rows(D_MODEL), _rows(256), _rows(256), _rows(256), _rows(256), _rows(PLE_DIM),
                  _full((1, D_MODEL)), _full((D_MODEL, 512)), _full((D_MODEL, 4 * D_MODEL)),
                  _full((1, 4 * D_MODEL)), _full((4, BRANCH_WIDTH, D_MODEL)), _full((D_MODEL, D_MODEL)),
                  _full((1, D_MODEL)), _full((PLE_DIM, D_MODEL)), _full((D_MODEL, D_MODEL))],
        out_specs=_rows(D_MODEL),
        out_shape=jax.ShapeDtypeStruct((s, D_MODEL), F32),
        compiler_params=_params(),
        name="merge_out",
    )(x, o_a, o_c, og_b, og_d, p_i, g_pre, w["wg"], w["wm"], w["b_merge"], w["w_branch"], w["w_out"],
      w["g_post"], w["w_ple"], w["w_ple_gate"])


IN_COLS = 8100
IN_HEAD = 932
PREP_ROWS = 128


def _swap_halves(x, d):
    n = x.shape[1]
    lane = lax.broadcasted_iota(jnp.int32, (1, n), 1)
    return jnp.where(lane % d < d // 2, pltpu.roll(x, n - d // 2, 1), pltpu.roll(x, d // 2, 1))


def _weight_layout_kernel(w_ref, wa_ref, wi_ref, wc_ref, w2_ref, wg_ref, wm_ref):
    a_q, a_k = w_ref[0, :, 0:256], w_ref[0, :, 256:512]
    wa_ref[0, :, 0:256] = a_q.astype(BF16)
    wa_ref[0, :, 256:512] = _swap_halves(a_q, HEAD_DIM).astype(BF16)
    wa_ref[0, :, 512:768] = a_k.astype(BF16)
    wa_ref[0, :, 768:1024] = _swap_halves(a_k, HEAD_DIM).astype(BF16)
    wa_ref[0, :, 1024:1280] = w_ref[0, :, 512:768].astype(BF16)
    a_iq = w_ref[0, :, 768:896]
    tail = w_ref[0, :, 896:1024]
    lane = lax.broadcasted_iota(jnp.int32, (1, 128), 1)
    wi_ref[0, :, 0:128] = a_iq.astype(BF16)
    wi_ref[0, :, 128:256] = _swap_halves(a_iq, IDX_DIM).astype(BF16)
    wi_ref[0, :, 256:384] = jnp.where(lane < IDX_DIM, tail, 0.0).astype(BF16)
    wi_ref[0, :, 384:512] = jnp.where(lane < IDX_DIM, _swap_halves(tail, IDX_DIM), 0.0).astype(BF16)
    wi_ref[0, :, 512:640] = jnp.where(lane < N_IDX_HEADS, pltpu.roll(tail, 128 - IDX_DIM, 1), 0.0).astype(BF16)

    def rest(lo, hi):
        return w_ref[0, :, IN_HEAD + lo:IN_HEAD + hi].astype(BF16)

    w2_ref[0, :, 0:768] = rest(0, 768)
    wc_ref[0] = rest(768, 1536)
    w2_ref[0, :, 768:1280] = rest(1536, 2048)
    wg_ref[0, :, 0:256] = rest(2048, 2304)
    w2_ref[0, :, 1280:1536] = rest(2304, 2560)
    wg_ref[0, :, 256:512] = rest(2560, 2816)
    w2_ref[0, :, 1536:1792] = rest(2816, 3072)
    wm_ref[0] = rest(3072, 7168)


def _weight_layout(w_in):
    depth = w_in.shape[0]
    assert w_in.shape[1:] == (D_MODEL, IN_COLS)
    widths = (1280, 640, 768, 1792, 512, 4096)
    return pl.pallas_call(
        _weight_layout_kernel,
        grid=(depth, D_MODEL // PREP_ROWS),
        in_specs=[pl.BlockSpec((1, PREP_ROWS, IN_COLS), lambda l, r: (l, r, 0))],
        out_specs=[pl.BlockSpec((1, PREP_ROWS, n), lambda l, r: (l, r, 0)) for n in widths],
        out_shape=[jax.ShapeDtypeStruct((depth, D_MODEL, n), BF16) for n in widths],
        compiler_params=_params(2),
        name="weight_layout",
    )(w_in)


def _layer_weights(i, w_proj, conv_w, conv_b, ln_g, ln_b, w_spatial, b_spatial, b_merge, w_branch, w_out,
                   g_post, w_ple, w_ple_gate):
    row = lambda v: v[None, :]
    wa, wi, wc, w2, wg, wm = (w[i] for w in w_proj)
    return {
        "wa": wa, "wi": wi, "wc": wc, "w2": w2, "wg": wg, "wm": wm,
        "conv_w": jnp.pad(conv_w[i], ((0, 8 - CONV_WIDTH), (0, 0))),
        "conv_b": row(conv_b[i]), "ln_g": row(ln_g[i]), "ln_b": row(ln_b[i]),
        "w_spatial": w_spatial[i],
        "b_spatial": jnp.repeat(b_spatial[i].T, GROUP_DIM, axis=1),
        "b_merge": row(b_merge[i]),
        "w_branch": w_branch[i].astype(BF16), "w_out": w_out[i].astype(BF16), "g_post": row(g_post[i]),
        "w_ple": w_ple[i].astype(BF16), "w_ple_gate": w_ple_gate[i].astype(BF16),
    }


def kernel(x, p, positions, g_pre, w_in, conv_w, conv_b, ln_g, ln_b, w_spatial, b_spatial, b_merge,
           w_branch, w_out, g_post, w_ple, w_ple_gate):
    batch, s, _ = x.shape
    assert batch == 1 and s % ROW_BLOCK == 0 and s % ATT_BLOCK == 0 and ATT_BLOCK >= min(TOPK_MAX, s // 4)
    depth = w_in.shape[0]
    xs = x[0]
    tables = _rope_tables(positions[0][:, None])
    w_proj = _weight_layout(w_in)
    for i in range(depth):
        w = _layer_weights(i, w_proj, conv_w, conv_b, ln_g, ln_b, w_spatial, b_spatial, b_merge, w_branch,
                           w_out, g_post, w_ple, w_ple_gate)
        g = g_pre[i][None, :]
        qa, ka, vat, iq, ik, iwt, cq, ck, cv = _attn_proj(xs, g, w, tables)
        og_b, og_d = _local_mix(xs, g, w)
        o_a = _dsa_attention(qa, iq, iwt, ka, vat, ik)
        o_c = _sb_attention(cq, ck, cv)
        xs = _merge_out(xs, o_a, o_c, og_b, og_d, p[i][0], g, w)
    return xs[None]
```

```python
import functools

import jax
import jax.numpy as jnp
from jax import lax
from jax.experimental import pallas as pl
from jax.experimental.pallas import tpu as pltpu

D_MODEL = 1024
BRANCH_WIDTH = 256
HEAD_DIM = 64
N_HEADS = 4
N_IDX_HEADS = 4
IDX_DIM = 32
TOPK_MAX = 256
CONV_WIDTH = 3
CHUNK = 128
N_GROUPS = 4
GROUP_DIM = BRANCH_WIDTH // N_GROUPS
PLE_DIM = 256
ROPE_THETA = 10000.0
EPS = 1e-6
IDX_W_SCALE = (N_IDX_HEADS * IDX_DIM) ** -0.5
QK_SCALE = HEAD_DIM ** -0.5
LOG2_E = 1.4426950408889634

ROW_BLOCK = 256
ATT_BLOCK = 256
DSA_KEY_TILE = 1024
TIE_BLOCK = 256
COUNT_ROWS = 1024
TAIL_ROWS = 256
CONV_HALO = 8
MASK_BIAS = -1e30
SCORE_FLOOR = float(jnp.finfo(jnp.float32).min)
SB_DEAD = 152.0
CAP_SLACK = 1.03
MIN_DENOMINATOR = 2.0 ** -64
VMEM_LIMIT = 56 * 1024 * 1024

BF16 = jnp.bfloat16
F32 = jnp.float32
NT_DIMS = (((1,), (1,)), ((), ()))


def _dot(a, b):
    return jnp.dot(a, b, preferred_element_type=F32)


def _dot_nt(a, b):
    return lax.dot_general(a, b, NT_DIMS, preferred_element_type=F32)


def _rmsnorm_rows(x, g):
    return x * lax.rsqrt(jnp.mean(x * x, axis=-1, keepdims=True) + EPS) * g


def _sigmoid(x):
    return 1.0 / (1.0 + jnp.exp(-x))


def _params(n_grid_dims=1):
    return pltpu.CompilerParams(
        dimension_semantics=("arbitrary",) * n_grid_dims, vmem_limit_bytes=VMEM_LIMIT)


def _full(shape):
    return pl.BlockSpec(shape, lambda i: (0,) * len(shape))


def _rows(width, block=ROW_BLOCK):
    return pl.BlockSpec((block, width), lambda i: (i, 0))


def _tile_lanes(x, period):
    lane = lax.broadcasted_iota(jnp.int32, (1, 128), 1)
    x = jnp.where(lane < period, x, 0.0)
    while period < 128:
        x = x + pltpu.roll(x, period, 1)
        period *= 2
    return x


def _rope_table_kernel(pos_ref, freq_ref, s64_ref, s32_ref, cos64_ref, sin64_ref, cos32_ref, sin32_ref):
    angle = pos_ref[...].astype(F32) * freq_ref[...]
    cos, sin = jnp.cos(angle), jnp.sin(angle)
    n64, n32 = HEAD_DIM // 2, IDX_DIM // 2
    cos_h, sin_h = _tile_lanes(cos, n64), _tile_lanes(sin, n64)
    cos64_ref[...] = jnp.concatenate([cos_h, cos_h], axis=1)
    sin64_ref[...] = jnp.concatenate([sin_h, sin_h], axis=1) * s64_ref[...]
    cos32_ref[...] = _tile_lanes(pltpu.roll(cos, 128 - n64, 1), n32)
    sin32_ref[...] = _tile_lanes(pltpu.roll(sin, 128 - n64, 1), n32) * s32_ref[...]


def _rope_sign(d, width):
    lane = jnp.arange(width)
    return jnp.where((lane % d) < d // 2, -1.0, 1.0).astype(F32)[None, :]


def _rope_tables(pos_col):
    s = pos_col.shape[0]
    inv_freq = lambda d: ROPE_THETA ** (-jnp.arange(0, d, 2, dtype=F32) / d)
    freq = jnp.concatenate([inv_freq(HEAD_DIM), inv_freq(IDX_DIM), jnp.zeros((80,), F32)])[None, :]
    s64, s32 = _rope_sign(HEAD_DIM, BRANCH_WIDTH), _rope_sign(IDX_DIM, N_IDX_HEADS * IDX_DIM)
    return pl.pallas_call(
        _rope_table_kernel,
        grid=(s // ROW_BLOCK,),
        in_specs=[_rows(1), _full((1, 128)), _full((1, 256)), _full((1, 128))],
        out_specs=[_rows(256), _rows(256), _rows(128), _rows(128)],
        out_shape=[jax.ShapeDtypeStruct((s, 256), F32), jax.ShapeDtypeStruct((s, 256), F32),
                   jax.ShapeDtypeStruct((s, 128), F32), jax.ShapeDtypeStruct((s, 128), F32)],
        compiler_params=_params(),
        name="rope_tables",
    )(pos_col, freq, s64, s32)


def _attn_proj_kernel(x_ref, g_ref, wa_ref, wi_ref, wc_ref,
                      cos64_ref, sin64_ref, cos32_ref, sin32_ref,
                      qa_ref, ka_ref, vat_ref, iq_ref, ik_ref, iwt_ref, cq_ref, ck_ref, cv_ref):
    h = _rmsnorm_rows(x_ref[...], g_ref[...]).astype(BF16)
    c64, s64 = cos64_ref[...], sin64_ref[...]
    c32, s32 = cos32_ref[...], sin32_ref[...]
    pa = _dot(h, wa_ref[...])
    qa_ref[...] = (pa[:, 0:256] * c64 + pa[:, 256:512] * s64) * (QK_SCALE * LOG2_E)
    ka_ref[...] = (pa[:, 512:768] * c64 + pa[:, 768:1024] * s64).astype(BF16)
    vat_ref[0] = pa[:, 1024:1280].T.astype(BF16)
    pi = _dot(h, wi_ref[...])
    iq_ref[...] = pi[:, 0:128] * c32 + pi[:, 128:256] * s32
    ik = pi[:, 256:384] * c32 + pi[:, 384:512] * s32
    ik_ref[...] = ik[:, 0:IDX_DIM].astype(BF16)
    iwt_ref[...] = (pi[:, 512:640] * IDX_W_SCALE).T[0:8, :]
    pc = _dot(h, wc_ref[...])
    cq_ref[...] = (pc[:, 0:256] * (QK_SCALE * LOG2_E)).astype(BF16)
    ck_ref[...] = pc[:, 256:512].astype(BF16)
    cv_ref[...] = pc[:, 512:768].astype(BF16)


def _attn_proj(x, g_pre, w, tables):
    s = x.shape[0]
    nb = s // ROW_BLOCK
    cos64, sin64, cos32, sin32 = tables
    return pl.pallas_call(
        _attn_proj_kernel,
        grid=(nb,),
        in_specs=[_rows(D_MODEL), _full((1, D_MODEL)), _full((D_MODEL, 1280)), _full((D_MODEL, 640)),
                  _full((D_MODEL, 768)), _rows(256), _rows(256), _rows(128), _rows(128)],
        out_specs=[_rows(256), _rows(256), pl.BlockSpec((1, 256, ROW_BLOCK), lambda i: (i, 0, 0)),
                   _rows(128), _rows(IDX_DIM), pl.BlockSpec((8, ROW_BLOCK), lambda i: (0, i)),
                   _rows(256), _rows(256), _rows(256)],
        out_shape=[jax.ShapeDtypeStruct((s, 256), F32), jax.ShapeDtypeStruct((s, 256), BF16),
                   jax.ShapeDtypeStruct((nb, 256, ROW_BLOCK), BF16),
                   jax.ShapeDtypeStruct((s, 128), F32), jax.ShapeDtypeStruct((s, IDX_DIM), BF16),
                   jax.ShapeDtypeStruct((8, s), F32),
                   jax.ShapeDtypeStruct((s, 256), BF16), jax.ShapeDtypeStruct((s, 256), BF16),
                   jax.ShapeDtypeStruct((s, 256), BF16)],
        compiler_params=_params(),
        name="attn_proj",
    )(x, g_pre, w["wa"], w["wi"], w["wc"], cos64, sin64, cos32, sin32)


def _local_mix_kernel(x_ref, g_ref, w_ref, convw_ref, convb_ref, lng_ref, lnb_ref, ws_ref, bs_ref,
                      ogb_ref, ogd_ref, ypad_ref):
    t = x_ref.shape[0]

    @pl.when(pl.program_id(0) == 0)
    def _():
        ypad_ref[0:CONV_HALO, :] = jnp.zeros((CONV_HALO, BRANCH_WIDTH), F32)

    h = _rmsnorm_rows(x_ref[...], g_ref[...]).astype(BF16)
    pr = _dot(h, w_ref[...])
    gate_b, gate_c, x_in = pr[:, 0:256], pr[:, 256:512], pr[:, 512:768]
    d_u, d_v = pr[:, 768:1024], pr[:, 1024:1280]
    silu_b, silu_d = pr[:, 1280:1536], pr[:, 1536:1792]

    y = gate_c * x_in
    ypad_ref[CONV_HALO:CONV_HALO + t, :] = y
    y1 = ypad_ref[CONV_HALO - 1:CONV_HALO - 1 + t, :]
    y2 = ypad_ref[CONV_HALO - 2:CONV_HALO - 2 + t, :]
    conv = convw_ref[2:3, :] * y + convw_ref[1:2, :] * y1 + convw_ref[0:1, :] * y2
    ypad_ref[0:CONV_HALO, :] = y[t - CONV_HALO:t, :]
    o_b = gate_b * (conv + convb_ref[...])
    ogb_ref[...] = (o_b * (silu_b * _sigmoid(silu_b))).astype(BF16)

    mu = jnp.mean(d_v, axis=-1, keepdims=True)
    dc = d_v - mu
    var = jnp.mean(dc * dc, axis=-1, keepdims=True)
    vn = dc * lax.rsqrt(var + EPS) * lng_ref[...] + lnb_ref[...]
    group = lax.broadcasted_iota(jnp.int32, (1, BRANCH_WIDTH), 1) // GROUP_DIM
    tril = (lax.broadcasted_iota(jnp.int32, (CHUNK, CHUNK), 0)
            >= lax.broadcasted_iota(jnp.int32, (CHUNK, CHUNK), 1))
    wm = [jnp.where(tril, ws_ref[g], 0.0).astype(BF16) for g in range(N_GROUPS)]
    mixed = []
    for c in range(t // CHUNK):
        vc = vn[c * CHUNK:(c + 1) * CHUNK, :]
        m = bs_ref[...]
        for g in range(N_GROUPS):
            m = m + _dot(wm[g], jnp.where(group == g, vc, 0.0).astype(BF16))
        mixed.append(m)
    o_d = d_u * jnp.concatenate(mixed, axis=0)
    ogd_ref[...] = (o_d * (silu_d * _sigmoid(silu_d))).astype(BF16)


def _local_mix(x, g_pre, w):
    s = x.shape[0]
    return pl.pallas_call(
        _local_mix_kernel,
        grid=(s // ROW_BLOCK,),
        in_specs=[_rows(D_MODEL), _full((1, D_MODEL)), _full((D_MODEL, 1792)), _full((8, 256)),
                  _full((1, 256)), _full((1, 256)), _full((1, 256)),
                  _full((N_GROUPS, CHUNK, CHUNK)), _full((CHUNK, 256))],
        out_specs=[_rows(256), _rows(256)],
        out_shape=[jax.ShapeDtypeStruct((s, 256), BF16), jax.ShapeDtypeStruct((s, 256), BF16)],
        scratch_shapes=[pltpu.VMEM((ROW_BLOCK + CONV_HALO, BRANCH_WIDTH), F32)],
        compiler_params=_params(),
        name="local_mix",
    )(x, g_pre, w["w2"], w["conv_w"], w["conv_b"], w["ln_g"], w["ln_b"], w["w_spatial"], w["b_spatial"])


I16 = jnp.int16
I16_MIN = -32768
PACK_ROWS = 16
ACC_ROWS = HEAD_DIM + PACK_ROWS


def _sortable_halves(score):
    bits = lax.bitcast_convert_type(score, jnp.int32)
    key = bits ^ ((bits >> 31) & jnp.int32(0x7FFFFFFF))
    hi = (key >> 16).astype(I16)
    lo = ((key & jnp.int32(0xFFFF)) + jnp.int32(I16_MIN)).astype(I16)
    return hi, lo


def _rows16(row32):
    return jnp.broadcast_to(row32, (PACK_ROWS, row32.shape[1])).astype(I16)


def _over_rows(steps, fn, carry):
    n_big, n_tail = steps
    carry = lax.fori_loop(
        0, n_big, lambda c, x: fn(pl.multiple_of(c * COUNT_ROWS, COUNT_ROWS), COUNT_ROWS, x), carry)
    tail_start = n_big * COUNT_ROWS
    return lax.fori_loop(
        0, n_tail, lambda c, x: fn(pl.multiple_of(tail_start + c * TAIL_ROWS, TAIL_ROWS), TAIL_ROWS, x), carry)


def _count_pass(ref, steps, cand):
    q = ref.shape[1]
    cand16 = _rows16(cand)
    one, zero = jnp.ones((PACK_ROWS, q), I16), jnp.zeros((PACK_ROWS, q), I16)
    n_acc = 4

    def body(base, rows, accs):
        accs = list(accs)
        blk = ref[pl.ds(base, rows), :]
        for r in range(rows // PACK_ROWS):
            v = blk[PACK_ROWS * r:PACK_ROWS * (r + 1), :]
            accs[r % n_acc] = accs[r % n_acc] + jnp.where(v >= cand16, one, zero)
        return tuple(accs)

    assert ref.shape[0] // (PACK_ROWS * n_acc) < 2 ** 15
    accs = _over_rows(steps, body, (zero,) * n_acc)
    total = sum(a.astype(jnp.int32) for a in accs)
    return jnp.sum(total, axis=0, keepdims=True)


def _bisect16(ref, steps, target):
    zero = jnp.zeros_like(target)

    def probe(cand, t, above):
        cnt = _count_pass(ref, steps, cand)
        enough = cnt >= target
        return jnp.where(enough, cand, t), jnp.where(enough, above, cnt)

    def step(b, state):
        t, above = state
        return probe(t | (jnp.int32(1) << (14 - b)), t, above)

    return lax.fori_loop(0, 15, step, probe(zero, jnp.full_like(target, I16_MIN), zero))


def _dsa_kernel(q_ref, iq_ref, iwt_ref, k_ref, vt_ref, ik_ref, o_ref,
                hi_ref, lo_ref, qm_ref, iqt_ref, acc_ref, m_ref, knorm_ref, *, topk):
    nq = q_ref.shape[0]
    kb = DSA_KEY_TILE
    i = pl.program_id(0)
    n_tiles = (i * nq + nq + kb - 1) // kb
    t_idx = i * nq + lax.broadcasted_iota(jnp.int32, (1, nq), 1)
    row = lax.broadcasted_iota(jnp.int32, (kb, 1), 0)

    qt = q_ref[...].T
    head_of_row = lax.broadcasted_iota(jnp.int32, (BRANCH_WIDTH, 1), 0) // HEAD_DIM
    for h in range(N_HEADS):
        qm_ref[h] = jnp.where(head_of_row == h, qt, 0.0).astype(BF16)
    iqt_ref[...] = iq_ref[...].T.astype(BF16)

    @pl.when(i == 0)
    def _():
        head_of_lane = (lax.broadcasted_iota(jnp.int32, (BRANCH_WIDTH, 1), 0) // HEAD_DIM
                        == lax.broadcasted_iota(jnp.int32, (1, 128), 1)).astype(BF16)

        def widest(c, best):
            kf = k_ref[pl.ds(pl.multiple_of(c * kb, kb), kb), :].astype(F32)
            return jnp.maximum(best, jnp.max(_dot((kf * kf).astype(BF16), head_of_lane), axis=0, keepdims=True))

        best = lax.fori_loop(0, k_ref.shape[0] // kb, widest, jnp.zeros((1, 128), F32))
        knorm_ref[...] = jnp.broadcast_to(best, knorm_ref.shape)

    logit_cap = jnp.zeros((1, nq), F32)
    for h in range(N_HEADS):
        qf = qm_ref[h].astype(F32)
        q_sq = jnp.sum(qf * qf, axis=0, keepdims=True)
        logit_cap = jnp.maximum(logit_cap, jnp.sqrt(q_sq * knorm_ref[0:1, h:h + 1]) * CAP_SLACK + CAP_SLACK)
    w_rows = [iwt_ref[h:h + 1, :] for h in range(N_IDX_HEADS)]

    def score_tile(j, carry, on_diagonal):
        base = pl.multiple_of(j * kb, kb)
        ikb = ik_ref[pl.ds(base, kb), :]
        sc = w_rows[0] * jnp.maximum(_dot(ikb, iqt_ref[0:IDX_DIM, :]), 0.0)
        for h in range(1, N_IDX_HEADS):
            logit = _dot(ikb, iqt_ref[h * IDX_DIM:(h + 1) * IDX_DIM, :])
            sc = sc + w_rows[h] * jnp.maximum(logit, 0.0)
        if on_diagonal:
            sc = jnp.where(base + row <= t_idx, sc, SCORE_FLOOR)
        hi_ref[pl.ds(base, kb), :], lo_ref[pl.ds(base, kb), :] = _sortable_halves(sc)
        return carry

    n_below = (i * nq) // kb
    lax.fori_loop(0, n_below, functools.partial(score_tile, on_diagonal=False), 0)
    lax.fori_loop(n_below, n_tiles, functools.partial(score_tile, on_diagonal=True), 0)

    causal_rows = (i + 1) * nq
    steps = (causal_rows // COUNT_ROWS, (causal_rows % COUNT_ROWS) // TAIL_ROWS)
    want = jnp.full((1, nq), topk, jnp.int32)
    t_hi, above_hi = _bisect16(hi_ref, steps, want)
    t_hi16 = _rows16(t_hi)
    floor16 = jnp.full((PACK_ROWS, nq), I16_MIN, I16)

    def keep_bucket(base, rows, carry):
        hi, lo = hi_ref[pl.ds(base, rows), :], lo_ref[pl.ds(base, rows), :]
        lo_ref[pl.ds(base, rows), :] = jnp.concatenate(
            [jnp.where(hi[PACK_ROWS * r:PACK_ROWS * (r + 1), :] == t_hi16,
                       lo[PACK_ROWS * r:PACK_ROWS * (r + 1), :], floor16)
             for r in range(rows // PACK_ROWS)], axis=0)
        return carry

    _over_rows(steps, keep_bucket, 0)
    t_lo, above_lo = _bisect16(lo_ref, steps, want - above_hi)
    quota = (topk - above_hi - above_lo).astype(F32)
    t_hi = jnp.where(t_idx < topk, jnp.int32(I16_MIN), t_hi)
    t_hi16, t_lo16 = _rows16(t_hi), _rows16(t_lo)

    incl_lower = (lax.broadcasted_iota(jnp.int32, (TIE_BLOCK, TIE_BLOCK), 0)
                  >= lax.broadcasted_iota(jnp.int32, (TIE_BLOCK, TIE_BLOCK), 1)).astype(BF16)
    one16, zero16 = jnp.ones((PACK_ROWS, nq), BF16), jnp.zeros((PACK_ROWS, nq), BF16)

    def attend_tile(j, ties_before, on_diagonal, running_max):
        base = pl.multiple_of(j * kb, kb)
        hi_t, lo_t = hi_ref[pl.ds(base, kb), :], lo_ref[pl.ds(base, kb), :]
        above, tie = [], []
        for r in range(kb // PACK_ROWS):
            hi16 = hi_t[PACK_ROWS * r:PACK_ROWS * (r + 1), :]
            lo16 = lo_t[PACK_ROWS * r:PACK_ROWS * (r + 1), :]
            bucket = hi16 == t_hi16
            above.append(jnp.where((hi16 > t_hi16) | (bucket & (lo16 > t_lo16)), one16, zero16))
            tie.append(jnp.where(bucket & (lo16 == t_lo16), one16, zero16))
        above, tie = jnp.concatenate(above, axis=0), jnp.concatenate(tie, axis=0)
        parts, running = [], ties_before
        for u in range(kb // TIE_BLOCK):
            parts.append(_dot(incl_lower, tie[u * TIE_BLOCK:(u + 1) * TIE_BLOCK, :]) + running)
            running = parts[-1][TIE_BLOCK - 1:TIE_BLOCK, :]
        ties_upto = jnp.concatenate(parts, axis=0)
        keep = above + tie * jnp.where(ties_upto <= quota, 1.0, 0.0).astype(BF16)
        if on_diagonal:
            keep = keep * jnp.where(base + row <= t_idx, 1.0, 0.0).astype(BF16)
        bias = ((keep - 1.0) * (-MASK_BIAS)).astype(F32)
        kt = k_ref[pl.ds(base, kb), :]
        n_sub = kb // ATT_BLOCK
        vts = [vt_ref[j * n_sub + u] for u in range(n_sub)]
        ones_rows = jnp.ones((PACK_ROWS, ATT_BLOCK), BF16)

        def value_matmul(h, p):
            hs = slice(h * HEAD_DIM, (h + 1) * HEAD_DIM)
            pb = p.astype(BF16)
            return sum(_dot(jnp.concatenate([vts[u][hs, :], ones_rows], axis=0),
                            pb[u * ATT_BLOCK:(u + 1) * ATT_BLOCK, :]) for u in range(n_sub))

        if running_max:
            logits = [_dot(kt, qm_ref[h]) + bias for h in range(N_HEADS)]
            m_old = [m_ref[h:h + 1, :] for h in range(N_HEADS)]
            m_new = [jnp.maximum(m_old[h], jnp.max(logits[h], axis=0, keepdims=True)) for h in range(N_HEADS)]
            alpha = [jnp.exp2(m_old[h] - m_new[h]) for h in range(N_HEADS)]
            probs = [jnp.exp2(logits[h] - m_new[h]) for h in range(N_HEADS)]
            for h in range(N_HEADS):
                m_ref[h:h + 1, :] = m_new[h]
                rows = slice(h * ACC_ROWS, (h + 1) * ACC_ROWS)
                acc_ref[rows, :] = alpha[h] * acc_ref[rows, :] + value_matmul(h, probs[h])
        else:
            shift = bias - logit_cap
            probs = [jnp.exp2(_dot(kt, qm_ref[h]) + shift) for h in range(N_HEADS)]
            for h in range(N_HEADS):
                rows = slice(h * ACC_ROWS, (h + 1) * ACC_ROWS)
                acc_ref[rows, :] += value_matmul(h, probs[h])
        return ties_upto[kb - 1:kb, :]

    n_below = (i * nq) // kb

    def attend(running_max):
        acc_ref[...] = jnp.zeros(acc_ref.shape, F32)
        below = functools.partial(attend_tile, on_diagonal=False, running_max=running_max)
        diagonal = functools.partial(attend_tile, on_diagonal=True, running_max=running_max)
        lax.fori_loop(n_below, n_tiles, diagonal, lax.fori_loop(0, n_below, below, jnp.zeros((1, nq), F32)))

    attend(running_max=False)
    denominators = jnp.concatenate(
        [acc_ref[h * ACC_ROWS + HEAD_DIM:h * ACC_ROWS + HEAD_DIM + 1, :] for h in range(N_HEADS)], axis=0)
    weakest = jnp.min(jnp.min(denominators, axis=1, keepdims=True), axis=0, keepdims=True)[0, 0]

    @pl.when(jnp.logical_not(weakest >= MIN_DENOMINATOR))
    def _():
        m_ref[...] = jnp.full(m_ref.shape, MASK_BIAS, F32)
        attend(running_max=True)

    heads = [acc_ref[h * ACC_ROWS:h * ACC_ROWS + HEAD_DIM, :]
             / acc_ref[h * ACC_ROWS + HEAD_DIM:h * ACC_ROWS + HEAD_DIM + 1, :] for h in range(N_HEADS)]
    o_ref[...] = jnp.concatenate(heads, axis=0).T


def _dsa_attention(qa, iq, iwt, ka, vat, ik):
    s = qa.shape[0]
    nq = ATT_BLOCK
    topk = min(TOPK_MAX, s // 4)
    assert s % DSA_KEY_TILE == 0 and DSA_KEY_TILE % ATT_BLOCK == 0
    assert ATT_BLOCK % TAIL_ROWS == 0 and COUNT_ROWS % TAIL_ROWS == 0 and DSA_KEY_TILE % TIE_BLOCK == 0
    return pl.pallas_call(
        functools.partial(_dsa_kernel, topk=topk),
        grid=(s // nq,),
        in_specs=[_rows(256, nq), _rows(128, nq), pl.BlockSpec((8, nq), lambda i: (0, i)),
                  _full((s, 256)), _full((s // ATT_BLOCK, 256, ATT_BLOCK)), _full((s, IDX_DIM))],
        out_specs=_rows(256, nq),
        out_shape=jax.ShapeDtypeStruct((s, 256), F32),
        scratch_shapes=[pltpu.VMEM((s, nq), I16), pltpu.VMEM((s, nq), I16), pltpu.VMEM((N_HEADS, 256, nq), BF16),
                        pltpu.VMEM((128, nq), BF16), pltpu.VMEM((N_HEADS * ACC_ROWS, nq), F32),
                        pltpu.VMEM((8, nq), F32), pltpu.VMEM((8, 128), F32)],
        compiler_params=_params(),
        name="dsa_attn",
    )(qa, iq, iwt, ka, vat, ik)


def _sb_kernel(q_ref, k_ref, v_ref, o_ref, acc_ref, carry_ref):
    nq = q_ref.shape[0]
    kb = ATT_BLOCK
    i = pl.program_id(0)
    q = q_ref[...]
    lane_head = lax.broadcasted_iota(jnp.int32, (1, BRANCH_WIDTH), 1) // HEAD_DIM
    qh = [jnp.where(lane_head == h, q, jnp.zeros_like(q)) for h in range(N_HEADS)]
    later = (lax.broadcasted_iota(jnp.int32, (2 * kb, kb), 0) % kb
             > lax.broadcasted_iota(jnp.int32, (2 * kb, kb), 1)).astype(BF16)
    acc_ref[...] = jnp.zeros(acc_ref.shape, F32)
    carry_ref[...] = jnp.zeros(carry_ref.shape, F32)

    def walk_tile(j, on_diagonal):
        base = pl.multiple_of(j * kb, kb)
        kt = k_ref[pl.ds(base, kb), :]
        vt = v_ref[pl.ds(base, kb), :]
        if on_diagonal:
            strict = (lax.broadcasted_iota(jnp.int32, (1, kb), 1)
                      < lax.broadcasted_iota(jnp.int32, (nq, 1), 0))
        heads = range(N_HEADS)
        z = [_dot_nt(qh[h], kt) for h in heads]
        softplus = [jnp.maximum(z[h], 0.0) + jnp.log2(1.0 + jnp.exp2(-jnp.abs(z[h]))) for h in heads]
        sp = [jnp.where(strict, softplus[h], 0.0) for h in heads] if on_diagonal else softplus
        hi = [sp[h].astype(BF16) for h in heads]
        lo = [(sp[h] - hi[h].astype(F32)).astype(BF16) for h in heads]
        after = [_dot(jnp.concatenate([hi[h], lo[h]], axis=1), later) for h in heads]
        c = [carry_ref[h] for h in heads]
        wts = [jnp.exp2((z[h] - softplus[h]) - (after[h] + c[h])) for h in heads]
        if on_diagonal:
            wts = [jnp.where(strict, wts[h], 0.0) for h in heads]
        out = sum(_dot(wts[h].astype(BF16), jnp.where(lane_head == h, vt, jnp.zeros_like(vt))) for h in heads)
        acc_ref[...] += out
        lowest = jnp.full((1, 1), jnp.inf, F32)
        for h in heads:
            c_new = c[h] + (after[h][:, 0:1] + sp[h][:, 0:1])
            carry_ref[h] = c_new
            lowest = jnp.minimum(lowest, jnp.min(c_new, axis=0, keepdims=True))
        return (lowest[0, 0] < SB_DEAD).astype(jnp.int32)

    def alive(state):
        j, live = state
        return jnp.logical_and(j >= 0, live > 0)

    def walk(state):
        j, _ = state
        return j - 1, walk_tile(j, on_diagonal=False)

    lax.while_loop(alive, walk, (i - 1, walk_tile(i, on_diagonal=True)))
    o_ref[...] = acc_ref[...]


def _sb_attention(cq, ck, cv):
    s = cq.shape[0]
    nq = ATT_BLOCK
    return pl.pallas_call(
        _sb_kernel,
        grid=(s // nq,),
        in_specs=[_rows(256, nq), _full((s, 256)), _full((s, 256))],
        out_specs=_rows(256, nq),
        out_shape=jax.ShapeDtypeStruct((s, 256), F32),
        scratch_shapes=[pltpu.VMEM((nq, 256), F32), pltpu.VMEM((N_HEADS, nq, 1), F32)],
        compiler_params=_params(),
        name="sb_attn",
    )(cq, ck, cv)


def _merge_out_kernel(x_ref, oa_ref, oc_ref, ogb_ref, ogd_ref, p_ref, g_ref, wg_ref, wm_ref, bm_ref,
                      wb_ref, wo_ref, gpost_ref, wple_ref, wpg_ref, out_ref):
    x = x_ref[...]
    h = _rmsnorm_rows(x, g_ref[...]).astype(BF16)
    gates = _dot(h, wg_ref[...])
    silu = gates * _sigmoid(gates)
    branch_in = [(oa_ref[...] * silu[:, 0:256]).astype(BF16), ogb_ref[...],
                 (oc_ref[...] * silu[:, 256:512]).astype(BF16), ogd_ref[...]]
    merged = jnp.zeros((x.shape[0], D_MODEL), F32)
    for n in range(4):
        cols = slice(n * D_MODEL, (n + 1) * D_MODEL)
        gate = _sigmoid(_dot(h, wm_ref[:, cols]) + bm_ref[:, cols])
        merged = merged + gate * _dot(branch_in[n], wb_ref[n])
    y = _dot(merged.astype(BF16), wo_ref[...])
    x1 = x + _rmsnorm_rows(y, gpost_ref[...])
    ple = _dot(p_ref[...].astype(BF16), wple_ref[...])
    out_ref[...] = x1 + ple * _sigmoid(_dot(x1.astype(BF16), wpg_ref[...]))


def _merge_out(x, o_a, o_c, og_b, og_d, p_i, g_pre, w):
    s = x.shape[0]
    return pl.pallas_call(
        _merge_out_kernel,
        grid=(s // ROW_BLOCK,),
        in_specs=[_rows(D_MODEL), _rows(256), _rows(256), _rows(256), _rows(256), _rows(PLE_DIM),
                  _full((1, D_MODEL)), _full((D_MODEL, 512)), _full((D_MODEL, 4 * D_MODEL)),
                  _full((1, 4 * D_MODEL)), _full((4, BRANCH_WIDTH, D_MODEL)), _full((D_MODEL, D_MODEL)),
                  _full((1, D_MODEL)), _full((PLE_DIM, D_MODEL)), _full((D_MODEL, D_MODEL))],
        out_specs=_rows(D_MODEL),
        out_shape=jax.ShapeDtypeStruct((s, D_MODEL), F32),
        compiler_params=_params(),
        name="merge_out",
    )(x, o_a, o_c, og_b, og_d, p_i, g_pre, w["wg"], w["wm"], w["b_merge"], w["w_branch"], w["w_out"],
      w["g_post"], w["w_ple"], w["w_ple_gate"])


IN_COLS = 8100
IN_HEAD = 932
PREP_ROWS = 128


def _swap_halves(x, d):
    n = x.shape[1]
    lane = lax.broadcasted_iota(jnp.int32, (1, n), 1)
    return jnp.where(lane % d < d // 2, pltpu.roll(x, n - d // 2, 1), pltpu.roll(x, d // 2, 1))


def _weight_layout_kernel(w_ref, wa_ref, wi_ref, wc_ref, w2_ref, wg_ref, wm_ref):
    a_q, a_k = w_ref[0, :, 0:256], w_ref[0, :, 256:512]
    wa_ref[0, :, 0:256] = a_q.astype(BF16)
    wa_ref[0, :, 256:512] = _swap_halves(a_q, HEAD_DIM).astype(BF16)
    wa_ref[0, :, 512:768] = a_k.astype(BF16)
    wa_ref[0, :, 768:1024] = _swap_halves(a_k, HEAD_DIM).astype(BF16)
    wa_ref[0, :, 1024:1280] = w_ref[0, :, 512:768].astype(BF16)
    a_iq = w_ref[0, :, 768:896]
    tail = w_ref[0, :, 896:1024]
    lane = lax.broadcasted_iota(jnp.int32, (1, 128), 1)
    wi_ref[0, :, 0:128] = a_iq.astype(BF16)
    wi_ref[0, :, 128:256] = _swap_halves(a_iq, IDX_DIM).astype(BF16)
    wi_ref[0, :, 256:384] = jnp.where(lane < IDX_DIM, tail, 0.0).astype(BF16)
    wi_ref[0, :, 384:512] = jnp.where(lane < IDX_DIM, _swap_halves(tail, IDX_DIM), 0.0).astype(BF16)
    wi_ref[0, :, 512:640] = jnp.where(lane < N_IDX_HEADS, pltpu.roll(tail, 128 - IDX_DIM, 1), 0.0).astype(BF16)

    def rest(lo, hi):
        return w_ref[0, :, IN_HEAD + lo:IN_HEAD + hi].astype(BF16)

    w2_ref[0, :, 0:768] = rest(0, 768)
    wc_ref[0] = rest(768, 1536)
    w2_ref[0, :, 768:1280] = rest(1536, 2048)
    wg_ref[0, :, 0:256] = rest(2048, 2304)
    w2_ref[0, :, 1280:1536] = rest(2304, 2560)
    wg_ref[0, :, 256:512] = rest(2560, 2816)
    w2_ref[0, :, 1536:1792] = rest(2816, 3072)
    wm_ref[0] = rest(3072, 7168)


def _weight_layout(w_in):
    depth = w_in.shape[0]
    assert w_in.shape[1:] == (D_MODEL, IN_COLS)
    widths = (1280, 640, 768, 1792, 512, 4096)
    return pl.pallas_call(
        _weight_layout_kernel,
        grid=(depth, D_MODEL // PREP_ROWS),
        in_specs=[pl.BlockSpec((1, PREP_ROWS, IN_COLS), lambda l, r: (l, r, 0))],
        out_specs=[pl.BlockSpec((1, PREP_ROWS, n), lambda l, r: (l, r, 0)) for n in widths],
        out_shape=[jax.ShapeDtypeStruct((depth, D_MODEL, n), BF16) for n in widths],
        compiler_params=_params(2),
        name="weight_layout",
    )(w_in)


def _layer_weights(i, w_proj, conv_w, conv_b, ln_g, ln_b, w_spatial, b_spatial, b_merge, w_branch, w_out,
                   g_post, w_ple, w_ple_gate):
    row = lambda v: v[None, :]
    wa, wi, wc, w2, wg, wm = (w[i] for w in w_proj)
    return {
        "wa": wa, "wi": wi, "wc": wc, "w2": w2, "wg": wg, "wm": wm,
        "conv_w": jnp.pad(conv_w[i], ((0, 8 - CONV_WIDTH), (0, 0))),
        "conv_b": row(conv_b[i]), "ln_g": row(ln_g[i]), "ln_b": row(ln_b[i]),
        "w_spatial": w_spatial[i],
        "b_spatial": jnp.repeat(b_spatial[i].T, GROUP_DIM, axis=1),
        "b_merge": row(b_merge[i]),
        "w_branch": w_branch[i].astype(BF16), "w_out": w_out[i].astype(BF16), "g_post": row(g_post[i]),
        "w_ple": w_ple[i].astype(BF16), "w_ple_gate": w_ple_gate[i].astype(BF16),
    }


def kernel(x, p, positions, g_pre, w_in, conv_w, conv_b, ln_g, ln_b, w_spatial, b_spatial, b_merge,
           w_branch, w_out, g_post, w_ple, w_ple_gate):
    batch, s, _ = x.shape
    assert batch == 1 and s % ROW_BLOCK == 0 and s % ATT_BLOCK == 0 and ATT_BLOCK >= min(TOPK_MAX, s // 4)
    depth = w_in.shape[0]
    xs = x[0]
    tables = _rope_tables(positions[0][:, None])
    w_proj = _weight_layout(w_in)
    for i in range(depth):
        w = _layer_weights(i, w_proj, conv_w, conv_b, ln_g, ln_b, w_spatial, b_spatial, b_merge, w_branch,
                           w_out, g_post, w_ple, w_ple_gate)
        g = g_pre[i][None, :]
        qa, ka, vat, iq, ik, iwt, cq, ck, cv = _attn_proj(xs, g, w, tables)
        og_b, og_d = _local_mix(xs, g, w)
        o_a = _dsa_attention(qa, iq, iwt, ka, vat, ik)
        o_c = _sb_attention(cq, ck, cv)
        xs = _merge_out(xs, o_a, o_c, og_b, og_d, p[i][0], g, w)
    return xs[None]
```

```python
import functools

import jax
import jax.numpy as jnp
from jax import lax
from jax.experimental import pallas as pl
from jax.experimental.pallas import tpu as pltpu

D_MODEL = 1024
BRANCH_WIDTH = 256
HEAD_DIM = 64
N_HEADS = 4
N_IDX_HEADS = 4
IDX_DIM = 32
TOPK_MAX = 256
CONV_WIDTH = 3
CHUNK = 128
N_GROUPS = 4
GROUP_DIM = BRANCH_WIDTH // N_GROUPS
PLE_DIM = 256
ROPE_THETA = 10000.0
EPS = 1e-6
IDX_W_SCALE = (N_IDX_HEADS * IDX_DIM) ** -0.5
QK_SCALE = HEAD_DIM ** -0.5
LOG2_E = 1.4426950408889634

ROW_BLOCK = 256
ATT_BLOCK = 256
DSA_KEY_TILE = 1024
TIE_BLOCK = 256
COUNT_ROWS = 1024
TAIL_ROWS = 256
CONV_HALO = 8
MASK_BIAS = -1e30
SCORE_FLOOR = float(jnp.finfo(jnp.float32).min)
SB_DEAD = 152.0
CAP_SLACK = 1.03
MIN_DENOMINATOR = 2.0 ** -64
VMEM_LIMIT = 56 * 1024 * 1024

BF16 = jnp.bfloat16
F32 = jnp.float32
NT_DIMS = (((1,), (1,)), ((), ()))


def _dot(a, b):
    return jnp.dot(a, b, preferred_element_type=F32)


def _dot_nt(a, b):
    return lax.dot_general(a, b, NT_DIMS, preferred_element_type=F32)


def _rmsnorm_rows(x, g):
    return x * lax.rsqrt(jnp.mean(x * x, axis=-1, keepdims=True) + EPS) * g


def _sigmoid(x):
    return 1.0 / (1.0 + jnp.exp(-x))


def _params(n_grid_dims=1):
    return pltpu.CompilerParams(
        dimension_semantics=("arbitrary",) * n_grid_dims, vmem_limit_bytes=VMEM_LIMIT)


def _full(shape):
    return pl.BlockSpec(shape, lambda i: (0,) * len(shape))


def _rows(width, block=ROW_BLOCK):
    return pl.BlockSpec((block, width), lambda i: (i, 0))


def _tile_lanes(x, period):
    lane = lax.broadcasted_iota(jnp.int32, (1, 128), 1)
    x = jnp.where(lane < period, x, 0.0)
    while period < 128:
        x = x + pltpu.roll(x, period, 1)
        period *= 2
    return x


def _rope_table_kernel(pos_ref, freq_ref, s64_ref, s32_ref, cos64_ref, sin64_ref, cos32_ref, sin32_ref):
    angle = pos_ref[...].astype(F32) * freq_ref[...]
    cos, sin = jnp.cos(angle), jnp.sin(angle)
    n64, n32 = HEAD_DIM // 2, IDX_DIM // 2
    cos_h, sin_h = _tile_lanes(cos, n64), _tile_lanes(sin, n64)
    cos64_ref[...] = jnp.concatenate([cos_h, cos_h], axis=1)
    sin64_ref[...] = jnp.concatenate([sin_h, sin_h], axis=1) * s64_ref[...]
    cos32_ref[...] = _tile_lanes(pltpu.roll(cos, 128 - n64, 1), n32)
    sin32_ref[...] = _tile_lanes(pltpu.roll(sin, 128 - n64, 1), n32) * s32_ref[...]


def _rope_sign(d, width):
    lane = jnp.arange(width)
    return jnp.where((lane % d) < d // 2, -1.0, 1.0).astype(F32)[None, :]


def _rope_tables(pos_col):
    s = pos_col.shape[0]
    inv_freq = lambda d: ROPE_THETA ** (-jnp.arange(0, d, 2, dtype=F32) / d)
    freq = jnp.concatenate([inv_freq(HEAD_DIM), inv_freq(IDX_DIM), jnp.zeros((80,), F32)])[None, :]
    s64, s32 = _rope_sign(HEAD_DIM, BRANCH_WIDTH), _rope_sign(IDX_DIM, N_IDX_HEADS * IDX_DIM)
    return pl.pallas_call(
        _rope_table_kernel,
        grid=(s // ROW_BLOCK,),
        in_specs=[_rows(1), _full((1, 128)), _full((1, 256)), _full((1, 128))],
        out_specs=[_rows(256), _rows(256), _rows(128), _rows(128)],
        out_shape=[jax.ShapeDtypeStruct((s, 256), F32), jax.ShapeDtypeStruct((s, 256), F32),
                   jax.ShapeDtypeStruct((s, 128), F32), jax.ShapeDtypeStruct((s, 128), F32)],
        compiler_params=_params(),
        name="rope_tables",
    )(pos_col, freq, s64, s32)


def _swap_halves(x, d):
    n = x.shape[1]
    lane = lax.broadcasted_iota(jnp.int32, (1, n), 1)
    return jnp.where(lane % d < d // 2, pltpu.roll(x, n - d // 2, 1), pltpu.roll(x, d // 2, 1))


def _attn_proj_kernel(x_ref, g_ref, wa_ref, wi_ref, wc_ref,
                      cos64_ref, sin64_ref, cos32_ref, sin32_ref,
                      qa_ref, ka_ref, vat_ref, iq_ref, ik_ref, iwt_ref, cq_ref, ck_ref, cv_ref):
    h = _rmsnorm_rows(x_ref[...], g_ref[...]).astype(BF16)
    c64, s64 = cos64_ref[...], sin64_ref[...]
    c32, s32 = cos32_ref[...], sin32_ref[...]
    pa = _dot(h, wa_ref[...])
    q, k = pa[:, 0:256], pa[:, 256:512]
    qa_ref[...] = (q * c64 + _swap_halves(q, HEAD_DIM) * s64) * (QK_SCALE * LOG2_E)
    ka_ref[...] = (k * c64 + _swap_halves(k, HEAD_DIM) * s64).astype(BF16)
    vat_ref[0] = pa[:, 512:768].T.astype(BF16)
    pi = _dot(h, wi_ref[...])
    iq, ik = pi[:, 0:128], pi[:, 128:256]
    iq_ref[...] = iq * c32 + _swap_halves(iq, IDX_DIM) * s32
    ik_ref[...] = (ik * c32 + _swap_halves(ik, IDX_DIM) * s32)[:, 0:IDX_DIM].astype(BF16)
    iwt_ref[...] = (pi[:, 256:384] * IDX_W_SCALE).T[0:8, :]
    pc = _dot(h, wc_ref[...])
    cq_ref[...] = (pc[:, 0:256] * (QK_SCALE * LOG2_E)).astype(BF16)
    ck_ref[...] = pc[:, 256:512].astype(BF16)
    cv_ref[...] = pc[:, 512:768].astype(BF16)


def _attn_proj(x, g_pre, w, tables):
    s = x.shape[0]
    nb = s // ROW_BLOCK
    cos64, sin64, cos32, sin32 = tables
    return pl.pallas_call(
        _attn_proj_kernel,
        grid=(nb,),
        in_specs=[_rows(D_MODEL), _full((1, D_MODEL)), _full((D_MODEL, 768)), _full((D_MODEL, 384)),
                  _full((D_MODEL, 768)), _rows(256), _rows(256), _rows(128), _rows(128)],
        out_specs=[_rows(256), _rows(256), pl.BlockSpec((1, 256, ROW_BLOCK), lambda i: (i, 0, 0)),
                   _rows(128), _rows(IDX_DIM), pl.BlockSpec((8, ROW_BLOCK), lambda i: (0, i)),
                   _rows(256), _rows(256), _rows(256)],
        out_shape=[jax.ShapeDtypeStruct((s, 256), F32), jax.ShapeDtypeStruct((s, 256), BF16),
                   jax.ShapeDtypeStruct((nb, 256, ROW_BLOCK), BF16),
                   jax.ShapeDtypeStruct((s, 128), F32), jax.ShapeDtypeStruct((s, IDX_DIM), BF16),
                   jax.ShapeDtypeStruct((8, s), F32),
                   jax.ShapeDtypeStruct((s, 256), BF16), jax.ShapeDtypeStruct((s, 256), BF16),
                   jax.ShapeDtypeStruct((s, 256), BF16)],
        compiler_params=_params(),
        name="attn_proj",
    )(x, g_pre, w["wa"], w["wi"], w["wc"], cos64, sin64, cos32, sin32)


def _local_mix_kernel(x_ref, g_ref, w_ref, convw_ref, convb_ref, lng_ref, lnb_ref, ws_ref, bs_ref,
                      ogb_ref, ogd_ref, ypad_ref):
    t = x_ref.shape[0]

    @pl.when(pl.program_id(0) == 0)
    def _():
        ypad_ref[0:CONV_HALO, :] = jnp.zeros((CONV_HALO, BRANCH_WIDTH), F32)

    h = _rmsnorm_rows(x_ref[...], g_ref[...]).astype(BF16)
    pr = _dot(h, w_ref[...])
    gate_b, gate_c, x_in = pr[:, 0:256], pr[:, 256:512], pr[:, 512:768]
    d_u, d_v = pr[:, 768:1024], pr[:, 1024:1280]
    silu_b, silu_d = pr[:, 1280:1536], pr[:, 1536:1792]

    y = gate_c * x_in
    ypad_ref[CONV_HALO:CONV_HALO + t, :] = y
    y1 = ypad_ref[CONV_HALO - 1:CONV_HALO - 1 + t, :]
    y2 = ypad_ref[CONV_HALO - 2:CONV_HALO - 2 + t, :]
    conv = convw_ref[2:3, :] * y + convw_ref[1:2, :] * y1 + convw_ref[0:1, :] * y2
    ypad_ref[0:CONV_HALO, :] = y[t - CONV_HALO:t, :]
    o_b = gate_b * (conv + convb_ref[...])
    ogb_ref[...] = (o_b * (silu_b * _sigmoid(silu_b))).astype(BF16)

    mu = jnp.mean(d_v, axis=-1, keepdims=True)
    dc = d_v - mu
    var = jnp.mean(dc * dc, axis=-1, keepdims=True)
    vn = dc * lax.rsqrt(var + EPS) * lng_ref[...] + lnb_ref[...]
    group = lax.broadcasted_iota(jnp.int32, (1, BRANCH_WIDTH), 1) // GROUP_DIM
    tril = (lax.broadcasted_iota(jnp.int32, (CHUNK, CHUNK), 0)
            >= lax.broadcasted_iota(jnp.int32, (CHUNK, CHUNK), 1))
    wm = [jnp.where(tril, ws_ref[g], 0.0).astype(BF16) for g in range(N_GROUPS)]
    mixed = []
    for c in range(t // CHUNK):
        vc = vn[c * CHUNK:(c + 1) * CHUNK, :]
        m = bs_ref[...]
        for g in range(N_GROUPS):
            m = m + _dot(wm[g], jnp.where(group == g, vc, 0.0).astype(BF16))
        mixed.append(m)
    o_d = d_u * jnp.concatenate(mixed, axis=0)
    ogd_ref[...] = (o_d * (silu_d * _sigmoid(silu_d))).astype(BF16)


def _local_mix(x, g_pre, w):
    s = x.shape[0]
    return pl.pallas_call(
        _local_mix_kernel,
        grid=(s // ROW_BLOCK,),
        in_specs=[_rows(D_MODEL), _full((1, D_MODEL)), _full((D_MODEL, 1792)), _full((8, 256)),
                  _full((1, 256)), _full((1, 256)), _full((1, 256)),
                  _full((N_GROUPS, CHUNK, CHUNK)), _full((CHUNK, 256))],
        out_specs=[_rows(256), _rows(256)],
        out_shape=[jax.ShapeDtypeStruct((s, 256), BF16), jax.ShapeDtypeStruct((s, 256), BF16)],
        scratch_shapes=[pltpu.VMEM((ROW_BLOCK + CONV_HALO, BRANCH_WIDTH), F32)],
        compiler_params=_params(),
        name="local_mix",
    )(x, g_pre, w["w2"], w["conv_w"], w["conv_b"], w["ln_g"], w["ln_b"], w["w_spatial"], w["b_spatial"])


I16 = jnp.int16
I16_MIN = -32768
PACK_ROWS = 16
ACC_ROWS = HEAD_DIM + PACK_ROWS


def _sortable_halves(score):
    bits = lax.bitcast_convert_type(score, jnp.int32)
    key = bits ^ ((bits >> 31) & jnp.int32(0x7FFFFFFF))
    hi = (key >> 16).astype(I16)
    lo = ((key & jnp.int32(0xFFFF)) + jnp.int32(I16_MIN)).astype(I16)
    return hi, lo


def _rows16(row32):
    return jnp.broadcast_to(row32, (PACK_ROWS, row32.shape[1])).astype(I16)


def _over_rows(steps, fn, carry):
    n_big, n_tail = steps
    carry = lax.fori_loop(
        0, n_big, lambda c, x: fn(pl.multiple_of(c * COUNT_ROWS, COUNT_ROWS), COUNT_ROWS, x), carry)
    tail_start = n_big * COUNT_ROWS
    return lax.fori_loop(
        0, n_tail, lambda c, x: fn(pl.multiple_of(tail_start + c * TAIL_ROWS, TAIL_ROWS), TAIL_ROWS, x), carry)


def _count_pass(ref, steps, cand):
    q = ref.shape[1]
    cand16 = _rows16(cand)
    one, zero = jnp.ones((PACK_ROWS, q), I16), jnp.zeros((PACK_ROWS, q), I16)
    n_acc = 4

    def body(base, rows, accs):
        accs = list(accs)
        blk = ref[pl.ds(base, rows), :]
        for r in range(rows // PACK_ROWS):
            v = blk[PACK_ROWS * r:PACK_ROWS * (r + 1), :]
            accs[r % n_acc] = accs[r % n_acc] + jnp.where(v >= cand16, one, zero)
        return tuple(accs)

    assert ref.shape[0] // (PACK_ROWS * n_acc) < 2 ** 15
    accs = _over_rows(steps, body, (zero,) * n_acc)
    total = sum(a.astype(jnp.int32) for a in accs)
    return jnp.sum(total, axis=0, keepdims=True)


def _bisect16(ref, steps, target):
    zero = jnp.zeros_like(target)

    def probe(cand, t, above):
        cnt = _count_pass(ref, steps, cand)
        enough = cnt >= target
        return jnp.where(enough, cand, t), jnp.where(enough, above, cnt)

    def step(b, state):
        t, above = state
        return probe(t | (jnp.int32(1) << (14 - b)), t, above)

    return lax.fori_loop(0, 15, step, probe(zero, jnp.full_like(target, I16_MIN), zero))


def _dsa_kernel(q_ref, iq_ref, iwt_ref, k_ref, vt_ref, ik_ref, o_ref,
                hi_ref, lo_ref, qm_ref, iqt_ref, acc_ref, m_ref, knorm_ref, *, topk):
    nq = q_ref.shape[0]
    kb = DSA_KEY_TILE
    i = pl.program_id(0)
    n_tiles = (i * nq + nq + kb - 1) // kb
    t_idx = i * nq + lax.broadcasted_iota(jnp.int32, (1, nq), 1)
    row = lax.broadcasted_iota(jnp.int32, (kb, 1), 0)

    qt = q_ref[...].T
    head_of_row = lax.broadcasted_iota(jnp.int32, (BRANCH_WIDTH, 1), 0) // HEAD_DIM
    for h in range(N_HEADS):
        qm_ref[h] = jnp.where(head_of_row == h, qt, 0.0).astype(BF16)
    iqt_ref[...] = iq_ref[...].T.astype(BF16)

    @pl.when(i == 0)
    def _():
        head_of_lane = (lax.broadcasted_iota(jnp.int32, (BRANCH_WIDTH, 1), 0) // HEAD_DIM
                        == lax.broadcasted_iota(jnp.int32, (1, 128), 1)).astype(BF16)

        def widest(c, best):
            kf = k_ref[pl.ds(pl.multiple_of(c * kb, kb), kb), :].astype(F32)
            return jnp.maximum(best, jnp.max(_dot((kf * kf).astype(BF16), head_of_lane), axis=0, keepdims=True))

        best = lax.fori_loop(0, k_ref.shape[0] // kb, widest, jnp.zeros((1, 128), F32))
        knorm_ref[...] = jnp.broadcast_to(best, knorm_ref.shape)

    logit_cap = jnp.zeros((1, nq), F32)
    for h in range(N_HEADS):
        qf = qm_ref[h].astype(F32)
        q_sq = jnp.sum(qf * qf, axis=0, keepdims=True)
        logit_cap = jnp.maximum(logit_cap, jnp.sqrt(q_sq * knorm_ref[0:1, h:h + 1]) * CAP_SLACK + CAP_SLACK)
    w_rows = [iwt_ref[h:h + 1, :] for h in range(N_IDX_HEADS)]

    def score_tile(j, carry, on_diagonal):
        base = pl.multiple_of(j * kb, kb)
        ikb = ik_ref[pl.ds(base, kb), :]
        sc = w_rows[0] * jnp.maximum(_dot(ikb, iqt_ref[0:IDX_DIM, :]), 0.0)
        for h in range(1, N_IDX_HEADS):
            logit = _dot(ikb, iqt_ref[h * IDX_DIM:(h + 1) * IDX_DIM, :])
            sc = sc + w_rows[h] * jnp.maximum(logit, 0.0)
        if on_diagonal:
            sc = jnp.where(base + row <= t_idx, sc, SCORE_FLOOR)
        hi_ref[pl.ds(base, kb), :], lo_ref[pl.ds(base, kb), :] = _sortable_halves(sc)
        return carry

    n_below = (i * nq) // kb
    lax.fori_loop(0, n_below, functools.partial(score_tile, on_diagonal=False), 0)
    lax.fori_loop(n_below, n_tiles, functools.partial(score_tile, on_diagonal=True), 0)

    causal_rows = (i + 1) * nq
    steps = (causal_rows // COUNT_ROWS, (causal_rows % COUNT_ROWS) // TAIL_ROWS)
    want = jnp.full((1, nq), topk, jnp.int32)
    t_hi, above_hi = _bisect16(hi_ref, steps, want)
    t_hi16 = _rows16(t_hi)
    floor16 = jnp.full((PACK_ROWS, nq), I16_MIN, I16)

    def keep_bucket(base, rows, carry):
        hi, lo = hi_ref[pl.ds(base, rows), :], lo_ref[pl.ds(base, rows), :]
        lo_ref[pl.ds(base, rows), :] = jnp.concatenate(
            [jnp.where(hi[PACK_ROWS * r:PACK_ROWS * (r + 1), :] == t_hi16,
                       lo[PACK_ROWS * r:PACK_ROWS * (r + 1), :], floor16)
             for r in range(rows // PACK_ROWS)], axis=0)
        return carry

    _over_rows(steps, keep_bucket, 0)
    t_lo, above_lo = _bisect16(lo_ref, steps, want - above_hi)
    quota = (topk - above_hi - above_lo).astype(F32)
    t_hi = jnp.where(t_idx < topk, jnp.int32(I16_MIN), t_hi)
    t_hi16, t_lo16 = _rows16(t_hi), _rows16(t_lo)

    incl_lower = (lax.broadcasted_iota(jnp.int32, (TIE_BLOCK, TIE_BLOCK), 0)
                  >= lax.broadcasted_iota(jnp.int32, (TIE_BLOCK, TIE_BLOCK), 1)).astype(BF16)
    one16, zero16 = jnp.ones((PACK_ROWS, nq), BF16), jnp.zeros((PACK_ROWS, nq), BF16)

    def attend_tile(j, ties_before, on_diagonal, running_max):
        base = pl.multiple_of(j * kb, kb)
        hi_t, lo_t = hi_ref[pl.ds(base, kb), :], lo_ref[pl.ds(base, kb), :]
        above, tie = [], []
        for r in range(kb // PACK_ROWS):
            hi16 = hi_t[PACK_ROWS * r:PACK_ROWS * (r + 1), :]
            lo16 = lo_t[PACK_ROWS * r:PACK_ROWS * (r + 1), :]
            bucket = hi16 == t_hi16
            above.append(jnp.where((hi16 > t_hi16) | (bucket & (lo16 > t_lo16)), one16, zero16))
            tie.append(jnp.where(bucket & (lo16 == t_lo16), one16, zero16))
        above, tie = jnp.concatenate(above, axis=0), jnp.concatenate(tie, axis=0)
        parts, running = [], ties_before
        for u in range(kb // TIE_BLOCK):
            parts.append(_dot(incl_lower, tie[u * TIE_BLOCK:(u + 1) * TIE_BLOCK, :]) + running)
            running = parts[-1][TIE_BLOCK - 1:TIE_BLOCK, :]
        ties_upto = jnp.concatenate(parts, axis=0)
        keep = above + tie * jnp.where(ties_upto <= quota, 1.0, 0.0).astype(BF16)
        if on_diagonal:
            keep = keep * jnp.where(base + row <= t_idx, 1.0, 0.0).astype(BF16)
        bias = ((keep - 1.0) * (-MASK_BIAS)).astype(F32)
        kt = k_ref[pl.ds(base, kb), :]
        n_sub = kb // ATT_BLOCK
        vts = [vt_ref[j * n_sub + u] for u in range(n_sub)]
        ones_rows = jnp.ones((PACK_ROWS, ATT_BLOCK), BF16)

        def value_matmul(h, p):
            hs = slice(h * HEAD_DIM, (h + 1) * HEAD_DIM)
            pb = p.astype(BF16)
            return sum(_dot(jnp.concatenate([vts[u][hs, :], ones_rows], axis=0),
                            pb[u * ATT_BLOCK:(u + 1) * ATT_BLOCK, :]) for u in range(n_sub))

        if running_max:
            logits = [_dot(kt, qm_ref[h]) + bias for h in range(N_HEADS)]
            m_old = [m_ref[h:h + 1, :] for h in range(N_HEADS)]
            m_new = [jnp.maximum(m_old[h], jnp.max(logits[h], axis=0, keepdims=True)) for h in range(N_HEADS)]
            alpha = [jnp.exp2(m_old[h] - m_new[h]) for h in range(N_HEADS)]
            probs = [jnp.exp2(logits[h] - m_new[h]) for h in range(N_HEADS)]
            for h in range(N_HEADS):
                m_ref[h:h + 1, :] = m_new[h]
                rows = slice(h * ACC_ROWS, (h + 1) * ACC_ROWS)
                acc_ref[rows, :] = alpha[h] * acc_ref[rows, :] + value_matmul(h, probs[h])
        else:
            shift = bias - logit_cap
            probs = [jnp.exp2(_dot(kt, qm_ref[h]) + shift) for h in range(N_HEADS)]
            for h in range(N_HEADS):
                rows = slice(h * ACC_ROWS, (h + 1) * ACC_ROWS)
                acc_ref[rows, :] += value_matmul(h, probs[h])
        return ties_upto[kb - 1:kb, :]

    n_below = (i * nq) // kb

    def attend(running_max):
        acc_ref[...] = jnp.zeros(acc_ref.shape, F32)
        below = functools.partial(attend_tile, on_diagonal=False, running_max=running_max)
        diagonal = functools.partial(attend_tile, on_diagonal=True, running_max=running_max)
        lax.fori_loop(n_below, n_tiles, diagonal, lax.fori_loop(0, n_below, below, jnp.zeros((1, nq), F32)))

    attend(running_max=False)
    denominators = jnp.concatenate(
        [acc_ref[h * ACC_ROWS + HEAD_DIM:h * ACC_ROWS + HEAD_DIM + 1, :] for h in range(N_HEADS)], axis=0)
    weakest = jnp.min(jnp.min(denominators, axis=1, keepdims=True), axis=0, keepdims=True)[0, 0]

    @pl.when(jnp.logical_not(weakest >= MIN_DENOMINATOR))
    def _():
        m_ref[...] = jnp.full(m_ref.shape, MASK_BIAS, F32)
        attend(running_max=True)

    heads = [acc_ref[h * ACC_ROWS:h * ACC_ROWS + HEAD_DIM, :]
             / acc_ref[h * ACC_ROWS + HEAD_DIM:h * ACC_ROWS + HEAD_DIM + 1, :] for h in range(N_HEADS)]
    o_ref[...] = jnp.concatenate(heads, axis=0).T


def _dsa_attention(qa, iq, iwt, ka, vat, ik):
    s = qa.shape[0]
    nq = ATT_BLOCK
    topk = min(TOPK_MAX, s // 4)
    assert s % DSA_KEY_TILE == 0 and DSA_KEY_TILE % ATT_BLOCK == 0
    assert ATT_BLOCK % TAIL_ROWS == 0 and COUNT_ROWS % TAIL_ROWS == 0 and DSA_KEY_TILE % TIE_BLOCK == 0
    return pl.pallas_call(
        functools.partial(_dsa_kernel, topk=topk),
        grid=(s // nq,),
        in_specs=[_rows(256, nq), _rows(128, nq), pl.BlockSpec((8, nq), lambda i: (0, i)),
                  _full((s, 256)), _full((s // ATT_BLOCK, 256, ATT_BLOCK)), _full((s, IDX_DIM))],
        out_specs=_rows(256, nq),
        out_shape=jax.ShapeDtypeStruct((s, 256), F32),
        scratch_shapes=[pltpu.VMEM((s, nq), I16), pltpu.VMEM((s, nq), I16), pltpu.VMEM((N_HEADS, 256, nq), BF16),
                        pltpu.VMEM((128, nq), BF16), pltpu.VMEM((N_HEADS * ACC_ROWS, nq), F32),
                        pltpu.VMEM((8, nq), F32), pltpu.VMEM((8, 128), F32)],
        compiler_params=_params(),
        name="dsa_attn",
    )(qa, iq, iwt, ka, vat, ik)


def _sb_kernel(q_ref, k_ref, v_ref, o_ref, acc_ref, carry_ref):
    nq = q_ref.shape[0]
    kb = ATT_BLOCK
    i = pl.program_id(0)
    q = q_ref[...]
    lane_head = lax.broadcasted_iota(jnp.int32, (1, BRANCH_WIDTH), 1) // HEAD_DIM
    qh = [jnp.where(lane_head == h, q, jnp.zeros_like(q)) for h in range(N_HEADS)]
    later = (lax.broadcasted_iota(jnp.int32, (2 * kb, kb), 0) % kb
             > lax.broadcasted_iota(jnp.int32, (2 * kb, kb), 1)).astype(BF16)
    acc_ref[...] = jnp.zeros(acc_ref.shape, F32)
    carry_ref[...] = jnp.zeros(carry_ref.shape, F32)

    def walk_tile(j, on_diagonal):
        base = pl.multiple_of(j * kb, kb)
        kt = k_ref[pl.ds(base, kb), :]
        vt = v_ref[pl.ds(base, kb), :]
        if on_diagonal:
            strict = (lax.broadcasted_iota(jnp.int32, (1, kb), 1)
                      < lax.broadcasted_iota(jnp.int32, (nq, 1), 0))
        heads = range(N_HEADS)
        z = [_dot_nt(qh[h], kt) for h in heads]
        softplus = [jnp.maximum(z[h], 0.0) + jnp.log2(1.0 + jnp.exp2(-jnp.abs(z[h]))) for h in heads]
        sp = [jnp.where(strict, softplus[h], 0.0) for h in heads] if on_diagonal else softplus
        hi = [sp[h].astype(BF16) for h in heads]
        lo = [(sp[h] - hi[h].astype(F32)).astype(BF16) for h in heads]
        after = [_dot(jnp.concatenate([hi[h], lo[h]], axis=1), later) for h in heads]
        c = [carry_ref[h] for h in heads]
        wts = [jnp.exp2((z[h] - softplus[h]) - (after[h] + c[h])) for h in heads]
        if on_diagonal:
            wts = [jnp.where(strict, wts[h], 0.0) for h in heads]
        out = sum(_dot(wts[h].astype(BF16), jnp.where(lane_head == h, vt, jnp.zeros_like(vt))) for h in heads)
        acc_ref[...] += out
        lowest = jnp.full((1, 1), jnp.inf, F32)
        for h in heads:
            c_new = c[h] + (after[h][:, 0:1] + sp[h][:, 0:1])
            carry_ref[h] = c_new
            lowest = jnp.minimum(lowest, jnp.min(c_new, axis=0, keepdims=True))
        return (lowest[0, 0] < SB_DEAD).astype(jnp.int32)

    def alive(state):
        j, live = state
        return jnp.logical_and(j >= 0, live > 0)

    def walk(state):
        j, _ = state
        return j - 1, walk_tile(j, on_diagonal=False)

    lax.while_loop(alive, walk, (i - 1, walk_tile(i, on_diagonal=True)))
    o_ref[...] = acc_ref[...]


def _sb_attention(cq, ck, cv):
    s = cq.shape[0]
    nq = ATT_BLOCK
    return pl.pallas_call(
        _sb_kernel,
        grid=(s // nq,),
        in_specs=[_rows(256, nq), _full((s, 256)), _full((s, 256))],
        out_specs=_rows(256, nq),
        out_shape=jax.ShapeDtypeStruct((s, 256), F32),
        scratch_shapes=[pltpu.VMEM((nq, 256), F32), pltpu.VMEM((N_HEADS, nq, 1), F32)],
        compiler_params=_params(),
        name="sb_attn",
    )(cq, ck, cv)


def _merge_out_kernel(x_ref, oa_ref, oc_ref, ogb_ref, ogd_ref, p_ref, g_ref, wg_ref, wm_ref, bm_ref,
                      wb_ref, wo_ref, gpost_ref, wple_ref, wpg_ref, out_ref):
    x = x_ref[...]
    h = _rmsnorm_rows(x, g_ref[...]).astype(BF16)
    gates = _dot(h, wg_ref[...])
    silu = gates * _sigmoid(gates)
    branch_in = [(oa_ref[...] * silu[:, 0:256]).astype(BF16), ogb_ref[...],
                 (oc_ref[...] * silu[:, 256:512]).astype(BF16), ogd_ref[...]]
    merged = jnp.zeros((x.shape[0], D_MODEL), F32)
    for n in range(4):
        cols = slice(n * D_MODEL, (n + 1) * D_MODEL)
        gate = _sigmoid(_dot(h, wm_ref[:, cols]) + bm_ref[:, cols])
        merged = merged + gate * _dot(branch_in[n], wb_ref[n])
    y = _dot(merged.astype(BF16), wo_ref[...])
    x1 = x + _rmsnorm_rows(y, gpost_ref[...])
    ple = _dot(p_ref[...].astype(BF16), wple_ref[...])
    out_ref[...] = x1 + ple * _sigmoid(_dot(x1.astype(BF16), wpg_ref[...]))


def _merge_out(x, o_a, o_c, og_b, og_d, p_i, g_pre, w):
    s = x.shape[0]
    return pl.pallas_call(
        _merge_out_kernel,
        grid=(s // ROW_BLOCK,),
        in_specs=[_rows(D_MODEL), _rows(256), _rows(256), _rows(256), _rows(256), _rows(PLE_DIM),
                  _full((1, D_MODEL)), _full((D_MODEL, 512)), _full((D_MODEL, 4 * D_MODEL)),
                  _full((1, 4 * D_MODEL)), _full((4, BRANCH_WIDTH, D_MODEL)), _full((D_MODEL, D_MODEL)),
                  _full((1, D_MODEL)), _full((PLE_DIM, D_MODEL)), _full((D_MODEL, D_MODEL))],
        out_specs=_rows(D_MODEL),
        out_shape=jax.ShapeDtypeStruct((s, D_MODEL), F32),
        compiler_params=_params(),
        name="merge_out",
    )(x, o_a, o_c, og_b, og_d, p_i, g_pre, w["wg"], w["wm"], w["b_merge"], w["w_branch"], w["w_out"],
      w["g_post"], w["w_ple"], w["w_ple_gate"])


IN_COLS = 8100
IN_HEAD = 932
PREP_ROWS = 128


def _weight_layout_kernel(w_ref, wa_ref, wi_ref, wc_ref, w2_ref, wg_ref, wm_ref):
    wa_ref[0] = w_ref[0, :, 0:768].astype(BF16)
    tail = w_ref[0, :, 896:1024]
    lane = lax.broadcasted_iota(jnp.int32, (1, 128), 1)
    wi_ref[0, :, 0:128] = w_ref[0, :, 768:896].astype(BF16)
    wi_ref[0, :, 128:256] = jnp.where(lane < IDX_DIM, tail, 0.0).astype(BF16)
    wi_ref[0, :, 256:384] = jnp.where(lane < N_IDX_HEADS, pltpu.roll(tail, 128 - IDX_DIM, 1), 0.0).astype(BF16)

    def rest(lo, hi):
        return w_ref[0, :, IN_HEAD + lo:IN_HEAD + hi].astype(BF16)

    w2_ref[0, :, 0:768] = rest(0, 768)
    wc_ref[0] = rest(768, 1536)
    w2_ref[0, :, 768:1280] = rest(1536, 2048)
    wg_ref[0, :, 0:256] = rest(2048, 2304)
    w2_ref[0, :, 1280:1536] = rest(2304, 2560)
    wg_ref[0, :, 256:512] = rest(2560, 2816)
    w2_ref[0, :, 1536:1792] = rest(2816, 3072)
    wm_ref[0] = rest(3072, 7168)


def _weight_layout(w_in):
    depth = w_in.shape[0]
    assert w_in.shape[1:] == (D_MODEL, IN_COLS)
    widths = (768, 384, 768, 1792, 512, 4096)
    return pl.pallas_call(
        _weight_layout_kernel,
        grid=(depth, D_MODEL // PREP_ROWS),
        in_specs=[pl.BlockSpec((1, PREP_ROWS, IN_COLS), lambda l, r: (l, r, 0))],
        out_specs=[pl.BlockSpec((1, PREP_ROWS, n), lambda l, r: (l, r, 0)) for n in widths],
        out_shape=[jax.ShapeDtypeStruct((depth, D_MODEL, n), BF16) for n in widths],
        compiler_params=_params(2),
        name="weight_layout",
    )(w_in)


def _layer_weights(i, w_proj, conv_w, conv_b, ln_g, ln_b, w_spatial, b_spatial, b_merge, w_branch, w_out,
                   g_post, w_ple, w_ple_gate):
    row = lambda v: v[None, :]
    wa, wi, wc, w2, wg, wm = (w[i] for w in w_proj)
    return {
        "wa": wa, "wi": wi, "wc": wc, "w2": w2, "wg": wg, "wm": wm,
        "conv_w": jnp.pad(conv_w[i], ((0, 8 - CONV_WIDTH), (0, 0))),
        "conv_b": row(conv_b[i]), "ln_g": row(ln_g[i]), "ln_b": row(ln_b[i]),
        "w_spatial": w_spatial[i],
        "b_spatial": jnp.repeat(b_spatial[i].T, GROUP_DIM, axis=1),
        "b_merge": row(b_merge[i]),
        "w_branch": w_branch[i].astype(BF16), "w_out": w_out[i].astype(BF16), "g_post": row(g_post[i]),
        "w_ple": w_ple[i].astype(BF16), "w_ple_gate": w_ple_gate[i].astype(BF16),
    }


def kernel(x, p, positions, g_pre, w_in, conv_w, conv_b, ln_g, ln_b, w_spatial, b_spatial, b_merge,
           w_branch, w_out, g_post, w_ple, w_ple_gate):
    batch, s, _ = x.shape
    assert batch == 1 and s % ROW_BLOCK == 0 and s % ATT_BLOCK == 0 and ATT_BLOCK >= min(TOPK_MAX, s // 4)
    depth = w_in.shape[0]
    xs = x[0]
    tables = _rope_tables(positions[0][:, None])
    w_proj = _weight_layout(w_in)
    for i in range(depth):
        w = _layer_weights(i, w_proj, conv_w, conv_b, ln_g, ln_b, w_spatial, b_spatial, b_merge, w_branch,
                           w_out, g_post, w_ple, w_ple_gate)
        g = g_pre[i][None, :]
        qa, ka, vat, iq, ik, iwt, cq, ck, cv = _attn_proj(xs, g, w, tables)
        og_b, og_d = _local_mix(xs, g, w)
        o_a = _dsa_attention(qa, iq, iwt, ka, vat, ik)
        o_c = _sb_attention(cq, ck, cv)
        xs = _merge_out(xs, o_a, o_c, og_b, og_d, p[i][0], g, w)
    return xs[None]
```

```python
import functools

import jax
import jax.numpy as jnp
from jax import lax
from jax.experimental import pallas as pl
from jax.experimental.pallas import tpu as pltpu

D_MODEL = 1024
BRANCH_WIDTH = 256
HEAD_DIM = 64
N_HEADS = 4
N_IDX_HEADS = 4
IDX_DIM = 32
TOPK_MAX = 256
CONV_WIDTH = 3
CHUNK = 128
N_GROUPS = 4
GROUP_DIM = BRANCH_WIDTH // N_GROUPS
PLE_DIM = 256
ROPE_THETA = 10000.0
EPS = 1e-6
IDX_W_SCALE = (N_IDX_HEADS * IDX_DIM) ** -0.5
QK_SCALE = HEAD_DIM ** -0.5
LOG2_E = 1.4426950408889634

ROW_BLOCK = 256
ATT_BLOCK = 256
DSA_KEY_TILE = 1024
TIE_BLOCK = 256
COUNT_ROWS = 1024
TAIL_ROWS = 256
CONV_HALO = 8
MASK_BIAS = -1e30
SCORE_FLOOR = float(jnp.finfo(jnp.float32).min)
SB_DEAD = 152.0
CAP_SLACK = 1.03
MIN_DENOMINATOR = 2.0 ** -64
VMEM_LIMIT = 56 * 1024 * 1024

BF16 = jnp.bfloat16
F32 = jnp.float32
NT_DIMS = (((1,), (1,)), ((), ()))


def _dot(a, b):
    return jnp.dot(a, b, preferred_element_type=F32)


def _dot_nt(a, b):
    return lax.dot_general(a, b, NT_DIMS, preferred_element_type=F32)


def _rmsnorm_rows(x, g):
    return x * lax.rsqrt(jnp.mean(x * x, axis=-1, keepdims=True) + EPS) * g


def _sigmoid(x):
    return 1.0 / (1.0 + jnp.exp(-x))


def _params(n_grid_dims=1):
    return pltpu.CompilerParams(
        dimension_semantics=("arbitrary",) * n_grid_dims, vmem_limit_bytes=VMEM_LIMIT)


def _full(shape):
    return pl.BlockSpec(shape, lambda i: (0,) * len(shape))


def _rows(width, block=ROW_BLOCK):
    return pl.BlockSpec((block, width), lambda i: (i, 0))


def _tile_lanes(x, period):
    lane = lax.broadcasted_iota(jnp.int32, (1, 128), 1)
    x = jnp.where(lane < period, x, 0.0)
    while period < 128:
        x = x + pltpu.roll(x, period, 1)
        period *= 2
    return x


def _rope_table_kernel(pos_ref, freq_ref, s64_ref, s32_ref, cos64_ref, sin64_ref, cos32_ref, sin32_ref):
    angle = pos_ref[...].astype(F32) * freq_ref[...]
    cos, sin = jnp.cos(angle), jnp.sin(angle)
    n64, n32 = HEAD_DIM // 2, IDX_DIM // 2
    cos_h, sin_h = _tile_lanes(cos, n64), _tile_lanes(sin, n64)
    cos64_ref[...] = jnp.concatenate([cos_h, cos_h], axis=1)
    sin64_ref[...] = jnp.concatenate([sin_h, sin_h], axis=1) * s64_ref[...]
    cos32_ref[...] = _tile_lanes(pltpu.roll(cos, 128 - n64, 1), n32)
    sin32_ref[...] = _tile_lanes(pltpu.roll(sin, 128 - n64, 1), n32) * s32_ref[...]


def _rope_sign(d, width):
    lane = jnp.arange(width)
    return jnp.where((lane % d) < d // 2, -1.0, 1.0).astype(F32)[None, :]


def _rope_tables(pos_col):
    s = pos_col.shape[0]
    inv_freq = lambda d: ROPE_THETA ** (-jnp.arange(0, d, 2, dtype=F32) / d)
    freq = jnp.concatenate([inv_freq(HEAD_DIM), inv_freq(IDX_DIM), jnp.zeros((80,), F32)])[None, :]
    s64, s32 = _rope_sign(HEAD_DIM, BRANCH_WIDTH), _rope_sign(IDX_DIM, N_IDX_HEADS * IDX_DIM)
    return pl.pallas_call(
        _rope_table_kernel,
        grid=(s // ROW_BLOCK,),
        in_specs=[_rows(1), _full((1, 128)), _full((1, 256)), _full((1, 128))],
        out_specs=[_rows(256), _rows(256), _rows(128), _rows(128)],
        out_shape=[jax.ShapeDtypeStruct((s, 256), F32), jax.ShapeDtypeStruct((s, 256), F32),
                   jax.ShapeDtypeStruct((s, 128), F32), jax.ShapeDtypeStruct((s, 128), F32)],
        compiler_params=_params(),
        name="rope_tables",
    )(pos_col, freq, s64, s32)


def _swap_halves(x, d):
    n = x.shape[1]
    lane = lax.broadcasted_iota(jnp.int32, (1, n), 1)
    return jnp.where(lane % d < d // 2, pltpu.roll(x, n - d // 2, 1), pltpu.roll(x, d // 2, 1))


def _attn_proj_kernel(x_ref, g_ref, wa_ref, wi_ref, wc_ref,
                      cos64_ref, sin64_ref, cos32_ref, sin32_ref,
                      qa_ref, ka_ref, vat_ref, iq_ref, ik_ref, iwt_ref, cq_ref, ck_ref, cv_ref):
    h = _rmsnorm_rows(x_ref[...], g_ref[...]).astype(BF16)
    c64, s64 = cos64_ref[...], sin64_ref[...]
    c32, s32 = cos32_ref[...], sin32_ref[...]
    pa = _dot(h, wa_ref[...])
    q, k = pa[:, 0:256], pa[:, 256:512]
    qa_ref[...] = (q * c64 + _swap_halves(q, HEAD_DIM) * s64) * (QK_SCALE * LOG2_E)
    ka_ref[...] = (k * c64 + _swap_halves(k, HEAD_DIM) * s64).astype(BF16)
    vat_ref[0] = pa[:, 512:768].T.astype(BF16)
    pi = _dot(h, wi_ref[...])
    iq, ik = pi[:, 0:128], pi[:, 128:256]
    iq_ref[...] = iq * c32 + _swap_halves(iq, IDX_DIM) * s32
    ik_ref[...] = (ik * c32 + _swap_halves(ik, IDX_DIM) * s32)[:, 0:IDX_DIM].astype(BF16)
    iwt_ref[...] = (pi[:, 256:384] * IDX_W_SCALE).T[0:8, :]
    pc = _dot(h, wc_ref[...])
    cq_ref[...] = (pc[:, 0:256] * (QK_SCALE * LOG2_E)).astype(BF16)
    ck_ref[...] = pc[:, 256:512].astype(BF16)
    cv_ref[...] = pc[:, 512:768].astype(BF16)


def _attn_proj(x, g_pre, w, tables):
    s = x.shape[0]
    nb = s // ROW_BLOCK
    cos64, sin64, cos32, sin32 = tables
    return pl.pallas_call(
        _attn_proj_kernel,
        grid=(nb,),
        in_specs=[_rows(D_MODEL), _full((1, D_MODEL)), _full((D_MODEL, 768)), _full((D_MODEL, 384)),
                  _full((D_MODEL, 768)), _rows(256), _rows(256), _rows(128), _rows(128)],
        out_specs=[_rows(256), _rows(256), pl.BlockSpec((1, 256, ROW_BLOCK), lambda i: (i, 0, 0)),
                   _rows(128), _rows(IDX_DIM), pl.BlockSpec((8, ROW_BLOCK), lambda i: (0, i)),
                   _rows(256), _rows(256), _rows(256)],
        out_shape=[jax.ShapeDtypeStruct((s, 256), F32), jax.ShapeDtypeStruct((s, 256), BF16),
                   jax.ShapeDtypeStruct((nb, 256, ROW_BLOCK), BF16),
                   jax.ShapeDtypeStruct((s, 128), F32), jax.ShapeDtypeStruct((s, IDX_DIM), BF16),
                   jax.ShapeDtypeStruct((8, s), F32),
                   jax.ShapeDtypeStruct((s, 256), BF16), jax.ShapeDtypeStruct((s, 256), BF16),
                   jax.ShapeDtypeStruct((s, 256), BF16)],
        compiler_params=_params(),
        name="attn_proj",
    )(x, g_pre, w["wa"], w["wi"], w["wc"], cos64, sin64, cos32, sin32)


def _local_mix_kernel(x_ref, g_ref, w_ref, convw_ref, convb_ref, lng_ref, lnb_ref, ws_ref, bs_ref,
                      ogb_ref, ogd_ref, ypad_ref):
    t = x_ref.shape[0]

    @pl.when(pl.program_id(0) == 0)
    def _():
        ypad_ref[0:CONV_HALO, :] = jnp.zeros((CONV_HALO, BRANCH_WIDTH), F32)

    h = _rmsnorm_rows(x_ref[...], g_ref[...]).astype(BF16)
    pr = _dot(h, w_ref[...])
    gate_b, gate_c, x_in = pr[:, 0:256], pr[:, 256:512], pr[:, 512:768]
    d_u, d_v = pr[:, 768:1024], pr[:, 1024:1280]
    silu_b, silu_d = pr[:, 1280:1536], pr[:, 1536:1792]

    y = gate_c * x_in
    ypad_ref[CONV_HALO:CONV_HALO + t, :] = y
    y1 = ypad_ref[CONV_HALO - 1:CONV_HALO - 1 + t, :]
    y2 = ypad_ref[CONV_HALO - 2:CONV_HALO - 2 + t, :]
    conv = convw_ref[2:3, :] * y + convw_ref[1:2, :] * y1 + convw_ref[0:1, :] * y2
    ypad_ref[0:CONV_HALO, :] = y[t - CONV_HALO:t, :]
    o_b = gate_b * (conv + convb_ref[...])
    ogb_ref[...] = (o_b * (silu_b * _sigmoid(silu_b))).astype(BF16)

    mu = jnp.mean(d_v, axis=-1, keepdims=True)
    dc = d_v - mu
    var = jnp.mean(dc * dc, axis=-1, keepdims=True)
    vn = dc * lax.rsqrt(var + EPS) * lng_ref[...] + lnb_ref[...]
    group = lax.broadcasted_iota(jnp.int32, (1, BRANCH_WIDTH), 1) // GROUP_DIM
    tril = (lax.broadcasted_iota(jnp.int32, (CHUNK, CHUNK), 0)
            >= lax.broadcasted_iota(jnp.int32, (CHUNK, CHUNK), 1))
    wm = [jnp.where(tril, ws_ref[g], 0.0).astype(BF16) for g in range(N_GROUPS)]
    mixed = []
    for c in range(t // CHUNK):
        vc = vn[c * CHUNK:(c + 1) * CHUNK, :]
        m = bs_ref[...]
        for g in range(N_GROUPS):
            m = m + _dot(wm[g], jnp.where(group == g, vc, 0.0).astype(BF16))
        mixed.append(m)
    o_d = d_u * jnp.concatenate(mixed, axis=0)
    ogd_ref[...] = (o_d * (silu_d * _sigmoid(silu_d))).astype(BF16)


def _local_mix(x, g_pre, w):
    s = x.shape[0]
    return pl.pallas_call(
        _local_mix_kernel,
        grid=(s // ROW_BLOCK,),
        in_specs=[_rows(D_MODEL), _full((1, D_MODEL)), _full((D_MODEL, 1792)), _full((8, 256)),
                  _full((1, 256)), _full((1, 256)), _full((1, 256)),
                  _full((N_GROUPS, CHUNK, CHUNK)), _full((CHUNK, 256))],
        out_specs=[_rows(256), _rows(256)],
        out_shape=[jax.ShapeDtypeStruct((s, 256), BF16), jax.ShapeDtypeStruct((s, 256), BF16)],
        scratch_shapes=[pltpu.VMEM((ROW_BLOCK + CONV_HALO, BRANCH_WIDTH), F32)],
        compiler_params=_params(),
        name="local_mix",
    )(x, g_pre, w["w2"], w["conv_w"], w["conv_b"], w["ln_g"], w["ln_b"], w["w_spatial"], w["b_spatial"])


I16 = jnp.int16
I16_MIN = -32768
PACK_ROWS = 16
ACC_ROWS = HEAD_DIM + PACK_ROWS


def _sortable_halves(score):
    bits = lax.bitcast_convert_type(score, jnp.int32)
    key = bits ^ ((bits >> 31) & jnp.int32(0x7FFFFFFF))
    hi = (key >> 16).astype(I16)
    lo = ((key & jnp.int32(0xFFFF)) + jnp.int32(I16_MIN)).astype(I16)
    return hi, lo


def _rows16(row32):
    return jnp.broadcast_to(row32, (PACK_ROWS, row32.shape[1])).astype(I16)


def _over_rows(steps, fn, carry):
    n_big, n_tail = steps
    carry = lax.fori_loop(
        0, n_big, lambda c, x: fn(pl.multiple_of(c * COUNT_ROWS, COUNT_ROWS), COUNT_ROWS, x), carry)
    tail_start = n_big * COUNT_ROWS
    return lax.fori_loop(
        0, n_tail, lambda c, x: fn(pl.multiple_of(tail_start + c * TAIL_ROWS, TAIL_ROWS), TAIL_ROWS, x), carry)


def _count_pass(ref, steps, cand):
    q = ref.shape[1]
    cand16 = _rows16(cand)
    one, zero = jnp.ones((PACK_ROWS, q), I16), jnp.zeros((PACK_ROWS, q), I16)
    n_acc = 4

    def body(base, rows, accs):
        accs = list(accs)
        blk = ref[pl.ds(base, rows), :]
        for r in range(rows // PACK_ROWS):
            v = blk[PACK_ROWS * r:PACK_ROWS * (r + 1), :]
            accs[r % n_acc] = accs[r % n_acc] + jnp.where(v >= cand16, one, zero)
        return tuple(accs)

    assert ref.shape[0] // (PACK_ROWS * n_acc) < 2 ** 15
    accs = _over_rows(steps, body, (zero,) * n_acc)
    total = sum(a.astype(jnp.int32) for a in accs)
    return jnp.sum(total, axis=0, keepdims=True)


def _bisect16(ref, steps, target):
    zero = jnp.zeros_like(target)

    def probe(cand, t, above):
        cnt = _count_pass(ref, steps, cand)
        enough = cnt >= target
        return jnp.where(enough, cand, t), jnp.where(enough, above, cnt)

    def step(b, state):
        t, above = state
        return probe(t | (jnp.int32(1) << (14 - b)), t, above)

    return lax.fori_loop(0, 15, step, probe(zero, jnp.full_like(target, I16_MIN), zero))


def _dsa_kernel(q_ref, iq_ref, iwt_ref, k_ref, vt_ref, ik_ref, o_ref,
                hi_ref, lo_ref, qm_ref, iqt_ref, acc_ref, m_ref, knorm_ref, *, topk):
    nq = q_ref.shape[0]
    kb = DSA_KEY_TILE
    i = pl.program_id(0)
    t_idx = i * nq + lax.broadcasted_iota(jnp.int32, (1, nq), 1)
    n_below = (i * nq) // kb
    left = ((i + 1) * nq - n_below * kb) // TAIL_ROWS
    n_wide = (left > kb // (2 * TAIL_ROWS)).astype(jnp.int32)
    n_tail = left * (1 - n_wide)

    def over_tiles(fn, carry):
        carry = lax.fori_loop(0, n_below, lambda j, c: fn(pl.multiple_of(j * kb, kb), kb, c, False), carry)
        start = pl.multiple_of(n_below * kb, kb)
        carry = lax.fori_loop(0, n_wide, lambda j, c: fn(start, kb, c, True), carry)
        return lax.fori_loop(
            0, n_tail, lambda j, c: fn(pl.multiple_of(start + j * TAIL_ROWS, TAIL_ROWS), TAIL_ROWS, c, True), carry)

    qt = q_ref[...].T
    head_of_row = lax.broadcasted_iota(jnp.int32, (BRANCH_WIDTH, 1), 0) // HEAD_DIM
    for h in range(N_HEADS):
        qm_ref[h] = jnp.where(head_of_row == h, qt, 0.0).astype(BF16)
    iqt_ref[...] = iq_ref[...].T.astype(BF16)

    @pl.when(i == 0)
    def _():
        head_of_lane = (lax.broadcasted_iota(jnp.int32, (BRANCH_WIDTH, 1), 0) // HEAD_DIM
                        == lax.broadcasted_iota(jnp.int32, (1, 128), 1)).astype(BF16)

        def widest(c, best):
            kf = k_ref[pl.ds(pl.multiple_of(c * kb, kb), kb), :].astype(F32)
            return jnp.maximum(best, jnp.max(_dot((kf * kf).astype(BF16), head_of_lane), axis=0, keepdims=True))

        best = lax.fori_loop(0, k_ref.shape[0] // kb, widest, jnp.zeros((1, 128), F32))
        knorm_ref[...] = jnp.broadcast_to(best, knorm_ref.shape)

    logit_cap = jnp.zeros((1, nq), F32)
    for h in range(N_HEADS):
        qf = qm_ref[h].astype(F32)
        q_sq = jnp.sum(qf * qf, axis=0, keepdims=True)
        logit_cap = jnp.maximum(logit_cap, jnp.sqrt(q_sq * knorm_ref[0:1, h:h + 1]) * CAP_SLACK + CAP_SLACK)
    w_rows = [iwt_ref[h:h + 1, :] for h in range(N_IDX_HEADS)]

    def score_tile(base, rows, carry, on_diagonal):
        ikb = ik_ref[pl.ds(base, rows), :]
        sc = w_rows[0] * jnp.maximum(_dot(ikb, iqt_ref[0:IDX_DIM, :]), 0.0)
        for h in range(1, N_IDX_HEADS):
            logit = _dot(ikb, iqt_ref[h * IDX_DIM:(h + 1) * IDX_DIM, :])
            sc = sc + w_rows[h] * jnp.maximum(logit, 0.0)
        if on_diagonal:
            sc = jnp.where(base + lax.broadcasted_iota(jnp.int32, (rows, 1), 0) <= t_idx, sc, SCORE_FLOOR)
        hi_ref[pl.ds(base, rows), :], lo_ref[pl.ds(base, rows), :] = _sortable_halves(sc)
        return carry

    over_tiles(score_tile, 0)

    causal_rows = (i + 1) * nq
    steps = (causal_rows // COUNT_ROWS, (causal_rows % COUNT_ROWS) // TAIL_ROWS)
    want = jnp.full((1, nq), topk, jnp.int32)
    t_hi, above_hi = _bisect16(hi_ref, steps, want)
    t_hi16 = _rows16(t_hi)
    floor16 = jnp.full((PACK_ROWS, nq), I16_MIN, I16)

    def keep_bucket(base, rows, carry):
        hi, lo = hi_ref[pl.ds(base, rows), :], lo_ref[pl.ds(base, rows), :]
        lo_ref[pl.ds(base, rows), :] = jnp.concatenate(
            [jnp.where(hi[PACK_ROWS * r:PACK_ROWS * (r + 1), :] == t_hi16,
                       lo[PACK_ROWS * r:PACK_ROWS * (r + 1), :], floor16)
             for r in range(rows // PACK_ROWS)], axis=0)
        return carry

    _over_rows(steps, keep_bucket, 0)
    t_lo, above_lo = _bisect16(lo_ref, steps, want - above_hi)
    quota = (topk - above_hi - above_lo).astype(F32)
    t_hi = jnp.where(t_idx < topk, jnp.int32(I16_MIN), t_hi)
    t_hi16, t_lo16 = _rows16(t_hi), _rows16(t_lo)

    incl_lower = (lax.broadcasted_iota(jnp.int32, (TIE_BLOCK, TIE_BLOCK), 0)
                  >= lax.broadcasted_iota(jnp.int32, (TIE_BLOCK, TIE_BLOCK), 1)).astype(BF16)
    one16, zero16 = jnp.ones((PACK_ROWS, nq), BF16), jnp.zeros((PACK_ROWS, nq), BF16)

    def attend_tile(base, rows, ties_before, on_diagonal, running_max):
        hi_t, lo_t = hi_ref[pl.ds(base, rows), :], lo_ref[pl.ds(base, rows), :]
        above, tie = [], []
        for r in range(rows // PACK_ROWS):
            hi16 = hi_t[PACK_ROWS * r:PACK_ROWS * (r + 1), :]
            lo16 = lo_t[PACK_ROWS * r:PACK_ROWS * (r + 1), :]
            bucket = hi16 == t_hi16
            above.append(jnp.where((hi16 > t_hi16) | (bucket & (lo16 > t_lo16)), one16, zero16))
            tie.append(jnp.where(bucket & (lo16 == t_lo16), one16, zero16))
        above, tie = jnp.concatenate(above, axis=0), jnp.concatenate(tie, axis=0)
        parts, running = [], ties_before
        for u in range(rows // TIE_BLOCK):
            parts.append(_dot(incl_lower, tie[u * TIE_BLOCK:(u + 1) * TIE_BLOCK, :]) + running)
            running = parts[-1][TIE_BLOCK - 1:TIE_BLOCK, :]
        ties_upto = jnp.concatenate(parts, axis=0)
        keep = above + tie * jnp.where(ties_upto <= quota, 1.0, 0.0).astype(BF16)
        if on_diagonal:
            causal = base + lax.broadcasted_iota(jnp.int32, (rows, 1), 0) <= t_idx
            keep = keep * jnp.where(causal, 1.0, 0.0).astype(BF16)
        bias = ((keep - 1.0) * (-MASK_BIAS)).astype(F32)
        kt = k_ref[pl.ds(base, rows), :]
        n_sub = rows // ATT_BLOCK
        vts = [vt_ref[base // ATT_BLOCK + u] for u in range(n_sub)]
        ones_rows = jnp.ones((PACK_ROWS, ATT_BLOCK), BF16)

        def value_matmul(h, p):
            hs = slice(h * HEAD_DIM, (h + 1) * HEAD_DIM)
            pb = p.astype(BF16)
            return sum(_dot(jnp.concatenate([vts[u][hs, :], ones_rows], axis=0),
                            pb[u * ATT_BLOCK:(u + 1) * ATT_BLOCK, :]) for u in range(n_sub))

        if running_max:
            logits = [_dot(kt, qm_ref[h]) + bias for h in range(N_HEADS)]
            m_old = [m_ref[h:h + 1, :] for h in range(N_HEADS)]
            m_new = [jnp.maximum(m_old[h], jnp.max(logits[h], axis=0, keepdims=True)) for h in range(N_HEADS)]
            alpha = [jnp.exp2(m_old[h] - m_new[h]) for h in range(N_HEADS)]
            probs = [jnp.exp2(logits[h] - m_new[h]) for h in range(N_HEADS)]
            for h in range(N_HEADS):
                m_ref[h:h + 1, :] = m_new[h]
                head = slice(h * ACC_ROWS, (h + 1) * ACC_ROWS)
                acc_ref[head, :] = alpha[h] * acc_ref[head, :] + value_matmul(h, probs[h])
        else:
            shift = bias - logit_cap
            probs = [jnp.exp2(_dot(kt, qm_ref[h]) + shift) for h in range(N_HEADS)]
            for h in range(N_HEADS):
                head = slice(h * ACC_ROWS, (h + 1) * ACC_ROWS)
                acc_ref[head, :] += value_matmul(h, probs[h])
        return running

    def attend(running_max):
        acc_ref[...] = jnp.zeros(acc_ref.shape, F32)
        over_tiles(functools.partial(attend_tile, running_max=running_max), jnp.zeros((1, nq), F32))

    attend(running_max=False)
    denominators = jnp.concatenate(
        [acc_ref[h * ACC_ROWS + HEAD_DIM:h * ACC_ROWS + HEAD_DIM + 1, :] for h in range(N_HEADS)], axis=0)
    weakest = jnp.min(jnp.min(denominators, axis=1, keepdims=True), axis=0, keepdims=True)[0, 0]

    @pl.when(jnp.logical_not(weakest >= MIN_DENOMINATOR))
    def _():
        m_ref[...] = jnp.full(m_ref.shape, MASK_BIAS, F32)
        attend(running_max=True)

    heads = [acc_ref[h * ACC_ROWS:h * ACC_ROWS + HEAD_DIM, :]
             / acc_ref[h * ACC_ROWS + HEAD_DIM:h * ACC_ROWS + HEAD_DIM + 1, :] for h in range(N_HEADS)]
    o_ref[...] = jnp.concatenate(heads, axis=0).T


def _dsa_attention(qa, iq, iwt, ka, vat, ik):
    s = qa.shape[0]
    nq = ATT_BLOCK
    topk = min(TOPK_MAX, s // 4)
    assert s % DSA_KEY_TILE == 0 and DSA_KEY_TILE % ATT_BLOCK == 0
    assert ATT_BLOCK % TAIL_ROWS == 0 and COUNT_ROWS % TAIL_ROWS == 0 and DSA_KEY_TILE % TIE_BLOCK == 0
    return pl.pallas_call(
        functools.partial(_dsa_kernel, topk=topk),
        grid=(s // nq,),
        in_specs=[_rows(256, nq), _rows(128, nq), pl.BlockSpec((8, nq), lambda i: (0, i)),
                  _full((s, 256)), _full((s // ATT_BLOCK, 256, ATT_BLOCK)), _full((s, IDX_DIM))],
        out_specs=_rows(256, nq),
        out_shape=jax.ShapeDtypeStruct((s, 256), F32),
        scratch_shapes=[pltpu.VMEM((s, nq), I16), pltpu.VMEM((s, nq), I16), pltpu.VMEM((N_HEADS, 256, nq), BF16),
                        pltpu.VMEM((128, nq), BF16), pltpu.VMEM((N_HEADS * ACC_ROWS, nq), F32),
                        pltpu.VMEM((8, nq), F32), pltpu.VMEM((8, 128), F32)],
        compiler_params=_params(),
        name="dsa_attn",
    )(qa, iq, iwt, ka, vat, ik)


def _sb_kernel(q_ref, k_ref, v_ref, o_ref, acc_ref, carry_ref):
    nq = q_ref.shape[0]
    kb = ATT_BLOCK
    i = pl.program_id(0)
    q = q_ref[...]
    lane_head = lax.broadcasted_iota(jnp.int32, (1, BRANCH_WIDTH), 1) // HEAD_DIM
    qh = [jnp.where(lane_head == h, q, jnp.zeros_like(q)) for h in range(N_HEADS)]
    later = (lax.broadcasted_iota(jnp.int32, (2 * kb, kb), 0) % kb
             > lax.broadcasted_iota(jnp.int32, (2 * kb, kb), 1)).astype(BF16)
    acc_ref[...] = jnp.zeros(acc_ref.shape, F32)
    carry_ref[...] = jnp.zeros(carry_ref.shape, F32)

    def walk_tile(j, on_diagonal):
        base = pl.multiple_of(j * kb, kb)
        kt = k_ref[pl.ds(base, kb), :]
        vt = v_ref[pl.ds(base, kb), :]
        if on_diagonal:
            strict = (lax.broadcasted_iota(jnp.int32, (1, kb), 1)
                      < lax.broadcasted_iota(jnp.int32, (nq, 1), 0))
        heads = range(N_HEADS)
        z = [_dot_nt(qh[h], kt) for h in heads]
        softplus = [jnp.maximum(z[h], 0.0) + jnp.log2(1.0 + jnp.exp2(-jnp.abs(z[h]))) for h in heads]
        sp = [jnp.where(strict, softplus[h], 0.0) for h in heads] if on_diagonal else softplus
        hi = [sp[h].astype(BF16) for h in heads]
        lo = [(sp[h] - hi[h].astype(F32)).astype(BF16) for h in heads]
        after = [_dot(jnp.concatenate([hi[h], lo[h]], axis=1), later) for h in heads]
        c = [carry_ref[h] for h in heads]
        wts = [jnp.exp2((z[h] - softplus[h]) - (after[h] + c[h])) for h in heads]
        if on_diagonal:
            wts = [jnp.where(strict, wts[h], 0.0) for h in heads]
        out = sum(_dot(wts[h].astype(BF16), jnp.where(lane_head == h, vt, jnp.zeros_like(vt))) for h in heads)
        acc_ref[...] += out
        lowest = jnp.full((1, 1), jnp.inf, F32)
        for h in heads:
            c_new = c[h] + (after[h][:, 0:1] + sp[h][:, 0:1])
            carry_ref[h] = c_new
            lowest = jnp.minimum(lowest, jnp.min(c_new, axis=0, keepdims=True))
        return (lowest[0, 0] < SB_DEAD).astype(jnp.int32)

    def alive(state):
        j, live = state
        return jnp.logical_and(j >= 0, live > 0)

    def walk(state):
        j, _ = state
        return j - 1, walk_tile(j, on_diagonal=False)

    lax.while_loop(alive, walk, (i - 1, walk_tile(i, on_diagonal=True)))
    o_ref[...] = acc_ref[...]


def _sb_attention(cq, ck, cv):
    s = cq.shape[0]
    nq = ATT_BLOCK
    return pl.pallas_call(
        _sb_kernel,
        grid=(s // nq,),
        in_specs=[_rows(256, nq), _full((s, 256)), _full((s, 256))],
        out_specs=_rows(256, nq),
        out_shape=jax.ShapeDtypeStruct((s, 256), F32),
        scratch_shapes=[pltpu.VMEM((nq, 256), F32), pltpu.VMEM((N_HEADS, nq, 1), F32)],
        compiler_params=_params(),
        name="sb_attn",
    )(cq, ck, cv)


def _merge_out_kernel(x_ref, oa_ref, oc_ref, ogb_ref, ogd_ref, p_ref, g_ref, wg_ref, wm_ref, bm_ref,
                      wb_ref, wo_ref, gpost_ref, wple_ref, wpg_ref, out_ref):
    x = x_ref[...]
    h = _rmsnorm_rows(x, g_ref[...]).astype(BF16)
    gates = _dot(h, wg_ref[...])
    silu = gates * _sigmoid(gates)
    branch_in = [(oa_ref[...] * silu[:, 0:256]).astype(BF16), ogb_ref[...],
                 (oc_ref[...] * silu[:, 256:512]).astype(BF16), ogd_ref[...]]
    merged = jnp.zeros((x.shape[0], D_MODEL), F32)
    for n in range(4):
        cols = slice(n * D_MODEL, (n + 1) * D_MODEL)
        gate = _sigmoid(_dot(h, wm_ref[:, cols]) + bm_ref[:, cols])
        merged = merged + gate * _dot(branch_in[n], wb_ref[n])
    y = _dot(merged.astype(BF16), wo_ref[...])
    x1 = x + _rmsnorm_rows(y, gpost_ref[...])
    ple = _dot(p_ref[...].astype(BF16), wple_ref[...])
    out_ref[...] = x1 + ple * _sigmoid(_dot(x1.astype(BF16), wpg_ref[...]))


def _merge_out(x, o_a, o_c, og_b, og_d, p_i, g_pre, w):
    s = x.shape[0]
    return pl.pallas_call(
        _merge_out_kernel,
        grid=(s // ROW_BLOCK,),
        in_specs=[_rows(D_MODEL), _rows(256), _rows(256), _rows(256), _rows(256), _rows(PLE_DIM),
                  _full((1, D_MODEL)), _full((D_MODEL, 512)), _full((D_MODEL, 4 * D_MODEL)),
                  _full((1, 4 * D_MODEL)), _full((4, BRANCH_WIDTH, D_MODEL)), _full((D_MODEL, D_MODEL)),
                  _full((1, D_MODEL)), _full((PLE_DIM, D_MODEL)), _full((D_MODEL, D_MODEL))],
        out_specs=_rows(D_MODEL),
        out_shape=jax.ShapeDtypeStruct((s, D_MODEL), F32),
        compiler_params=_params(),
        name="merge_out",
    )(x, o_a, o_c, og_b, og_d, p_i, g_pre, w["wg"], w["wm"], w["b_merge"], w["w_branch"], w["w_out"],
      w["g_post"], w["w_ple"], w["w_ple_gate"])


IN_COLS = 8100
IN_HEAD = 932
PREP_ROWS = 128


def _weight_layout_kernel(w_ref, wa_ref, wi_ref, wc_ref, w2_ref, wg_ref, wm_ref):
    wa_ref[0] = w_ref[0, :, 0:768].astype(BF16)
    tail = w_ref[0, :, 896:1024]
    lane = lax.broadcasted_iota(jnp.int32, (1, 128), 1)
    wi_ref[0, :, 0:128] = w_ref[0, :, 768:896].astype(BF16)
    wi_ref[0, :, 128:256] = jnp.where(lane < IDX_DIM, tail, 0.0).astype(BF16)
    wi_ref[0, :, 256:384] = jnp.where(lane < N_IDX_HEADS, pltpu.roll(tail, 128 - IDX_DIM, 1), 0.0).astype(BF16)

    def rest(lo, hi):
        return w_ref[0, :, IN_HEAD + lo:IN_HEAD + hi].astype(BF16)

    w2_ref[0, :, 0:768] = rest(0, 768)
    wc_ref[0] = rest(768, 1536)
    w2_ref[0, :, 768:1280] = rest(1536, 2048)
    wg_ref[0, :, 0:256] = rest(2048, 2304)
    w2_ref[0, :, 1280:1536] = rest(2304, 2560)
    wg_ref[0, :, 256:512] = rest(2560, 2816)
    w2_ref[0, :, 1536:1792] = rest(2816, 3072)
    wm_ref[0] = rest(3072, 7168)


def _weight_layout(w_in):
    depth = w_in.shape[0]
    assert w_in.shape[1:] == (D_MODEL, IN_COLS)
    widths = (768, 384, 768, 1792, 512, 4096)
    return pl.pallas_call(
        _weight_layout_kernel,
        grid=(depth, D_MODEL // PREP_ROWS),
        in_specs=[pl.BlockSpec((1, PREP_ROWS, IN_COLS), lambda l, r: (l, r, 0))],
        out_specs=[pl.BlockSpec((1, PREP_ROWS, n), lambda l, r: (l, r, 0)) for n in widths],
        out_shape=[jax.ShapeDtypeStruct((depth, D_MODEL, n), BF16) for n in widths],
        compiler_params=_params(2),
        name="weight_layout",
    )(w_in)


def _layer_weights(i, w_proj, conv_w, conv_b, ln_g, ln_b, w_spatial, b_spatial, b_merge, w_branch, w_out,
                   g_post, w_ple, w_ple_gate):
    row = lambda v: v[None, :]
    wa, wi, wc, w2, wg, wm = (w[i] for w in w_proj)
    return {
        "wa": wa, "wi": wi, "wc": wc, "w2": w2, "wg": wg, "wm": wm,
        "conv_w": jnp.pad(conv_w[i], ((0, 8 - CONV_WIDTH), (0, 0))),
        "conv_b": row(conv_b[i]), "ln_g": row(ln_g[i]), "ln_b": row(ln_b[i]),
        "w_spatial": w_spatial[i],
        "b_spatial": jnp.repeat(b_spatial[i].T, GROUP_DIM, axis=1),
        "b_merge": row(b_merge[i]),
        "w_branch": w_branch[i].astype(BF16), "w_out": w_out[i].astype(BF16), "g_post": row(g_post[i]),
        "w_ple": w_ple[i].astype(BF16), "w_ple_gate": w_ple_gate[i].astype(BF16),
    }


def kernel(x, p, positions, g_pre, w_in, conv_w, conv_b, ln_g, ln_b, w_spatial, b_spatial, b_merge,
           w_branch, w_out, g_post, w_ple, w_ple_gate):
    batch, s, _ = x.shape
    assert batch == 1 and s % ROW_BLOCK == 0 and s % ATT_BLOCK == 0 and ATT_BLOCK >= min(TOPK_MAX, s // 4)
    depth = w_in.shape[0]
    xs = x[0]
    tables = _rope_tables(positions[0][:, None])
    w_proj = _weight_layout(w_in)
    for i in range(depth):
        w = _layer_weights(i, w_proj, conv_w, conv_b, ln_g, ln_b, w_spatial, b_spatial, b_merge, w_branch,
                           w_out, g_post, w_ple, w_ple_gate)
        g = g_pre[i][None, :]
        qa, ka, vat, iq, ik, iwt, cq, ck, cv = _attn_proj(xs, g, w, tables)
        og_b, og_d = _local_mix(xs, g, w)
        o_a = _dsa_attention(qa, iq, iwt, ka, vat, ik)
        o_c = _sb_attention(cq, ck, cv)
        xs = _merge_out(xs, o_a, o_c, og_b, og_d, p[i][0], g, w)
    return xs[None]
```

```python
import functools

import jax
import jax.numpy as jnp
from jax import lax
from jax.experimental import pallas as pl
from jax.experimental.pallas import tpu as pltpu

D_MODEL = 1024
BRANCH_WIDTH = 256
HEAD_DIM = 64
N_HEADS = 4
N_IDX_HEADS = 4
IDX_DIM = 32
TOPK_MAX = 256
CONV_WIDTH = 3
CHUNK = 128
N_GROUPS = 4
GROUP_DIM = BRANCH_WIDTH // N_GROUPS
PLE_DIM = 256
ROPE_THETA = 10000.0
EPS = 1e-6
IDX_W_SCALE = (N_IDX_HEADS * IDX_DIM) ** -0.5
QK_SCALE = HEAD_DIM ** -0.5
LOG2_E = 1.4426950408889634

ROW_BLOCK = 256
ATT_BLOCK = 256
DSA_KEY_TILE = 1024
TIE_BLOCK = 256
COUNT_ROWS = 1024
TAIL_ROWS = 256
CONV_HALO = 8
MASK_BIAS = -1e30
SCORE_FLOOR = float(jnp.finfo(jnp.float32).min)
SB_DEAD = 152.0
CAP_SLACK = 1.03
MIN_DENOMINATOR = 2.0 ** -64
VMEM_LIMIT = 56 * 1024 * 1024

BF16 = jnp.bfloat16
F32 = jnp.float32
NT_DIMS = (((1,), (1,)), ((), ()))


def _dot(a, b):
    return jnp.dot(a, b, preferred_element_type=F32)


def _dot_nt(a, b):
    return lax.dot_general(a, b, NT_DIMS, preferred_element_type=F32)


def _rmsnorm_rows(x, g):
    return x * lax.rsqrt(jnp.mean(x * x, axis=-1, keepdims=True) + EPS) * g


def _sigmoid(x):
    return 1.0 / (1.0 + jnp.exp(-x))


def _params(n_grid_dims=1):
    return pltpu.CompilerParams(
        dimension_semantics=("arbitrary",) * n_grid_dims, vmem_limit_bytes=VMEM_LIMIT)


def _full(shape):
    return pl.BlockSpec(shape, lambda i: (0,) * len(shape))


def _rows(width, block=ROW_BLOCK):
    return pl.BlockSpec((block, width), lambda i: (i, 0))


def _tile_lanes(x, period):
    lane = lax.broadcasted_iota(jnp.int32, (1, 128), 1)
    x = jnp.where(lane < period, x, 0.0)
    while period < 128:
        x = x + pltpu.roll(x, period, 1)
        period *= 2
    return x


def _rope_table_kernel(pos_ref, freq_ref, s64_ref, s32_ref, cos64_ref, sin64_ref, cos32_ref, sin32_ref):
    t = pos_ref.shape[0]
    pos = pos_ref[...].astype(F32)
    lane = lax.broadcasted_iota(jnp.int32, (1, 128), 1)
    angle = jnp.where(lane < 64, pos[0:t // 2, :], pos[t // 2:t, :]) * freq_ref[...]
    cos2, sin2 = jnp.cos(angle), jnp.sin(angle)
    cos = jnp.concatenate([cos2, pltpu.roll(cos2, 64, 1)], axis=0)
    sin = jnp.concatenate([sin2, pltpu.roll(sin2, 64, 1)], axis=0)
    n64, n32 = HEAD_DIM // 2, IDX_DIM // 2
    cos_h, sin_h = _tile_lanes(cos, n64), _tile_lanes(sin, n64)
    cos64_ref[...] = jnp.concatenate([cos_h, cos_h], axis=1)
    sin64_ref[...] = jnp.concatenate([sin_h, sin_h], axis=1) * s64_ref[...]
    cos32_ref[...] = _tile_lanes(pltpu.roll(cos, 128 - n64, 1), n32)
    sin32_ref[...] = _tile_lanes(pltpu.roll(sin, 128 - n64, 1), n32) * s32_ref[...]


def _rope_sign(d, width):
    lane = jnp.arange(width)
    return jnp.where((lane % d) < d // 2, -1.0, 1.0).astype(F32)[None, :]


def _rope_tables(pos_col):
    s = pos_col.shape[0]
    inv_freq = lambda d: ROPE_THETA ** (-jnp.arange(0, d, 2, dtype=F32) / d)
    half = jnp.concatenate([inv_freq(HEAD_DIM), inv_freq(IDX_DIM), jnp.zeros((16,), F32)])
    freq = jnp.concatenate([half, half])[None, :]
    s64, s32 = _rope_sign(HEAD_DIM, BRANCH_WIDTH), _rope_sign(IDX_DIM, N_IDX_HEADS * IDX_DIM)
    return pl.pallas_call(
        _rope_table_kernel,
        grid=(s // ROW_BLOCK,),
        in_specs=[_rows(1), _full((1, 128)), _full((1, 256)), _full((1, 128))],
        out_specs=[_rows(256), _rows(256), _rows(128), _rows(128)],
        out_shape=[jax.ShapeDtypeStruct((s, 256), F32), jax.ShapeDtypeStruct((s, 256), F32),
                   jax.ShapeDtypeStruct((s, 128), F32), jax.ShapeDtypeStruct((s, 128), F32)],
        compiler_params=_params(),
        name="rope_tables",
    )(pos_col, freq, s64, s32)


def _swap_halves(x, d):
    n = x.shape[1]
    lane = lax.broadcasted_iota(jnp.int32, (1, n), 1)
    return jnp.where(lane % d < d // 2, pltpu.roll(x, n - d // 2, 1), pltpu.roll(x, d // 2, 1))


def _attn_proj_kernel(x_ref, g_ref, wa_ref, wi_ref, wc_ref,
                      cos64_ref, sin64_ref, cos32_ref, sin32_ref,
                      qa_ref, ka_ref, vat_ref, iq_ref, ik_ref, iwt_ref, cq_ref, ck_ref, cv_ref):
    h = _rmsnorm_rows(x_ref[...], g_ref[...]).astype(BF16)
    c64, s64 = cos64_ref[...], sin64_ref[...]
    c32, s32 = cos32_ref[...], sin32_ref[...]
    pa = _dot(h, wa_ref[...])
    q, k = pa[:, 0:256], pa[:, 256:512]
    qa_ref[...] = (q * c64 + _swap_halves(q, HEAD_DIM) * s64) * (QK_SCALE * LOG2_E)
    ka_ref[...] = (k * c64 + _swap_halves(k, HEAD_DIM) * s64).astype(BF16)
    vat_ref[0] = pa[:, 512:768].T.astype(BF16)
    pi = _dot(h, wi_ref[...])
    iq, ik = pi[:, 0:128], pi[:, 128:256]
    iq_ref[...] = iq * c32 + _swap_halves(iq, IDX_DIM) * s32
    ik_ref[...] = (ik * c32 + _swap_halves(ik, IDX_DIM) * s32)[:, 0:IDX_DIM].astype(BF16)
    iwt_ref[...] = (pi[:, 256:384] * IDX_W_SCALE).T[0:8, :]
    pc = _dot(h, wc_ref[...])
    cq_ref[...] = (pc[:, 0:256] * (QK_SCALE * LOG2_E)).astype(BF16)
    ck_ref[...] = pc[:, 256:512].astype(BF16)
    cv_ref[...] = pc[:, 512:768].astype(BF16)


def _attn_proj(x, g_pre, w, tables):
    s = x.shape[0]
    nb = s // ROW_BLOCK
    cos64, sin64, cos32, sin32 = tables
    return pl.pallas_call(
        _attn_proj_kernel,
        grid=(nb,),
        in_specs=[_rows(D_MODEL), _full((1, D_MODEL)), _full((D_MODEL, 768)), _full((D_MODEL, 384)),
                  _full((D_MODEL, 768)), _rows(256), _rows(256), _rows(128), _rows(128)],
        out_specs=[_rows(256), _rows(256), pl.BlockSpec((1, 256, ROW_BLOCK), lambda i: (i, 0, 0)),
                   _rows(128), _rows(IDX_DIM), pl.BlockSpec((8, ROW_BLOCK), lambda i: (0, i)),
                   _rows(256), _rows(256), _rows(256)],
        out_shape=[jax.ShapeDtypeStruct((s, 256), F32), jax.ShapeDtypeStruct((s, 256), BF16),
                   jax.ShapeDtypeStruct((nb, 256, ROW_BLOCK), BF16),
                   jax.ShapeDtypeStruct((s, 128), F32), jax.ShapeDtypeStruct((s, IDX_DIM), BF16),
                   jax.ShapeDtypeStruct((8, s), F32),
                   jax.ShapeDtypeStruct((s, 256), BF16), jax.ShapeDtypeStruct((s, 256), BF16),
                   jax.ShapeDtypeStruct((s, 256), BF16)],
        compiler_params=_params(),
        name="attn_proj",
    )(x, g_pre, w["wa"], w["wi"], w["wc"], cos64, sin64, cos32, sin32)


def _local_mix_kernel(x_ref, g_ref, w_ref, convw_ref, convb_ref, lng_ref, lnb_ref, ws_ref, bs_ref,
                      ogb_ref, ogd_ref, ypad_ref):
    t = x_ref.shape[0]

    @pl.when(pl.program_id(0) == 0)
    def _():
        ypad_ref[0:CONV_HALO, :] = jnp.zeros((CONV_HALO, BRANCH_WIDTH), F32)

    h = _rmsnorm_rows(x_ref[...], g_ref[...]).astype(BF16)
    pr = _dot(h, w_ref[...])
    gate_b, gate_c, x_in = pr[:, 0:256], pr[:, 256:512], pr[:, 512:768]
    d_u, d_v = pr[:, 768:1024], pr[:, 1024:1280]
    silu_b, silu_d = pr[:, 1280:1536], pr[:, 1536:1792]

    y = gate_c * x_in
    ypad_ref[CONV_HALO:CONV_HALO + t, :] = y
    y1 = ypad_ref[CONV_HALO - 1:CONV_HALO - 1 + t, :]
    y2 = ypad_ref[CONV_HALO - 2:CONV_HALO - 2 + t, :]
    conv = convw_ref[2:3, :] * y + convw_ref[1:2, :] * y1 + convw_ref[0:1, :] * y2
    ypad_ref[0:CONV_HALO, :] = y[t - CONV_HALO:t, :]
    o_b = gate_b * (conv + convb_ref[...])
    ogb_ref[...] = (o_b * (silu_b * _sigmoid(silu_b))).astype(BF16)

    mu = jnp.mean(d_v, axis=-1, keepdims=True)
    dc = d_v - mu
    var = jnp.mean(dc * dc, axis=-1, keepdims=True)
    vn = dc * lax.rsqrt(var + EPS) * lng_ref[...] + lnb_ref[...]
    group = lax.broadcasted_iota(jnp.int32, (1, BRANCH_WIDTH), 1) // GROUP_DIM
    tril = (lax.broadcasted_iota(jnp.int32, (CHUNK, CHUNK), 0)
            >= lax.broadcasted_iota(jnp.int32, (CHUNK, CHUNK), 1))
    wm = [jnp.where(tril, ws_ref[g], 0.0).astype(BF16) for g in range(N_GROUPS)]
    mixed = []
    for c in range(t // CHUNK):
        vc = vn[c * CHUNK:(c + 1) * CHUNK, :]
        m = bs_ref[...]
        for g in range(N_GROUPS):
            m = m + _dot(wm[g], jnp.where(group == g, vc, 0.0).astype(BF16))
        mixed.append(m)
    o_d = d_u * jnp.concatenate(mixed, axis=0)
    ogd_ref[...] = (o_d * (silu_d * _sigmoid(silu_d))).astype(BF16)


def _local_mix(x, g_pre, w):
    s = x.shape[0]
    return pl.pallas_call(
        _local_mix_kernel,
        grid=(s // ROW_BLOCK,),
        in_specs=[_rows(D_MODEL), _full((1, D_MODEL)), _full((D_MODEL, 1792)), _full((8, 256)),
                  _full((1, 256)), _full((1, 256)), _full((1, 256)),
                  _full((N_GROUPS, CHUNK, CHUNK)), _full((CHUNK, 256))],
        out_specs=[_rows(256), _rows(256)],
        out_shape=[jax.ShapeDtypeStruct((s, 256), BF16), jax.ShapeDtypeStruct((s, 256), BF16)],
        scratch_shapes=[pltpu.VMEM((ROW_BLOCK + CONV_HALO, BRANCH_WIDTH), F32)],
        compiler_params=_params(),
        name="local_mix",
    )(x, g_pre, w["w2"], w["conv_w"], w["conv_b"], w["ln_g"], w["ln_b"], w["w_spatial"], w["b_spatial"])


I16 = jnp.int16
I16_MIN = -32768
PACK_ROWS = 16
ACC_ROWS = HEAD_DIM + PACK_ROWS


def _sortable_halves(score):
    bits = lax.bitcast_convert_type(score, jnp.int32)
    key = bits ^ ((bits >> 31) & jnp.int32(0x7FFFFFFF))
    hi = (key >> 16).astype(I16)
    lo = ((key & jnp.int32(0xFFFF)) + jnp.int32(I16_MIN)).astype(I16)
    return hi, lo


def _rows16(row32):
    return jnp.broadcast_to(row32, (PACK_ROWS, row32.shape[1])).astype(I16)


def _over_rows(steps, fn, carry):
    n_big, n_tail = steps
    carry = lax.fori_loop(
        0, n_big, lambda c, x: fn(pl.multiple_of(c * COUNT_ROWS, COUNT_ROWS), COUNT_ROWS, x), carry)
    tail_start = n_big * COUNT_ROWS
    return lax.fori_loop(
        0, n_tail, lambda c, x: fn(pl.multiple_of(tail_start + c * TAIL_ROWS, TAIL_ROWS), TAIL_ROWS, x), carry)


def _count_pass(ref, steps, cand):
    q = ref.shape[1]
    cand16 = _rows16(cand)
    one, zero = jnp.ones((PACK_ROWS, q), I16), jnp.zeros((PACK_ROWS, q), I16)
    n_acc = 4

    def body(base, rows, accs):
        accs = list(accs)
        blk = ref[pl.ds(base, rows), :]
        for r in range(rows // PACK_ROWS):
            v = blk[PACK_ROWS * r:PACK_ROWS * (r + 1), :]
            accs[r % n_acc] = accs[r % n_acc] + jnp.where(v >= cand16, one, zero)
        return tuple(accs)

    assert ref.shape[0] // (PACK_ROWS * n_acc) < 2 ** 15
    accs = _over_rows(steps, body, (zero,) * n_acc)
    total = sum(a.astype(jnp.int32) for a in accs)
    return jnp.sum(total, axis=0, keepdims=True)


def _bisect16(ref, steps, target):
    zero = jnp.zeros_like(target)

    def probe(cand, t, above):
        cnt = _count_pass(ref, steps, cand)
        enough = cnt >= target
        return jnp.where(enough, cand, t), jnp.where(enough, above, cnt)

    def step(b, state):
        t, above = state
        return probe(t | (jnp.int32(1) << (14 - b)), t, above)

    return lax.fori_loop(0, 15, step, probe(zero, jnp.full_like(target, I16_MIN), zero))


def _dsa_kernel(q_ref, iq_ref, iwt_ref, k_ref, vt_ref, ik_ref, o_ref,
                hi_ref, lo_ref, qm_ref, iqt_ref, acc_ref, m_ref, knorm_ref, *, topk):
    nq = q_ref.shape[0]
    kb = DSA_KEY_TILE
    i = pl.program_id(0)
    t_idx = i * nq + lax.broadcasted_iota(jnp.int32, (1, nq), 1)
    n_below = (i * nq) // kb
    left = ((i + 1) * nq - n_below * kb) // TAIL_ROWS
    n_wide = (left > kb // (2 * TAIL_ROWS)).astype(jnp.int32)
    n_tail = left * (1 - n_wide)

    def over_tiles(fn, carry):
        carry = lax.fori_loop(0, n_below, lambda j, c: fn(pl.multiple_of(j * kb, kb), kb, c, False), carry)
        start = pl.multiple_of(n_below * kb, kb)
        carry = lax.fori_loop(0, n_wide, lambda j, c: fn(start, kb, c, True), carry)
        return lax.fori_loop(
            0, n_tail, lambda j, c: fn(pl.multiple_of(start + j * TAIL_ROWS, TAIL_ROWS), TAIL_ROWS, c, True), carry)

    qt = q_ref[...].T
    head_of_row = lax.broadcasted_iota(jnp.int32, (BRANCH_WIDTH, 1), 0) // HEAD_DIM
    for h in range(N_HEADS):
        qm_ref[h] = jnp.where(head_of_row == h, qt, 0.0).astype(BF16)
    iqt_ref[...] = iq_ref[...].T.astype(BF16)

    @pl.when(i == 0)
    def _():
        head_of_lane = (lax.broadcasted_iota(jnp.int32, (BRANCH_WIDTH, 1), 0) // HEAD_DIM
                        == lax.broadcasted_iota(jnp.int32, (1, 128), 1)).astype(BF16)

        def widest(c, best):
            kf = k_ref[pl.ds(pl.multiple_of(c * kb, kb), kb), :].astype(F32)
            return jnp.maximum(best, jnp.max(_dot((kf * kf).astype(BF16), head_of_lane), axis=0, keepdims=True))

        best = lax.fori_loop(0, k_ref.shape[0] // kb, widest, jnp.zeros((1, 128), F32))
        knorm_ref[...] = jnp.broadcast_to(best, knorm_ref.shape)

    logit_cap = jnp.zeros((1, nq), F32)
    for h in range(N_HEADS):
        qf = qm_ref[h].astype(F32)
        q_sq = jnp.sum(qf * qf, axis=0, keepdims=True)
        logit_cap = jnp.maximum(logit_cap, jnp.sqrt(q_sq * knorm_ref[0:1, h:h + 1]) * CAP_SLACK + CAP_SLACK)
    w_rows = [iwt_ref[h:h + 1, :] for h in range(N_IDX_HEADS)]

    def score_tile(base, rows, carry, on_diagonal):
        ikb = ik_ref[pl.ds(base, rows), :]
        sc = w_rows[0] * jnp.maximum(_dot(ikb, iqt_ref[0:IDX_DIM, :]), 0.0)
        for h in range(1, N_IDX_HEADS):
            logit = _dot(ikb, iqt_ref[h * IDX_DIM:(h + 1) * IDX_DIM, :])
            sc = sc + w_rows[h] * jnp.maximum(logit, 0.0)
        if on_diagonal:
            sc = jnp.where(base + lax.broadcasted_iota(jnp.int32, (rows, 1), 0) <= t_idx, sc, SCORE_FLOOR)
        hi_ref[pl.ds(base, rows), :], lo_ref[pl.ds(base, rows), :] = _sortable_halves(sc)
        return carry

    over_tiles(score_tile, 0)

    causal_rows = (i + 1) * nq
    steps = (causal_rows // COUNT_ROWS, (causal_rows % COUNT_ROWS) // TAIL_ROWS)
    want = jnp.full((1, nq), topk, jnp.int32)
    t_hi, above_hi = _bisect16(hi_ref, steps, want)
    t_hi16 = _rows16(t_hi)
    floor16 = jnp.full((PACK_ROWS, nq), I16_MIN, I16)

    def keep_bucket(base, rows, carry):
        hi, lo = hi_ref[pl.ds(base, rows), :], lo_ref[pl.ds(base, rows), :]
        lo_ref[pl.ds(base, rows), :] = jnp.concatenate(
            [jnp.where(hi[PACK_ROWS * r:PACK_ROWS * (r + 1), :] == t_hi16,
                       lo[PACK_ROWS * r:PACK_ROWS * (r + 1), :], floor16)
             for r in range(rows // PACK_ROWS)], axis=0)
        return carry

    _over_rows(steps, keep_bucket, 0)
    t_lo, above_lo = _bisect16(lo_ref, steps, want - above_hi)
    quota = (topk - above_hi - above_lo).astype(F32)
    t_hi = jnp.where(t_idx < topk, jnp.int32(I16_MIN), t_hi)
    t_hi16, t_lo16 = _rows16(t_hi), _rows16(t_lo)

    incl_lower = (lax.broadcasted_iota(jnp.int32, (TIE_BLOCK, TIE_BLOCK), 0)
                  >= lax.broadcasted_iota(jnp.int32, (TIE_BLOCK, TIE_BLOCK), 1)).astype(BF16)
    one16, zero16 = jnp.ones((PACK_ROWS, nq), BF16), jnp.zeros((PACK_ROWS, nq), BF16)

    def attend_tile(base, rows, ties_before, on_diagonal, running_max):
        hi_t, lo_t = hi_ref[pl.ds(base, rows), :], lo_ref[pl.ds(base, rows), :]
        above, tie = [], []
        for r in range(rows // PACK_ROWS):
            hi16 = hi_t[PACK_ROWS * r:PACK_ROWS * (r + 1), :]
            lo16 = lo_t[PACK_ROWS * r:PACK_ROWS * (r + 1), :]
            bucket = hi16 == t_hi16
            above.append(jnp.where((hi16 > t_hi16) | (bucket & (lo16 > t_lo16)), one16, zero16))
            tie.append(jnp.where(bucket & (lo16 == t_lo16), one16, zero16))
        above, tie = jnp.concatenate(above, axis=0), jnp.concatenate(tie, axis=0)
        parts, running = [], ties_before
        for u in range(rows // TIE_BLOCK):
            parts.append(_dot(incl_lower, tie[u * TIE_BLOCK:(u + 1) * TIE_BLOCK, :]) + running)
            running = parts[-1][TIE_BLOCK - 1:TIE_BLOCK, :]
        ties_upto = jnp.concatenate(parts, axis=0)
        keep = above + tie * jnp.where(ties_upto <= quota, 1.0, 0.0).astype(BF16)
        if on_diagonal:
            causal = base + lax.broadcasted_iota(jnp.int32, (rows, 1), 0) <= t_idx
            keep = keep * jnp.where(causal, 1.0, 0.0).astype(BF16)
        bias = ((keep - 1.0) * (-MASK_BIAS)).astype(F32)
        kt = k_ref[pl.ds(base, rows), :]
        n_sub = rows // ATT_BLOCK
        vts = [vt_ref[base // ATT_BLOCK + u] for u in range(n_sub)]
        ones_rows = jnp.ones((PACK_ROWS, ATT_BLOCK), BF16)

        def value_matmul(h, p):
            hs = slice(h * HEAD_DIM, (h + 1) * HEAD_DIM)
            pb = p.astype(BF16)
            return sum(_dot(jnp.concatenate([vts[u][hs, :], ones_rows], axis=0),
                            pb[u * ATT_BLOCK:(u + 1) * ATT_BLOCK, :]) for u in range(n_sub))

        if running_max:
            logits = [_dot(kt, qm_ref[h]) + bias for h in range(N_HEADS)]
            m_old = [m_ref[h:h + 1, :] for h in range(N_HEADS)]
            m_new = [jnp.maximum(m_old[h], jnp.max(logits[h], axis=0, keepdims=True)) for h in range(N_HEADS)]
            alpha = [jnp.exp2(m_old[h] - m_new[h]) for h in range(N_HEADS)]
            probs = [jnp.exp2(logits[h] - m_new[h]) for h in range(N_HEADS)]
            for h in range(N_HEADS):
                m_ref[h:h + 1, :] = m_new[h]
                head = slice(h * ACC_ROWS, (h + 1) * ACC_ROWS)
                acc_ref[head, :] = alpha[h] * acc_ref[head, :] + value_matmul(h, probs[h])
        else:
            shift = bias - logit_cap
            probs = [jnp.exp2(_dot(kt, qm_ref[h]) + shift) for h in range(N_HEADS)]
            for h in range(N_HEADS):
                head = slice(h * ACC_ROWS, (h + 1) * ACC_ROWS)
                acc_ref[head, :] += value_matmul(h, probs[h])
        return running

    def attend(running_max):
        acc_ref[...] = jnp.zeros(acc_ref.shape, F32)
        over_tiles(functools.partial(attend_tile, running_max=running_max), jnp.zeros((1, nq), F32))

    attend(running_max=False)
    denominators = jnp.concatenate(
        [acc_ref[h * ACC_ROWS + HEAD_DIM:h * ACC_ROWS + HEAD_DIM + 1, :] for h in range(N_HEADS)], axis=0)
    weakest = jnp.min(jnp.min(denominators, axis=1, keepdims=True), axis=0, keepdims=True)[0, 0]

    @pl.when(jnp.logical_not(weakest >= MIN_DENOMINATOR))
    def _():
        m_ref[...] = jnp.full(m_ref.shape, MASK_BIAS, F32)
        attend(running_max=True)

    heads = [acc_ref[h * ACC_ROWS:h * ACC_ROWS + HEAD_DIM, :]
             / acc_ref[h * ACC_ROWS + HEAD_DIM:h * ACC_ROWS + HEAD_DIM + 1, :] for h in range(N_HEADS)]
    o_ref[...] = jnp.concatenate(heads, axis=0).T


def _dsa_attention(qa, iq, iwt, ka, vat, ik):
    s = qa.shape[0]
    nq = ATT_BLOCK
    topk = min(TOPK_MAX, s // 4)
    assert s % DSA_KEY_TILE == 0 and DSA_KEY_TILE % ATT_BLOCK == 0
    assert ATT_BLOCK % TAIL_ROWS == 0 and COUNT_ROWS % TAIL_ROWS == 0 and DSA_KEY_TILE % TIE_BLOCK == 0
    return pl.pallas_call(
        functools.partial(_dsa_kernel, topk=topk),
        grid=(s // nq,),
        in_specs=[_rows(256, nq), _rows(128, nq), pl.BlockSpec((8, nq), lambda i: (0, i)),
                  _full((s, 256)), _full((s // ATT_BLOCK, 256, ATT_BLOCK)), _full((s, IDX_DIM))],
        out_specs=_rows(256, nq),
        out_shape=jax.ShapeDtypeStruct((s, 256), F32),
        scratch_shapes=[pltpu.VMEM((s, nq), I16), pltpu.VMEM((s, nq), I16), pltpu.VMEM((N_HEADS, 256, nq), BF16),
                        pltpu.VMEM((128, nq), BF16), pltpu.VMEM((N_HEADS * ACC_ROWS, nq), F32),
                        pltpu.VMEM((8, nq), F32), pltpu.VMEM((8, 128), F32)],
        compiler_params=_params(),
        name="dsa_attn",
    )(qa, iq, iwt, ka, vat, ik)


def _sb_kernel(q_ref, k_ref, v_ref, o_ref, acc_ref, carry_ref):
    nq = q_ref.shape[0]
    kb = ATT_BLOCK
    i = pl.program_id(0)
    q = q_ref[...]
    lane_head = lax.broadcasted_iota(jnp.int32, (1, BRANCH_WIDTH), 1) // HEAD_DIM
    qh = [jnp.where(lane_head == h, q, jnp.zeros_like(q)) for h in range(N_HEADS)]
    later = (lax.broadcasted_iota(jnp.int32, (2 * kb, kb), 0) % kb
             > lax.broadcasted_iota(jnp.int32, (2 * kb, kb), 1)).astype(BF16)
    acc_ref[...] = jnp.zeros(acc_ref.shape, F32)
    carry_ref[...] = jnp.zeros(carry_ref.shape, F32)

    def walk_tile(j, on_diagonal):
        base = pl.multiple_of(j * kb, kb)
        kt = k_ref[pl.ds(base, kb), :]
        vt = v_ref[pl.ds(base, kb), :]
        if on_diagonal:
            strict = (lax.broadcasted_iota(jnp.int32, (1, kb), 1)
                      < lax.broadcasted_iota(jnp.int32, (nq, 1), 0))
        heads = range(N_HEADS)
        z = [_dot_nt(qh[h], kt) for h in heads]
        softplus = [jnp.maximum(z[h], 0.0) + jnp.log2(1.0 + jnp.exp2(-jnp.abs(z[h]))) for h in heads]
        sp = [jnp.where(strict, softplus[h], 0.0) for h in heads] if on_diagonal else softplus
        hi = [sp[h].astype(BF16) for h in heads]
        lo = [(sp[h] - hi[h].astype(F32)).astype(BF16) for h in heads]
        after = [_dot(jnp.concatenate([hi[h], lo[h]], axis=1), later) for h in heads]
        c = [carry_ref[h] for h in heads]
        wts = [jnp.exp2((z[h] - softplus[h]) - (after[h] + c[h])) for h in heads]
        if on_diagonal:
            wts = [jnp.where(strict, wts[h], 0.0) for h in heads]
        out = sum(_dot(wts[h].astype(BF16), jnp.where(lane_head == h, vt, jnp.zeros_like(vt))) for h in heads)
        acc_ref[...] += out
        lowest = jnp.full((1, 1), jnp.inf, F32)
        for h in heads:
            c_new = c[h] + (after[h][:, 0:1] + sp[h][:, 0:1])
            carry_ref[h] = c_new
            lowest = jnp.minimum(lowest, jnp.min(c_new, axis=0, keepdims=True))
        if on_diagonal:
            return jnp.int32(1)
        return (lowest[0, 0] < SB_DEAD).astype(jnp.int32)

    def alive(state):
        j, live = state
        return jnp.logical_and(j >= 0, live > 0)

    def walk(state):
        j, _ = state
        return j - 1, walk_tile(j, on_diagonal=False)

    lax.while_loop(alive, walk, (i - 1, walk_tile(i, on_diagonal=True)))
    o_ref[...] = acc_ref[...]


def _sb_attention(cq, ck, cv):
    s = cq.shape[0]
    nq = ATT_BLOCK
    return pl.pallas_call(
        _sb_kernel,
        grid=(s // nq,),
        in_specs=[_rows(256, nq), _full((s, 256)), _full((s, 256))],
        out_specs=_rows(256, nq),
        out_shape=jax.ShapeDtypeStruct((s, 256), F32),
        scratch_shapes=[pltpu.VMEM((nq, 256), F32), pltpu.VMEM((N_HEADS, nq, 1), F32)],
        compiler_params=_params(),
        name="sb_attn",
    )(cq, ck, cv)


def _merge_out_kernel(x_ref, oa_ref, oc_ref, ogb_ref, ogd_ref, p_ref, g_ref, wg_ref, wm_ref, bm_ref,
                      wb_ref, wo_ref, gpost_ref, wple_ref, wpg_ref, out_ref):
    x = x_ref[...]
    h = _rmsnorm_rows(x, g_ref[...]).astype(BF16)
    gates = _dot(h, wg_ref[...])
    silu = gates * _sigmoid(gates)
    branch_in = [(oa_ref[...] * silu[:, 0:256]).astype(BF16), ogb_ref[...],
                 (oc_ref[...] * silu[:, 256:512]).astype(BF16), ogd_ref[...]]
    merged = jnp.zeros((x.shape[0], D_MODEL), F32)
    for n in range(4):
        cols = slice(n * D_MODEL, (n + 1) * D_MODEL)
        gate = _sigmoid(_dot(h, wm_ref[:, cols]) + bm_ref[:, cols])
        merged = merged + gate * _dot(branch_in[n], wb_ref[n])
    y = _dot(merged.astype(BF16), wo_ref[...])
    x1 = x + _rmsnorm_rows(y, gpost_ref[...])
    ple = _dot(p_ref[...].astype(BF16), wple_ref[...])
    out_ref[...] = x1 + ple * _sigmoid(_dot(x1.astype(BF16), wpg_ref[...]))


def _merge_out(x, o_a, o_c, og_b, og_d, p_i, g_pre, w):
    s = x.shape[0]
    return pl.pallas_call(
        _merge_out_kernel,
        grid=(s // ROW_BLOCK,),
        in_specs=[_rows(D_MODEL), _rows(256), _rows(256), _rows(256), _rows(256), _rows(PLE_DIM),
                  _full((1, D_MODEL)), _full((D_MODEL, 512)), _full((D_MODEL, 4 * D_MODEL)),
                  _full((1, 4 * D_MODEL)), _full((4, BRANCH_WIDTH, D_MODEL)), _full((D_MODEL, D_MODEL)),
                  _full((1, D_MODEL)), _full((PLE_DIM, D_MODEL)), _full((D_MODEL, D_MODEL))],
        out_specs=_rows(D_MODEL),
        out_shape=jax.ShapeDtypeStruct((s, D_MODEL), F32),
        compiler_params=_params(),
        name="merge_out",
    )(x, o_a, o_c, og_b, og_d, p_i, g_pre, w["wg"], w["wm"], w["b_merge"], w["w_branch"], w["w_out"],
      w["g_post"], w["w_ple"], w["w_ple_gate"])


IN_COLS = 8100
IN_HEAD = 932
PREP_ROWS = 128


def _weight_layout_kernel(w_ref, wa_ref, wi_ref, wc_ref, w2_ref, wg_ref, wm_ref):
    wa_ref[0] = w_ref[0, :, 0:768].astype(BF16)
    tail = w_ref[0, :, 896:1024]
    lane = lax.broadcasted_iota(jnp.int32, (1, 128), 1)
    wi_ref[0, :, 0:128] = w_ref[0, :, 768:896].astype(BF16)
    wi_ref[0, :, 128:256] = jnp.where(lane < IDX_DIM, tail, 0.0).astype(BF16)
    wi_ref[0, :, 256:384] = jnp.where(lane < N_IDX_HEADS, pltpu.roll(tail, 128 - IDX_DIM, 1), 0.0).astype(BF16)

    def rest(lo, hi):
        return w_ref[0, :, IN_HEAD + lo:IN_HEAD + hi].astype(BF16)

    w2_ref[0, :, 0:768] = rest(0, 768)
    wc_ref[0] = rest(768, 1536)
    w2_ref[0, :, 768:1280] = rest(1536, 2048)
    wg_ref[0, :, 0:256] = rest(2048, 2304)
    w2_ref[0, :, 1280:1536] = rest(2304, 2560)
    wg_ref[0, :, 256:512] = rest(2560, 2816)
    w2_ref[0, :, 1536:1792] = rest(2816, 3072)
    wm_ref[0] = rest(3072, 7168)


def _weight_layout(w_in):
    depth = w_in.shape[0]
    assert w_in.shape[1:] == (D_MODEL, IN_COLS)
    widths = (768, 384, 768, 1792, 512, 4096)
    return pl.pallas_call(
        _weight_layout_kernel,
        grid=(depth, D_MODEL // PREP_ROWS),
        in_specs=[pl.BlockSpec((1, PREP_ROWS, IN_COLS), lambda l, r: (l, r, 0))],
        out_specs=[pl.BlockSpec((1, PREP_ROWS, n), lambda l, r: (l, r, 0)) for n in widths],
        out_shape=[jax.ShapeDtypeStruct((depth, D_MODEL, n), BF16) for n in widths],
        compiler_params=_params(2),
        name="weight_layout",
    )(w_in)


def _layer_weights(i, w_proj, conv_w, conv_b, ln_g, ln_b, w_spatial, b_spatial, b_merge, w_branch, w_out,
                   g_post, w_ple, w_ple_gate):
    row = lambda v: v[None, :]
    wa, wi, wc, w2, wg, wm = (w[i] for w in w_proj)
    return {
        "wa": wa, "wi": wi, "wc": wc, "w2": w2, "wg": wg, "wm": wm,
        "conv_w": jnp.pad(conv_w[i], ((0, 8 - CONV_WIDTH), (0, 0))),
        "conv_b": row(conv_b[i]), "ln_g": row(ln_g[i]), "ln_b": row(ln_b[i]),
        "w_spatial": w_spatial[i],
        "b_spatial": jnp.repeat(b_spatial[i].T, GROUP_DIM, axis=1),
        "b_merge": row(b_merge[i]),
        "w_branch": w_branch[i].astype(BF16), "w_out": w_out[i].astype(BF16), "g_post": row(g_post[i]),
        "w_ple": w_ple[i].astype(BF16), "w_ple_gate": w_ple_gate[i].astype(BF16),
    }


def kernel(x, p, positions, g_pre, w_in, conv_w, conv_b, ln_g, ln_b, w_spatial, b_spatial, b_merge,
           w_branch, w_out, g_post, w_ple, w_ple_gate):
    batch, s, _ = x.shape
    assert batch == 1 and s % ROW_BLOCK == 0 and s % ATT_BLOCK == 0 and ATT_BLOCK >= min(TOPK_MAX, s // 4)
    depth = w_in.shape[0]
    xs = x[0]
    tables = _rope_tables(positions[0][:, None])
    w_proj = _weight_layout(w_in)
    for i in range(depth):
        w = _layer_weights(i, w_proj, conv_w, conv_b, ln_g, ln_b, w_spatial, b_spatial, b_merge, w_branch,
                           w_out, g_post, w_ple, w_ple_gate)
        g = g_pre[i][None, :]
        qa, ka, vat, iq, ik, iwt, cq, ck, cv = _attn_proj(xs, g, w, tables)
        og_b, og_d = _local_mix(xs, g, w)
        o_a = _dsa_attention(qa, iq, iwt, ka, vat, ik)
        o_c = _sb_attention(cq, ck, cv)
        xs = _merge_out(xs, o_a, o_c, og_b, og_d, p[i][0], g, w)
    return xs[None]
```

```python
import functools

import jax
import jax.numpy as jnp
from jax import lax
from jax.experimental import pallas as pl
from jax.experimental.pallas import tpu as pltpu

D_MODEL = 1024
BRANCH_WIDTH = 256
HEAD_DIM = 64
N_HEADS = 4
N_IDX_HEADS = 4
IDX_DIM = 32
TOPK_MAX = 256
CONV_WIDTH = 3
CHUNK = 128
N_GROUPS = 4
GROUP_DIM = BRANCH_WIDTH // N_GROUPS
PLE_DIM = 256
ROPE_THETA = 10000.0
EPS = 1e-6
IDX_W_SCALE = (N_IDX_HEADS * IDX_DIM) ** -0.5
QK_SCALE = HEAD_DIM ** -0.5
LOG2_E = 1.4426950408889634

ROW_BLOCK = 256
ATT_BLOCK = 256
DSA_KEY_TILE = 2048
TIE_BLOCK = 256
COUNT_ROWS = 1024
TAIL_ROWS = 256
CONV_HALO = 8
MASK_BIAS = -1e30
SCORE_FLOOR = float(jnp.finfo(jnp.float32).min)
SB_DEAD = 152.0
CAP_SLACK = 1.03
MIN_DENOMINATOR = 2.0 ** -64
VMEM_LIMIT = 56 * 1024 * 1024

BF16 = jnp.bfloat16
F32 = jnp.float32
NT_DIMS = (((1,), (1,)), ((), ()))


def _dot(a, b):
    return jnp.dot(a, b, preferred_element_type=F32)


def _dot_nt(a, b):
    return lax.dot_general(a, b, NT_DIMS, preferred_element_type=F32)


def _rmsnorm_rows(x, g):
    return x * lax.rsqrt(jnp.mean(x * x, axis=-1, keepdims=True) + EPS) * g


def _sigmoid(x):
    return 1.0 / (1.0 + jnp.exp(-x))


def _params(n_grid_dims=1):
    return pltpu.CompilerParams(
        dimension_semantics=("arbitrary",) * n_grid_dims, vmem_limit_bytes=VMEM_LIMIT)


def _full(shape):
    return pl.BlockSpec(shape, lambda i: (0,) * len(shape))


def _rows(width, block=ROW_BLOCK):
    return pl.BlockSpec((block, width), lambda i: (i, 0))


def _tile_lanes(x, period):
    lane = lax.broadcasted_iota(jnp.int32, (1, 128), 1)
    x = jnp.where(lane < period, x, 0.0)
    while period < 128:
        x = x + pltpu.roll(x, period, 1)
        period *= 2
    return x


def _rope_table_kernel(pos_ref, freq_ref, s64_ref, s32_ref, cos64_ref, sin64_ref, cos32_ref, sin32_ref):
    t = pos_ref.shape[0]
    pos = pos_ref[...].astype(F32)
    lane = lax.broadcasted_iota(jnp.int32, (1, 128), 1)
    angle = jnp.where(lane < 64, pos[0:t // 2, :], pos[t // 2:t, :]) * freq_ref[...]
    cos2, sin2 = jnp.cos(angle), jnp.sin(angle)
    cos = jnp.concatenate([cos2, pltpu.roll(cos2, 64, 1)], axis=0)
    sin = jnp.concatenate([sin2, pltpu.roll(sin2, 64, 1)], axis=0)
    n64, n32 = HEAD_DIM // 2, IDX_DIM // 2
    cos_h, sin_h = _tile_lanes(cos, n64), _tile_lanes(sin, n64)
    cos64_ref[...] = jnp.concatenate([cos_h, cos_h], axis=1)
    sin64_ref[...] = jnp.concatenate([sin_h, sin_h], axis=1) * s64_ref[...]
    cos32_ref[...] = _tile_lanes(pltpu.roll(cos, 128 - n64, 1), n32)
    sin32_ref[...] = _tile_lanes(pltpu.roll(sin, 128 - n64, 1), n32) * s32_ref[...]


def _rope_sign(d, width):
    lane = jnp.arange(width)
    return jnp.where((lane % d) < d // 2, -1.0, 1.0).astype(F32)[None, :]


def _rope_tables(pos_col):
    s = pos_col.shape[0]
    inv_freq = lambda d: ROPE_THETA ** (-jnp.arange(0, d, 2, dtype=F32) / d)
    half = jnp.concatenate([inv_freq(HEAD_DIM), inv_freq(IDX_DIM), jnp.zeros((16,), F32)])
    freq = jnp.concatenate([half, half])[None, :]
    s64, s32 = _rope_sign(HEAD_DIM, BRANCH_WIDTH), _rope_sign(IDX_DIM, N_IDX_HEADS * IDX_DIM)
    return pl.pallas_call(
        _rope_table_kernel,
        grid=(s // ROW_BLOCK,),
        in_specs=[_rows(1), _full((1, 128)), _full((1, 256)), _full((1, 128))],
        out_specs=[_rows(256), _rows(256), _rows(128), _rows(128)],
        out_shape=[jax.ShapeDtypeStruct((s, 256), F32), jax.ShapeDtypeStruct((s, 256), F32),
                   jax.ShapeDtypeStruct((s, 128), F32), jax.ShapeDtypeStruct((s, 128), F32)],
        compiler_params=_params(),
        name="rope_tables",
    )(pos_col, freq, s64, s32)


def _swap_halves(x, d):
    n = x.shape[1]
    lane = lax.broadcasted_iota(jnp.int32, (1, n), 1)
    return jnp.where(lane % d < d // 2, pltpu.roll(x, n - d // 2, 1), pltpu.roll(x, d // 2, 1))


def _attn_proj_kernel(x_ref, g_ref, wa_ref, wi_ref, wc_ref,
                      cos64_ref, sin64_ref, cos32_ref, sin32_ref,
                      qa_ref, ka_ref, vat_ref, iq_ref, ik_ref, iwt_ref, cq_ref, ck_ref, cv_ref):
    h = _rmsnorm_rows(x_ref[...], g_ref[...]).astype(BF16)
    c64, s64 = cos64_ref[...], sin64_ref[...]
    c32, s32 = cos32_ref[...], sin32_ref[...]
    pa = _dot(h, wa_ref[...])
    q, k = pa[:, 0:256], pa[:, 256:512]
    qa_ref[...] = (q * c64 + _swap_halves(q, HEAD_DIM) * s64) * (QK_SCALE * LOG2_E)
    ka_ref[...] = (k * c64 + _swap_halves(k, HEAD_DIM) * s64).astype(BF16)
    vat_ref[0] = pa[:, 512:768].T.astype(BF16)
    pi = _dot(h, wi_ref[...])
    iq, ik = pi[:, 0:128], pi[:, 128:256]
    iq_ref[...] = iq * c32 + _swap_halves(iq, IDX_DIM) * s32
    ik_ref[...] = (ik * c32 + _swap_halves(ik, IDX_DIM) * s32)[:, 0:IDX_DIM].astype(BF16)
    iwt_ref[...] = (pi[:, 256:384] * IDX_W_SCALE).T[0:8, :]
    pc = _dot(h, wc_ref[...])
    cq_ref[...] = (pc[:, 0:256] * (QK_SCALE * LOG2_E)).astype(BF16)
    ck_ref[...] = pc[:, 256:512].astype(BF16)
    cv_ref[...] = pc[:, 512:768].astype(BF16)


def _attn_proj(x, g_pre, w, tables):
    s = x.shape[0]
    nb = s // ROW_BLOCK
    cos64, sin64, cos32, sin32 = tables
    return pl.pallas_call(
        _attn_proj_kernel,
        grid=(nb,),
        in_specs=[_rows(D_MODEL), _full((1, D_MODEL)), _full((D_MODEL, 768)), _full((D_MODEL, 384)),
                  _full((D_MODEL, 768)), _rows(256), _rows(256), _rows(128), _rows(128)],
        out_specs=[_rows(256), _rows(256), pl.BlockSpec((1, 256, ROW_BLOCK), lambda i: (i, 0, 0)),
                   _rows(128), _rows(IDX_DIM), pl.BlockSpec((8, ROW_BLOCK), lambda i: (0, i)),
                   _rows(256), _rows(256), _rows(256)],
        out_shape=[jax.ShapeDtypeStruct((s, 256), F32), jax.ShapeDtypeStruct((s, 256), BF16),
                   jax.ShapeDtypeStruct((nb, 256, ROW_BLOCK), BF16),
                   jax.ShapeDtypeStruct((s, 128), F32), jax.ShapeDtypeStruct((s, IDX_DIM), BF16),
                   jax.ShapeDtypeStruct((8, s), F32),
                   jax.ShapeDtypeStruct((s, 256), BF16), jax.ShapeDtypeStruct((s, 256), BF16),
                   jax.ShapeDtypeStruct((s, 256), BF16)],
        compiler_params=_params(),
        name="attn_proj",
    )(x, g_pre, w["wa"], w["wi"], w["wc"], cos64, sin64, cos32, sin32)


def _local_mix_kernel(x_ref, g_ref, w_ref, convw_ref, convb_ref, lng_ref, lnb_ref, ws_ref, bs_ref,
                      ogb_ref, ogd_ref, ypad_ref):
    t = x_ref.shape[0]

    @pl.when(pl.program_id(0) == 0)
    def _():
        ypad_ref[0:CONV_HALO, :] = jnp.zeros((CONV_HALO, BRANCH_WIDTH), F32)

    h = _rmsnorm_rows(x_ref[...], g_ref[...]).astype(BF16)
    pr = _dot(h, w_ref[...])
    gate_b, gate_c, x_in = pr[:, 0:256], pr[:, 256:512], pr[:, 512:768]
    d_u, d_v = pr[:, 768:1024], pr[:, 1024:1280]
    silu_b, silu_d = pr[:, 1280:1536], pr[:, 1536:1792]

    y = gate_c * x_in
    ypad_ref[CONV_HALO:CONV_HALO + t, :] = y
    y1 = ypad_ref[CONV_HALO - 1:CONV_HALO - 1 + t, :]
    y2 = ypad_ref[CONV_HALO - 2:CONV_HALO - 2 + t, :]
    conv = convw_ref[2:3, :] * y + convw_ref[1:2, :] * y1 + convw_ref[0:1, :] * y2
    ypad_ref[0:CONV_HALO, :] = y[t - CONV_HALO:t, :]
    o_b = gate_b * (conv + convb_ref[...])
    ogb_ref[...] = (o_b * (silu_b * _sigmoid(silu_b))).astype(BF16)

    mu = jnp.mean(d_v, axis=-1, keepdims=True)
    dc = d_v - mu
    var = jnp.mean(dc * dc, axis=-1, keepdims=True)
    vn = dc * lax.rsqrt(var + EPS) * lng_ref[...] + lnb_ref[...]
    group = lax.broadcasted_iota(jnp.int32, (1, BRANCH_WIDTH), 1) // GROUP_DIM
    tril = (lax.broadcasted_iota(jnp.int32, (CHUNK, CHUNK), 0)
            >= lax.broadcasted_iota(jnp.int32, (CHUNK, CHUNK), 1))
    wm = [jnp.where(tril, ws_ref[g], 0.0).astype(BF16) for g in range(N_GROUPS)]
    mixed = []
    for c in range(t // CHUNK):
        vc = vn[c * CHUNK:(c + 1) * CHUNK, :]
        m = bs_ref[...]
        for g in range(N_GROUPS):
            m = m + _dot(wm[g], jnp.where(group == g, vc, 0.0).astype(BF16))
        mixed.append(m)
    o_d = d_u * jnp.concatenate(mixed, axis=0)
    ogd_ref[...] = (o_d * (silu_d * _sigmoid(silu_d))).astype(BF16)


def _local_mix(x, g_pre, w):
    s = x.shape[0]
    return pl.pallas_call(
        _local_mix_kernel,
        grid=(s // ROW_BLOCK,),
        in_specs=[_rows(D_MODEL), _full((1, D_MODEL)), _full((D_MODEL, 1792)), _full((8, 256)),
                  _full((1, 256)), _full((1, 256)), _full((1, 256)),
                  _full((N_GROUPS, CHUNK, CHUNK)), _full((CHUNK, 256))],
        out_specs=[_rows(256), _rows(256)],
        out_shape=[jax.ShapeDtypeStruct((s, 256), BF16), jax.ShapeDtypeStruct((s, 256), BF16)],
        scratch_shapes=[pltpu.VMEM((ROW_BLOCK + CONV_HALO, BRANCH_WIDTH), F32)],
        compiler_params=_params(),
        name="local_mix",
    )(x, g_pre, w["w2"], w["conv_w"], w["conv_b"], w["ln_g"], w["ln_b"], w["w_spatial"], w["b_spatial"])


I16 = jnp.int16
I16_MIN = -32768
PACK_ROWS = 16
ACC_ROWS = HEAD_DIM + PACK_ROWS


def _sortable_halves(score):
    bits = lax.bitcast_convert_type(score, jnp.int32)
    key = bits ^ ((bits >> 31) & jnp.int32(0x7FFFFFFF))
    hi = (key >> 16).astype(I16)
    lo = ((key & jnp.int32(0xFFFF)) + jnp.int32(I16_MIN)).astype(I16)
    return hi, lo


def _rows16(row32):
    return jnp.broadcast_to(row32, (PACK_ROWS, row32.shape[1])).astype(I16)


def _over_rows(steps, fn, carry):
    n_big, n_tail = steps
    carry = lax.fori_loop(
        0, n_big, lambda c, x: fn(pl.multiple_of(c * COUNT_ROWS, COUNT_ROWS), COUNT_ROWS, x), carry)
    tail_start = n_big * COUNT_ROWS
    return lax.fori_loop(
        0, n_tail, lambda c, x: fn(pl.multiple_of(tail_start + c * TAIL_ROWS, TAIL_ROWS), TAIL_ROWS, x), carry)


def _count_pass(ref, steps, cand):
    q = ref.shape[1]
    cand16 = _rows16(cand)
    one, zero = jnp.ones((PACK_ROWS, q), I16), jnp.zeros((PACK_ROWS, q), I16)
    n_acc = 4

    def body(base, rows, accs):
        accs = list(accs)
        blk = ref[pl.ds(base, rows), :]
        for r in range(rows // PACK_ROWS):
            v = blk[PACK_ROWS * r:PACK_ROWS * (r + 1), :]
            accs[r % n_acc] = accs[r % n_acc] + jnp.where(v >= cand16, one, zero)
        return tuple(accs)

    assert ref.shape[0] // (PACK_ROWS * n_acc) < 2 ** 15
    accs = _over_rows(steps, body, (zero,) * n_acc)
    total = sum(a.astype(jnp.int32) for a in accs)
    return jnp.sum(total, axis=0, keepdims=True)


def _bisect16(ref, steps, target):
    zero = jnp.zeros_like(target)

    def probe(cand, t, above):
        cnt = _count_pass(ref, steps, cand)
        enough = cnt >= target
        return jnp.where(enough, cand, t), jnp.where(enough, above, cnt)

    def step(b, state):
        t, above = state
        return probe(t | (jnp.int32(1) << (14 - b)), t, above)

    return lax.fori_loop(0, 15, step, probe(zero, jnp.full_like(target, I16_MIN), zero))


def _dsa_kernel(q_ref, iq_ref, iwt_ref, k_ref, vt_ref, ik_ref, o_ref,
                hi_ref, lo_ref, qm_ref, iqt_ref, acc_ref, m_ref, knorm_ref, *, topk):
    nq = q_ref.shape[0]
    kb = DSA_KEY_TILE
    i = pl.program_id(0)
    t_idx = i * nq + lax.broadcasted_iota(jnp.int32, (1, nq), 1)
    n_below = (i * nq) // kb
    left = ((i + 1) * nq - n_below * kb) // TAIL_ROWS
    n_wide = (left > kb // (2 * TAIL_ROWS)).astype(jnp.int32)
    n_tail = left * (1 - n_wide)

    def over_tiles(fn, carry):
        carry = lax.fori_loop(0, n_below, lambda j, c: fn(pl.multiple_of(j * kb, kb), kb, c, False), carry)
        start = pl.multiple_of(n_below * kb, kb)
        carry = lax.fori_loop(0, n_wide, lambda j, c: fn(start, kb, c, True), carry)
        return lax.fori_loop(
            0, n_tail, lambda j, c: fn(pl.multiple_of(start + j * TAIL_ROWS, TAIL_ROWS), TAIL_ROWS, c, True), carry)

    qt = q_ref[...].T
    head_of_row = lax.broadcasted_iota(jnp.int32, (BRANCH_WIDTH, 1), 0) // HEAD_DIM
    for h in range(N_HEADS):
        qm_ref[h] = jnp.where(head_of_row == h, qt, 0.0).astype(BF16)
    iqt_ref[...] = iq_ref[...].T.astype(BF16)

    @pl.when(i == 0)
    def _():
        head_of_lane = (lax.broadcasted_iota(jnp.int32, (BRANCH_WIDTH, 1), 0) // HEAD_DIM
                        == lax.broadcasted_iota(jnp.int32, (1, 128), 1)).astype(BF16)

        def widest(c, best):
            kf = k_ref[pl.ds(pl.multiple_of(c * kb, kb), kb), :].astype(F32)
            return jnp.maximum(best, jnp.max(_dot((kf * kf).astype(BF16), head_of_lane), axis=0, keepdims=True))

        best = lax.fori_loop(0, k_ref.shape[0] // kb, widest, jnp.zeros((1, 128), F32))
        knorm_ref[...] = jnp.broadcast_to(best, knorm_ref.shape)

    logit_cap = jnp.zeros((1, nq), F32)
    qt_sq = qt * qt
    for h in range(N_HEADS):
        q_sq = jnp.sum(qt_sq[h * HEAD_DIM:(h + 1) * HEAD_DIM, :], axis=0, keepdims=True)
        logit_cap = jnp.maximum(logit_cap, jnp.sqrt(q_sq * knorm_ref[0:1, h:h + 1]) * CAP_SLACK + CAP_SLACK)
    w_rows = [iwt_ref[h:h + 1, :] for h in range(N_IDX_HEADS)]

    def score_tile(base, rows, carry, on_diagonal):
        ikb = ik_ref[pl.ds(base, rows), :]
        sc = w_rows[0] * jnp.maximum(_dot(ikb, iqt_ref[0:IDX_DIM, :]), 0.0)
        for h in range(1, N_IDX_HEADS):
            logit = _dot(ikb, iqt_ref[h * IDX_DIM:(h + 1) * IDX_DIM, :])
            sc = sc + w_rows[h] * jnp.maximum(logit, 0.0)
        if on_diagonal:
            sc = jnp.where(base + lax.broadcasted_iota(jnp.int32, (rows, 1), 0) <= t_idx, sc, SCORE_FLOOR)
        hi_ref[pl.ds(base, rows), :], lo_ref[pl.ds(base, rows), :] = _sortable_halves(sc)
        return carry

    over_tiles(score_tile, 0)

    causal_rows = (i + 1) * nq
    steps = (causal_rows // COUNT_ROWS, (causal_rows % COUNT_ROWS) // TAIL_ROWS)
    want = jnp.full((1, nq), topk, jnp.int32)
    t_hi, above_hi = _bisect16(hi_ref, steps, want)
    t_hi16 = _rows16(t_hi)
    floor16 = jnp.full((PACK_ROWS, nq), I16_MIN, I16)

    def keep_bucket(base, rows, carry):
        hi, lo = hi_ref[pl.ds(base, rows), :], lo_ref[pl.ds(base, rows), :]
        lo_ref[pl.ds(base, rows), :] = jnp.concatenate(
            [jnp.where(hi[PACK_ROWS * r:PACK_ROWS * (r + 1), :] == t_hi16,
                       lo[PACK_ROWS * r:PACK_ROWS * (r + 1), :], floor16)
             for r in range(rows // PACK_ROWS)], axis=0)
        return carry

    _over_rows(steps, keep_bucket, 0)
    t_lo, above_lo = _bisect16(lo_ref, steps, want - above_hi)
    quota = (topk - above_hi - above_lo).astype(F32)
    t_hi = jnp.where(t_idx < topk, jnp.int32(I16_MIN), t_hi)
    t_hi16, t_lo16 = _rows16(t_hi), _rows16(t_lo)

    incl_lower = (lax.broadcasted_iota(jnp.int32, (TIE_BLOCK, TIE_BLOCK), 0)
                  >= lax.broadcasted_iota(jnp.int32, (TIE_BLOCK, TIE_BLOCK), 1)).astype(BF16)
    one16, zero16 = jnp.ones((PACK_ROWS, nq), BF16), jnp.zeros((PACK_ROWS, nq), BF16)

    def attend_tile(base, rows, ties_before, on_diagonal, running_max):
        hi_t, lo_t = hi_ref[pl.ds(base, rows), :], lo_ref[pl.ds(base, rows), :]
        above, tie = [], []
        for r in range(rows // PACK_ROWS):
            hi16 = hi_t[PACK_ROWS * r:PACK_ROWS * (r + 1), :]
            lo16 = lo_t[PACK_ROWS * r:PACK_ROWS * (r + 1), :]
            bucket = hi16 == t_hi16
            above.append(jnp.where((hi16 > t_hi16) | (bucket & (lo16 > t_lo16)), one16, zero16))
            tie.append(jnp.where(bucket & (lo16 == t_lo16), one16, zero16))
        above, tie = jnp.concatenate(above, axis=0), jnp.concatenate(tie, axis=0)
        parts, running = [], ties_before
        for u in range(rows // TIE_BLOCK):
            parts.append(_dot(incl_lower, tie[u * TIE_BLOCK:(u + 1) * TIE_BLOCK, :]) + running)
            running = parts[-1][TIE_BLOCK - 1:TIE_BLOCK, :]
        ties_upto = jnp.concatenate(parts, axis=0)
        keep = above + tie * jnp.where(ties_upto <= quota, 1.0, 0.0).astype(BF16)
        if on_diagonal:
            causal = base + lax.broadcasted_iota(jnp.int32, (rows, 1), 0) <= t_idx
            keep = keep * jnp.where(causal, 1.0, 0.0).astype(BF16)
        bias = ((keep - 1.0) * (-MASK_BIAS)).astype(F32)
        kt = k_ref[pl.ds(base, rows), :]
        n_sub = rows // ATT_BLOCK
        vts = [vt_ref[base // ATT_BLOCK + u] for u in range(n_sub)]
        ones_rows = jnp.ones((PACK_ROWS, ATT_BLOCK), BF16)

        def value_matmul(h, p):
            hs = slice(h * HEAD_DIM, (h + 1) * HEAD_DIM)
            pb = p.astype(BF16)
            return sum(_dot(jnp.concatenate([vts[u][hs, :], ones_rows], axis=0),
                            pb[u * ATT_BLOCK:(u + 1) * ATT_BLOCK, :]) for u in range(n_sub))

        if running_max:
            logits = [_dot(kt, qm_ref[h]) + bias for h in range(N_HEADS)]
            m_old = [m_ref[h:h + 1, :] for h in range(N_HEADS)]
            m_new = [jnp.maximum(m_old[h], jnp.max(logits[h], axis=0, keepdims=True)) for h in range(N_HEADS)]
            alpha = [jnp.exp2(m_old[h] - m_new[h]) for h in range(N_HEADS)]
            probs = [jnp.exp2(logits[h] - m_new[h]) for h in range(N_HEADS)]
            for h in range(N_HEADS):
                m_ref[h:h + 1, :] = m_new[h]
                head = slice(h * ACC_ROWS, (h + 1) * ACC_ROWS)
                acc_ref[head, :] = alpha[h] * acc_ref[head, :] + value_matmul(h, probs[h])
        else:
            shift = bias - logit_cap
            probs = [jnp.exp2(_dot(kt, qm_ref[h]) + shift) for h in range(N_HEADS)]
            for h in range(N_HEADS):
                head = slice(h * ACC_ROWS, (h + 1) * ACC_ROWS)
                acc_ref[head, :] += value_matmul(h, probs[h])
        return running

    def attend(running_max):
        acc_ref[...] = jnp.zeros(acc_ref.shape, F32)
        over_tiles(functools.partial(attend_tile, running_max=running_max), jnp.zeros((1, nq), F32))

    attend(running_max=False)
    denominators = jnp.concatenate(
        [acc_ref[h * ACC_ROWS + HEAD_DIM:h * ACC_ROWS + HEAD_DIM + 1, :] for h in range(N_HEADS)], axis=0)
    weakest = jnp.min(jnp.min(denominators, axis=1, keepdims=True), axis=0, keepdims=True)[0, 0]

    @pl.when(jnp.logical_not(weakest >= MIN_DENOMINATOR))
    def _():
        m_ref[...] = jnp.full(m_ref.shape, MASK_BIAS, F32)
        attend(running_max=True)

    heads = [acc_ref[h * ACC_ROWS:h * ACC_ROWS + HEAD_DIM, :]
             / acc_ref[h * ACC_ROWS + HEAD_DIM:h * ACC_ROWS + HEAD_DIM + 1, :] for h in range(N_HEADS)]
    o_ref[...] = jnp.concatenate(heads, axis=0).T


def _dsa_attention(qa, iq, iwt, ka, vat, ik):
    s = qa.shape[0]
    nq = ATT_BLOCK
    topk = min(TOPK_MAX, s // 4)
    assert s % DSA_KEY_TILE == 0 and DSA_KEY_TILE % ATT_BLOCK == 0
    assert ATT_BLOCK % TAIL_ROWS == 0 and COUNT_ROWS % TAIL_ROWS == 0 and DSA_KEY_TILE % TIE_BLOCK == 0
    return pl.pallas_call(
        functools.partial(_dsa_kernel, topk=topk),
        grid=(s // nq,),
        in_specs=[_rows(256, nq), _rows(128, nq), pl.BlockSpec((8, nq), lambda i: (0, i)),
                  _full((s, 256)), _full((s // ATT_BLOCK, 256, ATT_BLOCK)), _full((s, IDX_DIM))],
        out_specs=_rows(256, nq),
        out_shape=jax.ShapeDtypeStruct((s, 256), F32),
        scratch_shapes=[pltpu.VMEM((s, nq), I16), pltpu.VMEM((s, nq), I16), pltpu.VMEM((N_HEADS, 256, nq), BF16),
                        pltpu.VMEM((128, nq), BF16), pltpu.VMEM((N_HEADS * ACC_ROWS, nq), F32),
                        pltpu.VMEM((8, nq), F32), pltpu.VMEM((8, 128), F32)],
        compiler_params=_params(),
        name="dsa_attn",
    )(qa, iq, iwt, ka, vat, ik)


def _sb_kernel(q_ref, k_ref, v_ref, o_ref, acc_ref, carry_ref):
    nq = q_ref.shape[0]
    kb = ATT_BLOCK
    i = pl.program_id(0)
    q = q_ref[...]
    lane_head = lax.broadcasted_iota(jnp.int32, (1, BRANCH_WIDTH), 1) // HEAD_DIM
    qh = [jnp.where(lane_head == h, q, jnp.zeros_like(q)) for h in range(N_HEADS)]
    later = (lax.broadcasted_iota(jnp.int32, (2 * kb, kb), 0) % kb
             > lax.broadcasted_iota(jnp.int32, (2 * kb, kb), 1)).astype(BF16)
    acc_ref[...] = jnp.zeros(acc_ref.shape, F32)
    carry_ref[...] = jnp.zeros(carry_ref.shape, F32)

    def walk_tile(j, on_diagonal):
        base = pl.multiple_of(j * kb, kb)
        kt = k_ref[pl.ds(base, kb), :]
        vt = v_ref[pl.ds(base, kb), :]
        if on_diagonal:
            strict = (lax.broadcasted_iota(jnp.int32, (1, kb), 1)
                      < lax.broadcasted_iota(jnp.int32, (nq, 1), 0))
        heads = range(N_HEADS)
        z = [_dot_nt(qh[h], kt) for h in heads]
        softplus = [jnp.maximum(z[h], 0.0) + jnp.log2(1.0 + jnp.exp2(-jnp.abs(z[h]))) for h in heads]
        sp = [jnp.where(strict, softplus[h], 0.0) for h in heads] if on_diagonal else softplus
        hi = [sp[h].astype(BF16) for h in heads]
        lo = [(sp[h] - hi[h].astype(F32)).astype(BF16) for h in heads]
        after = [_dot(jnp.concatenate([hi[h], lo[h]], axis=1), later) for h in heads]
        c = [carry_ref[h] for h in heads]
        wts = [jnp.exp2((z[h] - softplus[h]) - (after[h] + c[h])) for h in heads]
        if on_diagonal:
            wts = [jnp.where(strict, wts[h], 0.0) for h in heads]
        out = sum(_dot(wts[h].astype(BF16), jnp.where(lane_head == h, vt, jnp.zeros_like(vt))) for h in heads)
        acc_ref[...] += out
        lowest = jnp.full((1, 1), jnp.inf, F32)
        for h in heads:
            c_new = c[h] + (after[h][:, 0:1] + sp[h][:, 0:1])
            carry_ref[h] = c_new
            lowest = jnp.minimum(lowest, jnp.min(c_new, axis=0, keepdims=True))
        if on_diagonal:
            return jnp.int32(1)
        return (lowest[0, 0] < SB_DEAD).astype(jnp.int32)

    def alive(state):
        j, live = state
        return jnp.logical_and(j >= 0, live > 0)

    def walk(state):
        j, _ = state
        return j - 1, walk_tile(j, on_diagonal=False)

    lax.while_loop(alive, walk, (i - 1, walk_tile(i, on_diagonal=True)))
    o_ref[...] = acc_ref[...]


def _sb_attention(cq, ck, cv):
    s = cq.shape[0]
    nq = ATT_BLOCK
    return pl.pallas_call(
        _sb_kernel,
        grid=(s // nq,),
        in_specs=[_rows(256, nq), _full((s, 256)), _full((s, 256))],
        out_specs=_rows(256, nq),
        out_shape=jax.ShapeDtypeStruct((s, 256), F32),
        scratch_shapes=[pltpu.VMEM((nq, 256), F32), pltpu.VMEM((N_HEADS, nq, 1), F32)],
        compiler_params=_params(),
        name="sb_attn",
    )(cq, ck, cv)


def _merge_out_kernel(x_ref, oa_ref, oc_ref, ogb_ref, ogd_ref, p_ref, g_ref, wg_ref, wm_ref, bm_ref,
                      wb_ref, wo_ref, gpost_ref, wple_ref, wpg_ref, out_ref):
    x = x_ref[...]
    h = _rmsnorm_rows(x, g_ref[...]).astype(BF16)
    gates = _dot(h, wg_ref[...])
    silu = gates * _sigmoid(gates)
    branch_in = [(oa_ref[...] * silu[:, 0:256]).astype(BF16), ogb_ref[...],
                 (oc_ref[...] * silu[:, 256:512]).astype(BF16), ogd_ref[...]]
    merged = jnp.zeros((x.shape[0], D_MODEL), F32)
    for n in range(4):
        cols = slice(n * D_MODEL, (n + 1) * D_MODEL)
        gate = _sigmoid(_dot(h, wm_ref[:, cols]) + bm_ref[:, cols])
        merged = merged + gate * _dot(branch_in[n], wb_ref[n])
    y = _dot(merged.astype(BF16), wo_ref[...])
    x1 = x + _rmsnorm_rows(y, gpost_ref[...])
    ple = _dot(p_ref[...].astype(BF16), wple_ref[...])
    out_ref[...] = x1 + ple * _sigmoid(_dot(x1.astype(BF16), wpg_ref[...]))


def _merge_out(x, o_a, o_c, og_b, og_d, p_i, g_pre, w):
    s = x.shape[0]
    return pl.pallas_call(
        _merge_out_kernel,
        grid=(s // ROW_BLOCK,),
        in_specs=[_rows(D_MODEL), _rows(256), _rows(256), _rows(256), _rows(256), _rows(PLE_DIM),
                  _full((1, D_MODEL)), _full((D_MODEL, 512)), _full((D_MODEL, 4 * D_MODEL)),
                  _full((1, 4 * D_MODEL)), _full((4, BRANCH_WIDTH, D_MODEL)), _full((D_MODEL, D_MODEL)),
                  _full((1, D_MODEL)), _full((PLE_DIM, D_MODEL)), _full((D_MODEL, D_MODEL))],
        out_specs=_rows(D_MODEL),
        out_shape=jax.ShapeDtypeStruct((s, D_MODEL), F32),
        compiler_params=_params(),
        name="merge_out",
    )(x, o_a, o_c, og_b, og_d, p_i, g_pre, w["wg"], w["wm"], w["b_merge"], w["w_branch"], w["w_out"],
      w["g_post"], w["w_ple"], w["w_ple_gate"])


IN_COLS = 8100
IN_HEAD = 932
PREP_ROWS = 128


def _weight_layout_kernel(w_ref, wa_ref, wi_ref, wc_ref, w2_ref, wg_ref, wm_ref):
    wa_ref[0] = w_ref[0, :, 0:768].astype(BF16)
    tail = w_ref[0, :, 896:1024]
    lane = lax.broadcasted_iota(jnp.int32, (1, 128), 1)
    wi_ref[0, :, 0:128] = w_ref[0, :, 768:896].astype(BF16)
    wi_ref[0, :, 128:256] = jnp.where(lane < IDX_DIM, tail, 0.0).astype(BF16)
    wi_ref[0, :, 256:384] = jnp.where(lane < N_IDX_HEADS, pltpu.roll(tail, 128 - IDX_DIM, 1), 0.0).astype(BF16)

    def rest(lo, hi):
        return w_ref[0, :, IN_HEAD + lo:IN_HEAD + hi].astype(BF16)

    w2_ref[0, :, 0:768] = rest(0, 768)
    wc_ref[0] = rest(768, 1536)
    w2_ref[0, :, 768:1280] = rest(1536, 2048)
    wg_ref[0, :, 0:256] = rest(2048, 2304)
    w2_ref[0, :, 1280:1536] = rest(2304, 2560)
    wg_ref[0, :, 256:512] = rest(2560, 2816)
    w2_ref[0, :, 1536:1792] = rest(2816, 3072)
    wm_ref[0] = rest(3072, 7168)


def _weight_layout(w_in):
    depth = w_in.shape[0]
    assert w_in.shape[1:] == (D_MODEL, IN_COLS)
    widths = (768, 384, 768, 1792, 512, 4096)
    return pl.pallas_call(
        _weight_layout_kernel,
        grid=(depth, D_MODEL // PREP_ROWS),
        in_specs=[pl.BlockSpec((1, PREP_ROWS, IN_COLS), lambda l, r: (l, r, 0))],
        out_specs=[pl.BlockSpec((1, PREP_ROWS, n), lambda l, r: (l, r, 0)) for n in widths],
        out_shape=[jax.ShapeDtypeStruct((depth, D_MODEL, n), BF16) for n in widths],
        compiler_params=_params(2),
        name="weight_layout",
    )(w_in)


def _layer_weights(i, w_proj, conv_w, conv_b, ln_g, ln_b, w_spatial, b_spatial, b_merge, w_branch, w_out,
                   g_post, w_ple, w_ple_gate):
    row = lambda v: v[None, :]
    wa, wi, wc, w2, wg, wm = (w[i] for w in w_proj)
    return {
        "wa": wa, "wi": wi, "wc": wc, "w2": w2, "wg": wg, "wm": wm,
        "conv_w": jnp.pad(conv_w[i], ((0, 8 - CONV_WIDTH), (0, 0))),
        "conv_b": row(conv_b[i]), "ln_g": row(ln_g[i]), "ln_b": row(ln_b[i]),
        "w_spatial": w_spatial[i],
        "b_spatial": jnp.repeat(b_spatial[i].T, GROUP_DIM, axis=1),
        "b_merge": row(b_merge[i]),
        "w_branch": w_branch[i].astype(BF16), "w_out": w_out[i].astype(BF16), "g_post": row(g_post[i]),
        "w_ple": w_ple[i].astype(BF16), "w_ple_gate": w_ple_gate[i].astype(BF16),
    }


def kernel(x, p, positions, g_pre, w_in, conv_w, conv_b, ln_g, ln_b, w_spatial, b_spatial, b_merge,
           w_branch, w_out, g_post, w_ple, w_ple_gate):
    batch, s, _ = x.shape
    assert batch == 1 and s % ROW_BLOCK == 0 and s % ATT_BLOCK == 0 and ATT_BLOCK >= min(TOPK_MAX, s // 4)
    depth = w_in.shape[0]
    xs = x[0]
    tables = _rope_tables(positions[0][:, None])
    w_proj = _weight_layout(w_in)
    for i in range(depth):
        w = _layer_weights(i, w_proj, conv_w, conv_b, ln_g, ln_b, w_spatial, b_spatial, b_merge, w_branch,
                           w_out, g_post, w_ple, w_ple_gate)
        g = g_pre[i][None, :]
        qa, ka, vat, iq, ik, iwt, cq, ck, cv = _attn_proj(xs, g, w, tables)
        og_b, og_d = _local_mix(xs, g, w)
        o_a = _dsa_attention(qa, iq, iwt, ka, vat, ik)
        o_c = _sb_attention(cq, ck, cv)
        xs = _merge_out(xs, o_a, o_c, og_b, og_d, p[i][0], g, w)
    return xs[None]
```

```python
import functools

import jax
import jax.numpy as jnp
from jax import lax
from jax.experimental import pallas as pl
from jax.experimental.pallas import tpu as pltpu

D_MODEL = 1024
BRANCH_WIDTH = 256
HEAD_DIM = 64
N_HEADS = 4
N_IDX_HEADS = 4
IDX_DIM = 32
TOPK_MAX = 256
CONV_WIDTH = 3
CHUNK = 128
N_GROUPS = 4
GROUP_DIM = BRANCH_WIDTH // N_GROUPS
PLE_DIM = 256
ROPE_THETA = 10000.0
EPS = 1e-6
IDX_W_SCALE = (N_IDX_HEADS * IDX_DIM) ** -0.5
QK_SCALE = HEAD_DIM ** -0.5
LOG2_E = 1.4426950408889634

ROW_BLOCK = 256
ATT_BLOCK = 256
DSA_KEY_TILE = 2048
TIE_BLOCK = 256
COUNT_ROWS = 1024
TAIL_ROWS = 256
CONV_HALO = 8
MASK_BIAS = -1e30
SCORE_FLOOR = float(jnp.finfo(jnp.float32).min)
SB_DEAD = 152.0
CAP_SLACK = 1.03
MIN_DENOMINATOR = 2.0 ** -64
VMEM_LIMIT = 56 * 1024 * 1024

BF16 = jnp.bfloat16
F32 = jnp.float32
NT_DIMS = (((1,), (1,)), ((), ()))


def _dot(a, b):
    return jnp.dot(a, b, preferred_element_type=F32)


def _dot_nt(a, b):
    return lax.dot_general(a, b, NT_DIMS, preferred_element_type=F32)


def _rmsnorm_rows(x, g):
    return x * lax.rsqrt(jnp.mean(x * x, axis=-1, keepdims=True) + EPS) * g


def _sigmoid(x):
    return 1.0 / (1.0 + jnp.exp(-x))


def _params(n_grid_dims=1):
    return pltpu.CompilerParams(
        dimension_semantics=("arbitrary",) * n_grid_dims, vmem_limit_bytes=VMEM_LIMIT)


def _full(shape):
    return pl.BlockSpec(shape, lambda i: (0,) * len(shape))


def _rows(width, block=ROW_BLOCK):
    return pl.BlockSpec((block, width), lambda i: (i, 0))


def _tile_lanes(x, period):
    lane = lax.broadcasted_iota(jnp.int32, (1, 128), 1)
    x = jnp.where(lane < period, x, 0.0)
    while period < 128:
        x = x + pltpu.roll(x, period, 1)
        period *= 2
    return x


def _rope_table_kernel(pos_ref, freq_ref, s64_ref, s32_ref, cos64_ref, sin64_ref, cos32_ref, sin32_ref):
    t = pos_ref.shape[0]
    pos = pos_ref[...].astype(F32)
    lane = lax.broadcasted_iota(jnp.int32, (1, 128), 1)
    angle = jnp.where(lane < 64, pos[0:t // 2, :], pos[t // 2:t, :]) * freq_ref[...]
    cos2, sin2 = jnp.cos(angle), jnp.sin(angle)
    cos = jnp.concatenate([cos2, pltpu.roll(cos2, 64, 1)], axis=0)
    sin = jnp.concatenate([sin2, pltpu.roll(sin2, 64, 1)], axis=0)
    n64, n32 = HEAD_DIM // 2, IDX_DIM // 2
    cos_h, sin_h = _tile_lanes(cos, n64), _tile_lanes(sin, n64)
    cos64_ref[...] = jnp.concatenate([cos_h, cos_h], axis=1)
    sin64_ref[...] = jnp.concatenate([sin_h, sin_h], axis=1) * s64_ref[...]
    cos32_ref[...] = _tile_lanes(pltpu.roll(cos, 128 - n64, 1), n32)
    sin32_ref[...] = _tile_lanes(pltpu.roll(sin, 128 - n64, 1), n32) * s32_ref[...]


def _rope_sign(d, width):
    lane = jnp.arange(width)
    return jnp.where((lane % d) < d // 2, -1.0, 1.0).astype(F32)[None, :]


def _rope_tables(pos_col):
    s = pos_col.shape[0]
    inv_freq = lambda d: ROPE_THETA ** (-jnp.arange(0, d, 2, dtype=F32) / d)
    half = jnp.concatenate([inv_freq(HEAD_DIM), inv_freq(IDX_DIM), jnp.zeros((16,), F32)])
    freq = jnp.concatenate([half, half])[None, :]
    s64, s32 = _rope_sign(HEAD_DIM, BRANCH_WIDTH), _rope_sign(IDX_DIM, N_IDX_HEADS * IDX_DIM)
    return pl.pallas_call(
        _rope_table_kernel,
        grid=(s // ROW_BLOCK,),
        in_specs=[_rows(1), _full((1, 128)), _full((1, 256)), _full((1, 128))],
        out_specs=[_rows(256), _rows(256), _rows(128), _rows(128)],
        out_shape=[jax.ShapeDtypeStruct((s, 256), F32), jax.ShapeDtypeStruct((s, 256), F32),
                   jax.ShapeDtypeStruct((s, 128), F32), jax.ShapeDtypeStruct((s, 128), F32)],
        compiler_params=_params(),
        name="rope_tables",
    )(pos_col, freq, s64, s32)


def _swap_halves(x, d):
    n = x.shape[1]
    lane = lax.broadcasted_iota(jnp.int32, (1, n), 1)
    return jnp.where(lane % d < d // 2, pltpu.roll(x, n - d // 2, 1), pltpu.roll(x, d // 2, 1))


def _attn_proj_kernel(x_ref, g_ref, wa_ref, wi_ref, wc_ref,
                      cos64_ref, sin64_ref, cos32_ref, sin32_ref,
                      qa_ref, ka_ref, vat_ref, iq_ref, ik_ref, iwt_ref, cq_ref, ck_ref, cv_ref):
    h = _rmsnorm_rows(x_ref[...], g_ref[...]).astype(BF16)
    c64, s64 = cos64_ref[...], sin64_ref[...]
    c32, s32 = cos32_ref[...], sin32_ref[...]
    pa = _dot(h, wa_ref[...])
    q, k = pa[:, 0:256], pa[:, 256:512]
    qa_ref[...] = (q * c64 + _swap_halves(q, HEAD_DIM) * s64) * (QK_SCALE * LOG2_E)
    ka_ref[...] = (k * c64 + _swap_halves(k, HEAD_DIM) * s64).astype(BF16)
    vat_ref[0] = pa[:, 512:768].T.astype(BF16)
    pi = _dot(h, wi_ref[...])
    iq, ik = pi[:, 0:128], pi[:, 128:256]
    iq_ref[...] = iq * c32 + _swap_halves(iq, IDX_DIM) * s32
    ik_ref[...] = (ik * c32 + _swap_halves(ik, IDX_DIM) * s32)[:, 0:IDX_DIM].astype(BF16)
    iwt_ref[...] = (pi[:, 256:384] * IDX_W_SCALE).T[0:8, :]
    pc = _dot(h, wc_ref[...])
    cq_ref[...] = (pc[:, 0:256] * (QK_SCALE * LOG2_E)).astype(BF16)
    ck_ref[...] = pc[:, 256:512].astype(BF16)
    cv_ref[...] = pc[:, 512:768].astype(BF16)


def _attn_proj(x, g_pre, w, tables):
    s = x.shape[0]
    nb = s // ROW_BLOCK
    cos64, sin64, cos32, sin32 = tables
    return pl.pallas_call(
        _attn_proj_kernel,
        grid=(nb,),
        in_specs=[_rows(D_MODEL), _full((1, D_MODEL)), _full((D_MODEL, 768)), _full((D_MODEL, 384)),
                  _full((D_MODEL, 768)), _rows(256), _rows(256), _rows(128), _rows(128)],
        out_specs=[_rows(256), _rows(256), pl.BlockSpec((1, 256, ROW_BLOCK), lambda i: (i, 0, 0)),
                   _rows(128), _rows(IDX_DIM), pl.BlockSpec((8, ROW_BLOCK), lambda i: (0, i)),
                   _rows(256), _rows(256), _rows(256)],
        out_shape=[jax.ShapeDtypeStruct((s, 256), F32), jax.ShapeDtypeStruct((s, 256), BF16),
                   jax.ShapeDtypeStruct((nb, 256, ROW_BLOCK), BF16),
                   jax.ShapeDtypeStruct((s, 128), F32), jax.ShapeDtypeStruct((s, IDX_DIM), BF16),
                   jax.ShapeDtypeStruct((8, s), F32),
                   jax.ShapeDtypeStruct((s, 256), BF16), jax.ShapeDtypeStruct((s, 256), BF16),
                   jax.ShapeDtypeStruct((s, 256), BF16)],
        compiler_params=_params(),
        name="attn_proj",
    )(x, g_pre, w["wa"], w["wi"], w["wc"], cos64, sin64, cos32, sin32)


def _local_mix_kernel(x_ref, g_ref, w_ref, convw_ref, convb_ref, lng_ref, lnb_ref, ws_ref, bs_ref,
                      ogb_ref, ogd_ref, ypad_ref):
    t = x_ref.shape[0]

    @pl.when(pl.program_id(0) == 0)
    def _():
        ypad_ref[0:CONV_HALO, :] = jnp.zeros((CONV_HALO, BRANCH_WIDTH), F32)

    h = _rmsnorm_rows(x_ref[...], g_ref[...]).astype(BF16)
    pr = _dot(h, w_ref[...])
    gate_b, gate_c, x_in = pr[:, 0:256], pr[:, 256:512], pr[:, 512:768]
    d_u, d_v = pr[:, 768:1024], pr[:, 1024:1280]
    silu_b, silu_d = pr[:, 1280:1536], pr[:, 1536:1792]

    y = gate_c * x_in
    ypad_ref[CONV_HALO:CONV_HALO + t, :] = y
    y1 = ypad_ref[CONV_HALO - 1:CONV_HALO - 1 + t, :]
    y2 = ypad_ref[CONV_HALO - 2:CONV_HALO - 2 + t, :]
    conv = convw_ref[2:3, :] * y + convw_ref[1:2, :] * y1 + convw_ref[0:1, :] * y2
    ypad_ref[0:CONV_HALO, :] = y[t - CONV_HALO:t, :]
    o_b = gate_b * (conv + convb_ref[...])
    ogb_ref[...] = (o_b * (silu_b * _sigmoid(silu_b))).astype(BF16)

    mu = jnp.mean(d_v, axis=-1, keepdims=True)
    dc = d_v - mu
    var = jnp.mean(dc * dc, axis=-1, keepdims=True)
    vn = dc * lax.rsqrt(var + EPS) * lng_ref[...] + lnb_ref[...]
    group = lax.broadcasted_iota(jnp.int32, (1, BRANCH_WIDTH), 1) // GROUP_DIM
    tril = (lax.broadcasted_iota(jnp.int32, (CHUNK, CHUNK), 0)
            >= lax.broadcasted_iota(jnp.int32, (CHUNK, CHUNK), 1))
    wm = [jnp.where(tril, ws_ref[g], 0.0).astype(BF16) for g in range(N_GROUPS)]
    mixed = []
    for c in range(t // CHUNK):
        vc = vn[c * CHUNK:(c + 1) * CHUNK, :]
        m = bs_ref[...]
        for g in range(N_GROUPS):
            m = m + _dot(wm[g], jnp.where(group == g, vc, 0.0).astype(BF16))
        mixed.append(m)
    o_d = d_u * jnp.concatenate(mixed, axis=0)
    ogd_ref[...] = (o_d * (silu_d * _sigmoid(silu_d))).astype(BF16)


def _local_mix(x, g_pre, w):
    s = x.shape[0]
    return pl.pallas_call(
        _local_mix_kernel,
        grid=(s // ROW_BLOCK,),
        in_specs=[_rows(D_MODEL), _full((1, D_MODEL)), _full((D_MODEL, 1792)), _full((8, 256)),
                  _full((1, 256)), _full((1, 256)), _full((1, 256)),
                  _full((N_GROUPS, CHUNK, CHUNK)), _full((CHUNK, 256))],
        out_specs=[_rows(256), _rows(256)],
        out_shape=[jax.ShapeDtypeStruct((s, 256), BF16), jax.ShapeDtypeStruct((s, 256), BF16)],
        scratch_shapes=[pltpu.VMEM((ROW_BLOCK + CONV_HALO, BRANCH_WIDTH), F32)],
        compiler_params=_params(),
        name="local_mix",
    )(x, g_pre, w["w2"], w["conv_w"], w["conv_b"], w["ln_g"], w["ln_b"], w["w_spatial"], w["b_spatial"])


I16 = jnp.int16
I16_MIN = -32768
PACK_ROWS = 16
ACC_ROWS = HEAD_DIM + PACK_ROWS


def _sortable_halves(score):
    bits = lax.bitcast_convert_type(score, jnp.int32)
    key = bits ^ ((bits >> 31) & jnp.int32(0x7FFFFFFF))
    hi = (key >> 16).astype(I16)
    lo = ((key & jnp.int32(0xFFFF)) + jnp.int32(I16_MIN)).astype(I16)
    return hi, lo


def _rows16(row32):
    return jnp.broadcast_to(row32, (PACK_ROWS, row32.shape[1])).astype(I16)


def _over_rows(steps, fn, carry):
    n_big, n_tail = steps
    carry = lax.fori_loop(
        0, n_big, lambda c, x: fn(pl.multiple_of(c * COUNT_ROWS, COUNT_ROWS), COUNT_ROWS, x), carry)
    tail_start = n_big * COUNT_ROWS
    return lax.fori_loop(
        0, n_tail, lambda c, x: fn(pl.multiple_of(tail_start + c * TAIL_ROWS, TAIL_ROWS), TAIL_ROWS, x), carry)


def _count_pass(ref, steps, cand):
    q = ref.shape[1]
    cand16 = _rows16(cand)
    one, zero = jnp.ones((PACK_ROWS, q), I16), jnp.zeros((PACK_ROWS, q), I16)
    n_acc = 4

    def body(base, rows, accs):
        accs = list(accs)
        blk = ref[pl.ds(base, rows), :]
        for r in range(rows // PACK_ROWS):
            v = blk[PACK_ROWS * r:PACK_ROWS * (r + 1), :]
            accs[r % n_acc] = accs[r % n_acc] + jnp.where(v >= cand16, one, zero)
        return tuple(accs)

    assert ref.shape[0] // (PACK_ROWS * n_acc) < 2 ** 15
    accs = _over_rows(steps, body, (zero,) * n_acc)
    total = sum(a.astype(jnp.int32) for a in accs)
    return jnp.sum(total, axis=0, keepdims=True)


def _bisect16(ref, steps, target):
    zero = jnp.zeros_like(target)

    def probe(cand, t, above):
        cnt = _count_pass(ref, steps, cand)
        enough = cnt >= target
        return jnp.where(enough, cand, t), jnp.where(enough, above, cnt)

    def step(b, state):
        t, above = state
        return probe(t | (jnp.int32(1) << (14 - b)), t, above)

    return lax.fori_loop(0, 15, step, probe(zero, jnp.full_like(target, I16_MIN), zero))


GROUP = 256
GROUP_BATCH = 8


def _sortable_key(score):
    bits = lax.bitcast_convert_type(score, jnp.int32)
    return bits ^ ((bits >> 31) | jnp.int32(-2 ** 31))


def _bit_planes(blocks):
    a = list(blocks)
    mask, j = 0x0000FFFF, 16
    while j:
        m32 = jnp.int32(mask - (1 << 32) if mask >= (1 << 31) else mask)
        k = 0
        while k < 32:
            t = (a[k] ^ lax.shift_right_logical(a[k + j], jnp.int32(j))) & m32
            a[k] = a[k] ^ t
            a[k + j] = a[k + j] ^ (t << j)
            k = (k + j + 1) & ~j
        j >>= 1
        mask ^= (mask << j) & 0xFFFFFFFF
    return a


def _over_groups(n_groups, fn, carry):
    def batch(c, x):
        for u in range(GROUP_BATCH):
            x = fn(c * GROUP_BATCH + u, x)
        return x

    n_batches = n_groups // GROUP_BATCH
    carry = lax.fori_loop(0, n_batches, batch, carry)
    return lax.fori_loop(n_batches * GROUP_BATCH, n_groups, fn, carry)


def _dsa_kernel(q_ref, iq_ref, iwt_ref, k_ref, vt_ref, ik_ref, o_ref,
                planes_ref, eq_ref, gt_ref, qm_ref, iqt_ref, acc_ref, m_ref, knorm_ref, *, topk):
    nq = q_ref.shape[0]
    kb = DSA_KEY_TILE
    i = pl.program_id(0)
    t_idx = i * nq + lax.broadcasted_iota(jnp.int32, (1, nq), 1)
    n_below = (i * nq) // kb
    left = ((i + 1) * nq - n_below * kb) // TAIL_ROWS
    n_wide = (left > kb // (2 * TAIL_ROWS)).astype(jnp.int32)
    n_tail = left * (1 - n_wide)

    def over_tiles(fn, carry):
        carry = lax.fori_loop(0, n_below, lambda j, c: fn(pl.multiple_of(j * kb, kb), kb, c, False), carry)
        start = pl.multiple_of(n_below * kb, kb)
        carry = lax.fori_loop(0, n_wide, lambda j, c: fn(start, kb, c, True), carry)
        return lax.fori_loop(
            0, n_tail, lambda j, c: fn(pl.multiple_of(start + j * TAIL_ROWS, TAIL_ROWS), TAIL_ROWS, c, True), carry)

    qt = q_ref[...].T
    head_of_row = lax.broadcasted_iota(jnp.int32, (BRANCH_WIDTH, 1), 0) // HEAD_DIM
    for h in range(N_HEADS):
        qm_ref[h] = jnp.where(head_of_row == h, qt, 0.0).astype(BF16)
    iqt_ref[...] = iq_ref[...].T.astype(BF16)

    @pl.when(i == 0)
    def _():
        head_of_lane = (lax.broadcasted_iota(jnp.int32, (BRANCH_WIDTH, 1), 0) // HEAD_DIM
                        == lax.broadcasted_iota(jnp.int32, (1, 128), 1)).astype(BF16)

        def widest(c, best):
            kf = k_ref[pl.ds(pl.multiple_of(c * kb, kb), kb), :].astype(F32)
            return jnp.maximum(best, jnp.max(_dot((kf * kf).astype(BF16), head_of_lane), axis=0, keepdims=True))

        best = lax.fori_loop(0, k_ref.shape[0] // kb, widest, jnp.zeros((1, 128), F32))
        knorm_ref[...] = jnp.broadcast_to(best, knorm_ref.shape)

    logit_cap = jnp.zeros((1, nq), F32)
    qt_sq = qt * qt
    for h in range(N_HEADS):
        q_sq = jnp.sum(qt_sq[h * HEAD_DIM:(h + 1) * HEAD_DIM, :], axis=0, keepdims=True)
        logit_cap = jnp.maximum(logit_cap, jnp.sqrt(q_sq * knorm_ref[0:1, h:h + 1]) * CAP_SLACK + CAP_SLACK)
    w_rows = [iwt_ref[h:h + 1, :] for h in range(N_IDX_HEADS)]

    def score_tile(base, rows, carry, on_diagonal):
        ikb = ik_ref[pl.ds(base, rows), :]
        sc = w_rows[0] * jnp.maximum(_dot(ikb, iqt_ref[0:IDX_DIM, :]), 0.0)
        for h in range(1, N_IDX_HEADS):
            logit = _dot(ikb, iqt_ref[h * IDX_DIM:(h + 1) * IDX_DIM, :])
            sc = sc + w_rows[h] * jnp.maximum(logit, 0.0)
        if on_diagonal:
            sc = jnp.where(base + lax.broadcasted_iota(jnp.int32, (rows, 1), 0) <= t_idx, sc, SCORE_FLOOR)
        key = _sortable_key(sc)
        for u in range(rows // GROUP):
            planes = _bit_planes([key[u * GROUP + 8 * j:u * GROUP + 8 * j + 8, :] for j in range(32)])
            first = (base // GROUP + u) * 32
            for s in range(32):
                planes_ref[first + s] = planes[s]
        return carry

    over_tiles(score_tile, 0)

    n_groups = ((i + 1) * nq) // GROUP
    all_ones, no_bits = jnp.full((8, nq), -1, jnp.int32), jnp.zeros((8, nq), jnp.int32)

    def reset(g, carry):
        eq_ref[g], gt_ref[g] = all_ones, no_bits
        return carry

    _over_groups(n_groups, reset, 0)

    def clear(g, carry):
        eq_ref[g], gt_ref[g] = no_bits, no_bits
        return carry

    lax.fori_loop(n_groups, (n_below + n_wide) * (kb // GROUP) + n_tail * (TAIL_ROWS // GROUP), clear, 0)

    def decide_bit(s, carry):
        def count(g, acc):
            at_least = gt_ref[g] | (eq_ref[g] & planes_ref[g * 32 + s])
            return acc + lax.population_count(at_least)

        cnt = jnp.sum(_over_groups(n_groups, count, no_bits), axis=0, keepdims=True)
        take = jnp.broadcast_to(jnp.where(cnt >= topk, jnp.int32(-1), jnp.int32(0)), (8, nq))

        def update(g, c):
            eq = eq_ref[g]
            with_bit = eq & planes_ref[g * 32 + s]
            eq_ref[g] = (with_bit & take) | ((eq ^ with_bit) & ~take)
            gt_ref[g] = gt_ref[g] | (with_bit & ~take)
            return c

        return _over_groups(n_groups, update, carry)

    lax.fori_loop(0, 32, decide_bit, 0)
    n_above = jnp.sum(_over_groups(n_groups, lambda g, acc: acc + lax.population_count(gt_ref[g]), no_bits),
                      axis=0, keepdims=True)
    quota = (topk - n_above).astype(F32)
    keep_all = jnp.broadcast_to(jnp.where(t_idx < topk, jnp.int32(-1), jnp.int32(0)), (8, nq))

    incl_lower = (lax.broadcasted_iota(jnp.int32, (TIE_BLOCK, TIE_BLOCK), 0)
                  >= lax.broadcasted_iota(jnp.int32, (TIE_BLOCK, TIE_BLOCK), 1)).astype(BF16)
    one16, zero16 = jnp.ones((PACK_ROWS, nq), BF16), jnp.zeros((PACK_ROWS, nq), BF16)

    def attend_tile(base, rows, ties_before, on_diagonal, running_max):
        above, tie = [], []
        for u in range(rows // GROUP):
            g = base // GROUP + u
            gt_w = gt_ref[g] | keep_all
            eq_w = eq_ref[g] & ~keep_all
            for j in range(32):
                above.append(lax.shift_right_logical(gt_w, jnp.int32(31 - j)) & 1)
                tie.append(lax.shift_right_logical(eq_w, jnp.int32(31 - j)) & 1)
        above = jnp.concatenate(above, axis=0).astype(F32).astype(BF16)
        tie = jnp.concatenate(tie, axis=0).astype(F32).astype(BF16)
        parts, running = [], ties_before
        for u in range(rows // TIE_BLOCK):
            parts.append(_dot(incl_lower, tie[u * TIE_BLOCK:(u + 1) * TIE_BLOCK, :]) + running)
            running = parts[-1][TIE_BLOCK - 1:TIE_BLOCK, :]
        ties_upto = jnp.concatenate(parts, axis=0)
        keep = above + tie * jnp.where(ties_upto <= quota, 1.0, 0.0).astype(BF16)
        if on_diagonal:
            causal = base + lax.broadcasted_iota(jnp.int32, (rows, 1), 0) <= t_idx
            keep = keep * jnp.where(causal, 1.0, 0.0).astype(BF16)
        bias = ((keep - 1.0) * (-MASK_BIAS)).astype(F32)
        kt = k_ref[pl.ds(base, rows), :]
        n_sub = rows // ATT_BLOCK
        vts = [vt_ref[base // ATT_BLOCK + u] for u in range(n_sub)]
        ones_rows = jnp.ones((PACK_ROWS, ATT_BLOCK), BF16)

        def value_matmul(h, p):
            hs = slice(h * HEAD_DIM, (h + 1) * HEAD_DIM)
            pb = p.astype(BF16)
            return sum(_dot(jnp.concatenate([vts[u][hs, :], ones_rows], axis=0),
                            pb[u * ATT_BLOCK:(u + 1) * ATT_BLOCK, :]) for u in range(n_sub))

        if running_max:
            logits = [_dot(kt, qm_ref[h]) + bias for h in range(N_HEADS)]
            m_old = [m_ref[h:h + 1, :] for h in range(N_HEADS)]
            m_new = [jnp.maximum(m_old[h], jnp.max(logits[h], axis=0, keepdims=True)) for h in range(N_HEADS)]
            alpha = [jnp.exp2(m_old[h] - m_new[h]) for h in range(N_HEADS)]
            probs = [jnp.exp2(logits[h] - m_new[h]) for h in range(N_HEADS)]
            for h in range(N_HEADS):
                m_ref[h:h + 1, :] = m_new[h]
                head = slice(h * ACC_ROWS, (h + 1) * ACC_ROWS)
                acc_ref[head, :] = alpha[h] * acc_ref[head, :] + value_matmul(h, probs[h])
        else:
            shift = bias - logit_cap
            probs = [jnp.exp2(_dot(kt, qm_ref[h]) + shift) for h in range(N_HEADS)]
            for h in range(N_HEADS):
                head = slice(h * ACC_ROWS, (h + 1) * ACC_ROWS)
                acc_ref[head, :] += value_matmul(h, probs[h])
        return running

    def attend(running_max):
        acc_ref[...] = jnp.zeros(acc_ref.shape, F32)
        over_tiles(functools.partial(attend_tile, running_max=running_max), jnp.zeros((1, nq), F32))

    attend(running_max=False)
    denominators = jnp.concatenate(
        [acc_ref[h * ACC_ROWS + HEAD_DIM:h * ACC_ROWS + HEAD_DIM + 1, :] for h in range(N_HEADS)], axis=0)
    weakest = jnp.min(jnp.min(denominators, axis=1, keepdims=True), axis=0, keepdims=True)[0, 0]

    @pl.when(jnp.logical_not(weakest >= MIN_DENOMINATOR))
    def _():
        m_ref[...] = jnp.full(m_ref.shape, MASK_BIAS, F32)
        attend(running_max=True)

    heads = [acc_ref[h * ACC_ROWS:h * ACC_ROWS + HEAD_DIM, :]
             / acc_ref[h * ACC_ROWS + HEAD_DIM:h * ACC_ROWS + HEAD_DIM + 1, :] for h in range(N_HEADS)]
    o_ref[...] = jnp.concatenate(heads, axis=0).T


def _dsa_attention(qa, iq, iwt, ka, vat, ik):
    s = qa.shape[0]
    nq = ATT_BLOCK
    topk = min(TOPK_MAX, s // 4)
    assert s % DSA_KEY_TILE == 0 and DSA_KEY_TILE % ATT_BLOCK == 0
    assert ATT_BLOCK % TAIL_ROWS == 0 and DSA_KEY_TILE % TIE_BLOCK == 0
    assert TAIL_ROWS % GROUP == 0 and DSA_KEY_TILE % GROUP == 0 and ATT_BLOCK % GROUP == 0
    return pl.pallas_call(
        functools.partial(_dsa_kernel, topk=topk),
        grid=(s // nq,),
        in_specs=[_rows(256, nq), _rows(128, nq), pl.BlockSpec((8, nq), lambda i: (0, i)),
                  _full((s, 256)), _full((s // ATT_BLOCK, 256, ATT_BLOCK)), _full((s, IDX_DIM))],
        out_specs=_rows(256, nq),
        out_shape=jax.ShapeDtypeStruct((s, 256), F32),
        scratch_shapes=[pltpu.VMEM((s // GROUP * 32, 8, nq), jnp.int32), pltpu.VMEM((s // GROUP, 8, nq), jnp.int32),
                        pltpu.VMEM((s // GROUP, 8, nq), jnp.int32), pltpu.VMEM((N_HEADS, 256, nq), BF16),
                        pltpu.VMEM((128, nq), BF16), pltpu.VMEM((N_HEADS * ACC_ROWS, nq), F32),
                        pltpu.VMEM((8, nq), F32), pltpu.VMEM((8, 128), F32)],
        compiler_params=_params(),
        name="dsa_attn",
    )(qa, iq, iwt, ka, vat, ik)


def _sb_kernel(q_ref, k_ref, v_ref, o_ref, acc_ref, carry_ref):
    nq = q_ref.shape[0]
    kb = ATT_BLOCK
    i = pl.program_id(0)
    q = q_ref[...]
    lane_head = lax.broadcasted_iota(jnp.int32, (1, BRANCH_WIDTH), 1) // HEAD_DIM
    qh = [jnp.where(lane_head == h, q, jnp.zeros_like(q)) for h in range(N_HEADS)]
    later = (lax.broadcasted_iota(jnp.int32, (2 * kb, kb), 0) % kb
             > lax.broadcasted_iota(jnp.int32, (2 * kb, kb), 1)).astype(BF16)
    acc_ref[...] = jnp.zeros(acc_ref.shape, F32)
    carry_ref[...] = jnp.zeros(carry_ref.shape, F32)

    def walk_tile(j, on_diagonal):
        base = pl.multiple_of(j * kb, kb)
        kt = k_ref[pl.ds(base, kb), :]
        vt = v_ref[pl.ds(base, kb), :]
        if on_diagonal:
            strict = (lax.broadcasted_iota(jnp.int32, (1, kb), 1)
                      < lax.broadcasted_iota(jnp.int32, (nq, 1), 0))
        heads = range(N_HEADS)
        z = [_dot_nt(qh[h], kt) for h in heads]
        softplus = [jnp.maximum(z[h], 0.0) + jnp.log2(1.0 + jnp.exp2(-jnp.abs(z[h]))) for h in heads]
        sp = [jnp.where(strict, softplus[h], 0.0) for h in heads] if on_diagonal else softplus
        hi = [sp[h].astype(BF16) for h in heads]
        lo = [(sp[h] - hi[h].astype(F32)).astype(BF16) for h in heads]
        after = [_dot(jnp.concatenate([hi[h], lo[h]], axis=1), later) for h in heads]
        c = [carry_ref[h] for h in heads]
        wts = [jnp.exp2((z[h] - softplus[h]) - (after[h] + c[h])) for h in heads]
        if on_diagonal:
            wts = [jnp.where(strict, wts[h], 0.0) for h in heads]
        out = sum(_dot(wts[h].astype(BF16), jnp.where(lane_head == h, vt, jnp.zeros_like(vt))) for h in heads)
        acc_ref[...] += out
        lowest = jnp.full((1, 1), jnp.inf, F32)
        for h in heads:
            c_new = c[h] + (after[h][:, 0:1] + sp[h][:, 0:1])
            carry_ref[h] = c_new
            lowest = jnp.minimum(lowest, jnp.min(c_new, axis=0, keepdims=True))
        if on_diagonal:
            return jnp.int32(1)
        return (lowest[0, 0] < SB_DEAD).astype(jnp.int32)

    def alive(state):
        j, live = state
        return jnp.logical_and(j >= 0, live > 0)

    def walk(state):
        j, _ = state
        return j - 1, walk_tile(j, on_diagonal=False)

    lax.while_loop(alive, walk, (i - 1, walk_tile(i, on_diagonal=True)))
    o_ref[...] = acc_ref[...]


def _sb_attention(cq, ck, cv):
    s = cq.shape[0]
    nq = ATT_BLOCK
    return pl.pallas_call(
        _sb_kernel,
        grid=(s // nq,),
        in_specs=[_rows(256, nq), _full((s, 256)), _full((s, 256))],
        out_specs=_rows(256, nq),
        out_shape=jax.ShapeDtypeStruct((s, 256), F32),
        scratch_shapes=[pltpu.VMEM((nq, 256), F32), pltpu.VMEM((N_HEADS, nq, 1), F32)],
        compiler_params=_params(),
        name="sb_attn",
    )(cq, ck, cv)


def _merge_out_kernel(x_ref, oa_ref, oc_ref, ogb_ref, ogd_ref, p_ref, g_ref, wg_ref, wm_ref, bm_ref,
                      wb_ref, wo_ref, gpost_ref, wple_ref, wpg_ref, out_ref):
    x = x_ref[...]
    h = _rmsnorm_rows(x, g_ref[...]).astype(BF16)
    gates = _dot(h, wg_ref[...])
    silu = gates * _sigmoid(gates)
    branch_in = [(oa_ref[...] * silu[:, 0:256]).astype(BF16), ogb_ref[...],
                 (oc_ref[...] * silu[:, 256:512]).astype(BF16), ogd_ref[...]]
    merged = jnp.zeros((x.shape[0], D_MODEL), F32)
    for n in range(4):
        cols = slice(n * D_MODEL, (n + 1) * D_MODEL)
        gate = _sigmoid(_dot(h, wm_ref[:, cols]) + bm_ref[:, cols])
        merged = merged + gate * _dot(branch_in[n], wb_ref[n])
    y = _dot(merged.astype(BF16), wo_ref[...])
    x1 = x + _rmsnorm_rows(y, gpost_ref[...])
    ple = _dot(p_ref[...].astype(BF16), wple_ref[...])
    out_ref[...] = x1 + ple * _sigmoid(_dot(x1.astype(BF16), wpg_ref[...]))


def _merge_out(x, o_a, o_c, og_b, og_d, p_i, g_pre, w):
    s = x.shape[0]
    return pl.pallas_call(
        _merge_out_kernel,
        grid=(s // ROW_BLOCK,),
        in_specs=[_rows(D_MODEL), _rows(256), _rows(256), _rows(256), _rows(256), _rows(PLE_DIM),
                  _full((1, D_MODEL)), _full((D_MODEL, 512)), _full((D_MODEL, 4 * D_MODEL)),
                  _full((1, 4 * D_MODEL)), _full((4, BRANCH_WIDTH, D_MODEL)), _full((D_MODEL, D_MODEL)),
                  _full((1, D_MODEL)), _full((PLE_DIM, D_MODEL)), _full((D_MODEL, D_MODEL))],
        out_specs=_rows(D_MODEL),
        out_shape=jax.ShapeDtypeStruct((s, D_MODEL), F32),
        compiler_params=_params(),
        name="merge_out",
    )(x, o_a, o_c, og_b, og_d, p_i, g_pre, w["wg"], w["wm"], w["b_merge"], w["w_branch"], w["w_out"],
      w["g_post"], w["w_ple"], w["w_ple_gate"])


IN_COLS = 8100
IN_HEAD = 932
PREP_ROWS = 128


def _weight_layout_kernel(w_ref, wa_ref, wi_ref, wc_ref, w2_ref, wg_ref, wm_ref):
    wa_ref[0] = w_ref[0, :, 0:768].astype(BF16)
    tail = w_ref[0, :, 896:1024]
    lane = lax.broadcasted_iota(jnp.int32, (1, 128), 1)
    wi_ref[0, :, 0:128] = w_ref[0, :, 768:896].astype(BF16)
    wi_ref[0, :, 128:256] = jnp.where(lane < IDX_DIM, tail, 0.0).astype(BF16)
    wi_ref[0, :, 256:384] = jnp.where(lane < N_IDX_HEADS, pltpu.roll(tail, 128 - IDX_DIM, 1), 0.0).astype(BF16)

    def rest(lo, hi):
        return w_ref[0, :, IN_HEAD + lo:IN_HEAD + hi].astype(BF16)

    w2_ref[0, :, 0:768] = rest(0, 768)
    wc_ref[0] = rest(768, 1536)
    w2_ref[0, :, 768:1280] = rest(1536, 2048)
    wg_ref[0, :, 0:256] = rest(2048, 2304)
    w2_ref[0, :, 1280:1536] = rest(2304, 2560)
    wg_ref[0, :, 256:512] = rest(2560, 2816)
    w2_ref[0, :, 1536:1792] = rest(2816, 3072)
    wm_ref[0] = rest(3072, 7168)


def _weight_layout(w_in):
    depth = w_in.shape[0]
    assert w_in.shape[1:] == (D_MODEL, IN_COLS)
    widths = (768, 384, 768, 1792, 512, 4096)
    return pl.pallas_call(
        _weight_layout_kernel,
        grid=(depth, D_MODEL // PREP_ROWS),
        in_specs=[pl.BlockSpec((1, PREP_ROWS, IN_COLS), lambda l, r: (l, r, 0))],
        out_specs=[pl.BlockSpec((1, PREP_ROWS, n), lambda l, r: (l, r, 0)) for n in widths],
        out_shape=[jax.ShapeDtypeStruct((depth, D_MODEL, n), BF16) for n in widths],
        compiler_params=_params(2),
        name="weight_layout",
    )(w_in)


def _layer_weights(i, w_proj, conv_w, conv_b, ln_g, ln_b, w_spatial, b_spatial, b_merge, w_branch, w_out,
                   g_post, w_ple, w_ple_gate):
    row = lambda v: v[None, :]
    wa, wi, wc, w2, wg, wm = (w[i] for w in w_proj)
    return {
        "wa": wa, "wi": wi, "wc": wc, "w2": w2, "wg": wg, "wm": wm,
        "conv_w": jnp.pad(conv_w[i], ((0, 8 - CONV_WIDTH), (0, 0))),
        "conv_b": row(conv_b[i]), "ln_g": row(ln_g[i]), "ln_b": row(ln_b[i]),
        "w_spatial": w_spatial[i],
        "b_spatial": jnp.repeat(b_spatial[i].T, GROUP_DIM, axis=1),
        "b_merge": row(b_merge[i]),
        "w_branch": w_branch[i].astype(BF16), "w_out": w_out[i].astype(BF16), "g_post": row(g_post[i]),
        "w_ple": w_ple[i].astype(BF16), "w_ple_gate": w_ple_gate[i].astype(BF16),
    }


def kernel(x, p, positions, g_pre, w_in, conv_w, conv_b, ln_g, ln_b, w_spatial, b_spatial, b_merge,
           w_branch, w_out, g_post, w_ple, w_ple_gate):
    batch, s, _ = x.shape
    assert batch == 1 and s % ROW_BLOCK == 0 and s % ATT_BLOCK == 0 and ATT_BLOCK >= min(TOPK_MAX, s // 4)
    depth = w_in.shape[0]
    xs = x[0]
    tables = _rope_tables(positions[0][:, None])
    w_proj = _weight_layout(w_in)
    for i in range(depth):
        w = _layer_weights(i, w_proj, conv_w, conv_b, ln_g, ln_b, w_spatial, b_spatial, b_merge, w_branch,
                           w_out, g_post, w_ple, w_ple_gate)
        g = g_pre[i][None, :]
        qa, ka, vat, iq, ik, iwt, cq, ck, cv = _attn_proj(xs, g, w, tables)
        og_b, og_d = _local_mix(xs, g, w)
        o_a = _dsa_attention(qa, iq, iwt, ka, vat, ik)
        o_c = _sb_attention(cq, ck, cv)
        xs = _merge_out(xs, o_a, o_c, og_b, og_d, p[i][0], g, w)
    return xs[None]
```

```python
import functools

import jax
import jax.numpy as jnp
from jax import lax
from jax.experimental import pallas as pl
from jax.experimental.pallas import tpu as pltpu

D_MODEL = 1024
BRANCH_WIDTH = 256
HEAD_DIM = 64
N_HEADS = 4
N_IDX_HEADS = 4
IDX_DIM = 32
TOPK_MAX = 256
CONV_WIDTH = 3
CHUNK = 128
N_GROUPS = 4
GROUP_DIM = BRANCH_WIDTH // N_GROUPS
PLE_DIM = 256
ROPE_THETA = 10000.0
EPS = 1e-6
IDX_W_SCALE = (N_IDX_HEADS * IDX_DIM) ** -0.5
QK_SCALE = HEAD_DIM ** -0.5
LOG2_E = 1.4426950408889634

ROW_BLOCK = 256
ATT_BLOCK = 256
DSA_KEY_TILE = 2048
TIE_BLOCK = 256
TAIL_ROWS = 256
CONV_HALO = 8
MASK_BIAS = -1e30
SCORE_FLOOR = float(jnp.finfo(jnp.float32).min)
SB_DEAD = 152.0
CAP_SLACK = 1.03
MIN_DENOMINATOR = 2.0 ** -64
VMEM_LIMIT = 56 * 1024 * 1024

BF16 = jnp.bfloat16
F32 = jnp.float32
NT_DIMS = (((1,), (1,)), ((), ()))


def _dot(a, b):
    return jnp.dot(a, b, preferred_element_type=F32)


def _dot_nt(a, b):
    return lax.dot_general(a, b, NT_DIMS, preferred_element_type=F32)


def _rmsnorm_rows(x, g):
    return x * lax.rsqrt(jnp.mean(x * x, axis=-1, keepdims=True) + EPS) * g


def _sigmoid(x):
    return 1.0 / (1.0 + jnp.exp(-x))


def _params(n_grid_dims=1):
    return pltpu.CompilerParams(
        dimension_semantics=("arbitrary",) * n_grid_dims, vmem_limit_bytes=VMEM_LIMIT)


def _full(shape):
    return pl.BlockSpec(shape, lambda i: (0,) * len(shape))


def _rows(width, block=ROW_BLOCK):
    return pl.BlockSpec((block, width), lambda i: (i, 0))


def _tile_lanes(x, period):
    lane = lax.broadcasted_iota(jnp.int32, (1, 128), 1)
    x = jnp.where(lane < period, x, 0.0)
    while period < 128:
        x = x + pltpu.roll(x, period, 1)
        period *= 2
    return x


def _rope_table_kernel(pos_ref, freq_ref, s64_ref, s32_ref, cos64_ref, sin64_ref, cos32_ref, sin32_ref):
    t = pos_ref.shape[0]
    pos = pos_ref[...].astype(F32)
    lane = lax.broadcasted_iota(jnp.int32, (1, 128), 1)
    angle = jnp.where(lane < 64, pos[0:t // 2, :], pos[t // 2:t, :]) * freq_ref[...]
    cos2, sin2 = jnp.cos(angle), jnp.sin(angle)
    cos = jnp.concatenate([cos2, pltpu.roll(cos2, 64, 1)], axis=0)
    sin = jnp.concatenate([sin2, pltpu.roll(sin2, 64, 1)], axis=0)
    n64, n32 = HEAD_DIM // 2, IDX_DIM // 2
    cos_h, sin_h = _tile_lanes(cos, n64), _tile_lanes(sin, n64)
    cos64_ref[...] = jnp.concatenate([cos_h, cos_h], axis=1)
    sin64_ref[...] = jnp.concatenate([sin_h, sin_h], axis=1) * s64_ref[...]
    cos32_ref[...] = _tile_lanes(pltpu.roll(cos, 128 - n64, 1), n32)
    sin32_ref[...] = _tile_lanes(pltpu.roll(sin, 128 - n64, 1), n32) * s32_ref[...]


def _rope_sign(d, width):
    lane = jnp.arange(width)
    return jnp.where((lane % d) < d // 2, -1.0, 1.0).astype(F32)[None, :]


def _rope_tables(pos_col):
    s = pos_col.shape[0]
    inv_freq = lambda d: ROPE_THETA ** (-jnp.arange(0, d, 2, dtype=F32) / d)
    half = jnp.concatenate([inv_freq(HEAD_DIM), inv_freq(IDX_DIM), jnp.zeros((16,), F32)])
    freq = jnp.concatenate([half, half])[None, :]
    s64, s32 = _rope_sign(HEAD_DIM, BRANCH_WIDTH), _rope_sign(IDX_DIM, N_IDX_HEADS * IDX_DIM)
    return pl.pallas_call(
        _rope_table_kernel,
        grid=(s // ROW_BLOCK,),
        in_specs=[_rows(1), _full((1, 128)), _full((1, 256)), _full((1, 128))],
        out_specs=[_rows(256), _rows(256), _rows(128), _rows(128)],
        out_shape=[jax.ShapeDtypeStruct((s, 256), F32), jax.ShapeDtypeStruct((s, 256), F32),
                   jax.ShapeDtypeStruct((s, 128), F32), jax.ShapeDtypeStruct((s, 128), F32)],
        compiler_params=_params(),
        name="rope_tables",
    )(pos_col, freq, s64, s32)


def _swap_halves(x, d):
    n = x.shape[1]
    lane = lax.broadcasted_iota(jnp.int32, (1, n), 1)
    return jnp.where(lane % d < d // 2, pltpu.roll(x, n - d // 2, 1), pltpu.roll(x, d // 2, 1))


def _attn_proj_kernel(x_ref, g_ref, wa_ref, wi_ref, wc_ref,
                      cos64_ref, sin64_ref, cos32_ref, sin32_ref,
                      qa_ref, ka_ref, vat_ref, iq_ref, ik_ref, iwt_ref, cq_ref, ck_ref, cv_ref):
    h = _rmsnorm_rows(x_ref[...], g_ref[...]).astype(BF16)
    c64, s64 = cos64_ref[...], sin64_ref[...]
    c32, s32 = cos32_ref[...], sin32_ref[...]
    pa = _dot(h, wa_ref[...])
    q, k = pa[:, 0:256], pa[:, 256:512]
    qa_ref[...] = (q * c64 + _swap_halves(q, HEAD_DIM) * s64) * (QK_SCALE * LOG2_E)
    ka_ref[...] = (k * c64 + _swap_halves(k, HEAD_DIM) * s64).astype(BF16)
    vat_ref[0] = pa[:, 512:768].T.astype(BF16)
    pi = _dot(h, wi_ref[...])
    iq, ik = pi[:, 0:128], pi[:, 128:256]
    iq_ref[...] = iq * c32 + _swap_halves(iq, IDX_DIM) * s32
    ik_ref[...] = (ik * c32 + _swap_halves(ik, IDX_DIM) * s32)[:, 0:IDX_DIM].astype(BF16)
    iwt_ref[...] = (pi[:, 256:384] * IDX_W_SCALE).T[0:8, :]
    pc = _dot(h, wc_ref[...])
    cq_ref[...] = (pc[:, 0:256] * (QK_SCALE * LOG2_E)).astype(BF16)
    ck_ref[...] = pc[:, 256:512].astype(BF16)
    cv_ref[...] = pc[:, 512:768].astype(BF16)


def _attn_proj(x, g_pre, w, tables):
    s = x.shape[0]
    nb = s // ROW_BLOCK
    cos64, sin64, cos32, sin32 = tables
    return pl.pallas_call(
        _attn_proj_kernel,
        grid=(nb,),
        in_specs=[_rows(D_MODEL), _full((1, D_MODEL)), _full((D_MODEL, 768)), _full((D_MODEL, 384)),
                  _full((D_MODEL, 768)), _rows(256), _rows(256), _rows(128), _rows(128)],
        out_specs=[_rows(256), _rows(256), pl.BlockSpec((1, 256, ROW_BLOCK), lambda i: (i, 0, 0)),
                   _rows(128), _rows(IDX_DIM), pl.BlockSpec((8, ROW_BLOCK), lambda i: (0, i)),
                   _rows(256), _rows(256), _rows(256)],
        out_shape=[jax.ShapeDtypeStruct((s, 256), F32), jax.ShapeDtypeStruct((s, 256), BF16),
                   jax.ShapeDtypeStruct((nb, 256, ROW_BLOCK), BF16),
                   jax.ShapeDtypeStruct((s, 128), F32), jax.ShapeDtypeStruct((s, IDX_DIM), BF16),
                   jax.ShapeDtypeStruct((8, s), F32),
                   jax.ShapeDtypeStruct((s, 256), BF16), jax.ShapeDtypeStruct((s, 256), BF16),
                   jax.ShapeDtypeStruct((s, 256), BF16)],
        compiler_params=_params(),
        name="attn_proj",
    )(x, g_pre, w["wa"], w["wi"], w["wc"], cos64, sin64, cos32, sin32)


def _local_mix_kernel(x_ref, g_ref, w_ref, convw_ref, convb_ref, lng_ref, lnb_ref, ws_ref, bs_ref,
                      ogb_ref, ogd_ref, ypad_ref):
    t = x_ref.shape[0]

    @pl.when(pl.program_id(0) == 0)
    def _():
        ypad_ref[0:CONV_HALO, :] = jnp.zeros((CONV_HALO, BRANCH_WIDTH), F32)

    h = _rmsnorm_rows(x_ref[...], g_ref[...]).astype(BF16)
    pr = _dot(h, w_ref[...])
    gate_b, gate_c, x_in = pr[:, 0:256], pr[:, 256:512], pr[:, 512:768]
    d_u, d_v = pr[:, 768:1024], pr[:, 1024:1280]
    silu_b, silu_d = pr[:, 1280:1536], pr[:, 1536:1792]

    y = gate_c * x_in
    ypad_ref[CONV_HALO:CONV_HALO + t, :] = y
    y1 = ypad_ref[CONV_HALO - 1:CONV_HALO - 1 + t, :]
    y2 = ypad_ref[CONV_HALO - 2:CONV_HALO - 2 + t, :]
    conv = convw_ref[2:3, :] * y + convw_ref[1:2, :] * y1 + convw_ref[0:1, :] * y2
    ypad_ref[0:CONV_HALO, :] = y[t - CONV_HALO:t, :]
    o_b = gate_b * (conv + convb_ref[...])
    ogb_ref[...] = (o_b * (silu_b * _sigmoid(silu_b))).astype(BF16)

    mu = jnp.mean(d_v, axis=-1, keepdims=True)
    dc = d_v - mu
    var = jnp.mean(dc * dc, axis=-1, keepdims=True)
    vn = dc * lax.rsqrt(var + EPS) * lng_ref[...] + lnb_ref[...]
    group = lax.broadcasted_iota(jnp.int32, (1, BRANCH_WIDTH), 1) // GROUP_DIM
    tril = (lax.broadcasted_iota(jnp.int32, (CHUNK, CHUNK), 0)
            >= lax.broadcasted_iota(jnp.int32, (CHUNK, CHUNK), 1))
    wm = [jnp.where(tril, ws_ref[g], 0.0).astype(BF16) for g in range(N_GROUPS)]
    mixed = []
    for c in range(t // CHUNK):
        vc = vn[c * CHUNK:(c + 1) * CHUNK, :]
        m = bs_ref[...]
        for g in range(N_GROUPS):
            m = m + _dot(wm[g], jnp.where(group == g, vc, 0.0).astype(BF16))
        mixed.append(m)
    o_d = d_u * jnp.concatenate(mixed, axis=0)
    ogd_ref[...] = (o_d * (silu_d * _sigmoid(silu_d))).astype(BF16)


def _local_mix(x, g_pre, w):
    s = x.shape[0]
    return pl.pallas_call(
        _local_mix_kernel,
        grid=(s // ROW_BLOCK,),
        in_specs=[_rows(D_MODEL), _full((1, D_MODEL)), _full((D_MODEL, 1792)), _full((8, 256)),
                  _full((1, 256)), _full((1, 256)), _full((1, 256)),
                  _full((N_GROUPS, CHUNK, CHUNK)), _full((CHUNK, 256))],
        out_specs=[_rows(256), _rows(256)],
        out_shape=[jax.ShapeDtypeStruct((s, 256), BF16), jax.ShapeDtypeStruct((s, 256), BF16)],
        scratch_shapes=[pltpu.VMEM((ROW_BLOCK + CONV_HALO, BRANCH_WIDTH), F32)],
        compiler_params=_params(),
        name="local_mix",
    )(x, g_pre, w["w2"], w["conv_w"], w["conv_b"], w["ln_g"], w["ln_b"], w["w_spatial"], w["b_spatial"])


PACK_ROWS = 16
ACC_ROWS = HEAD_DIM + PACK_ROWS


GROUP = 256
GROUP_BATCH = 8


def _sortable_key(score):
    bits = lax.bitcast_convert_type(score, jnp.int32)
    return bits ^ ((bits >> 31) | jnp.int32(-2 ** 31))


def _bit_planes(blocks):
    a = list(blocks)
    mask, j = 0x0000FFFF, 16
    while j:
        m32 = jnp.int32(mask - (1 << 32) if mask >= (1 << 31) else mask)
        k = 0
        while k < 32:
            t = (a[k] ^ lax.shift_right_logical(a[k + j], jnp.int32(j))) & m32
            a[k] = a[k] ^ t
            a[k + j] = a[k + j] ^ (t << j)
            k = (k + j + 1) & ~j
        j >>= 1
        mask ^= (mask << j) & 0xFFFFFFFF
    return a


def _over_groups(n_groups, fn, carry):
    def batch(c, x):
        for u in range(GROUP_BATCH):
            x = fn(c * GROUP_BATCH + u, x)
        return x

    n_batches = n_groups // GROUP_BATCH
    carry = lax.fori_loop(0, n_batches, batch, carry)
    return lax.fori_loop(n_batches * GROUP_BATCH, n_groups, fn, carry)


def _dsa_kernel(q_ref, iq_ref, iwt_ref, k_ref, vt_ref, ik_ref, o_ref,
                planes_ref, eq_ref, gt_ref, qm_ref, iqt_ref, acc_ref, m_ref, knorm_ref, *, topk):
    nq = q_ref.shape[0]
    kb = DSA_KEY_TILE
    i = pl.program_id(0)
    t_idx = i * nq + lax.broadcasted_iota(jnp.int32, (1, nq), 1)
    n_below = (i * nq) // kb
    left = ((i + 1) * nq - n_below * kb) // TAIL_ROWS
    n_wide = (left > kb // (2 * TAIL_ROWS)).astype(jnp.int32)
    n_tail = left * (1 - n_wide)

    def over_tiles(fn, carry):
        carry = lax.fori_loop(0, n_below, lambda j, c: fn(pl.multiple_of(j * kb, kb), kb, c, False), carry)
        start = pl.multiple_of(n_below * kb, kb)
        carry = lax.fori_loop(0, n_wide, lambda j, c: fn(start, kb, c, True), carry)
        return lax.fori_loop(
            0, n_tail, lambda j, c: fn(pl.multiple_of(start + j * TAIL_ROWS, TAIL_ROWS), TAIL_ROWS, c, True), carry)

    qt = q_ref[...].T
    head_of_row = lax.broadcasted_iota(jnp.int32, (BRANCH_WIDTH, 1), 0) // HEAD_DIM
    for h in range(N_HEADS):
        qm_ref[h] = jnp.where(head_of_row == h, qt, 0.0).astype(BF16)
    iqt_ref[...] = iq_ref[...].T.astype(BF16)

    @pl.when(i == 0)
    def _():
        head_of_lane = (lax.broadcasted_iota(jnp.int32, (BRANCH_WIDTH, 1), 0) // HEAD_DIM
                        == lax.broadcasted_iota(jnp.int32, (1, 128), 1)).astype(BF16)

        def widest(c, best):
            kf = k_ref[pl.ds(pl.multiple_of(c * kb, kb), kb), :].astype(F32)
            return jnp.maximum(best, jnp.max(_dot((kf * kf).astype(BF16), head_of_lane), axis=0, keepdims=True))

        best = lax.fori_loop(0, k_ref.shape[0] // kb, widest, jnp.zeros((1, 128), F32))
        knorm_ref[...] = jnp.broadcast_to(best, knorm_ref.shape)

    logit_cap = jnp.zeros((1, nq), F32)
    qt_sq = qt * qt
    for h in range(N_HEADS):
        q_sq = jnp.sum(qt_sq[h * HEAD_DIM:(h + 1) * HEAD_DIM, :], axis=0, keepdims=True)
        logit_cap = jnp.maximum(logit_cap, jnp.sqrt(q_sq * knorm_ref[0:1, h:h + 1]) * CAP_SLACK + CAP_SLACK)
    w_rows = [iwt_ref[h:h + 1, :] for h in range(N_IDX_HEADS)]

    def score_tile(base, rows, carry, on_diagonal):
        ikb = ik_ref[pl.ds(base, rows), :]
        sc = w_rows[0] * jnp.maximum(_dot(ikb, iqt_ref[0:IDX_DIM, :]), 0.0)
        for h in range(1, N_IDX_HEADS):
            logit = _dot(ikb, iqt_ref[h * IDX_DIM:(h + 1) * IDX_DIM, :])
            sc = sc + w_rows[h] * jnp.maximum(logit, 0.0)
        if on_diagonal:
            sc = jnp.where(base + lax.broadcasted_iota(jnp.int32, (rows, 1), 0) <= t_idx, sc, SCORE_FLOOR)
        key = _sortable_key(sc)
        for u in range(rows // GROUP):
            planes = _bit_planes([key[u * GROUP + 8 * j:u * GROUP + 8 * j + 8, :] for j in range(32)])
            first = (base // GROUP + u) * 32
            for s in range(32):
                planes_ref[first + s] = planes[s]
        return carry

    over_tiles(score_tile, 0)

    n_groups = ((i + 1) * nq) // GROUP
    all_ones, no_bits = jnp.full((8, nq), -1, jnp.int32), jnp.zeros((8, nq), jnp.int32)

    def reset(g, carry):
        eq_ref[g], gt_ref[g] = all_ones, no_bits
        return carry

    _over_groups(n_groups, reset, 0)

    def clear(g, carry):
        eq_ref[g], gt_ref[g] = no_bits, no_bits
        return carry

    lax.fori_loop(n_groups, (n_below + n_wide) * (kb // GROUP) + n_tail * (TAIL_ROWS // GROUP), clear, 0)

    def count_first(g, acc):
        return acc + lax.population_count(gt_ref[g] | (eq_ref[g] & planes_ref[g * 32]))

    def decide_bit(s, partial):
        cnt = jnp.sum(partial, axis=0, keepdims=True)
        take = jnp.broadcast_to(jnp.where(cnt >= topk, jnp.int32(-1), jnp.int32(0)), (8, nq))
        after = jnp.minimum(s + 1, 31)

        def update_and_count(g, acc):
            eq = eq_ref[g]
            with_bit = eq & planes_ref[g * 32 + s]
            eq = (with_bit & take) | ((eq ^ with_bit) & ~take)
            gt = gt_ref[g] | (with_bit & ~take)
            eq_ref[g], gt_ref[g] = eq, gt
            return acc + lax.population_count(gt | (eq & planes_ref[g * 32 + after]))

        return _over_groups(n_groups, update_and_count, no_bits)

    lax.fori_loop(0, 32, decide_bit, _over_groups(n_groups, count_first, no_bits))
    n_above = jnp.sum(_over_groups(n_groups, lambda g, acc: acc + lax.population_count(gt_ref[g]), no_bits),
                      axis=0, keepdims=True)
    quota = (topk - n_above).astype(F32)
    keep_all = jnp.broadcast_to(jnp.where(t_idx < topk, jnp.int32(-1), jnp.int32(0)), (8, nq))

    incl_lower = (lax.broadcasted_iota(jnp.int32, (TIE_BLOCK, TIE_BLOCK), 0)
                  >= lax.broadcasted_iota(jnp.int32, (TIE_BLOCK, TIE_BLOCK), 1)).astype(BF16)

    def attend_tile(base, rows, ties_before, on_diagonal, running_max):
        above, tie = [], []
        for u in range(rows // GROUP):
            g = base // GROUP + u
            gt_w = gt_ref[g] | keep_all
            eq_w = eq_ref[g] & ~keep_all
            for j in range(32):
                above.append(lax.shift_right_logical(gt_w, jnp.int32(31 - j)) & 1)
                tie.append(lax.shift_right_logical(eq_w, jnp.int32(31 - j)) & 1)
        above = jnp.concatenate(above, axis=0).astype(F32).astype(BF16)
        tie = jnp.concatenate(tie, axis=0).astype(F32).astype(BF16)
        parts, running = [], ties_before
        for u in range(rows // TIE_BLOCK):
            parts.append(_dot(incl_lower, tie[u * TIE_BLOCK:(u + 1) * TIE_BLOCK, :]) + running)
            running = parts[-1][TIE_BLOCK - 1:TIE_BLOCK, :]
        ties_upto = jnp.concatenate(parts, axis=0)
        keep = above + tie * jnp.where(ties_upto <= quota, 1.0, 0.0).astype(BF16)
        if on_diagonal:
            causal = base + lax.broadcasted_iota(jnp.int32, (rows, 1), 0) <= t_idx
            keep = keep * jnp.where(causal, 1.0, 0.0).astype(BF16)
        bias = ((keep - 1.0) * (-MASK_BIAS)).astype(F32)
        kt = k_ref[pl.ds(base, rows), :]
        n_sub = rows // ATT_BLOCK
        vts = [vt_ref[base // ATT_BLOCK + u] for u in range(n_sub)]
        ones_rows = jnp.ones((PACK_ROWS, ATT_BLOCK), BF16)

        def value_matmul(h, p):
            hs = slice(h * HEAD_DIM, (h + 1) * HEAD_DIM)
            pb = p.astype(BF16)
            return sum(_dot(jnp.concatenate([vts[u][hs, :], ones_rows], axis=0),
                            pb[u * ATT_BLOCK:(u + 1) * ATT_BLOCK, :]) for u in range(n_sub))

        if running_max:
            logits = [_dot(kt, qm_ref[h]) + bias for h in range(N_HEADS)]
            m_old = [m_ref[h:h + 1, :] for h in range(N_HEADS)]
            m_new = [jnp.maximum(m_old[h], jnp.max(logits[h], axis=0, keepdims=True)) for h in range(N_HEADS)]
            alpha = [jnp.exp2(m_old[h] - m_new[h]) for h in range(N_HEADS)]
            probs = [jnp.exp2(logits[h] - m_new[h]) for h in range(N_HEADS)]
            for h in range(N_HEADS):
                m_ref[h:h + 1, :] = m_new[h]
                head = slice(h * ACC_ROWS, (h + 1) * ACC_ROWS)
                acc_ref[head, :] = alpha[h] * acc_ref[head, :] + value_matmul(h, probs[h])
        else:
            shift = bias - logit_cap
            probs = [jnp.exp2(_dot(kt, qm_ref[h]) + shift) for h in range(N_HEADS)]
            for h in range(N_HEADS):
                head = slice(h * ACC_ROWS, (h + 1) * ACC_ROWS)
                acc_ref[head, :] += value_matmul(h, probs[h])
        return running

    def attend(running_max):
        acc_ref[...] = jnp.zeros(acc_ref.shape, F32)
        over_tiles(functools.partial(attend_tile, running_max=running_max), jnp.zeros((1, nq), F32))

    attend(running_max=False)
    denominators = jnp.concatenate(
        [acc_ref[h * ACC_ROWS + HEAD_DIM:h * ACC_ROWS + HEAD_DIM + 1, :] for h in range(N_HEADS)], axis=0)
    weakest = jnp.min(jnp.min(denominators, axis=1, keepdims=True), axis=0, keepdims=True)[0, 0]

    @pl.when(jnp.logical_not(weakest >= MIN_DENOMINATOR))
    def _():
        m_ref[...] = jnp.full(m_ref.shape, MASK_BIAS, F32)
        attend(running_max=True)

    heads = [acc_ref[h * ACC_ROWS:h * ACC_ROWS + HEAD_DIM, :]
             / acc_ref[h * ACC_ROWS + HEAD_DIM:h * ACC_ROWS + HEAD_DIM + 1, :] for h in range(N_HEADS)]
    o_ref[...] = jnp.concatenate(heads, axis=0).T


def _dsa_attention(qa, iq, iwt, ka, vat, ik):
    s = qa.shape[0]
    nq = ATT_BLOCK
    topk = min(TOPK_MAX, s // 4)
    assert s % DSA_KEY_TILE == 0 and DSA_KEY_TILE % ATT_BLOCK == 0
    assert ATT_BLOCK % TAIL_ROWS == 0 and DSA_KEY_TILE % TIE_BLOCK == 0
    assert TAIL_ROWS % GROUP == 0 and DSA_KEY_TILE % GROUP == 0 and ATT_BLOCK % GROUP == 0
    return pl.pallas_call(
        functools.partial(_dsa_kernel, topk=topk),
        grid=(s // nq,),
        in_specs=[_rows(256, nq), _rows(128, nq), pl.BlockSpec((8, nq), lambda i: (0, i)),
                  _full((s, 256)), _full((s // ATT_BLOCK, 256, ATT_BLOCK)), _full((s, IDX_DIM))],
        out_specs=_rows(256, nq),
        out_shape=jax.ShapeDtypeStruct((s, 256), F32),
        scratch_shapes=[pltpu.VMEM((s // GROUP * 32, 8, nq), jnp.int32), pltpu.VMEM((s // GROUP, 8, nq), jnp.int32),
                        pltpu.VMEM((s // GROUP, 8, nq), jnp.int32), pltpu.VMEM((N_HEADS, 256, nq), BF16),
                        pltpu.VMEM((128, nq), BF16), pltpu.VMEM((N_HEADS * ACC_ROWS, nq), F32),
                        pltpu.VMEM((8, nq), F32), pltpu.VMEM((8, 128), F32)],
        compiler_params=_params(),
        name="dsa_attn",
    )(qa, iq, iwt, ka, vat, ik)


def _sb_kernel(q_ref, k_ref, v_ref, o_ref, acc_ref, carry_ref):
    nq = q_ref.shape[0]
    kb = ATT_BLOCK
    i = pl.program_id(0)
    q = q_ref[...]
    lane_head = lax.broadcasted_iota(jnp.int32, (1, BRANCH_WIDTH), 1) // HEAD_DIM
    qh = [jnp.where(lane_head == h, q, jnp.zeros_like(q)) for h in range(N_HEADS)]
    later = (lax.broadcasted_iota(jnp.int32, (2 * kb, kb), 0) % kb
             > lax.broadcasted_iota(jnp.int32, (2 * kb, kb), 1)).astype(BF16)
    acc_ref[...] = jnp.zeros(acc_ref.shape, F32)
    carry_ref[...] = jnp.zeros(carry_ref.shape, F32)

    def walk_tile(j, on_diagonal):
        base = pl.multiple_of(j * kb, kb)
        kt = k_ref[pl.ds(base, kb), :]
        vt = v_ref[pl.ds(base, kb), :]
        if on_diagonal:
            strict = (lax.broadcasted_iota(jnp.int32, (1, kb), 1)
                      < lax.broadcasted_iota(jnp.int32, (nq, 1), 0))
        heads = range(N_HEADS)
        z = [_dot_nt(qh[h], kt) for h in heads]
        softplus = [jnp.maximum(z[h], 0.0) + jnp.log2(1.0 + jnp.exp2(-jnp.abs(z[h]))) for h in heads]
        sp = [jnp.where(strict, softplus[h], 0.0) for h in heads] if on_diagonal else softplus
        hi = [sp[h].astype(BF16) for h in heads]
        lo = [(sp[h] - hi[h].astype(F32)).astype(BF16) for h in heads]
        after = [_dot(jnp.concatenate([hi[h], lo[h]], axis=1), later) for h in heads]
        c = [carry_ref[h] for h in heads]
        wts = [jnp.exp2((z[h] - softplus[h]) - (after[h] + c[h])) for h in heads]
        if on_diagonal:
            wts = [jnp.where(strict, wts[h], 0.0) for h in heads]
        out = sum(_dot(wts[h].astype(BF16), jnp.where(lane_head == h, vt, jnp.zeros_like(vt))) for h in heads)
        acc_ref[...] += out
        lowest = jnp.full((1, 1), jnp.inf, F32)
        for h in heads:
            c_new = c[h] + (after[h][:, 0:1] + sp[h][:, 0:1])
            carry_ref[h] = c_new
            lowest = jnp.minimum(lowest, jnp.min(c_new, axis=0, keepdims=True))
        if on_diagonal:
            return jnp.int32(1)
        return (lowest[0, 0] < SB_DEAD).astype(jnp.int32)

    def alive(state):
        j, live = state
        return jnp.logical_and(j >= 0, live > 0)

    def walk(state):
        j, _ = state
        return j - 1, walk_tile(j, on_diagonal=False)

    lax.while_loop(alive, walk, (i - 1, walk_tile(i, on_diagonal=True)))
    o_ref[...] = acc_ref[...]


def _sb_attention(cq, ck, cv):
    s = cq.shape[0]
    nq = ATT_BLOCK
    return pl.pallas_call(
        _sb_kernel,
        grid=(s // nq,),
        in_specs=[_rows(256, nq), _full((s, 256)), _full((s, 256))],
        out_specs=_rows(256, nq),
        out_shape=jax.ShapeDtypeStruct((s, 256), F32),
        scratch_shapes=[pltpu.VMEM((nq, 256), F32), pltpu.VMEM((N_HEADS, nq, 1), F32)],
        compiler_params=_params(),
        name="sb_attn",
    )(cq, ck, cv)


def _merge_out_kernel(x_ref, oa_ref, oc_ref, ogb_ref, ogd_ref, p_ref, g_ref, wg_ref, wm_ref, bm_ref,
                      wb_ref, wo_ref, gpost_ref, wple_ref, wpg_ref, out_ref):
    x = x_ref[...]
    h = _rmsnorm_rows(x, g_ref[...]).astype(BF16)
    gates = _dot(h, wg_ref[...])
    silu = gates * _sigmoid(gates)
    branch_in = [(oa_ref[...] * silu[:, 0:256]).astype(BF16), ogb_ref[...],
                 (oc_ref[...] * silu[:, 256:512]).astype(BF16), ogd_ref[...]]
    merged = jnp.zeros((x.shape[0], D_MODEL), F32)
    for n in range(4):
        cols = slice(n * D_MODEL, (n + 1) * D_MODEL)
        gate = _sigmoid(_dot(h, wm_ref[:, cols]) + bm_ref[:, cols])
        merged = merged + gate * _dot(branch_in[n], wb_ref[n])
    y = _dot(merged.astype(BF16), wo_ref[...])
    x1 = x + _rmsnorm_rows(y, gpost_ref[...])
    ple = _dot(p_ref[...].astype(BF16), wple_ref[...])
    out_ref[...] = x1 + ple * _sigmoid(_dot(x1.astype(BF16), wpg_ref[...]))


def _merge_out(x, o_a, o_c, og_b, og_d, p_i, g_pre, w):
    s = x.shape[0]
    return pl.pallas_call(
        _merge_out_kernel,
        grid=(s // ROW_BLOCK,),
        in_specs=[_rows(D_MODEL), _rows(256), _rows(256), _rows(256), _rows(256), _rows(PLE_DIM),
                  _full((1, D_MODEL)), _full((D_MODEL, 512)), _full((D_MODEL, 4 * D_MODEL)),
                  _full((1, 4 * D_MODEL)), _full((4, BRANCH_WIDTH, D_MODEL)), _full((D_MODEL, D_MODEL)),
                  _full((1, D_MODEL)), _full((PLE_DIM, D_MODEL)), _full((D_MODEL, D_MODEL))],
        out_specs=_rows(D_MODEL),
        out_shape=jax.ShapeDtypeStruct((s, D_MODEL), F32),
        compiler_params=_params(),
        name="merge_out",
    )(x, o_a, o_c, og_b, og_d, p_i, g_pre, w["wg"], w["wm"], w["b_merge"], w["w_branch"], w["w_out"],
      w["g_post"], w["w_ple"], w["w_ple_gate"])


IN_COLS = 8100
IN_HEAD = 932
PREP_ROWS = 128


def _weight_layout_kernel(w_ref, wa_ref, wi_ref, wc_ref, w2_ref, wg_ref, wm_ref):
    wa_ref[0] = w_ref[0, :, 0:768].astype(BF16)
    tail = w_ref[0, :, 896:1024]
    lane = lax.broadcasted_iota(jnp.int32, (1, 128), 1)
    wi_ref[0, :, 0:128] = w_ref[0, :, 768:896].astype(BF16)
    wi_ref[0, :, 128:256] = jnp.where(lane < IDX_DIM, tail, 0.0).astype(BF16)
    wi_ref[0, :, 256:384] = jnp.where(lane < N_IDX_HEADS, pltpu.roll(tail, 128 - IDX_DIM, 1), 0.0).astype(BF16)

    def rest(lo, hi):
        return w_ref[0, :, IN_HEAD + lo:IN_HEAD + hi].astype(BF16)

    w2_ref[0, :, 0:768] = rest(0, 768)
    wc_ref[0] = rest(768, 1536)
    w2_ref[0, :, 768:1280] = rest(1536, 2048)
    wg_ref[0, :, 0:256] = rest(2048, 2304)
    w2_ref[0, :, 1280:1536] = rest(2304, 2560)
    wg_ref[0, :, 256:512] = rest(2560, 2816)
    w2_ref[0, :, 1536:1792] = rest(2816, 3072)
    wm_ref[0] = rest(3072, 7168)


def _weight_layout(w_in):
    depth = w_in.shape[0]
    assert w_in.shape[1:] == (D_MODEL, IN_COLS)
    widths = (768, 384, 768, 1792, 512, 4096)
    return pl.pallas_call(
        _weight_layout_kernel,
        grid=(depth, D_MODEL // PREP_ROWS),
        in_specs=[pl.BlockSpec((1, PREP_ROWS, IN_COLS), lambda l, r: (l, r, 0))],
        out_specs=[pl.BlockSpec((1, PREP_ROWS, n), lambda l, r: (l, r, 0)) for n in widths],
        out_shape=[jax.ShapeDtypeStruct((depth, D_MODEL, n), BF16) for n in widths],
        compiler_params=_params(2),
        name="weight_layout",
    )(w_in)


def _layer_weights(i, w_proj, conv_w, conv_b, ln_g, ln_b, w_spatial, b_spatial, b_merge, w_branch, w_out,
                   g_post, w_ple, w_ple_gate):
    row = lambda v: v[None, :]
    wa, wi, wc, w2, wg, wm = (w[i] for w in w_proj)
    return {
        "wa": wa, "wi": wi, "wc": wc, "w2": w2, "wg": wg, "wm": wm,
        "conv_w": jnp.pad(conv_w[i], ((0, 8 - CONV_WIDTH), (0, 0))),
        "conv_b": row(conv_b[i]), "ln_g": row(ln_g[i]), "ln_b": row(ln_b[i]),
        "w_spatial": w_spatial[i],
        "b_spatial": jnp.repeat(b_spatial[i].T, GROUP_DIM, axis=1),
        "b_merge": row(b_merge[i]),
        "w_branch": w_branch[i].astype(BF16), "w_out": w_out[i].astype(BF16), "g_post": row(g_post[i]),
        "w_ple": w_ple[i].astype(BF16), "w_ple_gate": w_ple_gate[i].astype(BF16),
    }


def kernel(x, p, positions, g_pre, w_in, conv_w, conv_b, ln_g, ln_b, w_spatial, b_spatial, b_merge,
           w_branch, w_out, g_post, w_ple, w_ple_gate):
    batch, s, _ = x.shape
    assert batch == 1 and s % ROW_BLOCK == 0 and s % ATT_BLOCK == 0 and ATT_BLOCK >= min(TOPK_MAX, s // 4)
    depth = w_in.shape[0]
    xs = x[0]
    tables = _rope_tables(positions[0][:, None])
    w_proj = _weight_layout(w_in)
    for i in range(depth):
        w = _layer_weights(i, w_proj, conv_w, conv_b, ln_g, ln_b, w_spatial, b_spatial, b_merge, w_branch,
                           w_out, g_post, w_ple, w_ple_gate)
        g = g_pre[i][None, :]
        qa, ka, vat, iq, ik, iwt, cq, ck, cv = _attn_proj(xs, g, w, tables)
        og_b, og_d = _local_mix(xs, g, w)
        o_a = _dsa_attention(qa, iq, iwt, ka, vat, ik)
        o_c = _sb_attention(cq, ck, cv)
        xs = _merge_out(xs, o_a, o_c, og_b, og_d, p[i][0], g, w)
    return xs[None]
```

```python
import functools

import jax
import jax.numpy as jnp
from jax import lax
from jax.experimental import pallas as pl
from jax.experimental.pallas import tpu as pltpu

D_MODEL = 1024
BRANCH_WIDTH = 256
HEAD_DIM = 64
N_HEADS = 4
N_IDX_HEADS = 4
IDX_DIM = 32
TOPK_MAX = 256
CONV_WIDTH = 3
CHUNK = 128
N_GROUPS = 4
GROUP_DIM = BRANCH_WIDTH // N_GROUPS
PLE_DIM = 256
ROPE_THETA = 10000.0
EPS = 1e-6
IDX_W_SCALE = (N_IDX_HEADS * IDX_DIM) ** -0.5
QK_SCALE = HEAD_DIM ** -0.5
LOG2_E = 1.4426950408889634

ROW_BLOCK = 256
ATT_BLOCK = 256
DSA_KEY_TILE = 2048
TIE_BLOCK = 256
TAIL_ROWS = 256
CONV_HALO = 8
MASK_BIAS = -1e30
SCORE_FLOOR = float(jnp.finfo(jnp.float32).min)
SB_DEAD = 152.0
CAP_SLACK = 1.03
MIN_DENOMINATOR = 2.0 ** -64
VMEM_LIMIT = 56 * 1024 * 1024

BF16 = jnp.bfloat16
F32 = jnp.float32
NT_DIMS = (((1,), (1,)), ((), ()))


def _dot(a, b):
    return jnp.dot(a, b, preferred_element_type=F32)


def _dot_nt(a, b):
    return lax.dot_general(a, b, NT_DIMS, preferred_element_type=F32)


def _rmsnorm_rows(x, g):
    return x * lax.rsqrt(jnp.mean(x * x, axis=-1, keepdims=True) + EPS) * g


def _sigmoid(x):
    return 1.0 / (1.0 + jnp.exp(-x))


def _params(n_grid_dims=1):
    return pltpu.CompilerParams(
        dimension_semantics=("arbitrary",) * n_grid_dims, vmem_limit_bytes=VMEM_LIMIT)


def _full(shape):
    return pl.BlockSpec(shape, lambda i: (0,) * len(shape))


def _rows(width, block=ROW_BLOCK):
    return pl.BlockSpec((block, width), lambda i: (i, 0))


def _tile_lanes(x, period):
    lane = lax.broadcasted_iota(jnp.int32, (1, 128), 1)
    x = jnp.where(lane < period, x, 0.0)
    while period < 128:
        x = x + pltpu.roll(x, period, 1)
        period *= 2
    return x


def _rope_table_kernel(pos_ref, freq_ref, s64_ref, s32_ref, cos64_ref, sin64_ref, cos32_ref, sin32_ref):
    t = pos_ref.shape[0]
    pos = pos_ref[...].astype(F32)
    lane = lax.broadcasted_iota(jnp.int32, (1, 128), 1)
    angle = jnp.where(lane < 64, pos[0:t // 2, :], pos[t // 2:t, :]) * freq_ref[...]
    cos2, sin2 = jnp.cos(angle), jnp.sin(angle)
    cos = jnp.concatenate([cos2, pltpu.roll(cos2, 64, 1)], axis=0)
    sin = jnp.concatenate([sin2, pltpu.roll(sin2, 64, 1)], axis=0)
    n64, n32 = HEAD_DIM // 2, IDX_DIM // 2
    cos_h, sin_h = _tile_lanes(cos, n64), _tile_lanes(sin, n64)
    cos64_ref[...] = jnp.concatenate([cos_h, cos_h], axis=1)
    sin64_ref[...] = jnp.concatenate([sin_h, sin_h], axis=1) * s64_ref[...]
    cos32_ref[...] = _tile_lanes(pltpu.roll(cos, 128 - n64, 1), n32)
    sin32_ref[...] = _tile_lanes(pltpu.roll(sin, 128 - n64, 1), n32) * s32_ref[...]


def _rope_sign(d, width):
    lane = jnp.arange(width)
    return jnp.where((lane % d) < d // 2, -1.0, 1.0).astype(F32)[None, :]


def _rope_tables(pos_col):
    s = pos_col.shape[0]
    inv_freq = lambda d: ROPE_THETA ** (-jnp.arange(0, d, 2, dtype=F32) / d)
    half = jnp.concatenate([inv_freq(HEAD_DIM), inv_freq(IDX_DIM), jnp.zeros((16,), F32)])
    freq = jnp.concatenate([half, half])[None, :]
    s64, s32 = _rope_sign(HEAD_DIM, BRANCH_WIDTH), _rope_sign(IDX_DIM, N_IDX_HEADS * IDX_DIM)
    return pl.pallas_call(
        _rope_table_kernel,
        grid=(s // ROW_BLOCK,),
        in_specs=[_rows(1), _full((1, 128)), _full((1, 256)), _full((1, 128))],
        out_specs=[_rows(256), _rows(256), _rows(128), _rows(128)],
        out_shape=[jax.ShapeDtypeStruct((s, 256), F32), jax.ShapeDtypeStruct((s, 256), F32),
                   jax.ShapeDtypeStruct((s, 128), F32), jax.ShapeDtypeStruct((s, 128), F32)],
        compiler_params=_params(),
        name="rope_tables",
    )(pos_col, freq, s64, s32)


def _swap_halves(x, d):
    n = x.shape[1]
    lane = lax.broadcasted_iota(jnp.int32, (1, n), 1)
    return jnp.where(lane % d < d // 2, pltpu.roll(x, n - d // 2, 1), pltpu.roll(x, d // 2, 1))


def _attn_proj_kernel(x_ref, g_ref, wa_ref, wi_ref, wc_ref,
                      cos64_ref, sin64_ref, cos32_ref, sin32_ref,
                      qa_ref, ka_ref, vat_ref, iq_ref, ik_ref, iwt_ref, cq_ref, ck_ref, cv_ref):
    h = _rmsnorm_rows(x_ref[...], g_ref[...]).astype(BF16)
    c64, s64 = cos64_ref[...], sin64_ref[...]
    c32, s32 = cos32_ref[...], sin32_ref[...]
    pa = _dot(h, wa_ref[...])
    q, k = pa[:, 0:256], pa[:, 256:512]
    qa_ref[...] = (q * c64 + _swap_halves(q, HEAD_DIM) * s64) * (QK_SCALE * LOG2_E)
    ka_ref[...] = (k * c64 + _swap_halves(k, HEAD_DIM) * s64).astype(BF16)
    vat_ref[0] = pa[:, 512:768].T.astype(BF16)
    pi = _dot(h, wi_ref[...])
    iq, ik = pi[:, 0:128], pi[:, 128:256]
    iq_ref[...] = iq * c32 + _swap_halves(iq, IDX_DIM) * s32
    ik_ref[...] = (ik * c32 + _swap_halves(ik, IDX_DIM) * s32)[:, 0:IDX_DIM].astype(BF16)
    iwt_ref[...] = (pi[:, 256:384] * IDX_W_SCALE).T[0:8, :]
    pc = _dot(h, wc_ref[...])
    cq_ref[...] = (pc[:, 0:256] * (QK_SCALE * LOG2_E)).astype(BF16)
    ck_ref[...] = pc[:, 256:512].astype(BF16)
    cv_ref[...] = pc[:, 512:768].astype(BF16)


def _attn_proj(x, g_pre, w, tables):
    s = x.shape[0]
    nb = s // ROW_BLOCK
    cos64, sin64, cos32, sin32 = tables
    return pl.pallas_call(
        _attn_proj_kernel,
        grid=(nb,),
        in_specs=[_rows(D_MODEL), _full((1, D_MODEL)), _full((D_MODEL, 768)), _full((D_MODEL, 384)),
                  _full((D_MODEL, 768)), _rows(256), _rows(256), _rows(128), _rows(128)],
        out_specs=[_rows(256), _rows(256), pl.BlockSpec((1, 256, ROW_BLOCK), lambda i: (i, 0, 0)),
                   _rows(128), _rows(IDX_DIM), pl.BlockSpec((8, ROW_BLOCK), lambda i: (0, i)),
                   _rows(256), _rows(256), _rows(256)],
        out_shape=[jax.ShapeDtypeStruct((s, 256), F32), jax.ShapeDtypeStruct((s, 256), BF16),
                   jax.ShapeDtypeStruct((nb, 256, ROW_BLOCK), BF16),
                   jax.ShapeDtypeStruct((s, 128), F32), jax.ShapeDtypeStruct((s, IDX_DIM), BF16),
                   jax.ShapeDtypeStruct((8, s), F32),
                   jax.ShapeDtypeStruct((s, 256), BF16), jax.ShapeDtypeStruct((s, 256), BF16),
                   jax.ShapeDtypeStruct((s, 256), BF16)],
        compiler_params=_params(),
        name="attn_proj",
    )(x, g_pre, w["wa"], w["wi"], w["wc"], cos64, sin64, cos32, sin32)


def _local_mix_kernel(x_ref, g_ref, w_ref, convw_ref, convb_ref, lng_ref, lnb_ref, ws_ref, bs_ref,
                      ogb_ref, ogd_ref, ypad_ref):
    t = x_ref.shape[0]

    @pl.when(pl.program_id(0) == 0)
    def _():
        ypad_ref[0:CONV_HALO, :] = jnp.zeros((CONV_HALO, BRANCH_WIDTH), F32)

    h = _rmsnorm_rows(x_ref[...], g_ref[...]).astype(BF16)
    pr = _dot(h, w_ref[...])
    gate_b, gate_c, x_in = pr[:, 0:256], pr[:, 256:512], pr[:, 512:768]
    d_u, d_v = pr[:, 768:1024], pr[:, 1024:1280]
    silu_b, silu_d = pr[:, 1280:1536], pr[:, 1536:1792]

    y = gate_c * x_in
    ypad_ref[CONV_HALO:CONV_HALO + t, :] = y
    y1 = ypad_ref[CONV_HALO - 1:CONV_HALO - 1 + t, :]
    y2 = ypad_ref[CONV_HALO - 2:CONV_HALO - 2 + t, :]
    conv = convw_ref[2:3, :] * y + convw_ref[1:2, :] * y1 + convw_ref[0:1, :] * y2
    ypad_ref[0:CONV_HALO, :] = y[t - CONV_HALO:t, :]
    o_b = gate_b * (conv + convb_ref[...])
    ogb_ref[...] = (o_b * (silu_b * _sigmoid(silu_b))).astype(BF16)

    mu = jnp.mean(d_v, axis=-1, keepdims=True)
    dc = d_v - mu
    var = jnp.mean(dc * dc, axis=-1, keepdims=True)
    vn = dc * lax.rsqrt(var + EPS) * lng_ref[...] + lnb_ref[...]
    group = lax.broadcasted_iota(jnp.int32, (1, BRANCH_WIDTH), 1) // GROUP_DIM
    tril = (lax.broadcasted_iota(jnp.int32, (CHUNK, CHUNK), 0)
            >= lax.broadcasted_iota(jnp.int32, (CHUNK, CHUNK), 1))
    wm = [jnp.where(tril, ws_ref[g], 0.0).astype(BF16) for g in range(N_GROUPS)]
    mixed = []
    for c in range(t // CHUNK):
        vc = vn[c * CHUNK:(c + 1) * CHUNK, :]
        m = bs_ref[...]
        for g in range(N_GROUPS):
            m = m + _dot(wm[g], jnp.where(group == g, vc, 0.0).astype(BF16))
        mixed.append(m)
    o_d = d_u * jnp.concatenate(mixed, axis=0)
    ogd_ref[...] = (o_d * (silu_d * _sigmoid(silu_d))).astype(BF16)


def _local_mix(x, g_pre, w):
    s = x.shape[0]
    return pl.pallas_call(
        _local_mix_kernel,
        grid=(s // ROW_BLOCK,),
        in_specs=[_rows(D_MODEL), _full((1, D_MODEL)), _full((D_MODEL, 1792)), _full((8, 256)),
                  _full((1, 256)), _full((1, 256)), _full((1, 256)),
                  _full((N_GROUPS, CHUNK, CHUNK)), _full((CHUNK, 256))],
        out_specs=[_rows(256), _rows(256)],
        out_shape=[jax.ShapeDtypeStruct((s, 256), BF16), jax.ShapeDtypeStruct((s, 256), BF16)],
        scratch_shapes=[pltpu.VMEM((ROW_BLOCK + CONV_HALO, BRANCH_WIDTH), F32)],
        compiler_params=_params(),
        name="local_mix",
    )(x, g_pre, w["w2"], w["conv_w"], w["conv_b"], w["ln_g"], w["ln_b"], w["w_spatial"], w["b_spatial"])


def _pre_mix_kernel(*refs):
    x_ref, g_ref = refs[0], refs[1]
    attn_in, mix_in = refs[2:9], refs[9:16]
    attn_out, mix_out, ypad_ref = refs[16:25], refs[25:27], refs[27]
    _attn_proj_kernel(x_ref, g_ref, *attn_in, *attn_out)
    _local_mix_kernel(x_ref, g_ref, *mix_in, *mix_out, ypad_ref)


def _pre_mix(x, g_pre, w, tables):
    s = x.shape[0]
    nb = s // ROW_BLOCK
    cos64, sin64, cos32, sin32 = tables
    return pl.pallas_call(
        _pre_mix_kernel,
        grid=(nb,),
        in_specs=[_rows(D_MODEL), _full((1, D_MODEL)),
                  _full((D_MODEL, 768)), _full((D_MODEL, 384)), _full((D_MODEL, 768)),
                  _rows(256), _rows(256), _rows(128), _rows(128),
                  _full((D_MODEL, 1792)), _full((8, 256)), _full((1, 256)), _full((1, 256)), _full((1, 256)),
                  _full((N_GROUPS, CHUNK, CHUNK)), _full((CHUNK, 256))],
        out_specs=[_rows(256), _rows(256), pl.BlockSpec((1, 256, ROW_BLOCK), lambda i: (i, 0, 0)),
                   _rows(128), _rows(IDX_DIM), pl.BlockSpec((8, ROW_BLOCK), lambda i: (0, i)),
                   _rows(256), _rows(256), _rows(256), _rows(256), _rows(256)],
        out_shape=[jax.ShapeDtypeStruct((s, 256), F32), jax.ShapeDtypeStruct((s, 256), BF16),
                   jax.ShapeDtypeStruct((nb, 256, ROW_BLOCK), BF16),
                   jax.ShapeDtypeStruct((s, 128), F32), jax.ShapeDtypeStruct((s, IDX_DIM), BF16),
                   jax.ShapeDtypeStruct((8, s), F32),
                   jax.ShapeDtypeStruct((s, 256), BF16), jax.ShapeDtypeStruct((s, 256), BF16),
                   jax.ShapeDtypeStruct((s, 256), BF16),
                   jax.ShapeDtypeStruct((s, 256), BF16), jax.ShapeDtypeStruct((s, 256), BF16)],
        scratch_shapes=[pltpu.VMEM((ROW_BLOCK + CONV_HALO, BRANCH_WIDTH), F32)],
        compiler_params=_params(),
        name="pre_mix",
    )(x, g_pre, w["wa"], w["wi"], w["wc"], cos64, sin64, cos32, sin32,
      w["w2"], w["conv_w"], w["conv_b"], w["ln_g"], w["ln_b"], w["w_spatial"], w["b_spatial"])


PACK_ROWS = 16
ACC_ROWS = HEAD_DIM + PACK_ROWS


GROUP = 256
GROUP_BATCH = 8

def _sortable_key(score):
    bits = lax.bitcast_convert_type(score, jnp.int32)
    return bits ^ ((bits >> 31) | jnp.int32(-2 ** 31))


def _bit_planes(blocks):
    a = list(blocks)
    mask, j = 0x0000FFFF, 16
    while j:
        m32 = jnp.int32(mask - (1 << 32) if mask >= (1 << 31) else mask)
        k = 0
        while k < 32:
            t = (a[k] ^ lax.shift_right_logical(a[k + j], jnp.int32(j))) & m32
            a[k] = a[k] ^ t
            a[k + j] = a[k + j] ^ (t << j)
            k = (k + j + 1) & ~j
        j >>= 1
        mask ^= (mask << j) & 0xFFFFFFFF
    return a


def _over_groups(n_groups, fn, carry):
    def batch(c, x):
        for u in range(GROUP_BATCH):
            x = fn(c * GROUP_BATCH + u, x)
        return x

    n_batches = n_groups // GROUP_BATCH
    carry = lax.fori_loop(0, n_batches, batch, carry)
    return lax.fori_loop(n_batches * GROUP_BATCH, n_groups, fn, carry)


def _dsa_kernel(q_ref, iq_ref, iwt_ref, k_ref, vt_ref, ik_ref, o_ref,
                planes_ref, eq_ref, gt_ref, qm_ref, iqt_ref, acc_ref, m_ref, knorm_ref, *, topk):
    nq = q_ref.shape[0]
    kb = DSA_KEY_TILE
    i = pl.program_id(0)
    t_idx = i * nq + lax.broadcasted_iota(jnp.int32, (1, nq), 1)
    n_below = (i * nq) // kb
    left = ((i + 1) * nq - n_below * kb) // TAIL_ROWS
    n_wide = (left > kb // (2 * TAIL_ROWS)).astype(jnp.int32)
    n_tail = left * (1 - n_wide)

    def over_tiles(fn, carry):
        carry = lax.fori_loop(0, n_below, lambda j, c: fn(pl.multiple_of(j * kb, kb), kb, c, False), carry)
        start = pl.multiple_of(n_below * kb, kb)
        carry = lax.fori_loop(0, n_wide, lambda j, c: fn(start, kb, c, True), carry)
        return lax.fori_loop(
            0, n_tail, lambda j, c: fn(pl.multiple_of(start + j * TAIL_ROWS, TAIL_ROWS), TAIL_ROWS, c, True), carry)

    qt = q_ref[...].T
    head_of_row = lax.broadcasted_iota(jnp.int32, (BRANCH_WIDTH, 1), 0) // HEAD_DIM
    for h in range(N_HEADS):
        qm_ref[h] = jnp.where(head_of_row == h, qt, 0.0).astype(BF16)
    iqt_ref[...] = iq_ref[...].T.astype(BF16)

    @pl.when(i == 0)
    def _():
        head_of_lane = (lax.broadcasted_iota(jnp.int32, (BRANCH_WIDTH, 1), 0) // HEAD_DIM
                        == lax.broadcasted_iota(jnp.int32, (1, 128), 1)).astype(BF16)

        def widest(c, best):
            kf = k_ref[pl.ds(pl.multiple_of(c * kb, kb), kb), :].astype(F32)
            return jnp.maximum(best, jnp.max(_dot((kf * kf).astype(BF16), head_of_lane), axis=0, keepdims=True))

        best = lax.fori_loop(0, k_ref.shape[0] // kb, widest, jnp.zeros((1, 128), F32))
        knorm_ref[...] = jnp.broadcast_to(best, knorm_ref.shape)

    logit_cap = jnp.zeros((1, nq), F32)
    qt_sq = qt * qt
    for h in range(N_HEADS):
        q_sq = jnp.sum(qt_sq[h * HEAD_DIM:(h + 1) * HEAD_DIM, :], axis=0, keepdims=True)
        logit_cap = jnp.maximum(logit_cap, jnp.sqrt(q_sq * knorm_ref[0:1, h:h + 1]) * CAP_SLACK + CAP_SLACK)
    w_rows = [iwt_ref[h:h + 1, :] for h in range(N_IDX_HEADS)]

    def score_tile(base, rows, carry, on_diagonal):
        ikb = ik_ref[pl.ds(base, rows), :]
        sc = w_rows[0] * jnp.maximum(_dot(ikb, iqt_ref[0:IDX_DIM, :]), 0.0)
        for h in range(1, N_IDX_HEADS):
            logit = _dot(ikb, iqt_ref[h * IDX_DIM:(h + 1) * IDX_DIM, :])
            sc = sc + w_rows[h] * jnp.maximum(logit, 0.0)
        if on_diagonal:
            sc = jnp.where(base + lax.broadcasted_iota(jnp.int32, (rows, 1), 0) <= t_idx, sc, SCORE_FLOOR)
        key = _sortable_key(sc)
        for u in range(rows // GROUP):
            planes = _bit_planes([key[u * GROUP + 8 * j:u * GROUP + 8 * j + 8, :] for j in range(32)])
            first = (base // GROUP + u) * 32
            for s in range(32):
                planes_ref[first + s] = planes[s]
        return carry

    over_tiles(score_tile, 0)

    n_groups = ((i + 1) * nq) // GROUP
    all_ones, no_bits = jnp.full((8, nq), -1, jnp.int32), jnp.zeros((8, nq), jnp.int32)

    def reset(g, carry):
        eq_ref[g], gt_ref[g] = all_ones, no_bits
        return carry

    _over_groups(n_groups, reset, 0)

    def clear(g, carry):
        eq_ref[g], gt_ref[g] = no_bits, no_bits
        return carry

    lax.fori_loop(n_groups, (n_below + n_wide) * (kb // GROUP) + n_tail * (TAIL_ROWS // GROUP), clear, 0)

    def count_first(g, acc):
        return acc + lax.population_count(gt_ref[g] | (eq_ref[g] & planes_ref[g * 32]))

    def decide_bit(s, partial):
        cnt = jnp.sum(partial, axis=0, keepdims=True)
        take = jnp.broadcast_to(jnp.where(cnt >= topk, jnp.int32(-1), jnp.int32(0)), (8, nq))
        after = jnp.minimum(s + 1, 31)

        def update_and_count(g, acc):
            eq = eq_ref[g]
            with_bit = eq & planes_ref[g * 32 + s]
            eq = (with_bit & take) | ((eq ^ with_bit) & ~take)
            gt = gt_ref[g] | (with_bit & ~take)
            eq_ref[g], gt_ref[g] = eq, gt
            return acc + lax.population_count(gt | (eq & planes_ref[g * 32 + after]))

        return _over_groups(n_groups, update_and_count, no_bits)

    lax.fori_loop(0, 32, decide_bit, _over_groups(n_groups, count_first, no_bits))
    n_above = jnp.sum(_over_groups(n_groups, lambda g, acc: acc + lax.population_count(gt_ref[g]), no_bits),
                      axis=0, keepdims=True)
    quota = (topk - n_above).astype(F32)
    keep_all = jnp.broadcast_to(jnp.where(t_idx < topk, jnp.int32(-1), jnp.int32(0)), (8, nq))

    incl_lower = (lax.broadcasted_iota(jnp.int32, (TIE_BLOCK, TIE_BLOCK), 0)
                  >= lax.broadcasted_iota(jnp.int32, (TIE_BLOCK, TIE_BLOCK), 1)).astype(BF16)

    def attend_tile(base, rows, ties_before, on_diagonal, running_max):
        above, tie = [], []
        for u in range(rows // GROUP):
            g = base // GROUP + u
            gt_w = gt_ref[g] | keep_all
            eq_w = eq_ref[g] & ~keep_all
            for j in range(32):
                above.append(lax.shift_right_logical(gt_w, jnp.int32(31 - j)) & 1)
                tie.append(lax.shift_right_logical(eq_w, jnp.int32(31 - j)) & 1)
        above = jnp.concatenate(above, axis=0).astype(F32).astype(BF16)
        tie = jnp.concatenate(tie, axis=0).astype(F32).astype(BF16)
        parts, running = [], ties_before
        for u in range(rows // TIE_BLOCK):
            parts.append(_dot(incl_lower, tie[u * TIE_BLOCK:(u + 1) * TIE_BLOCK, :]) + running)
            running = parts[-1][TIE_BLOCK - 1:TIE_BLOCK, :]
        ties_upto = jnp.concatenate(parts, axis=0)
        keep = above + tie * jnp.where(ties_upto <= quota, 1.0, 0.0).astype(BF16)
        if on_diagonal:
            causal = base + lax.broadcasted_iota(jnp.int32, (rows, 1), 0) <= t_idx
            keep = keep * jnp.where(causal, 1.0, 0.0).astype(BF16)
        bias = ((keep - 1.0) * (-MASK_BIAS)).astype(F32)
        kt = k_ref[pl.ds(base, rows), :]
        n_sub = rows // ATT_BLOCK
        vts = [vt_ref[base // ATT_BLOCK + u] for u in range(n_sub)]
        ones_rows = jnp.ones((PACK_ROWS, ATT_BLOCK), BF16)

        def value_matmul(h, p):
            hs = slice(h * HEAD_DIM, (h + 1) * HEAD_DIM)
            pb = p.astype(BF16)
            return sum(_dot(jnp.concatenate([vts[u][hs, :], ones_rows], axis=0),
                            pb[u * ATT_BLOCK:(u + 1) * ATT_BLOCK, :]) for u in range(n_sub))

        if running_max:
            logits = [_dot(kt, qm_ref[h]) + bias for h in range(N_HEADS)]
            m_old = [m_ref[h:h + 1, :] for h in range(N_HEADS)]
            m_new = [jnp.maximum(m_old[h], jnp.max(logits[h], axis=0, keepdims=True)) for h in range(N_HEADS)]
            alpha = [jnp.exp2(m_old[h] - m_new[h]) for h in range(N_HEADS)]
            probs = [jnp.exp2(logits[h] - m_new[h]) for h in range(N_HEADS)]
            for h in range(N_HEADS):
                m_ref[h:h + 1, :] = m_new[h]
                head = slice(h * ACC_ROWS, (h + 1) * ACC_ROWS)
                acc_ref[head, :] = alpha[h] * acc_ref[head, :] + value_matmul(h, probs[h])
        else:
            shift = bias - logit_cap
            probs = [jnp.exp2(_dot(kt, qm_ref[h]) + shift) for h in range(N_HEADS)]
            for h in range(N_HEADS):
                head = slice(h * ACC_ROWS, (h + 1) * ACC_ROWS)
                acc_ref[head, :] += value_matmul(h, probs[h])
        return running

    def attend(running_max):
        acc_ref[...] = jnp.zeros(acc_ref.shape, F32)
        over_tiles(functools.partial(attend_tile, running_max=running_max), jnp.zeros((1, nq), F32))

    attend(running_max=False)
    denominators = jnp.concatenate(
        [acc_ref[h * ACC_ROWS + HEAD_DIM:h * ACC_ROWS + HEAD_DIM + 1, :] for h in range(N_HEADS)], axis=0)
    weakest = jnp.min(jnp.min(denominators, axis=1, keepdims=True), axis=0, keepdims=True)[0, 0]

    @pl.when(jnp.logical_not(weakest >= MIN_DENOMINATOR))
    def _():
        m_ref[...] = jnp.full(m_ref.shape, MASK_BIAS, F32)
        attend(running_max=True)

    heads = [acc_ref[h * ACC_ROWS:h * ACC_ROWS + HEAD_DIM, :]
             / acc_ref[h * ACC_ROWS + HEAD_DIM:h * ACC_ROWS + HEAD_DIM + 1, :] for h in range(N_HEADS)]
    o_ref[...] = jnp.concatenate(heads, axis=0).T


def _dsa_attention(qa, iq, iwt, ka, vat, ik):
    s = qa.shape[0]
    nq = ATT_BLOCK
    topk = min(TOPK_MAX, s // 4)
    assert s % DSA_KEY_TILE == 0 and DSA_KEY_TILE % ATT_BLOCK == 0
    assert ATT_BLOCK % TAIL_ROWS == 0 and DSA_KEY_TILE % TIE_BLOCK == 0
    assert TAIL_ROWS % GROUP == 0 and DSA_KEY_TILE % GROUP == 0 and ATT_BLOCK % GROUP == 0
    return pl.pallas_call(
        functools.partial(_dsa_kernel, topk=topk),
        grid=(s // nq,),
        in_specs=[_rows(256, nq), _rows(128, nq), pl.BlockSpec((8, nq), lambda i: (0, i)),
                  _full((s, 256)), _full((s // ATT_BLOCK, 256, ATT_BLOCK)), _full((s, IDX_DIM))],
        out_specs=_rows(256, nq),
        out_shape=jax.ShapeDtypeStruct((s, 256), F32),
        scratch_shapes=[pltpu.VMEM((s // GROUP * 32, 8, nq), jnp.int32), pltpu.VMEM((s // GROUP, 8, nq), jnp.int32),
                        pltpu.VMEM((s // GROUP, 8, nq), jnp.int32), pltpu.VMEM((N_HEADS, 256, nq), BF16),
                        pltpu.VMEM((128, nq), BF16), pltpu.VMEM((N_HEADS * ACC_ROWS, nq), F32),
                        pltpu.VMEM((8, nq), F32), pltpu.VMEM((8, 128), F32)],
        compiler_params=_params(),
        name="dsa_attn",
    )(qa, iq, iwt, ka, vat, ik)


def _sb_kernel(q_ref, k_ref, v_ref, o_ref, acc_ref, carry_ref):
    nq = q_ref.shape[0]
    kb = ATT_BLOCK
    i = pl.program_id(0)
    q = q_ref[...]
    lane_head = lax.broadcasted_iota(jnp.int32, (1, BRANCH_WIDTH), 1) // HEAD_DIM
    qh = [jnp.where(lane_head == h, q, jnp.zeros_like(q)) for h in range(N_HEADS)]
    later = (lax.broadcasted_iota(jnp.int32, (2 * kb, kb), 0) % kb
             > lax.broadcasted_iota(jnp.int32, (2 * kb, kb), 1)).astype(BF16)
    acc_ref[...] = jnp.zeros(acc_ref.shape, F32)
    carry_ref[...] = jnp.zeros(carry_ref.shape, F32)

    def walk_tile(j, on_diagonal):
        base = pl.multiple_of(j * kb, kb)
        kt = k_ref[pl.ds(base, kb), :]
        vt = v_ref[pl.ds(base, kb), :]
        if on_diagonal:
            strict = (lax.broadcasted_iota(jnp.int32, (1, kb), 1)
                      < lax.broadcasted_iota(jnp.int32, (nq, 1), 0))
        heads = range(N_HEADS)
        z = [_dot_nt(qh[h], kt) for h in heads]
        softplus = [jnp.maximum(z[h], 0.0) + jnp.log2(1.0 + jnp.exp2(-jnp.abs(z[h]))) for h in heads]
        sp = [jnp.where(strict, softplus[h], 0.0) for h in heads] if on_diagonal else softplus
        hi = [sp[h].astype(BF16) for h in heads]
        lo = [(sp[h] - hi[h].astype(F32)).astype(BF16) for h in heads]
        after = [_dot(jnp.concatenate([hi[h], lo[h]], axis=1), later) for h in heads]
        c = [carry_ref[h] for h in heads]
        wts = [jnp.exp2((z[h] - softplus[h]) - (after[h] + c[h])) for h in heads]
        if on_diagonal:
            wts = [jnp.where(strict, wts[h], 0.0) for h in heads]
        out = sum(_dot(wts[h].astype(BF16), jnp.where(lane_head == h, vt, jnp.zeros_like(vt))) for h in heads)
        acc_ref[...] += out
        lowest = jnp.full((1, 1), jnp.inf, F32)
        for h in heads:
            c_new = c[h] + (after[h][:, 0:1] + sp[h][:, 0:1])
            carry_ref[h] = c_new
            lowest = jnp.minimum(lowest, jnp.min(c_new, axis=0, keepdims=True))
        if on_diagonal:
            return jnp.int32(1)
        return (lowest[0, 0] < SB_DEAD).astype(jnp.int32)

    def alive(state):
        j, live = state
        return jnp.logical_and(j >= 0, live > 0)

    def walk(state):
        j, _ = state
        return j - 1, walk_tile(j, on_diagonal=False)

    lax.while_loop(alive, walk, (i - 1, walk_tile(i, on_diagonal=True)))
    o_ref[...] = acc_ref[...]


def _sb_attention(cq, ck, cv):
    s = cq.shape[0]
    nq = ATT_BLOCK
    return pl.pallas_call(
        _sb_kernel,
        grid=(s // nq,),
        in_specs=[_rows(256, nq), _full((s, 256)), _full((s, 256))],
        out_specs=_rows(256, nq),
        out_shape=jax.ShapeDtypeStruct((s, 256), F32),
        scratch_shapes=[pltpu.VMEM((nq, 256), F32), pltpu.VMEM((N_HEADS, nq, 1), F32)],
        compiler_params=_params(),
        name="sb_attn",
    )(cq, ck, cv)


def _merge_out_kernel(x_ref, oa_ref, oc_ref, ogb_ref, ogd_ref, p_ref, g_ref, wg_ref, wm_ref, bm_ref,
                      wb_ref, wo_ref, gpost_ref, wple_ref, wpg_ref, out_ref):
    x = x_ref[...]
    h = _rmsnorm_rows(x, g_ref[...]).astype(BF16)
    gates = _dot(h, wg_ref[...])
    silu = gates * _sigmoid(gates)
    branch_in = [(oa_ref[...] * silu[:, 0:256]).astype(BF16), ogb_ref[...],
                 (oc_ref[...] * silu[:, 256:512]).astype(BF16), ogd_ref[...]]
    merged = jnp.zeros((x.shape[0], D_MODEL), F32)
    for n in range(4):
        cols = slice(n * D_MODEL, (n + 1) * D_MODEL)
        gate = _sigmoid(_dot(h, wm_ref[:, cols]) + bm_ref[:, cols])
        merged = merged + gate * _dot(branch_in[n], wb_ref[n])
    y = _dot(merged.astype(BF16), wo_ref[...])
    x1 = x + _rmsnorm_rows(y, gpost_ref[...])
    ple = _dot(p_ref[...].astype(BF16), wple_ref[...])
    out_ref[...] = x1 + ple * _sigmoid(_dot(x1.astype(BF16), wpg_ref[...]))


def _merge_out(x, o_a, o_c, og_b, og_d, p_i, g_pre, w):
    s = x.shape[0]
    return pl.pallas_call(
        _merge_out_kernel,
        grid=(s // ROW_BLOCK,),
        in_specs=[_rows(D_MODEL), _rows(256), _rows(256), _rows(256), _rows(256), _rows(PLE_DIM),
                  _full((1, D_MODEL)), _full((D_MODEL, 512)), _full((D_MODEL, 4 * D_MODEL)),
                  _full((1, 4 * D_MODEL)), _full((4, BRANCH_WIDTH, D_MODEL)), _full((D_MODEL, D_MODEL)),
                  _full((1, D_MODEL)), _full((PLE_DIM, D_MODEL)), _full((D_MODEL, D_MODEL))],
        out_specs=_rows(D_MODEL),
        out_shape=jax.ShapeDtypeStruct((s, D_MODEL), F32),
        compiler_params=_params(),
        name="merge_out",
    )(x, o_a, o_c, og_b, og_d, p_i, g_pre, w["wg"], w["wm"], w["b_merge"], w["w_branch"], w["w_out"],
      w["g_post"], w["w_ple"], w["w_ple_gate"])


IN_COLS = 8100
IN_HEAD = 932
PREP_ROWS = 128


def _weight_layout_kernel(w_ref, wa_ref, wi_ref, wc_ref, w2_ref, wg_ref, wm_ref):
    wa_ref[0] = w_ref[0, :, 0:768].astype(BF16)
    tail = w_ref[0, :, 896:1024]
    lane = lax.broadcasted_iota(jnp.int32, (1, 128), 1)
    wi_ref[0, :, 0:128] = w_ref[0, :, 768:896].astype(BF16)
    wi_ref[0, :, 128:256] = jnp.where(lane < IDX_DIM, tail, 0.0).astype(BF16)
    wi_ref[0, :, 256:384] = jnp.where(lane < N_IDX_HEADS, pltpu.roll(tail, 128 - IDX_DIM, 1), 0.0).astype(BF16)

    def rest(lo, hi):
        return w_ref[0, :, IN_HEAD + lo:IN_HEAD + hi].astype(BF16)

    w2_ref[0, :, 0:768] = rest(0, 768)
    wc_ref[0] = rest(768, 1536)
    w2_ref[0, :, 768:1280] = rest(1536, 2048)
    wg_ref[0, :, 0:256] = rest(2048, 2304)
    w2_ref[0, :, 1280:1536] = rest(2304, 2560)
    wg_ref[0, :, 256:512] = rest(2560, 2816)
    w2_ref[0, :, 1536:1792] = rest(2816, 3072)
    wm_ref[0] = rest(3072, 7168)


def _weight_layout(w_in):
    depth = w_in.shape[0]
    assert w_in.shape[1:] == (D_MODEL, IN_COLS)
    widths = (768, 384, 768, 1792, 512, 4096)
    return pl.pallas_call(
        _weight_layout_kernel,
        grid=(depth, D_MODEL // PREP_ROWS),
        in_specs=[pl.BlockSpec((1, PREP_ROWS, IN_COLS), lambda l, r: (l, r, 0))],
        out_specs=[pl.BlockSpec((1, PREP_ROWS, n), lambda l, r: (l, r, 0)) for n in widths],
        out_shape=[jax.ShapeDtypeStruct((depth, D_MODEL, n), BF16) for n in widths],
        compiler_params=_params(2),
        name="weight_layout",
    )(w_in)


def _layer_weights(i, w_proj, conv_w, conv_b, ln_g, ln_b, w_spatial, b_spatial, b_merge, w_branch, w_out,
                   g_post, w_ple, w_ple_gate):
    row = lambda v: v[None, :]
    wa, wi, wc, w2, wg, wm = (w[i] for w in w_proj)
    return {
        "wa": wa, "wi": wi, "wc": wc, "w2": w2, "wg": wg, "wm": wm,
        "conv_w": jnp.pad(conv_w[i], ((0, 8 - CONV_WIDTH), (0, 0))),
        "conv_b": row(conv_b[i]), "ln_g": row(ln_g[i]), "ln_b": row(ln_b[i]),
        "w_spatial": w_spatial[i],
        "b_spatial": jnp.repeat(b_spatial[i].T, GROUP_DIM, axis=1),
        "b_merge": row(b_merge[i]),
        "w_branch": w_branch[i].astype(BF16), "w_out": w_out[i].astype(BF16), "g_post": row(g_post[i]),
        "w_ple": w_ple[i].astype(BF16), "w_ple_gate": w_ple_gate[i].astype(BF16),
    }


def kernel(x, p, positions, g_pre, w_in, conv_w, conv_b, ln_g, ln_b, w_spatial, b_spatial, b_merge,
           w_branch, w_out, g_post, w_ple, w_ple_gate):
    batch, s, _ = x.shape
    assert batch == 1 and s % ROW_BLOCK == 0 and s % ATT_BLOCK == 0 and ATT_BLOCK >= min(TOPK_MAX, s // 4)
    depth = w_in.shape[0]
    xs = x[0]
    tables = _rope_tables(positions[0][:, None])
    w_proj = _weight_layout(w_in)
    for i in range(depth):
        w = _layer_weights(i, w_proj, conv_w, conv_b, ln_g, ln_b, w_spatial, b_spatial, b_merge, w_branch,
                           w_out, g_post, w_ple, w_ple_gate)
        g = g_pre[i][None, :]
        qa, ka, vat, iq, ik, iwt, cq, ck, cv, og_b, og_d = _pre_mix(xs, g, w, tables)
        o_a = _dsa_attention(qa, iq, iwt, ka, vat, ik)
        o_c = _sb_attention(cq, ck, cv)
        xs = _merge_out(xs, o_a, o_c, og_b, og_d, p[i][0], g, w)
    return xs[None]
```
